```python
import math
import jax
import jax.numpy as jnp
from jax import lax
import numpy as np

D_MODEL = 1024
BATCH = 8
SEQ = 4096
DEPTH = 2

GRID_W = 64
CTX_LEN = 256
EPS = 1e-6
NEG_INF = -1e30

POOL_GROUPS = 4
POOL_GROUP_DIM = 128
POOL_DIM = POOL_GROUPS * POOL_GROUP_DIM
POOL_WINDOWS = (2, 4, 8, 16)
HEAD_DIM = 64
N_HEADS = 8
N_KV_HEADS = 2
GQA_GROUP = N_HEADS // N_KV_HEADS
Q_DIM = N_HEADS * HEAD_DIM
KV_DIM = N_KV_HEADS * HEAD_DIM
AB_IN_DIM = POOL_DIM + Q_DIM + 2 * KV_DIM
AB_OUT_DIM = POOL_DIM + Q_DIM
WINDOW = 128
BLOCK = 128
ROPE_THETA = 10000.0

GDN_HEADS = 8
GDN_HEAD_DIM = 128
GDN_DIM = GDN_HEADS * GDN_HEAD_DIM
CONV_K = 5
CHUNK = 64
GDN_IN_DIM = 4 * GDN_DIM + 4 * GDN_HEADS

D_FF = 2816
N_EXPERTS = 8
TOP_K = 2
D_EXPERT = 3584

kernel_name = 'hybrid_pool_swa_gdn_moe_dit'


def rmsnorm(x, g):
    xf = x.astype(jnp.float32)
    y = xf * lax.rsqrt(jnp.mean(xf * xf, axis=-1, keepdims=True) + EPS)
    return y.astype(x.dtype) * g


def l2norm(x):
    xf = x.astype(jnp.float32)
    return (xf * lax.rsqrt(jnp.sum(xf * xf, axis=-1, keepdims=True) + EPS)).astype(x.dtype)


def swiglu(h, w1, w3, w2):
    return (jax.nn.silu(h @ w1) * (h @ w3)) @ w2


def multiscale_pool(u, pool_w, pool_scale):
    B, L, _ = u.shape
    cs = jnp.cumsum(u.astype(jnp.float32), axis=1)
    cs = jnp.concatenate([jnp.zeros_like(cs[:, :1]), cs], axis=1)
    t = jnp.arange(L)
    means = []
    for gi, w in enumerate(POOL_WINDOWS):
        lo = jnp.clip(t - w // 2, 0, L)
        hi = jnp.clip(t + w // 2, 0, L)
        seg = cs[:, :, gi * POOL_GROUP_DIM:(gi + 1) * POOL_GROUP_DIM]
        means.append((seg[:, hi] - seg[:, lo]) / (hi - lo).astype(jnp.float32)[None, :, None])
    pooled = jnp.stack(means, axis=2)
    delta = (pooled - u.reshape(B, L, POOL_GROUPS, POOL_GROUP_DIM).astype(jnp.float32)).astype(u.dtype)
    y = jnp.einsum('blgc,gcd->blgd', delta, pool_w) * pool_scale.reshape(POOL_GROUPS, POOL_GROUP_DIM)
    return y.reshape(B, L, POOL_DIM)


def axial_rope(x, rows, cols):
    half = HEAD_DIM // 2
    inv = ROPE_THETA ** (-jnp.arange(0, half, 2, dtype=jnp.float32) / half)

    def rot(xa, pos):
        ang = pos[:, None] * inv[None, :]
        cos = jnp.cos(ang)[None, :, None, :].astype(x.dtype)
        sin = jnp.sin(ang)[None, :, None, :].astype(x.dtype)
        x1, x2 = xa[..., :half // 2], xa[..., half // 2:]
        return jnp.concatenate([x1 * cos - x2 * sin, x2 * cos + x1 * sin], axis=-1)

    return jnp.concatenate([rot(x[..., :half], rows), rot(x[..., half:], cols)], axis=-1)


def split_ab(p):
    B, L, _ = p.shape
    u = p[..., :POOL_DIM]
    q = p[..., POOL_DIM:POOL_DIM + Q_DIM].reshape(B, L, N_HEADS, HEAD_DIM)
    k = p[..., POOL_DIM + Q_DIM:POOL_DIM + Q_DIM + KV_DIM].reshape(B, L, N_KV_HEADS, HEAD_DIM)
    v = p[..., POOL_DIM + Q_DIM + KV_DIM:].reshape(B, L, N_KV_HEADS, HEAD_DIM)
    return u, q, k, v


def banded_attention(q, k, v, k_ctx, v_ctx, sinks):
    B, L, _, _ = q.shape
    Lc = k_ctx.shape[1]
    nb = L // BLOCK
    scale = HEAD_DIM ** -0.5
    qb = q.reshape(B, nb, BLOCK, N_KV_HEADS, GQA_GROUP, HEAD_DIM)
    pad = ((0, 0), (BLOCK, BLOCK), (0, 0), (0, 0))
    kp = jnp.pad(k, pad).reshape(B, nb + 2, BLOCK, N_KV_HEADS, HEAD_DIM)
    vp = jnp.pad(v, pad).reshape(B, nb + 2, BLOCK, N_KV_HEADS, HEAD_DIM)
    k_band = jnp.concatenate([kp[:, :-2], kp[:, 1:-1], kp[:, 2:]], axis=2)
    v_band = jnp.concatenate([vp[:, :-2], vp[:, 1:-1], vp[:, 2:]], axis=2)
    s_band = jnp.einsum('bnqhgd,bnkhd->bnhgqk', qb, k_band, preferred_element_type=jnp.float32) * scale
    qpos = jnp.arange(nb)[:, None] * BLOCK + jnp.arange(BLOCK)[None, :]
    kpos = (jnp.arange(nb)[:, None] - 1) * BLOCK + jnp.arange(3 * BLOCK)[None, :]
    allowed = ((jnp.abs(kpos[:, None, :] - qpos[:, :, None]) <= WINDOW)
               & (kpos[:, None, :] >= 0) & (kpos[:, None, :] < L))
    s_band = jnp.where(allowed[None, :, None, None], s_band, NEG_INF)
    s_ctx = jnp.einsum('bnqhgd,bkhd->bnhgqk', qb, k_ctx, preferred_element_type=jnp.float32) * scale
    sink = sinks.astype(jnp.float32).reshape(N_KV_HEADS, GQA_GROUP)
    s_sink = jnp.broadcast_to(sink[None, None, :, :, None, None], s_ctx.shape[:-1] + (1,))
    p = jax.nn.softmax(jnp.concatenate([s_sink, s_ctx, s_band], axis=-1), axis=-1)
    p_ctx = p[..., 1:1 + Lc].astype(v.dtype)
    p_band = p[..., 1 + Lc:].astype(v.dtype)
    o = (jnp.einsum('bnhgqk,bkhd->bnqhgd', p_ctx, v_ctx)
         + jnp.einsum('bnhgqk,bnkhd->bnqhgd', p_band, v_band))
    return o.reshape(B, L, Q_DIM)


def context_attention(q, k, v, sinks):
    B, Lc, _, _ = q.shape
    qc = q.reshape(B, Lc, N_KV_HEADS, GQA_GROUP, HEAD_DIM)
    s = jnp.einsum('bqhgd,bkhd->bhgqk', qc, k, preferred_element_type=jnp.float32) * HEAD_DIM ** -0.5
    sink = sinks.astype(jnp.float32).reshape(N_KV_HEADS, GQA_GROUP)
    s_sink = jnp.broadcast_to(sink[None, :, :, None, None], s.shape[:-1] + (1,))
    p = jax.nn.softmax(jnp.concatenate([s_sink, s], axis=-1), axis=-1)[..., 1:].astype(v.dtype)
    o = jnp.einsum('bhgqk,bkhd->bqhgd', p, v)
    return o.reshape(B, Lc, Q_DIM)


def pool_swa_mixer(h_lat, h_ctx, w_in, pool_w, pool_scale, sinks, w_out, ctx_out):
    L = h_lat.shape[1]
    rows_n = L // GRID_W
    rows = jnp.repeat(jnp.arange(rows_n, dtype=jnp.float32), GRID_W)
    cols = jnp.tile(jnp.arange(GRID_W, dtype=jnp.float32), rows_n)
    u_l, q_l, k_l, v_l = split_ab(h_lat @ w_in)
    u_c, q_c, k_c, v_c = split_ab(h_ctx @ w_in)
    q_l = axial_rope(q_l, rows, cols)
    k_l = axial_rope(k_l, rows, cols)
    a_l = banded_attention(q_l, k_l, v_l, k_c, v_c, sinks)
    y_lat = jnp.concatenate([multiscale_pool(u_l, pool_w, pool_scale), a_l], axis=-1) @ w_out
    if not ctx_out:
        return y_lat, None
    a_c = context_attention(q_c, k_c, v_c, sinks)
    y_ctx = jnp.concatenate([multiscale_pool(u_c, pool_w, pool_scale), a_c], axis=-1) @ w_out
    return y_lat, y_ctx


def centred_dwconv(x, w):
    return lax.conv_general_dilated(x, w[:, None, :], window_strides=(1,),
                                    padding=[(CONV_K // 2, CONV_K // 2)],
                                    dimension_numbers=('NWC', 'WIO', 'NWC'),
                                    feature_group_count=x.shape[-1])


def gated_delta_chunked(q, k, v, g, beta, s0):
    B, H, L, _ = q.shape
    DV = v.shape[-1]
    n = L // CHUNK

    def chunks(t):
        return t.astype(jnp.float32).reshape((B, H, n, CHUNK) + t.shape[3:])

    q, k, v, g, beta = chunks(q), chunks(k), chunks(v), chunks(g), chunks(beta)
    gam = jnp.cumsum(g, axis=-1)
    idx = jnp.arange(CHUNK)
    incl = idx[:, None] >= idx[None, :]
    strict = idx[:, None] > idx[None, :]
    diff = gam[..., :, None] - gam[..., None, :]
    decay = jnp.where(incl, jnp.exp(jnp.where(incl, diff, 0.0)), 0.0)
    kb = k * beta[..., None]
    m = jnp.where(strict, jnp.einsum('bhnid,bhnjd->bhnij', kb, k) * decay, 0.0)
    a = m + jnp.eye(CHUNK, dtype=jnp.float32)
    u = lax.linalg.triangular_solve(a, v * beta[..., None], left_side=True, lower=True, unit_diagonal=True)
    w = lax.linalg.triangular_solve(a, kb * jnp.exp(gam)[..., None], left_side=True, lower=True,
                                    unit_diagonal=True)
    qk = jnp.einsum('bhnid,bhnjd->bhnij', q, k) * decay
    qd = q * jnp.exp(gam)[..., None]
    kd = k * jnp.exp(gam[..., -1:] - gam)[..., None]
    cd = jnp.exp(gam[..., -1])
    xs = tuple(jnp.moveaxis(t, 2, 0) for t in (u, w, qk, qd, kd, cd))

    def step(s, inp):
        u_c, w_c, qk_c, qd_c, kd_c, cd_c = inp
        v_new = u_c - jnp.einsum('bhik,bhkv->bhiv', w_c, s)
        o_c = jnp.einsum('bhik,bhkv->bhiv', qd_c, s) + jnp.einsum('bhij,bhjv->bhiv', qk_c, v_new)
        s = s * cd_c[..., None, None] + jnp.einsum('bhik,bhiv->bhkv', kd_c, v_new)
        return s, o_c

    s_fin, o = lax.scan(step, s0, xs)
    o = jnp.moveaxis(o, 0, 2).reshape(B, H, L, DV)
    return o, s_fin


def gdn_project(h, w_in, conv_w, a_log, dt_bias):
    B, L, _ = h.shape
    p = h @ w_in
    qkv = jax.nn.silu(centred_dwconv(p[..., :3 * GDN_DIM], conv_w))

    def heads(t):
        return t.reshape(B, L, GDN_HEADS, GDN_HEAD_DIM).transpose(0, 2, 1, 3)

    q = l2norm(heads(qkv[..., :GDN_DIM])) * (GDN_HEAD_DIM ** -0.5)
    k = l2norm(heads(qkv[..., GDN_DIM:2 * GDN_DIM]))
    v = heads(qkv[..., 2 * GDN_DIM:])
    z = p[..., 3 * GDN_DIM:4 * GDN_DIM].reshape(B, L, GDN_HEADS, GDN_HEAD_DIM)
    ab = p[..., 4 * GDN_DIM:].astype(jnp.float32).reshape(B, L, 2, 2, GDN_HEADS)
    g = -jnp.exp(a_log.astype(jnp.float32)) * jax.nn.softplus(ab[:, :, :, 0] + dt_bias.astype(jnp.float32))
    beta = jax.nn.sigmoid(ab[:, :, :, 1])
    g = jnp.transpose(g, (2, 0, 3, 1))
    beta = jnp.transpose(beta, (2, 0, 3, 1))
    return q, k, v, z, g, beta


def gdn_output(o, z, out_norm, w_out):
    B, H, L, DV = o.shape
    o = jnp.transpose(o, (0, 2, 1, 3))
    y = rmsnorm(o, out_norm.astype(jnp.float32)) * jax.nn.silu(z.astype(jnp.float32))
    return y.astype(z.dtype).reshape(B, L, H * DV) @ w_out


def gdn_mixer(h_lat, h_ctx, w_in, conv_w, a_log, dt_bias, out_norm, w_out, ctx_out):
    ql, kl, vl, zl, gl, bl = gdn_project(h_lat, w_in, conv_w, a_log, dt_bias)
    qc, kc, vc, zc, gc, bc = gdn_project(h_ctx, w_in, conv_w, a_log, dt_bias)
    B = h_lat.shape[0]
    s0 = jnp.zeros((B, GDN_HEADS, GDN_HEAD_DIM, GDN_HEAD_DIM), jnp.float32)
    outs_l, outs_c = [], []
    for d in range(2):
        rev = (lambda t: jnp.flip(t, axis=2)) if d == 1 else (lambda t: t)
        oc, sc = gated_delta_chunked(rev(qc), rev(kc), rev(vc), rev(gc[d]), rev(bc[d]), s0)
        ol, _ = gated_delta_chunked(rev(ql), rev(kl), rev(vl), rev(gl[d]), rev(bl[d]), sc)
        outs_l.append(rev(ol))
        outs_c.append(rev(oc))
    y_lat = gdn_output(outs_l[0] + outs_l[1], zl, out_norm, w_out)
    if not ctx_out:
        return y_lat, None
    y_ctx = gdn_output(outs_c[0] + outs_c[1], zc, out_norm, w_out)
    return y_lat, y_ctx


def moe_swiglu(h, router, w1, w3, w2):
    shp = h.shape
    t = h.reshape(-1, shp[-1])
    logits = jnp.matmul(t, router, preferred_element_type=jnp.float32)
    top_v, top_i = lax.top_k(logits, TOP_K)
    gates = jax.nn.softmax(top_v, axis=-1)
    combine = jnp.sum(jax.nn.one_hot(top_i, N_EXPERTS, dtype=jnp.float32) * gates[..., None], axis=1)
    y = jnp.zeros(t.shape, jnp.float32)
    for e in range(N_EXPERTS):
        y = y + combine[:, e:e + 1] * swiglu(t, w1[e], w3[e], w2[e]).astype(jnp.float32)
    return y.astype(h.dtype).reshape(shp)


def run_layer(layer, x_lat, x_ctx, c, c_ctx, p, last):
    mod = jax.nn.silu(c) @ p['mod_w'] + p['mod_b']
    mod_c = jax.nn.silu(c_ctx) @ p['mod_w'] + p['mod_b']
    sh_m, sc_m, g_m, sh_f, sc_f, g_f = [t[:, None, :] for t in jnp.split(mod, 6, axis=-1)]
    csh_m, csc_m, cg_m, csh_f, csc_f, cg_f = jnp.split(mod_c, 6)
    h_lat = rmsnorm(x_lat, p['mix_pre']) * (1.0 + sc_m) + sh_m
    h_ctx = rmsnorm(x_ctx, p['mix_pre']) * (1.0 + csc_m) + csh_m
    if layer % 2 == 0:
        y_lat, y_ctx = pool_swa_mixer(h_lat, h_ctx, p['w_in'], p['pool_w'], p['pool_scale'], p['sinks'],
                                      p['w_out'], not last)

        def channel_mixer(h):
            return swiglu(h, p['ffn_w1'], p['ffn_w3'], p['ffn_w2'])
    else:
        y_lat, y_ctx = gdn_mixer(h_lat, h_ctx, p['w_in'], p['conv_w'], p['a_log'], p['dt_bias'],
                                 p['out_norm'], p['w_out'], not last)

        def channel_mixer(h):
            return moe_swiglu(h, p['router'], p['moe_w1'], p['moe_w3'], p['moe_w2'])
    x_lat = x_lat + g_m * rmsnorm(y_lat, p['mix_post'])
    h = rmsnorm(x_lat, p['ffn_pre']) * (1.0 + sc_f) + sh_f
    x_lat = x_lat + g_f * rmsnorm(channel_mixer(h), p['ffn_post'])
    if not last:
        x_ctx = x_ctx + cg_m * rmsnorm(y_ctx, p['mix_post'])
        hc = rmsnorm(x_ctx, p['ffn_pre']) * (1.0 + csc_f) + csh_f
        x_ctx = x_ctx + cg_f * rmsnorm(channel_mixer(hc), p['ffn_post'])
    return x_lat, x_ctx


def setup_inputs(seed: int = 0) -> dict:
    key = jax.random.key(seed)
    ks = jax.random.split(key, 40)
    f32 = jnp.float32
    D = D_MODEL

    def nrm(i, shape, scale):
        return jax.random.normal(ks[i], shape, f32) * scale

    def gain(i, n):
        return 1.0 + 0.05 * jax.random.normal(ks[i], (n,), f32)

    dt = jnp.exp(jax.random.uniform(ks[27], (2, GDN_HEADS), f32, minval=math.log(1e-3), maxval=math.log(1e-1)))
    dt_bias = dt + jnp.log(-jnp.expm1(-dt))
    return {
        'x': nrm(0, (BATCH, SEQ, D), 1.0),
        'c': nrm(1, (BATCH, D), 1.0),
        'ctx': nrm(2, (BATCH, CTX_LEN, D), 1.0),
        'c_ctx': nrm(3, (D,), 1.0),
        'l0_mod_w': nrm(4, (D, 6 * D), 0.5 * D ** -0.5),
        'l0_mod_b': nrm(5, (6 * D,), 0.02),
        'l0_mix_pre': gain(6, D),
        'l0_mix_post': gain(7, D),
        'l0_ffn_pre': gain(8, D),
        'l0_ffn_post': gain(9, D),
        'l0_w_in': nrm(10, (D, AB_IN_DIM), D ** -0.5),
        'l0_pool_w': nrm(11, (POOL_GROUPS, POOL_GROUP_DIM, POOL_GROUP_DIM), POOL_GROUP_DIM ** -0.5),
        'l0_pool_scale': gain(12, POOL_DIM),
        'l0_sinks': nrm(13, (N_HEADS,), 0.5),
        'l0_w_out': nrm(14, (AB_OUT_DIM, D), AB_OUT_DIM ** -0.5),
        'l0_ffn_w1': nrm(15, (D, D_FF), D ** -0.5),
        'l0_ffn_w3': nrm(16, (D, D_FF), D ** -0.5),
        'l0_ffn_w2': nrm(17, (D_FF, D), D_FF ** -0.5),
        'l1_mod_w': nrm(18, (D, 6 * D), 0.5 * D ** -0.5),
        'l1_mod_b': nrm(19, (6 * D,), 0.02),
        'l1_mix_pre': gain(20, D),
        'l1_mix_post': gain(21, D),
        'l1_ffn_pre': gain(22, D),
        'l1_ffn_post': gain(23, D),
        'l1_w_in': nrm(24, (D, GDN_IN_DIM), D ** -0.5),
        'l1_conv_w': nrm(25, (CONV_K, 3 * GDN_DIM), CONV_K ** -0.5),
        'l1_a_log': jnp.log(jax.random.uniform(ks[26], (2, GDN_HEADS), f32, minval=1.0, maxval=16.0)),
        'l1_dt_bias': dt_bias,
        'l1_out_norm': gain(28, GDN_HEAD_DIM),
        'l1_w_out': nrm(29, (GDN_DIM, D), GDN_DIM ** -0.5),
        'l1_router': nrm(30, (D, N_EXPERTS), D ** -0.5),
        'l1_moe_w1': nrm(31, (N_EXPERTS, D, D_EXPERT), D ** -0.5),
        'l1_moe_w3': nrm(32, (N_EXPERTS, D, D_EXPERT), D ** -0.5),
        'l1_moe_w2': nrm(33, (N_EXPERTS, D_EXPERT, D), D_EXPERT ** -0.5),
    }


def reference(x, c, ctx, c_ctx,
              l0_mod_w, l0_mod_b, l0_mix_pre, l0_mix_post, l0_ffn_pre, l0_ffn_post,
              l0_w_in, l0_pool_w, l0_pool_scale, l0_sinks, l0_w_out, l0_ffn_w1, l0_ffn_w3, l0_ffn_w2,
              l1_mod_w, l1_mod_b, l1_mix_pre, l1_mix_post, l1_ffn_pre, l1_ffn_post,
              l1_w_in, l1_conv_w, l1_a_log, l1_dt_bias, l1_out_norm, l1_w_out,
              l1_router, l1_moe_w1, l1_moe_w3, l1_moe_w2):
    layers = [
        dict(mod_w=l0_mod_w, mod_b=l0_mod_b, mix_pre=l0_mix_pre, mix_post=l0_mix_post,
             ffn_pre=l0_ffn_pre, ffn_post=l0_ffn_post, w_in=l0_w_in, pool_w=l0_pool_w,
             pool_scale=l0_pool_scale, sinks=l0_sinks, w_out=l0_w_out,
             ffn_w1=l0_ffn_w1, ffn_w3=l0_ffn_w3, ffn_w2=l0_ffn_w2),
        dict(mod_w=l1_mod_w, mod_b=l1_mod_b, mix_pre=l1_mix_pre, mix_post=l1_mix_post,
             ffn_pre=l1_ffn_pre, ffn_post=l1_ffn_post, w_in=l1_w_in, conv_w=l1_conv_w,
             a_log=l1_a_log, dt_bias=l1_dt_bias, out_norm=l1_out_norm, w_out=l1_w_out,
             router=l1_router, moe_w1=l1_moe_w1, moe_w3=l1_moe_w3, moe_w2=l1_moe_w2),
    ]
    x_lat, x_ctx = x, ctx
    for layer in range(DEPTH):
        x_lat, x_ctx = run_layer(layer, x_lat, x_ctx, c, c_ctx, layers[layer], layer == DEPTH - 1)
    return x_lat
```

```python
import functools
import math

import jax
import jax.numpy as jnp
from jax import lax
from jax.experimental import pallas as pl
from jax.experimental.pallas import tpu as pltpu

F32 = jnp.float32
BF16 = jnp.bfloat16
I32 = jnp.int32

LANES = 128
D_MODEL = 1024
GRID_W = 64
EPS = 1e-6
NEG_INF = -1e30

POOL_GROUPS = 4
POOL_GROUP_DIM = 128
POOL_DIM = POOL_GROUPS * POOL_GROUP_DIM
POOL_WINDOWS = (2, 4, 8, 16)
HEAD_DIM = 64
N_HEADS = 8
N_KV_HEADS = 2
GQA_GROUP = N_HEADS // N_KV_HEADS
Q_DIM = N_HEADS * HEAD_DIM
KV_DIM = N_KV_HEADS * HEAD_DIM
WINDOW = 128
BLOCK = 128
ROPE_THETA = 10000.0

GDN_HEADS = 8
GDN_HEAD_DIM = 128
GDN_DIM = GDN_HEADS * GDN_HEAD_DIM
CONV_K = 5
CHUNK = 64

N_EXPERTS = 8
TOP_K = 2

ROUTE_TILE = 512
MOE_TM = 512
MOE_TF = 512
ROW_TILE = 256


def _split_bf16(a):
    hi = a.astype(BF16)
    lo = (a - hi.astype(F32)).astype(BF16)
    return hi, lo


def _route_body(h_ref, rhi_ref, rlo_ref, route_ref, cnt_ref, carry_ref):
    i = pl.program_id(0)

    @pl.when(i == 0)
    def _():
        carry_ref[...] = jnp.zeros_like(carry_ref)

    h_hi, h_lo = _split_bf16(h_ref[...])
    r_hi = rhi_ref[...]
    r_lo = rlo_ref[...]
    logits = (jnp.dot(h_hi, r_hi, preferred_element_type=F32)
              + jnp.dot(h_hi, r_lo, preferred_element_type=F32)
              + jnp.dot(h_lo, r_hi, preferred_element_type=F32))
    tr = logits.shape[0]
    lane = lax.broadcasted_iota(I32, (tr, LANES), 1)
    logits = jnp.where(lane < N_EXPERTS, logits, -jnp.inf)
    m1 = jnp.max(logits, axis=-1, keepdims=True)
    i1 = jnp.min(jnp.where(logits == m1, lane, LANES), axis=-1, keepdims=True)
    rest = jnp.where(lane == i1, -jnp.inf, logits)
    m2 = jnp.max(rest, axis=-1, keepdims=True)
    i2 = jnp.min(jnp.where(rest == m2, lane, LANES), axis=-1, keepdims=True)
    e2 = jnp.exp(m2 - m1)
    g1 = 1.0 / (1.0 + e2)
    g2 = e2 / (1.0 + e2)

    onehot = ((lane == i1) | (lane == i2)).astype(F32)
    row = lax.broadcasted_iota(I32, (tr, tr), 0)
    col = lax.broadcasted_iota(I32, (tr, tr), 1)
    strict = (row > col).astype(BF16)
    before = jnp.dot(strict, onehot.astype(BF16), preferred_element_type=F32) + carry_ref[...]
    rank1 = jnp.sum(jnp.where(lane == i1, before, 0.0), axis=-1, keepdims=True)
    rank2 = jnp.sum(jnp.where(lane == i2, before, 0.0), axis=-1, keepdims=True)
    carry_ref[...] += jnp.sum(onehot, axis=0, keepdims=True)

    packed = jnp.where(lane == 0, i1.astype(F32), 0.0)
    packed = jnp.where(lane == 1, i2.astype(F32), packed)
    packed = jnp.where(lane == 2, g1, packed)
    packed = jnp.where(lane == 3, g2, packed)
    packed = jnp.where(lane == 4, rank1, packed)
    packed = jnp.where(lane == 5, rank2, packed)
    route_ref[...] = packed
    cnt_ref[...] = jnp.broadcast_to(carry_ref[...], cnt_ref.shape)


def moe_route(h, router):
    t, d = h.shape
    tr = min(ROUTE_TILE, t)
    r_pad = jnp.zeros((d, LANES), F32).at[:, :N_EXPERTS].set(router.astype(F32))
    r_hi, r_lo = _split_bf16(r_pad)
    route, cnt = pl.pallas_call(
        _route_body,
        grid=(t // tr,),
        in_specs=[
            pl.BlockSpec((tr, d), lambda i: (i, 0)),
            pl.BlockSpec((d, LANES), lambda i: (0, 0)),
            pl.BlockSpec((d, LANES), lambda i: (0, 0)),
        ],
        out_specs=[
            pl.BlockSpec((tr, LANES), lambda i: (i, 0)),
            pl.BlockSpec((8, LANES), lambda i: (0, 0)),
        ],
        out_shape=[
            jax.ShapeDtypeStruct((t, LANES), F32),
            jax.ShapeDtypeStruct((8, LANES), F32),
        ],
        scratch_shapes=[pltpu.VMEM((1, LANES), F32)],
        compiler_params=pltpu.CompilerParams(dimension_semantics=("arbitrary",)),
        name="moe_route",
    )(h, r_hi, r_lo)
    return route, cnt


def _row_copy(src_ref, src_row, dst_ref, dst_row, sem):
    return pltpu.make_async_copy(src_ref.at[pl.ds(src_row, 1)], dst_ref.at[pl.ds(dst_row, 1)], sem)


def _scatter_body(pos_ref, h_ref, xs_in_ref, xs_ref, sem, *, ts):
    del xs_in_ref
    i = pl.program_id(0)

    def issue(r, c):
        t = i * ts + r
        for k in range(TOP_K):
            _row_copy(h_ref, t, xs_ref, pos_ref[0, 0, TOP_K * r + k], sem).start()
        return c

    lax.fori_loop(0, ts, issue, 0)

    def drain(r, c):
        _row_copy(h_ref, 0, xs_ref, 0, sem).wait()
        return c

    lax.fori_loop(0, TOP_K * ts, drain, 0)


def moe_scatter(h, pos, p_rows):
    t, d = h.shape
    ts = min(ROW_TILE, t)
    nt = t // ts
    xs0 = jnp.zeros((p_rows, d), h.dtype)
    return pl.pallas_call(
        functools.partial(_scatter_body, ts=ts),
        grid=(nt,),
        in_specs=[
            pl.BlockSpec((1, 1, TOP_K * ts), lambda i: (i, 0, 0), memory_space=pltpu.SMEM),
            pl.BlockSpec(memory_space=pl.ANY),
            pl.BlockSpec(memory_space=pl.ANY),
        ],
        out_specs=pl.BlockSpec(memory_space=pl.ANY),
        out_shape=jax.ShapeDtypeStruct((p_rows, d), h.dtype),
        scratch_shapes=[pltpu.SemaphoreType.DMA(())],
        input_output_aliases={2: 0},
        compiler_params=pltpu.CompilerParams(dimension_semantics=("arbitrary",)),
        name="moe_scatter",
    )(pos.reshape(nt, 1, TOP_K * ts), h, xs0)


def _expert_body(te_ref, nu_ref, x_ref, w1_ref, w3_ref, w2_ref, o_ref):
    i = pl.program_id(0)
    j = pl.program_id(1)

    @pl.when(j == 0)
    def _():
        o_ref[...] = jnp.zeros_like(o_ref)

    @pl.when(i < nu_ref[0])
    def _():
        x = x_ref[...].astype(BF16)
        a = jnp.dot(x, w1_ref[0], preferred_element_type=F32)
        b = jnp.dot(x, w3_ref[0], preferred_element_type=F32)
        mid = (a * jax.nn.sigmoid(a) * b).astype(BF16)
        o_ref[...] += jnp.dot(mid, w2_ref[0], preferred_element_type=F32)


def moe_experts(xs, w1, w3, w2, tile_expert, n_used, tm):
    p_rows, d = xs.shape
    n_exp, _, d_exp = w1.shape
    tf = MOE_TF if d_exp % MOE_TF == 0 else d_exp
    nj = d_exp // tf
    n_tiles = p_rows // tm

    def jj(i, j, nu):
        return jnp.where(i < nu[0], j, nj - 1)

    grid_spec = pltpu.PrefetchScalarGridSpec(
        num_scalar_prefetch=2,
        grid=(n_tiles, nj),
        in_specs=[
            pl.BlockSpec((tm, d), lambda i, j, te, nu: (i, 0)),
            pl.BlockSpec((1, d, tf), lambda i, j, te, nu: (te[i], 0, jj(i, j, nu))),
            pl.BlockSpec((1, d, tf), lambda i, j, te, nu: (te[i], 0, jj(i, j, nu))),
            pl.BlockSpec((1, tf, d), lambda i, j, te, nu: (te[i], jj(i, j, nu), 0)),
        ],
        out_specs=pl.BlockSpec((tm, d), lambda i, j, te, nu: (i, 0)),
    )
    return pl.pallas_call(
        _expert_body,
        grid_spec=grid_spec,
        out_shape=jax.ShapeDtypeStruct((p_rows, d), F32),
        compiler_params=pltpu.CompilerParams(
            dimension_semantics=("arbitrary", "arbitrary"),
            vmem_limit_bytes=48 * 1024 * 1024),
        name="moe_experts",
    )(tile_expert, n_used, xs, w1, w3, w2)


def _combine_body(pos_ref, gate_ref, ys_ref, y_ref, buf, sem, *, ts):
    def issue(r, c):
        for k in range(TOP_K):
            _row_copy(ys_ref, pos_ref[0, 0, TOP_K * r + k], buf.at[k], r, sem).start()
        return c

    lax.fori_loop(0, ts, issue, 0)

    def drain(r, c):
        _row_copy(ys_ref, 0, buf.at[0], 0, sem).wait()
        return c

    lax.fori_loop(0, TOP_K * ts, drain, 0)
    g = gate_ref[...]
    y_ref[...] = g[:, 2:3] * buf[0] + g[:, 3:4] * buf[1]


def moe_combine(ys, pos, route):
    t = route.shape[0]
    d = ys.shape[1]
    ts = min(ROW_TILE, t)
    nt = t // ts
    return pl.pallas_call(
        functools.partial(_combine_body, ts=ts),
        grid=(nt,),
        in_specs=[
            pl.BlockSpec((1, 1, TOP_K * ts), lambda i: (i, 0, 0), memory_space=pltpu.SMEM),
            pl.BlockSpec((ts, LANES), lambda i: (i, 0)),
            pl.BlockSpec(memory_space=pl.ANY),
        ],
        out_specs=pl.BlockSpec((ts, d), lambda i: (i, 0)),
        out_shape=jax.ShapeDtypeStruct((t, d), F32),
        scratch_shapes=[pltpu.VMEM((TOP_K, ts, d), F32), pltpu.SemaphoreType.DMA(())],
        compiler_params=pltpu.CompilerParams(dimension_semantics=("arbitrary",)),
        name="moe_combine",
    )(pos.reshape(nt, 1, TOP_K * ts), route, ys)


def moe_swiglu(h, router, w1, w3, w2):
    shp = h.shape
    tok = h.reshape(-1, shp[-1])
    t = tok.shape[0]
    tm = min(MOE_TM, t)
    route, cnt = moe_route(tok, router)

    counts = cnt[0, :N_EXPERTS].astype(I32)
    padded = ((counts + tm - 1) // tm) * tm
    ends = jnp.cumsum(padded)
    starts = ends - padded
    expert = route[:, 0:TOP_K].astype(I32)
    rank = route[:, 4:4 + TOP_K].astype(I32)
    start_of = jnp.sum(jnp.where(expert[..., None] == jnp.arange(N_EXPERTS), starts, 0), axis=-1)
    pos = start_of + rank

    n_tiles = (TOP_K * t) // tm + N_EXPERTS
    n_used = (ends[-1] // tm).astype(I32)
    tile_idx = jnp.minimum(jnp.arange(n_tiles, dtype=I32), n_used - 1)
    tile_expert = jnp.sum(tile_idx[:, None] * tm >= ends[None, :], axis=-1).astype(I32)
    tile_expert = jnp.minimum(tile_expert, N_EXPERTS - 1)

    xs = moe_scatter(tok, pos, n_tiles * tm)
    ys = moe_experts(xs, w1.astype(BF16), w3.astype(BF16), w2.astype(BF16), tile_expert, n_used.reshape(1), tm)
    y = moe_combine(ys, pos, route)
    return y.reshape(shp)


def rmsnorm(x, g):
    xf = x.astype(F32)
    y = xf * lax.rsqrt(jnp.mean(xf * xf, axis=-1, keepdims=True) + EPS)
    return y.astype(x.dtype) * g


def l2norm(x):
    xf = x.astype(F32)
    return (xf * lax.rsqrt(jnp.sum(xf * xf, axis=-1, keepdims=True) + EPS)).astype(x.dtype)


def swiglu(h, w1, w3, w2):
    return (jax.nn.silu(h @ w1) * (h @ w3)) @ w2


def multiscale_pool(u, pool_w, pool_scale):
    B, L, _ = u.shape
    cs = jnp.cumsum(u.astype(F32), axis=1)
    cs = jnp.concatenate([jnp.zeros_like(cs[:, :1]), cs], axis=1)
    t = jnp.arange(L)
    means = []
    for gi, w in enumerate(POOL_WINDOWS):
        lo = jnp.clip(t - w // 2, 0, L)
        hi = jnp.clip(t + w // 2, 0, L)
        seg = cs[:, :, gi * POOL_GROUP_DIM:(gi + 1) * POOL_GROUP_DIM]
        means.append((seg[:, hi] - seg[:, lo]) / (hi - lo).astype(F32)[None, :, None])
    pooled = jnp.stack(means, axis=2)
    delta = (pooled - u.reshape(B, L, POOL_GROUPS, POOL_GROUP_DIM).astype(F32)).astype(u.dtype)
    y = jnp.einsum('blgc,gcd->blgd', delta, pool_w) * pool_scale.reshape(POOL_GROUPS, POOL_GROUP_DIM)
    return y.reshape(B, L, POOL_DIM)


def axial_rope(x, rows, cols):
    half = HEAD_DIM // 2
    inv = ROPE_THETA ** (-jnp.arange(0, half, 2, dtype=F32) / half)

    def rot(xa, pos):
        ang = pos[:, None] * inv[None, :]
        cos = jnp.cos(ang)[None, :, None, :].astype(x.dtype)
        sin = jnp.sin(ang)[None, :, None, :].astype(x.dtype)
        x1, x2 = xa[..., :half // 2], xa[..., half // 2:]
        return jnp.concatenate([x1 * cos - x2 * sin, x2 * cos + x1 * sin], axis=-1)

    return jnp.concatenate([rot(x[..., :half], rows), rot(x[..., half:], cols)], axis=-1)


def split_ab(p):
    B, L, _ = p.shape
    u = p[..., :POOL_DIM]
    q = p[..., POOL_DIM:POOL_DIM + Q_DIM].reshape(B, L, N_HEADS, HEAD_DIM)
    k = p[..., POOL_DIM + Q_DIM:POOL_DIM + Q_DIM + KV_DIM].reshape(B, L, N_KV_HEADS, HEAD_DIM)
    v = p[..., POOL_DIM + Q_DIM + KV_DIM:].reshape(B, L, N_KV_HEADS, HEAD_DIM)
    return u, q, k, v


def banded_attention(q, k, v, k_ctx, v_ctx, sinks):
    B, L, _, _ = q.shape
    Lc = k_ctx.shape[1]
    nb = L // BLOCK
    scale = HEAD_DIM ** -0.5
    qb = q.reshape(B, nb, BLOCK, N_KV_HEADS, GQA_GROUP, HEAD_DIM)
    pad = ((0, 0), (BLOCK, BLOCK), (0, 0), (0, 0))
    kp = jnp.pad(k, pad).reshape(B, nb + 2, BLOCK, N_KV_HEADS, HEAD_DIM)
    vp = jnp.pad(v, pad).reshape(B, nb + 2, BLOCK, N_KV_HEADS, HEAD_DIM)
    k_band = jnp.concatenate([kp[:, :-2], kp[:, 1:-1], kp[:, 2:]], axis=2)
    v_band = jnp.concatenate([vp[:, :-2], vp[:, 1:-1], vp[:, 2:]], axis=2)
    s_band = jnp.einsum('bnqhgd,bnkhd->bnhgqk', qb, k_band, preferred_element_type=F32) * scale
    qpos = jnp.arange(nb)[:, None] * BLOCK + jnp.arange(BLOCK)[None, :]
    kpos = (jnp.arange(nb)[:, None] - 1) * BLOCK + jnp.arange(3 * BLOCK)[None, :]
    allowed = ((jnp.abs(kpos[:, None, :] - qpos[:, :, None]) <= WINDOW)
               & (kpos[:, None, :] >= 0) & (kpos[:, None, :] < L))
    s_band = jnp.where(allowed[None, :, None, None], s_band, NEG_INF)
    s_ctx = jnp.einsum('bnqhgd,bkhd->bnhgqk', qb, k_ctx, preferred_element_type=F32) * scale
    sink = sinks.astype(F32).reshape(N_KV_HEADS, GQA_GROUP)
    s_sink = jnp.broadcast_to(sink[None, None, :, :, None, None], s_ctx.shape[:-1] + (1,))
    p = jax.nn.softmax(jnp.concatenate([s_sink, s_ctx, s_band], axis=-1), axis=-1)
    p_ctx = p[..., 1:1 + Lc].astype(v.dtype)
    p_band = p[..., 1 + Lc:].astype(v.dtype)
    o = (jnp.einsum('bnhgqk,bkhd->bnqhgd', p_ctx, v_ctx)
         + jnp.einsum('bnhgqk,bnkhd->bnqhgd', p_band, v_band))
    return o.reshape(B, L, Q_DIM)


def context_attention(q, k, v, sinks):
    B, Lc, _, _ = q.shape
    qc = q.reshape(B, Lc, N_KV_HEADS, GQA_GROUP, HEAD_DIM)
    s = jnp.einsum('bqhgd,bkhd->bhgqk', qc, k, preferred_element_type=F32) * HEAD_DIM ** -0.5
    sink = sinks.astype(F32).reshape(N_KV_HEADS, GQA_GROUP)
    s_sink = jnp.broadcast_to(sink[None, :, :, None, None], s.shape[:-1] + (1,))
    p = jax.nn.softmax(jnp.concatenate([s_sink, s], axis=-1), axis=-1)[..., 1:].astype(v.dtype)
    o = jnp.einsum('bhgqk,bkhd->bqhgd', p, v)
    return o.reshape(B, Lc, Q_DIM)


def pool_swa_mixer(h_lat, h_ctx, w_in, pool_w, pool_scale, sinks, w_out, ctx_out):
    L = h_lat.shape[1]
    rows_n = L // GRID_W
    rows = jnp.repeat(jnp.arange(rows_n, dtype=F32), GRID_W)
    cols = jnp.tile(jnp.arange(GRID_W, dtype=F32), rows_n)
    u_l, q_l, k_l, v_l = split_ab(h_lat @ w_in)
    u_c, q_c, k_c, v_c = split_ab(h_ctx @ w_in)
    q_l = axial_rope(q_l, rows, cols)
    k_l = axial_rope(k_l, rows, cols)
    a_l = banded_attention(q_l, k_l, v_l, k_c, v_c, sinks)
    y_lat = jnp.concatenate([multiscale_pool(u_l, pool_w, pool_scale), a_l], axis=-1) @ w_out
    if not ctx_out:
        return y_lat, None
    a_c = context_attention(q_c, k_c, v_c, sinks)
    y_ctx = jnp.concatenate([multiscale_pool(u_c, pool_w, pool_scale), a_c], axis=-1) @ w_out
    return y_lat, y_ctx


def centred_dwconv(x, w):
    return lax.conv_general_dilated(x, w[:, None, :], window_strides=(1,),
                                    padding=[(CONV_K // 2, CONV_K // 2)],
                                    dimension_numbers=('NWC', 'WIO', 'NWC'),
                                    feature_group_count=x.shape[-1])


def gated_delta_chunked(q, k, v, g, beta, s0):
    B, H, L, _ = q.shape
    DV = v.shape[-1]
    n = L // CHUNK

    def chunks(t):
        return t.astype(F32).reshape((B, H, n, CHUNK) + t.shape[3:])

    q, k, v, g, beta = chunks(q), chunks(k), chunks(v), chunks(g), chunks(beta)
    gam = jnp.cumsum(g, axis=-1)
    idx = jnp.arange(CHUNK)
    incl = idx[:, None] >= idx[None, :]
    strict = idx[:, None] > idx[None, :]
    diff = gam[..., :, None] - gam[..., None, :]
    decay = jnp.where(incl, jnp.exp(jnp.where(incl, diff, 0.0)), 0.0)
    kb = k * beta[..., None]
    m = jnp.where(strict, jnp.einsum('bhnid,bhnjd->bhnij', kb, k) * decay, 0.0)
    a = m + jnp.eye(CHUNK, dtype=F32)
    u = lax.linalg.triangular_solve(a, v * beta[..., None], left_side=True, lower=True, unit_diagonal=True)
    w = lax.linalg.triangular_solve(a, kb * jnp.exp(gam)[..., None], left_side=True, lower=True,
                                    unit_diagonal=True)
    qk = jnp.einsum('bhnid,bhnjd->bhnij', q, k) * decay
    qd = q * jnp.exp(gam)[..., None]
    kd = k * jnp.exp(gam[..., -1:] - gam)[..., None]
    cd = jnp.exp(gam[..., -1])
    xs = tuple(jnp.moveaxis(t, 2, 0) for t in (u, w, qk, qd, kd, cd))

    def step(s, inp):
        u_c, w_c, qk_c, qd_c, kd_c, cd_c = inp
        v_new = u_c - jnp.einsum('bhik,bhkv->bhiv', w_c, s)
        o_c = jnp.einsum('bhik,bhkv->bhiv', qd_c, s) + jnp.einsum('bhij,bhjv->bhiv', qk_c, v_new)
        s = s * cd_c[..., None, None] + jnp.einsum('bhik,bhiv->bhkv', kd_c, v_new)
        return s, o_c

    s_fin, o = lax.scan(step, s0, xs)
    o = jnp.moveaxis(o, 0, 2).reshape(B, H, L, DV)
    return o, s_fin


def gdn_project(h, w_in, conv_w, a_log, dt_bias):
    B, L, _ = h.shape
    p = h @ w_in
    qkv = jax.nn.silu(centred_dwconv(p[..., :3 * GDN_DIM], conv_w))

    def heads(t):
        return t.reshape(B, L, GDN_HEADS, GDN_HEAD_DIM).transpose(0, 2, 1, 3)

    q = l2norm(heads(qkv[..., :GDN_DIM])) * (GDN_HEAD_DIM ** -0.5)
    k = l2norm(heads(qkv[..., GDN_DIM:2 * GDN_DIM]))
    v = heads(qkv[..., 2 * GDN_DIM:])
    z = p[..., 3 * GDN_DIM:4 * GDN_DIM].reshape(B, L, GDN_HEADS, GDN_HEAD_DIM)
    ab = p[..., 4 * GDN_DIM:].astype(F32).reshape(B, L, 2, 2, GDN_HEADS)
    g = -jnp.exp(a_log.astype(F32)) * jax.nn.softplus(ab[:, :, :, 0] + dt_bias.astype(F32))
    beta = jax.nn.sigmoid(ab[:, :, :, 1])
    g = jnp.transpose(g, (2, 0, 3, 1))
    beta = jnp.transpose(beta, (2, 0, 3, 1))
    return q, k, v, z, g, beta


def gdn_output(o, z, out_norm, w_out):
    B, H, L, DV = o.shape
    o = jnp.transpose(o, (0, 2, 1, 3))
    y = rmsnorm(o, out_norm.astype(F32)) * jax.nn.silu(z.astype(F32))
    return y.astype(z.dtype).reshape(B, L, H * DV) @ w_out


def gdn_mixer(h_lat, h_ctx, w_in, conv_w, a_log, dt_bias, out_norm, w_out, ctx_out):
    ql, kl, vl, zl, gl, bl = gdn_project(h_lat, w_in, conv_w, a_log, dt_bias)
    qc, kc, vc, zc, gc, bc = gdn_project(h_ctx, w_in, conv_w, a_log, dt_bias)
    B = h_lat.shape[0]
    s0 = jnp.zeros((B, GDN_HEADS, GDN_HEAD_DIM, GDN_HEAD_DIM), F32)
    outs_l, outs_c = [], []
    for d in range(2):
        rev = (lambda t: jnp.flip(t, axis=2)) if d == 1 else (lambda t: t)
        oc, sc = gated_delta_chunked(rev(qc), rev(kc), rev(vc), rev(gc[d]), rev(bc[d]), s0)
        ol, _ = gated_delta_chunked(rev(ql), rev(kl), rev(vl), rev(gl[d]), rev(bl[d]), sc)
        outs_l.append(rev(ol))
        outs_c.append(rev(oc))
    y_lat = gdn_output(outs_l[0] + outs_l[1], zl, out_norm, w_out)
    if not ctx_out:
        return y_lat, None
    y_ctx = gdn_output(outs_c[0] + outs_c[1], zc, out_norm, w_out)
    return y_lat, y_ctx


def run_layer(layer, x_lat, x_ctx, c, c_ctx, p, last):
    mod = jax.nn.silu(c) @ p['mod_w'] + p['mod_b']
    mod_c = jax.nn.silu(c_ctx) @ p['mod_w'] + p['mod_b']
    sh_m, sc_m, g_m, sh_f, sc_f, g_f = [t[:, None, :] for t in jnp.split(mod, 6, axis=-1)]
    csh_m, csc_m, cg_m, csh_f, csc_f, cg_f = jnp.split(mod_c, 6)
    h_lat = rmsnorm(x_lat, p['mix_pre']) * (1.0 + sc_m) + sh_m
    h_ctx = rmsnorm(x_ctx, p['mix_pre']) * (1.0 + csc_m) + csh_m
    if layer % 2 == 0:
        y_lat, y_ctx = pool_swa_mixer(h_lat, h_ctx, p['w_in'], p['pool_w'], p['pool_scale'], p['sinks'],
                                      p['w_out'], not last)

        def channel_mixer(h):
            return swiglu(h, p['ffn_w1'], p['ffn_w3'], p['ffn_w2'])
    else:
        y_lat, y_ctx = gdn_mixer(h_lat, h_ctx, p['w_in'], p['conv_w'], p['a_log'], p['dt_bias'],
                                 p['out_norm'], p['w_out'], not last)

        def channel_mixer(h):
            return moe_swiglu(h, p['router'], p['moe_w1'], p['moe_w3'], p['moe_w2'])
    x_lat = x_lat + g_m * rmsnorm(y_lat, p['mix_post'])
    h = rmsnorm(x_lat, p['ffn_pre']) * (1.0 + sc_f) + sh_f
    x_lat = x_lat + g_f * rmsnorm(channel_mixer(h), p['ffn_post'])
    if not last:
        x_ctx = x_ctx + cg_m * rmsnorm(y_ctx, p['mix_post'])
        hc = rmsnorm(x_ctx, p['ffn_pre']) * (1.0 + csc_f) + csh_f
        x_ctx = x_ctx + cg_f * rmsnorm(channel_mixer(hc), p['ffn_post'])
    return x_lat, x_ctx


def kernel(x, c, ctx, c_ctx, l0_mod_w, l0_mod_b, l0_mix_pre, l0_mix_post, l0_ffn_pre, l0_ffn_post, l0_w_in, l0_pool_w, l0_pool_scale, l0_sinks, l0_w_out, l0_ffn_w1, l0_ffn_w3, l0_ffn_w2, l1_mod_w, l1_mod_b, l1_mix_pre, l1_mix_post, l1_ffn_pre, l1_ffn_post, l1_w_in, l1_conv_w, l1_a_log, l1_dt_bias, l1_out_norm, l1_w_out, l1_router, l1_moe_w1, l1_moe_w3, l1_moe_w2):
    layers = [
        dict(mod_w=l0_mod_w, mod_b=l0_mod_b, mix_pre=l0_mix_pre, mix_post=l0_mix_post,
             ffn_pre=l0_ffn_pre, ffn_post=l0_ffn_post, w_in=l0_w_in, pool_w=l0_pool_w,
             pool_scale=l0_pool_scale, sinks=l0_sinks, w_out=l0_w_out,
             ffn_w1=l0_ffn_w1, ffn_w3=l0_ffn_w3, ffn_w2=l0_ffn_w2),
        dict(mod_w=l1_mod_w, mod_b=l1_mod_b, mix_pre=l1_mix_pre, mix_post=l1_mix_post,
             ffn_pre=l1_ffn_pre, ffn_post=l1_ffn_post, w_in=l1_w_in, conv_w=l1_conv_w,
             a_log=l1_a_log, dt_bias=l1_dt_bias, out_norm=l1_out_norm, w_out=l1_w_out,
             router=l1_router, moe_w1=l1_moe_w1, moe_w3=l1_moe_w3, moe_w2=l1_moe_w2),
    ]
    x_lat, x_ctx = x, ctx
    for layer in range(2):
        x_lat, x_ctx = run_layer(layer, x_lat, x_ctx, c, c_ctx, layers[layer], layer == 1)
    return x_lat
```

```python
import functools
import math

import jax
import jax.numpy as jnp
from jax import lax
from jax.experimental import pallas as pl
from jax.experimental.pallas import tpu as pltpu

F32 = jnp.float32
BF16 = jnp.bfloat16
I32 = jnp.int32

LANES = 128
D_MODEL = 1024
GRID_W = 64
EPS = 1e-6
NEG_INF = -1e30

POOL_GROUPS = 4
POOL_GROUP_DIM = 128
POOL_DIM = POOL_GROUPS * POOL_GROUP_DIM
POOL_WINDOWS = (2, 4, 8, 16)
HEAD_DIM = 64
N_HEADS = 8
N_KV_HEADS = 2
GQA_GROUP = N_HEADS // N_KV_HEADS
Q_DIM = N_HEADS * HEAD_DIM
KV_DIM = N_KV_HEADS * HEAD_DIM
WINDOW = 128
BLOCK = 128
ROPE_THETA = 10000.0

GDN_HEADS = 8
GDN_HEAD_DIM = 128
GDN_DIM = GDN_HEADS * GDN_HEAD_DIM
CONV_K = 5
CHUNK = 64

N_EXPERTS = 8
TOP_K = 2

ROUTE_TILE = 512
MOE_TM = 512
MOE_TF = 512
ROW_TILE = 256


def _split_bf16(a):
    hi = a.astype(BF16)
    lo = (a - hi.astype(F32)).astype(BF16)
    return hi, lo


def _route_body(h_ref, rhi_ref, rlo_ref, route_ref, cnt_ref, carry_ref):
    i = pl.program_id(0)

    @pl.when(i == 0)
    def _():
        carry_ref[...] = jnp.zeros_like(carry_ref)

    h_hi, h_lo = _split_bf16(h_ref[...])
    r_hi = rhi_ref[...]
    r_lo = rlo_ref[...]
    logits = (jnp.dot(h_hi, r_hi, preferred_element_type=F32)
              + jnp.dot(h_hi, r_lo, preferred_element_type=F32)
              + jnp.dot(h_lo, r_hi, preferred_element_type=F32))
    tr = logits.shape[0]
    lane = lax.broadcasted_iota(I32, (tr, LANES), 1)
    logits = jnp.where(lane < N_EXPERTS, logits, -jnp.inf)
    m1 = jnp.max(logits, axis=-1, keepdims=True)
    i1 = jnp.min(jnp.where(logits == m1, lane, LANES), axis=-1, keepdims=True)
    rest = jnp.where(lane == i1, -jnp.inf, logits)
    m2 = jnp.max(rest, axis=-1, keepdims=True)
    i2 = jnp.min(jnp.where(rest == m2, lane, LANES), axis=-1, keepdims=True)
    e2 = jnp.exp(m2 - m1)
    g1 = 1.0 / (1.0 + e2)
    g2 = e2 / (1.0 + e2)

    onehot = ((lane == i1) | (lane == i2)).astype(F32)
    row = lax.broadcasted_iota(I32, (tr, tr), 0)
    col = lax.broadcasted_iota(I32, (tr, tr), 1)
    strict = (row > col).astype(BF16)
    before = jnp.dot(strict, onehot.astype(BF16), preferred_element_type=F32) + carry_ref[...]
    rank1 = jnp.sum(jnp.where(lane == i1, before, 0.0), axis=-1, keepdims=True)
    rank2 = jnp.sum(jnp.where(lane == i2, before, 0.0), axis=-1, keepdims=True)
    carry_ref[...] += jnp.sum(onehot, axis=0, keepdims=True)

    packed = jnp.where(lane == 0, i1.astype(F32), 0.0)
    packed = jnp.where(lane == 1, i2.astype(F32), packed)
    packed = jnp.where(lane == 2, g1, packed)
    packed = jnp.where(lane == 3, g2, packed)
    packed = jnp.where(lane == 4, rank1, packed)
    packed = jnp.where(lane == 5, rank2, packed)
    route_ref[...] = packed
    cnt_ref[...] = jnp.broadcast_to(carry_ref[...], cnt_ref.shape)


def moe_route(h, router):
    t, d = h.shape
    tr = min(ROUTE_TILE, t)
    r_pad = jnp.zeros((d, LANES), F32).at[:, :N_EXPERTS].set(router.astype(F32))
    r_hi, r_lo = _split_bf16(r_pad)
    route, cnt = pl.pallas_call(
        _route_body,
        grid=(t // tr,),
        in_specs=[
            pl.BlockSpec((tr, d), lambda i: (i, 0)),
            pl.BlockSpec((d, LANES), lambda i: (0, 0)),
            pl.BlockSpec((d, LANES), lambda i: (0, 0)),
        ],
        out_specs=[
            pl.BlockSpec((tr, LANES), lambda i: (i, 0)),
            pl.BlockSpec((8, LANES), lambda i: (0, 0)),
        ],
        out_shape=[
            jax.ShapeDtypeStruct((t, LANES), F32),
            jax.ShapeDtypeStruct((8, LANES), F32),
        ],
        scratch_shapes=[pltpu.VMEM((1, LANES), F32)],
        compiler_params=pltpu.CompilerParams(dimension_semantics=("arbitrary",)),
        name="moe_route",
    )(h, r_hi, r_lo)
    return route, cnt


def _row_copy(src_ref, src_row, dst_ref, dst_row, sem):
    return pltpu.make_async_copy(src_ref.at[pl.ds(src_row, 1)], dst_ref.at[pl.ds(dst_row, 1)], sem)


def _scatter_body(pos_ref, h_ref, xs_in_ref, xs_ref, sem, *, ts):
    del xs_in_ref

    def issue(r, c):
        for k in range(TOP_K):
            _row_copy(h_ref, r, xs_ref, pos_ref[0, 0, TOP_K * r + k], sem).start()
        return c

    lax.fori_loop(0, ts, issue, 0)

    def drain(r, c):
        _row_copy(h_ref, 0, xs_ref, 0, sem).wait()
        return c

    lax.fori_loop(0, TOP_K * ts, drain, 0)


def moe_scatter(h, pos, p_rows):
    t, d = h.shape
    ts = min(ROW_TILE, t)
    nt = t // ts
    xs0 = jnp.zeros((p_rows, d), h.dtype)
    return pl.pallas_call(
        functools.partial(_scatter_body, ts=ts),
        grid=(nt,),
        in_specs=[
            pl.BlockSpec((1, 1, TOP_K * ts), lambda i: (i, 0, 0), memory_space=pltpu.SMEM),
            pl.BlockSpec((ts, d), lambda i: (i, 0)),
            pl.BlockSpec(memory_space=pl.ANY),
        ],
        out_specs=pl.BlockSpec(memory_space=pl.ANY),
        out_shape=jax.ShapeDtypeStruct((p_rows, d), h.dtype),
        scratch_shapes=[pltpu.SemaphoreType.DMA(())],
        input_output_aliases={2: 0},
        compiler_params=pltpu.CompilerParams(dimension_semantics=("arbitrary",), disable_bounds_checks=True),
        name="moe_scatter",
    )(pos.reshape(nt, 1, TOP_K * ts), h, xs0)


def _expert_body(te_ref, nu_ref, x_ref, w1_ref, w3_ref, w2_ref, o_ref):
    i = pl.program_id(0)
    j = pl.program_id(1)

    @pl.when(j == 0)
    def _():
        o_ref[...] = jnp.zeros_like(o_ref)

    @pl.when(i < nu_ref[0])
    def _():
        x = x_ref[...].astype(BF16)
        a = jnp.dot(x, w1_ref[0], preferred_element_type=F32)
        b = jnp.dot(x, w3_ref[0], preferred_element_type=F32)
        mid = (a * jax.nn.sigmoid(a) * b).astype(BF16)
        o_ref[...] += jnp.dot(mid, w2_ref[0], preferred_element_type=F32)


def moe_experts(xs, w1, w3, w2, tile_expert, n_used, tm):
    p_rows, d = xs.shape
    n_exp, _, d_exp = w1.shape
    tf = MOE_TF if d_exp % MOE_TF == 0 else d_exp
    nj = d_exp // tf
    n_tiles = p_rows // tm

    def jj(i, j, nu):
        return jnp.where(i < nu[0], j, nj - 1)

    grid_spec = pltpu.PrefetchScalarGridSpec(
        num_scalar_prefetch=2,
        grid=(n_tiles, nj),
        in_specs=[
            pl.BlockSpec((tm, d), lambda i, j, te, nu: (i, 0)),
            pl.BlockSpec((1, d, tf), lambda i, j, te, nu: (te[i], 0, jj(i, j, nu))),
            pl.BlockSpec((1, d, tf), lambda i, j, te, nu: (te[i], 0, jj(i, j, nu))),
            pl.BlockSpec((1, tf, d), lambda i, j, te, nu: (te[i], jj(i, j, nu), 0)),
        ],
        out_specs=pl.BlockSpec((tm, d), lambda i, j, te, nu: (i, 0)),
    )
    return pl.pallas_call(
        _expert_body,
        grid_spec=grid_spec,
        out_shape=jax.ShapeDtypeStruct((p_rows, d), F32),
        compiler_params=pltpu.CompilerParams(
            dimension_semantics=("arbitrary", "arbitrary"),
            vmem_limit_bytes=48 * 1024 * 1024),
        name="moe_experts",
    )(tile_expert, n_used, xs, w1, w3, w2)


def _combine_body(pos_ref, gate_ref, ys_ref, y_ref, buf, sem, *, ts):
    def issue(r, c):
        for k in range(TOP_K):
            _row_copy(ys_ref, pos_ref[0, 0, TOP_K * r + k], buf.at[k], r, sem).start()
        return c

    lax.fori_loop(0, ts, issue, 0)

    def drain(r, c):
        _row_copy(ys_ref, 0, buf.at[0], 0, sem).wait()
        return c

    lax.fori_loop(0, TOP_K * ts, drain, 0)
    g = gate_ref[...]
    y_ref[...] = g[:, 2:3] * buf[0] + g[:, 3:4] * buf[1]


def moe_combine(ys, pos, route):
    t = route.shape[0]
    d = ys.shape[1]
    ts = min(ROW_TILE, t)
    nt = t // ts
    return pl.pallas_call(
        functools.partial(_combine_body, ts=ts),
        grid=(nt,),
        in_specs=[
            pl.BlockSpec((1, 1, TOP_K * ts), lambda i: (i, 0, 0), memory_space=pltpu.SMEM),
            pl.BlockSpec((ts, LANES), lambda i: (i, 0)),
            pl.BlockSpec(memory_space=pl.ANY),
        ],
        out_specs=pl.BlockSpec((ts, d), lambda i: (i, 0)),
        out_shape=jax.ShapeDtypeStruct((t, d), F32),
        scratch_shapes=[pltpu.VMEM((TOP_K, ts, d), F32), pltpu.SemaphoreType.DMA(())],
        compiler_params=pltpu.CompilerParams(dimension_semantics=("arbitrary",), disable_bounds_checks=True),
        name="moe_combine",
    )(pos.reshape(nt, 1, TOP_K * ts), route, ys)


def moe_swiglu(h, router, w1, w3, w2):
    shp = h.shape
    tok = h.reshape(-1, shp[-1])
    t = tok.shape[0]
    tm = min(MOE_TM, t)
    route, cnt = moe_route(tok, router)

    counts = cnt[0, :N_EXPERTS].astype(I32)
    padded = ((counts + tm - 1) // tm) * tm
    ends = jnp.cumsum(padded)
    starts = ends - padded
    expert = route[:, 0:TOP_K].astype(I32)
    rank = route[:, 4:4 + TOP_K].astype(I32)
    start_of = jnp.sum(jnp.where(expert[..., None] == jnp.arange(N_EXPERTS), starts, 0), axis=-1)
    pos = start_of + rank

    n_tiles = (TOP_K * t) // tm + N_EXPERTS
    n_used = (ends[-1] // tm).astype(I32)
    tile_idx = jnp.minimum(jnp.arange(n_tiles, dtype=I32), n_used - 1)
    tile_expert = jnp.sum(tile_idx[:, None] * tm >= ends[None, :], axis=-1).astype(I32)
    tile_expert = jnp.minimum(tile_expert, N_EXPERTS - 1)

    xs = moe_scatter(tok, pos, n_tiles * tm)
    ys = moe_experts(xs, w1.astype(BF16), w3.astype(BF16), w2.astype(BF16), tile_expert, n_used.reshape(1), tm)
    y = moe_combine(ys, pos, route)
    return y.reshape(shp)


GC = 128
GD = GDN_HEAD_DIM
PAD = 8


def _bdot(a, b):
    return jnp.dot(a.astype(BF16), b.astype(BF16), preferred_element_type=F32)


def _bdot_nt(a, b):
    return lax.dot_general(a.astype(BF16), b.astype(BF16), (((1,), (1,)), ((), ())), preferred_element_type=F32)


def _unit_tri_inverse(lm, row, col):
    eye = (row == col).astype(F32)
    base = 16
    same = (row // base) == (col // base)
    x = jnp.where(same, -lm, 0.0)
    t = eye + x
    for _ in range(3):
        x = _bdot(x, x)
        t = t + _bdot(t, x)
    s = base
    while s < GC:
        off = jnp.where(((row // (2 * s)) == (col // (2 * s))) & ((row // s) != (col // s)), lm, 0.0)
        t = t - _bdot(t, _bdot(off, t))
        s *= 2
    return t


def _gdn_body(alog_ref, dtb_ref, q_ref, k_ref, v_ref, z_ref, ab_ref, cq_ref, ck_ref, cv_ref, onorm_ref,
              o_ref, pad_s, q_s, k_s, v_s, o_s, gate_s, state_s, *, n_ctx_chunks, n_chunks, n_heads):
    h = pl.program_id(1)
    ltot = n_chunks * GC
    lc = n_ctx_chunks * GC

    rloc = lax.broadcasted_iota(I32, (GC, GD), 0)

    def conv_into(src_ref, w_ref, dst_ref, normalise, scale):
        pad_s[pl.ds(0, PAD), :] = jnp.zeros((PAD, GD), F32)
        pad_s[pl.ds(PAD + ltot, PAD), :] = jnp.zeros((PAD, GD), F32)
        pad_s[pl.ds(PAD, ltot), :] = src_ref[0].astype(F32)
        w = w_ref[...].astype(F32)

        def chunk(c, carry):
            t0 = pl.multiple_of(c * GC, GC)
            win = pad_s[pl.ds(t0, GC + 2 * PAD), :]
            first = (c == 0) | (c == n_ctx_chunks)
            last = (c == n_ctx_chunks - 1) | (c == n_chunks - 1)
            acc = jnp.zeros((GC, GD), F32)
            for j in range(CONV_K):
                sh = CONV_K // 2 - j
                rolled = win if sh == 0 else pltpu.roll(win, sh % (GC + 2 * PAD), axis=0)
                tap = rolled[PAD:PAD + GC]
                if sh > 0:
                    tap = jnp.where(first & (rloc < sh), 0.0, tap)
                elif sh < 0:
                    tap = jnp.where(last & (rloc >= GC + sh), 0.0, tap)
                acc = acc + tap * w[j:j + 1, :]
            y = acc * jax.nn.sigmoid(acc)
            if normalise:
                y = y * (lax.rsqrt(jnp.sum(y * y, axis=-1, keepdims=True) + EPS) * scale)
            dst_ref[pl.ds(t0, GC), :] = y
            return carry

        lax.fori_loop(0, n_chunks, chunk, 0)

    conv_into(q_ref, cq_ref, q_s, True, GD ** -0.5)
    conv_into(k_ref, ck_ref, k_s, True, 1.0)
    conv_into(v_ref, cv_ref, v_s, False, 1.0)

    row = lax.broadcasted_iota(I32, (GC, GC), 0)
    col = lax.broadcasted_iota(I32, (GC, GC), 1)
    for d in range(2):
        a = ab_ref[0, d * 2 * n_heads + h]
        b = ab_ref[0, d * 2 * n_heads + n_heads + h]
        xa = a + dtb_ref[d, h]
        softplus = jnp.maximum(xa, 0.0) + jnp.log1p(jnp.exp(-jnp.abs(xa)))
        g = -jnp.exp(alog_ref[d, h]) * softplus
        beta = 1.0 / (1.0 + jnp.exp(-b))
        tri = ((row <= col) if d == 0 else (row >= col)).astype(BF16)
        g_hi = g.astype(BF16)
        g_lo = (g - g_hi.astype(F32)).astype(BF16)
        gam = (jnp.dot(g_hi, tri, preferred_element_type=F32) + jnp.dot(g_lo, tri, preferred_element_type=F32))
        gate_s[d, 0] = gam
        gate_s[d, 1] = beta
        gate_s[d, 2] = jnp.broadcast_to(jnp.sum(g, axis=-1, keepdims=True), g.shape)

    state_s[...] = jnp.zeros_like(state_s)
    eye = row == col

    def step(s, carry):
        for d in range(2):
            if d == 0:
                c = s
            else:
                c = jnp.where(s < n_ctx_chunks, n_ctx_chunks - 1 - s, n_chunks - 1 - (s - n_ctx_chunks))
            t0 = pl.multiple_of(c * GC, GC)
            q = q_s[pl.ds(t0, GC), :]
            k = k_s[pl.ds(t0, GC), :]
            v = v_s[pl.ds(t0, GC), :]
            gam_row = gate_s[d, 0, pl.ds(c, 1), :]
            beta_row = gate_s[d, 1, pl.ds(c, 1), :]
            tot_row = gate_s[d, 2, pl.ds(c, 1), :]
            gam_col = jnp.sum(jnp.where(eye, jnp.broadcast_to(gam_row, (GC, GC)), 0.0), axis=1, keepdims=True)
            incl = (row >= col) if d == 0 else (row <= col)
            strict = (row > col) if d == 0 else (row < col)
            decay = jnp.exp(jnp.where(incl, gam_col - gam_row, 0.0))
            kk = _bdot_nt(k, k)
            qk = _bdot_nt(q, k)
            lm = jnp.where(strict, kk * decay * beta_row, 0.0)
            t_inv = _unit_tri_inverse(lm, row, col)
            e_col = jnp.exp(gam_col)
            uw = _bdot(t_inv, jnp.concatenate([v, k * e_col], axis=1))
            u_hat = uw[:, :GD]
            w_hat = uw[:, GD:]
            qd = q * e_col
            qkb = jnp.where(incl, qk * decay, 0.0) * beta_row
            kdt = k.T * (jnp.exp(tot_row - gam_row) * beta_row)
            cd = jnp.exp(tot_row[:, 0:1])
            st = state_s[d]
            ws = _bdot(jnp.concatenate([w_hat, qd], axis=0), st)
            v_new = u_hat - ws[:GC]
            o = ws[GC:] + _bdot(qkb, v_new)
            state_s[d] = st * cd + _bdot(kdt, v_new)
            o_s[d, pl.ds(t0, GC), :] = o
        return carry

    lax.fori_loop(0, n_chunks, step, 0)

    gain = onorm_ref[...].astype(F32)

    def finish(c, carry):
        t0 = pl.multiple_of(lc + c * GC, GC)
        o = o_s[0, pl.ds(t0, GC), :] + o_s[1, pl.ds(t0, GC), :]
        z = z_ref[0, pl.ds(t0, GC), :].astype(F32)
        y = o * lax.rsqrt(jnp.mean(o * o, axis=-1, keepdims=True) + EPS) * gain
        y = y * (z * jax.nn.sigmoid(z))
        o_ref[0, pl.ds(pl.multiple_of(c * GC, GC), GC), :] = y.astype(o_ref.dtype)
        return carry

    lax.fori_loop(0, n_chunks - n_ctx_chunks, finish, 0)


def gdn_core(p_all, ab_rows, conv_w, a_log, dt_bias, out_norm, n_ctx):
    bsz, ltot, _ = p_all.shape
    n_heads = a_log.shape[1]
    n_chunks = ltot // GC
    n_ctx_chunks = n_ctx // GC
    lat = ltot - n_ctx
    n_rows = ab_rows.shape[2]
    body = functools.partial(_gdn_body, n_ctx_chunks=n_ctx_chunks, n_chunks=n_chunks, n_heads=n_heads)
    smem = pl.BlockSpec(memory_space=pltpu.SMEM)

    def col(off):
        return pl.BlockSpec((1, ltot, GD), lambda b, h: (b, 0, off * n_heads + h))

    def cw(off):
        return pl.BlockSpec((CONV_K, GD), lambda b, h: (0, off * n_heads + h))

    return pl.pallas_call(
        body,
        grid=(bsz, n_heads),
        in_specs=[smem, smem, col(0), col(1), col(2), col(3),
                  pl.BlockSpec((1, 4 * n_heads, n_rows, GC), lambda b, h: (b, 0, 0, 0)),
                  cw(0), cw(1), cw(2),
                  pl.BlockSpec((1, GD), lambda b, h: (0, 0))],
        out_specs=pl.BlockSpec((1, lat, GD), lambda b, h: (b, 0, h)),
        out_shape=jax.ShapeDtypeStruct((bsz, lat, n_heads * GD), BF16),
        scratch_shapes=[
            pltpu.VMEM((ltot + 2 * PAD, GD), F32),
            pltpu.VMEM((ltot, GD), F32),
            pltpu.VMEM((ltot, GD), F32),
            pltpu.VMEM((ltot, GD), F32),
            pltpu.VMEM((2, ltot, GD), F32),
            pltpu.VMEM((2, 3, n_rows, GC), F32),
            pltpu.VMEM((2, GD, GD), F32),
        ],
        compiler_params=pltpu.CompilerParams(
            dimension_semantics=("arbitrary", "arbitrary"),
            vmem_limit_bytes=56 * 1024 * 1024),
        name="gdn_core",
    )(a_log.astype(F32), dt_bias.astype(F32), p_all, p_all, p_all, p_all, ab_rows,
      conv_w, conv_w, conv_w, out_norm.reshape(1, GD))


def gdn_gate_rows(ab, n_heads):
    bsz, ltot, _ = ab.shape
    n_chunks = ltot // GC
    n_rows = -(-n_chunks // 8) * 8
    t = jnp.transpose(ab.astype(F32), (0, 2, 1)).reshape(bsz, 4 * n_heads, n_chunks, GC)
    return jnp.pad(t, ((0, 0), (0, 0), (0, n_rows - n_chunks), (0, 0)))


def gdn_mixer_pallas(h_lat, h_ctx, w_in, conv_w, a_log, dt_bias, out_norm, w_out):
    n_ctx = h_ctx.shape[1]
    n_heads = a_log.shape[1]
    p_all = jnp.concatenate([h_ctx, h_lat], axis=1) @ w_in
    ab_rows = gdn_gate_rows(p_all[..., 4 * n_heads * GD:], n_heads)
    y = gdn_core(p_all, ab_rows, conv_w, a_log, dt_bias, out_norm, n_ctx)
    return y.astype(F32) @ w_out


def rmsnorm(x, g):
    xf = x.astype(F32)
    y = xf * lax.rsqrt(jnp.mean(xf * xf, axis=-1, keepdims=True) + EPS)
    return y.astype(x.dtype) * g


def l2norm(x):
    xf = x.astype(F32)
    return (xf * lax.rsqrt(jnp.sum(xf * xf, axis=-1, keepdims=True) + EPS)).astype(x.dtype)


def swiglu(h, w1, w3, w2):
    return (jax.nn.silu(h @ w1) * (h @ w3)) @ w2


def multiscale_pool(u, pool_w, pool_scale):
    B, L, _ = u.shape
    cs = jnp.cumsum(u.astype(F32), axis=1)
    cs = jnp.concatenate([jnp.zeros_like(cs[:, :1]), cs], axis=1)
    t = jnp.arange(L)
    means = []
    for gi, w in enumerate(POOL_WINDOWS):
        lo = jnp.clip(t - w // 2, 0, L)
        hi = jnp.clip(t + w // 2, 0, L)
        seg = cs[:, :, gi * POOL_GROUP_DIM:(gi + 1) * POOL_GROUP_DIM]
        means.append((seg[:, hi] - seg[:, lo]) / (hi - lo).astype(F32)[None, :, None])
    pooled = jnp.stack(means, axis=2)
    delta = (pooled - u.reshape(B, L, POOL_GROUPS, POOL_GROUP_DIM).astype(F32)).astype(u.dtype)
    y = jnp.einsum('blgc,gcd->blgd', delta, pool_w) * pool_scale.reshape(POOL_GROUPS, POOL_GROUP_DIM)
    return y.reshape(B, L, POOL_DIM)


def axial_rope(x, rows, cols):
    half = HEAD_DIM // 2
    inv = ROPE_THETA ** (-jnp.arange(0, half, 2, dtype=F32) / half)

    def rot(xa, pos):
        ang = pos[:, None] * inv[None, :]
        cos = jnp.cos(ang)[None, :, None, :].astype(x.dtype)
        sin = jnp.sin(ang)[None, :, None, :].astype(x.dtype)
        x1, x2 = xa[..., :half // 2], xa[..., half // 2:]
        return jnp.concatenate([x1 * cos - x2 * sin, x2 * cos + x1 * sin], axis=-1)

    return jnp.concatenate([rot(x[..., :half], rows), rot(x[..., half:], cols)], axis=-1)


def split_ab(p):
    B, L, _ = p.shape
    u = p[..., :POOL_DIM]
    q = p[..., POOL_DIM:POOL_DIM + Q_DIM].reshape(B, L, N_HEADS, HEAD_DIM)
    k = p[..., POOL_DIM + Q_DIM:POOL_DIM + Q_DIM + KV_DIM].reshape(B, L, N_KV_HEADS, HEAD_DIM)
    v = p[..., POOL_DIM + Q_DIM + KV_DIM:].reshape(B, L, N_KV_HEADS, HEAD_DIM)
    return u, q, k, v


def banded_attention(q, k, v, k_ctx, v_ctx, sinks):
    B, L, _, _ = q.shape
    Lc = k_ctx.shape[1]
    nb = L // BLOCK
    scale = HEAD_DIM ** -0.5
    qb = q.reshape(B, nb, BLOCK, N_KV_HEADS, GQA_GROUP, HEAD_DIM)
    pad = ((0, 0), (BLOCK, BLOCK), (0, 0), (0, 0))
    kp = jnp.pad(k, pad).reshape(B, nb + 2, BLOCK, N_KV_HEADS, HEAD_DIM)
    vp = jnp.pad(v, pad).reshape(B, nb + 2, BLOCK, N_KV_HEADS, HEAD_DIM)
    k_band = jnp.concatenate([kp[:, :-2], kp[:, 1:-1], kp[:, 2:]], axis=2)
    v_band = jnp.concatenate([vp[:, :-2], vp[:, 1:-1], vp[:, 2:]], axis=2)
    s_band = jnp.einsum('bnqhgd,bnkhd->bnhgqk', qb, k_band, preferred_element_type=F32) * scale
    qpos = jnp.arange(nb)[:, None] * BLOCK + jnp.arange(BLOCK)[None, :]
    kpos = (jnp.arange(nb)[:, None] - 1) * BLOCK + jnp.arange(3 * BLOCK)[None, :]
    allowed = ((jnp.abs(kpos[:, None, :] - qpos[:, :, None]) <= WINDOW)
               & (kpos[:, None, :] >= 0) & (kpos[:, None, :] < L))
    s_band = jnp.where(allowed[None, :, None, None], s_band, NEG_INF)
    s_ctx = jnp.einsum('bnqhgd,bkhd->bnhgqk', qb, k_ctx, preferred_element_type=F32) * scale
    sink = sinks.astype(F32).reshape(N_KV_HEADS, GQA_GROUP)
    s_sink = jnp.broadcast_to(sink[None, None, :, :, None, None], s_ctx.shape[:-1] + (1,))
    p = jax.nn.softmax(jnp.concatenate([s_sink, s_ctx, s_band], axis=-1), axis=-1)
    p_ctx = p[..., 1:1 + Lc].astype(v.dtype)
    p_band = p[..., 1 + Lc:].astype(v.dtype)
    o = (jnp.einsum('bnhgqk,bkhd->bnqhgd', p_ctx, v_ctx)
         + jnp.einsum('bnhgqk,bnkhd->bnqhgd', p_band, v_band))
    return o.reshape(B, L, Q_DIM)


def context_attention(q, k, v, sinks):
    B, Lc, _, _ = q.shape
    qc = q.reshape(B, Lc, N_KV_HEADS, GQA_GROUP, HEAD_DIM)
    s = jnp.einsum('bqhgd,bkhd->bhgqk', qc, k, preferred_element_type=F32) * HEAD_DIM ** -0.5
    sink = sinks.astype(F32).reshape(N_KV_HEADS, GQA_GROUP)
    s_sink = jnp.broadcast_to(sink[None, :, :, None, None], s.shape[:-1] + (1,))
    p = jax.nn.softmax(jnp.concatenate([s_sink, s], axis=-1), axis=-1)[..., 1:].astype(v.dtype)
    o = jnp.einsum('bhgqk,bkhd->bqhgd', p, v)
    return o.reshape(B, Lc, Q_DIM)


def pool_swa_mixer(h_lat, h_ctx, w_in, pool_w, pool_scale, sinks, w_out, ctx_out):
    L = h_lat.shape[1]
    rows_n = L // GRID_W
    rows = jnp.repeat(jnp.arange(rows_n, dtype=F32), GRID_W)
    cols = jnp.tile(jnp.arange(GRID_W, dtype=F32), rows_n)
    u_l, q_l, k_l, v_l = split_ab(h_lat @ w_in)
    u_c, q_c, k_c, v_c = split_ab(h_ctx @ w_in)
    q_l = axial_rope(q_l, rows, cols)
    k_l = axial_rope(k_l, rows, cols)
    a_l = banded_attention(q_l, k_l, v_l, k_c, v_c, sinks)
    y_lat = jnp.concatenate([multiscale_pool(u_l, pool_w, pool_scale), a_l], axis=-1) @ w_out
    if not ctx_out:
        return y_lat, None
    a_c = context_attention(q_c, k_c, v_c, sinks)
    y_ctx = jnp.concatenate([multiscale_pool(u_c, pool_w, pool_scale), a_c], axis=-1) @ w_out
    return y_lat, y_ctx


def centred_dwconv(x, w):
    return lax.conv_general_dilated(x, w[:, None, :], window_strides=(1,),
                                    padding=[(CONV_K // 2, CONV_K // 2)],
                                    dimension_numbers=('NWC', 'WIO', 'NWC'),
                                    feature_group_count=x.shape[-1])


def gated_delta_chunked(q, k, v, g, beta, s0):
    B, H, L, _ = q.shape
    DV = v.shape[-1]
    n = L // CHUNK

    def chunks(t):
        return t.astype(F32).reshape((B, H, n, CHUNK) + t.shape[3:])

    q, k, v, g, beta = chunks(q), chunks(k), chunks(v), chunks(g), chunks(beta)
    gam = jnp.cumsum(g, axis=-1)
    idx = jnp.arange(CHUNK)
    incl = idx[:, None] >= idx[None, :]
    strict = idx[:, None] > idx[None, :]
    diff = gam[..., :, None] - gam[..., None, :]
    decay = jnp.where(incl, jnp.exp(jnp.where(incl, diff, 0.0)), 0.0)
    kb = k * beta[..., None]
    m = jnp.where(strict, jnp.einsum('bhnid,bhnjd->bhnij', kb, k) * decay, 0.0)
    a = m + jnp.eye(CHUNK, dtype=F32)
    u = lax.linalg.triangular_solve(a, v * beta[..., None], left_side=True, lower=True, unit_diagonal=True)
    w = lax.linalg.triangular_solve(a, kb * jnp.exp(gam)[..., None], left_side=True, lower=True,
                                    unit_diagonal=True)
    qk = jnp.einsum('bhnid,bhnjd->bhnij', q, k) * decay
    qd = q * jnp.exp(gam)[..., None]
    kd = k * jnp.exp(gam[..., -1:] - gam)[..., None]
    cd = jnp.exp(gam[..., -1])
    xs = tuple(jnp.moveaxis(t, 2, 0) for t in (u, w, qk, qd, kd, cd))

    def step(s, inp):
        u_c, w_c, qk_c, qd_c, kd_c, cd_c = inp
        v_new = u_c - jnp.einsum('bhik,bhkv->bhiv', w_c, s)
        o_c = jnp.einsum('bhik,bhkv->bhiv', qd_c, s) + jnp.einsum('bhij,bhjv->bhiv', qk_c, v_new)
        s = s * cd_c[..., None, None] + jnp.einsum('bhik,bhiv->bhkv', kd_c, v_new)
        return s, o_c

    s_fin, o = lax.scan(step, s0, xs)
    o = jnp.moveaxis(o, 0, 2).reshape(B, H, L, DV)
    return o, s_fin


def gdn_project(h, w_in, conv_w, a_log, dt_bias):
    B, L, _ = h.shape
    p = h @ w_in
    qkv = jax.nn.silu(centred_dwconv(p[..., :3 * GDN_DIM], conv_w))

    def heads(t):
        return t.reshape(B, L, GDN_HEADS, GDN_HEAD_DIM).transpose(0, 2, 1, 3)

    q = l2norm(heads(qkv[..., :GDN_DIM])) * (GDN_HEAD_DIM ** -0.5)
    k = l2norm(heads(qkv[..., GDN_DIM:2 * GDN_DIM]))
    v = heads(qkv[..., 2 * GDN_DIM:])
    z = p[..., 3 * GDN_DIM:4 * GDN_DIM].reshape(B, L, GDN_HEADS, GDN_HEAD_DIM)
    ab = p[..., 4 * GDN_DIM:].astype(F32).reshape(B, L, 2, 2, GDN_HEADS)
    g = -jnp.exp(a_log.astype(F32)) * jax.nn.softplus(ab[:, :, :, 0] + dt_bias.astype(F32))
    beta = jax.nn.sigmoid(ab[:, :, :, 1])
    g = jnp.transpose(g, (2, 0, 3, 1))
    beta = jnp.transpose(beta, (2, 0, 3, 1))
    return q, k, v, z, g, beta


def gdn_output(o, z, out_norm, w_out):
    B, H, L, DV = o.shape
    o = jnp.transpose(o, (0, 2, 1, 3))
    y = rmsnorm(o, out_norm.astype(F32)) * jax.nn.silu(z.astype(F32))
    return y.astype(z.dtype).reshape(B, L, H * DV) @ w_out


def gdn_mixer(h_lat, h_ctx, w_in, conv_w, a_log, dt_bias, out_norm, w_out, ctx_out):
    ql, kl, vl, zl, gl, bl = gdn_project(h_lat, w_in, conv_w, a_log, dt_bias)
    qc, kc, vc, zc, gc, bc = gdn_project(h_ctx, w_in, conv_w, a_log, dt_bias)
    B = h_lat.shape[0]
    s0 = jnp.zeros((B, GDN_HEADS, GDN_HEAD_DIM, GDN_HEAD_DIM), F32)
    outs_l, outs_c = [], []
    for d in range(2):
        rev = (lambda t: jnp.flip(t, axis=2)) if d == 1 else (lambda t: t)
        oc, sc = gated_delta_chunked(rev(qc), rev(kc), rev(vc), rev(gc[d]), rev(bc[d]), s0)
        ol, _ = gated_delta_chunked(rev(ql), rev(kl), rev(vl), rev(gl[d]), rev(bl[d]), sc)
        outs_l.append(rev(ol))
        outs_c.append(rev(oc))
    y_lat = gdn_output(outs_l[0] + outs_l[1], zl, out_norm, w_out)
    if not ctx_out:
        return y_lat, None
    y_ctx = gdn_output(outs_c[0] + outs_c[1], zc, out_norm, w_out)
    return y_lat, y_ctx


def run_layer(layer, x_lat, x_ctx, c, c_ctx, p, last):
    mod = jax.nn.silu(c) @ p['mod_w'] + p['mod_b']
    mod_c = jax.nn.silu(c_ctx) @ p['mod_w'] + p['mod_b']
    sh_m, sc_m, g_m, sh_f, sc_f, g_f = [t[:, None, :] for t in jnp.split(mod, 6, axis=-1)]
    csh_m, csc_m, cg_m, csh_f, csc_f, cg_f = jnp.split(mod_c, 6)
    h_lat = rmsnorm(x_lat, p['mix_pre']) * (1.0 + sc_m) + sh_m
    h_ctx = rmsnorm(x_ctx, p['mix_pre']) * (1.0 + csc_m) + csh_m
    if layer % 2 == 0:
        y_lat, y_ctx = pool_swa_mixer(h_lat, h_ctx, p['w_in'], p['pool_w'], p['pool_scale'], p['sinks'],
                                      p['w_out'], not last)

        def channel_mixer(h):
            return swiglu(h, p['ffn_w1'], p['ffn_w3'], p['ffn_w2'])
    else:
        assert last
        y_lat = gdn_mixer_pallas(h_lat, h_ctx, p['w_in'], p['conv_w'], p['a_log'], p['dt_bias'],
                                 p['out_norm'], p['w_out'])
        y_ctx = None

        def channel_mixer(h):
            return moe_swiglu(h, p['router'], p['moe_w1'], p['moe_w3'], p['moe_w2'])
    x_lat = x_lat + g_m * rmsnorm(y_lat, p['mix_post'])
    h = rmsnorm(x_lat, p['ffn_pre']) * (1.0 + sc_f) + sh_f
    x_lat = x_lat + g_f * rmsnorm(channel_mixer(h), p['ffn_post'])
    if not last:
        x_ctx = x_ctx + cg_m * rmsnorm(y_ctx, p['mix_post'])
        hc = rmsnorm(x_ctx, p['ffn_pre']) * (1.0 + csc_f) + csh_f
        x_ctx = x_ctx + cg_f * rmsnorm(channel_mixer(hc), p['ffn_post'])
    return x_lat, x_ctx


def kernel(x, c, ctx, c_ctx, l0_mod_w, l0_mod_b, l0_mix_pre, l0_mix_post, l0_ffn_pre, l0_ffn_post, l0_w_in, l0_pool_w, l0_pool_scale, l0_sinks, l0_w_out, l0_ffn_w1, l0_ffn_w3, l0_ffn_w2, l1_mod_w, l1_mod_b, l1_mix_pre, l1_mix_post, l1_ffn_pre, l1_ffn_post, l1_w_in, l1_conv_w, l1_a_log, l1_dt_bias, l1_out_norm, l1_w_out, l1_router, l1_moe_w1, l1_moe_w3, l1_moe_w2):
    layers = [
        dict(mod_w=l0_mod_w, mod_b=l0_mod_b, mix_pre=l0_mix_pre, mix_post=l0_mix_post,
             ffn_pre=l0_ffn_pre, ffn_post=l0_ffn_post, w_in=l0_w_in, pool_w=l0_pool_w,
             pool_scale=l0_pool_scale, sinks=l0_sinks, w_out=l0_w_out,
             ffn_w1=l0_ffn_w1, ffn_w3=l0_ffn_w3, ffn_w2=l0_ffn_w2),
        dict(mod_w=l1_mod_w, mod_b=l1_mod_b, mix_pre=l1_mix_pre, mix_post=l1_mix_post,
             ffn_pre=l1_ffn_pre, ffn_post=l1_ffn_post, w_in=l1_w_in, conv_w=l1_conv_w,
             a_log=l1_a_log, dt_bias=l1_dt_bias, out_norm=l1_out_norm, w_out=l1_w_out,
             router=l1_router, moe_w1=l1_moe_w1, moe_w3=l1_moe_w3, moe_w2=l1_moe_w2),
    ]
    x_lat, x_ctx = x, ctx
    for layer in range(2):
        x_lat, x_ctx = run_layer(layer, x_lat, x_ctx, c, c_ctx, layers[layer], layer == 1)
    return x_lat
```

```python
import functools
import math

import jax
import jax.numpy as jnp
from jax import lax
from jax.experimental import pallas as pl
from jax.experimental.pallas import tpu as pltpu

F32 = jnp.float32
BF16 = jnp.bfloat16
I32 = jnp.int32

LANES = 128
D_MODEL = 1024
GRID_W = 64
EPS = 1e-6
NEG_INF = -1e30

POOL_GROUPS = 4
POOL_GROUP_DIM = 128
POOL_DIM = POOL_GROUPS * POOL_GROUP_DIM
POOL_WINDOWS = (2, 4, 8, 16)
HEAD_DIM = 64
N_HEADS = 8
N_KV_HEADS = 2
GQA_GROUP = N_HEADS // N_KV_HEADS
Q_DIM = N_HEADS * HEAD_DIM
KV_DIM = N_KV_HEADS * HEAD_DIM
WINDOW = 128
BLOCK = 128
ROPE_THETA = 10000.0

GDN_HEADS = 8
GDN_HEAD_DIM = 128
GDN_DIM = GDN_HEADS * GDN_HEAD_DIM
CONV_K = 5
CHUNK = 64

N_EXPERTS = 8
TOP_K = 2

ROUTE_TILE = 512
MOE_TM = 512
MOE_TF = 512
ROW_TILE = 256


def _split_bf16(a):
    hi = a.astype(BF16)
    lo = (a - hi.astype(F32)).astype(BF16)
    return hi, lo


def _route_body(h_ref, rhi_ref, rlo_ref, route_ref, cnt_ref, carry_ref):
    i = pl.program_id(0)

    @pl.when(i == 0)
    def _():
        carry_ref[...] = jnp.zeros_like(carry_ref)

    h_hi, h_lo = _split_bf16(h_ref[...])
    r_hi = rhi_ref[...]
    r_lo = rlo_ref[...]
    logits = (jnp.dot(h_hi, r_hi, preferred_element_type=F32)
              + jnp.dot(h_hi, r_lo, preferred_element_type=F32)
              + jnp.dot(h_lo, r_hi, preferred_element_type=F32))
    tr = logits.shape[0]
    lane = lax.broadcasted_iota(I32, (tr, LANES), 1)
    logits = jnp.where(lane < N_EXPERTS, logits, -jnp.inf)
    m1 = jnp.max(logits, axis=-1, keepdims=True)
    i1 = jnp.min(jnp.where(logits == m1, lane, LANES), axis=-1, keepdims=True)
    rest = jnp.where(lane == i1, -jnp.inf, logits)
    m2 = jnp.max(rest, axis=-1, keepdims=True)
    i2 = jnp.min(jnp.where(rest == m2, lane, LANES), axis=-1, keepdims=True)
    e2 = jnp.exp(m2 - m1)
    g1 = 1.0 / (1.0 + e2)
    g2 = e2 / (1.0 + e2)

    onehot = ((lane == i1) | (lane == i2)).astype(F32)
    row = lax.broadcasted_iota(I32, (tr, tr), 0)
    col = lax.broadcasted_iota(I32, (tr, tr), 1)
    strict = (row > col).astype(BF16)
    before = jnp.dot(strict, onehot.astype(BF16), preferred_element_type=F32) + carry_ref[...]
    rank1 = jnp.sum(jnp.where(lane == i1, before, 0.0), axis=-1, keepdims=True)
    rank2 = jnp.sum(jnp.where(lane == i2, before, 0.0), axis=-1, keepdims=True)
    carry_ref[...] += jnp.sum(onehot, axis=0, keepdims=True)

    packed = jnp.where(lane == 0, i1.astype(F32), 0.0)
    packed = jnp.where(lane == 1, i2.astype(F32), packed)
    packed = jnp.where(lane == 2, g1, packed)
    packed = jnp.where(lane == 3, g2, packed)
    packed = jnp.where(lane == 4, rank1, packed)
    packed = jnp.where(lane == 5, rank2, packed)
    route_ref[...] = packed
    cnt_ref[...] = jnp.broadcast_to(carry_ref[...], cnt_ref.shape)


def moe_route(h, router):
    t, d = h.shape
    tr = min(ROUTE_TILE, t)
    r_pad = jnp.zeros((d, LANES), F32).at[:, :N_EXPERTS].set(router.astype(F32))
    r_hi, r_lo = _split_bf16(r_pad)
    route, cnt = pl.pallas_call(
        _route_body,
        grid=(t // tr,),
        in_specs=[
            pl.BlockSpec((tr, d), lambda i: (i, 0)),
            pl.BlockSpec((d, LANES), lambda i: (0, 0)),
            pl.BlockSpec((d, LANES), lambda i: (0, 0)),
        ],
        out_specs=[
            pl.BlockSpec((tr, LANES), lambda i: (i, 0)),
            pl.BlockSpec((8, LANES), lambda i: (0, 0)),
        ],
        out_shape=[
            jax.ShapeDtypeStruct((t, LANES), F32),
            jax.ShapeDtypeStruct((8, LANES), F32),
        ],
        scratch_shapes=[pltpu.VMEM((1, LANES), F32)],
        compiler_params=pltpu.CompilerParams(dimension_semantics=("arbitrary",)),
        name="moe_route",
    )(h, r_hi, r_lo)
    return route, cnt


def _row_copy(src_ref, src_row, dst_ref, dst_row, sem):
    return pltpu.make_async_copy(src_ref.at[pl.ds(src_row, 1)], dst_ref.at[pl.ds(dst_row, 1)], sem)


def _scatter_body(pos_ref, h_ref, xs_in_ref, xs_ref, sem, *, ts):
    del xs_in_ref

    def issue(r, c):
        for k in range(TOP_K):
            _row_copy(h_ref, r, xs_ref, pos_ref[0, 0, TOP_K * r + k], sem).start()
        return c

    lax.fori_loop(0, ts, issue, 0)

    def drain(r, c):
        _row_copy(h_ref, 0, xs_ref, 0, sem).wait()
        return c

    lax.fori_loop(0, TOP_K * ts, drain, 0)


def moe_scatter(h, pos, p_rows):
    t, d = h.shape
    ts = min(ROW_TILE, t)
    nt = t // ts
    xs0 = jnp.zeros((p_rows, d), h.dtype)
    return pl.pallas_call(
        functools.partial(_scatter_body, ts=ts),
        grid=(nt,),
        in_specs=[
            pl.BlockSpec((1, 1, TOP_K * ts), lambda i: (i, 0, 0), memory_space=pltpu.SMEM),
            pl.BlockSpec((ts, d), lambda i: (i, 0)),
            pl.BlockSpec(memory_space=pl.ANY),
        ],
        out_specs=pl.BlockSpec(memory_space=pl.ANY),
        out_shape=jax.ShapeDtypeStruct((p_rows, d), h.dtype),
        scratch_shapes=[pltpu.SemaphoreType.DMA(())],
        input_output_aliases={2: 0},
        compiler_params=pltpu.CompilerParams(dimension_semantics=("arbitrary",), disable_bounds_checks=True),
        name="moe_scatter",
    )(pos.reshape(nt, 1, TOP_K * ts), h, xs0)


def _expert_body(te_ref, nu_ref, x_ref, w1_ref, w3_ref, w2_ref, o_ref):
    i = pl.program_id(0)
    j = pl.program_id(1)

    @pl.when(j == 0)
    def _():
        o_ref[...] = jnp.zeros_like(o_ref)

    @pl.when(i < nu_ref[0])
    def _():
        x = x_ref[...].astype(BF16)
        a = jnp.dot(x, w1_ref[0], preferred_element_type=F32)
        b = jnp.dot(x, w3_ref[0], preferred_element_type=F32)
        mid = (a * jax.nn.sigmoid(a) * b).astype(BF16)
        o_ref[...] += jnp.dot(mid, w2_ref[0], preferred_element_type=F32)


def moe_experts(xs, w1, w3, w2, tile_expert, n_used, tm):
    p_rows, d = xs.shape
    n_exp, _, d_exp = w1.shape
    tf = MOE_TF if d_exp % MOE_TF == 0 else d_exp
    nj = d_exp // tf
    n_tiles = p_rows // tm

    def jj(i, j, nu):
        return jnp.where(i < nu[0], j, nj - 1)

    grid_spec = pltpu.PrefetchScalarGridSpec(
        num_scalar_prefetch=2,
        grid=(n_tiles, nj),
        in_specs=[
            pl.BlockSpec((tm, d), lambda i, j, te, nu: (i, 0)),
            pl.BlockSpec((1, d, tf), lambda i, j, te, nu: (te[i], 0, jj(i, j, nu))),
            pl.BlockSpec((1, d, tf), lambda i, j, te, nu: (te[i], 0, jj(i, j, nu))),
            pl.BlockSpec((1, tf, d), lambda i, j, te, nu: (te[i], jj(i, j, nu), 0)),
        ],
        out_specs=pl.BlockSpec((tm, d), lambda i, j, te, nu: (i, 0)),
    )
    return pl.pallas_call(
        _expert_body,
        grid_spec=grid_spec,
        out_shape=jax.ShapeDtypeStruct((p_rows, d), F32),
        compiler_params=pltpu.CompilerParams(
            dimension_semantics=("arbitrary", "arbitrary"),
            vmem_limit_bytes=48 * 1024 * 1024),
        name="moe_experts",
    )(tile_expert, n_used, xs, w1, w3, w2)


def _combine_body(pos_ref, gate_ref, ys_ref, y_ref, buf, sem, *, ts):
    def issue(r, c):
        for k in range(TOP_K):
            _row_copy(ys_ref, pos_ref[0, 0, TOP_K * r + k], buf.at[k], r, sem).start()
        return c

    lax.fori_loop(0, ts, issue, 0)

    def drain(r, c):
        _row_copy(ys_ref, 0, buf.at[0], 0, sem).wait()
        return c

    lax.fori_loop(0, TOP_K * ts, drain, 0)
    g = gate_ref[...]
    y_ref[...] = g[:, 2:3] * buf[0] + g[:, 3:4] * buf[1]


def moe_combine(ys, pos, route):
    t = route.shape[0]
    d = ys.shape[1]
    ts = min(ROW_TILE, t)
    nt = t // ts
    return pl.pallas_call(
        functools.partial(_combine_body, ts=ts),
        grid=(nt,),
        in_specs=[
            pl.BlockSpec((1, 1, TOP_K * ts), lambda i: (i, 0, 0), memory_space=pltpu.SMEM),
            pl.BlockSpec((ts, LANES), lambda i: (i, 0)),
            pl.BlockSpec(memory_space=pl.ANY),
        ],
        out_specs=pl.BlockSpec((ts, d), lambda i: (i, 0)),
        out_shape=jax.ShapeDtypeStruct((t, d), F32),
        scratch_shapes=[pltpu.VMEM((TOP_K, ts, d), F32), pltpu.SemaphoreType.DMA(())],
        compiler_params=pltpu.CompilerParams(dimension_semantics=("arbitrary",), disable_bounds_checks=True),
        name="moe_combine",
    )(pos.reshape(nt, 1, TOP_K * ts), route, ys)


def moe_swiglu(h, router, w1, w3, w2):
    shp = h.shape
    tok = h.reshape(-1, shp[-1])
    t = tok.shape[0]
    tm = min(MOE_TM, t)
    route, cnt = moe_route(tok, router)

    counts = cnt[0, :N_EXPERTS].astype(I32)
    padded = ((counts + tm - 1) // tm) * tm
    ends = jnp.cumsum(padded)
    starts = ends - padded
    expert = route[:, 0:TOP_K].astype(I32)
    rank = route[:, 4:4 + TOP_K].astype(I32)
    start_of = jnp.sum(jnp.where(expert[..., None] == jnp.arange(N_EXPERTS), starts, 0), axis=-1)
    pos = start_of + rank

    n_tiles = (TOP_K * t) // tm + N_EXPERTS
    n_used = (ends[-1] // tm).astype(I32)
    tile_idx = jnp.minimum(jnp.arange(n_tiles, dtype=I32), n_used - 1)
    tile_expert = jnp.sum(tile_idx[:, None] * tm >= ends[None, :], axis=-1).astype(I32)
    tile_expert = jnp.minimum(tile_expert, N_EXPERTS - 1)

    xs = moe_scatter(tok, pos, n_tiles * tm)
    ys = moe_experts(xs, w1.astype(BF16), w3.astype(BF16), w2.astype(BF16), tile_expert, n_used.reshape(1), tm)
    y = moe_combine(ys, pos, route)
    return y.reshape(shp)


GC = 128
GD = GDN_HEAD_DIM
PAD = 8
GDN_UNROLL = 4
GDN_BASE = 16
N_MERGE = 3
M_INCL_F, M_STRICT_F, M_INCL_B, M_STRICT_B, M_EYE, M_BASE, M_OFF0 = 0, 1, 2, 3, 4, 5, 6
N_MASKS = M_OFF0 + N_MERGE


def _bdot(a, b):
    return jnp.dot(a.astype(BF16), b.astype(BF16), preferred_element_type=F32)


def _bdot_nt(a, b):
    return lax.dot_general(a.astype(BF16), b.astype(BF16), (((1,), (1,)), ((), ())), preferred_element_type=F32)


def _gdn_body(alog_ref, dtb_ref, q_ref, k_ref, v_ref, z_ref, ab_ref, cq_ref, ck_ref, cv_ref, onorm_ref,
              o_ref, pq_s, pk_s, pv_s, o_s, b_s, gq_s, gate_s, state_s, mask_s,
              *, n_ctx_chunks, n_chunks, n_heads, unroll):
    h = pl.program_id(1)
    ltot = n_chunks * GC
    lc = n_ctx_chunks * GC

    row = lax.broadcasted_iota(I32, (GC, GC), 0)
    col = lax.broadcasted_iota(I32, (GC, GC), 1)
    mask_s[M_INCL_F] = (row >= col).astype(F32)
    mask_s[M_STRICT_F] = (row > col).astype(F32)
    mask_s[M_INCL_B] = (row <= col).astype(F32)
    mask_s[M_STRICT_B] = (row < col).astype(F32)
    mask_s[M_EYE] = (row == col).astype(F32)
    mask_s[M_BASE] = ((row // GDN_BASE) == (col // GDN_BASE)).astype(F32)
    for lvl in range(N_MERGE):
        s = GDN_BASE << lvl
        mask_s[M_OFF0 + lvl] = (((row // (2 * s)) == (col // (2 * s))) & ((row // s) != (col // s))).astype(F32)

    for src_ref, dst in ((q_ref, pq_s), (k_ref, pk_s), (v_ref, pv_s)):
        dst[pl.ds(0, PAD), :] = jnp.zeros((PAD, GD), F32)
        dst[pl.ds(PAD + ltot, PAD), :] = jnp.zeros((PAD, GD), F32)
        dst[pl.ds(PAD, ltot), :] = src_ref[0].astype(F32)

    for d in range(2):
        a = ab_ref[0, d * 2 * n_heads + h]
        b = ab_ref[0, d * 2 * n_heads + n_heads + h]
        xa = a + dtb_ref[d, h]
        softplus = jnp.maximum(xa, 0.0) + jnp.log1p(jnp.exp(-jnp.abs(xa)))
        g = -jnp.exp(alog_ref[d, h]) * softplus
        beta = 1.0 / (1.0 + jnp.exp(-b))
        tri = mask_s[M_INCL_B if d == 0 else M_INCL_F].astype(BF16)
        g_hi = g.astype(BF16)
        g_lo = (g - g_hi.astype(F32)).astype(BF16)
        gam = (jnp.dot(g_hi, tri, preferred_element_type=F32) + jnp.dot(g_lo, tri, preferred_element_type=F32))
        gate_s[d, 0] = gam
        gate_s[d, 1] = beta
        gate_s[d, 2] = jnp.broadcast_to(jnp.sum(g, axis=-1, keepdims=True), g.shape)

    rloc = lax.broadcasted_iota(I32, (GC, GD), 0)

    def conv_chunk(c, pad_ref, w_ref, normalise, scale):
        t0 = pl.multiple_of(c * GC, GC)
        win = pad_ref[pl.ds(t0, GC + 2 * PAD), :]
        first = (c == 0) | (c == n_ctx_chunks)
        last = (c == n_ctx_chunks - 1) | (c == n_chunks - 1)
        acc = jnp.zeros((GC, GD), F32)
        for j in range(CONV_K):
            sh = CONV_K // 2 - j
            rolled = win if sh == 0 else pltpu.roll(win, sh % (GC + 2 * PAD), axis=0)
            tap = rolled[PAD:PAD + GC]
            if sh > 0:
                tap = jnp.where(first & (rloc < sh), 0.0, tap)
            elif sh < 0:
                tap = jnp.where(last & (rloc >= GC + sh), 0.0, tap)
            acc = acc + tap * w_ref[j:j + 1, :].astype(F32)
        y = acc * jax.nn.sigmoid(acc)
        if normalise:
            y = y * (lax.rsqrt(jnp.sum(y * y, axis=-1, keepdims=True) + EPS) * scale)
        return y

    def prep(i, carry):
        chunks = [jnp.minimum(i * unroll + u, n_chunks - 1) for u in range(unroll)]
        qs = [conv_chunk(c, pq_s, cq_ref, True, GD ** -0.5) for c in chunks]
        ks = [conv_chunk(c, pk_s, ck_ref, True, 1.0) for c in chunks]
        vs = [conv_chunk(c, pv_s, cv_ref, False, 1.0) for c in chunks]
        kqs = [_bdot_nt(jnp.concatenate([k, q], axis=0), k) for k, q in zip(ks, qs)]
        kts = [k.T for k in ks]
        lanes = [(u, d) for u in range(unroll) for d in range(2)]
        lms, dbs, e_cols, kdts = [], [], [], []
        for u, d in lanes:
            c = chunks[u]
            gam_row = gate_s[d, 0, pl.ds(c, 1), :]
            beta_row = gate_s[d, 1, pl.ds(c, 1), :]
            tot_row = gate_s[d, 2, pl.ds(c, 1), :]
            gam_col = jnp.sum(mask_s[M_EYE] * gam_row, axis=1, keepdims=True)
            db = jnp.exp((gam_col - gam_row) * mask_s[M_INCL_F if d == 0 else M_INCL_B]) * beta_row
            dbs.append(db)
            lms.append(kqs[u][:GC] * db * mask_s[M_STRICT_F if d == 0 else M_STRICT_B])
            e_cols.append(jnp.exp(gam_col))
            kdts.append(kts[u] * (jnp.exp(tot_row - gam_row) * beta_row))
        xs = [-lm * mask_s[M_BASE] for lm in lms]
        ts = [mask_s[M_EYE] + x for x in xs]
        for _ in range(3):
            xs = [_bdot(x, x) for x in xs]
            ts = [t + _bdot(t, x) for t, x in zip(ts, xs)]
        for lvl in range(N_MERGE):
            ys = [_bdot(lm * mask_s[M_OFF0 + lvl], t) for lm, t in zip(lms, ts)]
            ts = [t - _bdot(t, y) for t, y in zip(ts, ys)]
        wus = [_bdot(t, jnp.concatenate([ks[u] * e, vs[u]], axis=1))
               for t, e, (u, d) in zip(ts, e_cols, lanes)]
        x12s = []
        for wu, db, kdt, (u, d) in zip(wus, dbs, kdts, lanes):
            qkb = kqs[u][GC:] * db * mask_s[M_INCL_F if d == 0 else M_INCL_B]
            x12s.append(_bdot(jnp.concatenate([kdt, qkb], axis=0), wu))
        for x12, e, (u, d) in zip(x12s, e_cols, lanes):
            t0 = pl.multiple_of(chunks[u] * GC, GC)
            b_s[d, pl.ds(t0, GC), :] = x12[:GD, GD:]
            o_s[d, pl.ds(t0, GC), :] = x12[GD:, GD:]
            q_eff = qs[u] * e - x12[GD:, :GD]
            gq_s[d, pl.ds(pl.multiple_of(2 * t0, 2 * GC), 2 * GC), :] = (
                jnp.concatenate([x12[:GD, :GD], q_eff], axis=0).astype(BF16))
        return carry

    lax.fori_loop(0, -(-n_chunks // unroll), prep, 0)

    state_s[...] = jnp.zeros_like(state_s)

    def scan(s, carry):
        for d in range(2):
            if d == 0:
                c = s
            else:
                c = jnp.where(s < n_ctx_chunks, n_ctx_chunks - 1 - s, n_chunks - 1 - (s - n_ctx_chunks))
            t0 = pl.multiple_of(c * GC, GC)
            st = state_s[d]
            r = jnp.dot(gq_s[d, pl.ds(pl.multiple_of(2 * t0, 2 * GC), 2 * GC), :], st.astype(BF16),
                        preferred_element_type=F32)
            cd = jnp.exp(gate_s[d, 2, pl.ds(c, 1), 0:1])
            state_s[d] = st * cd - r[:GD] + b_s[d, pl.ds(t0, GC), :]
            o_s[d, pl.ds(t0, GC), :] += r[GD:]
        return carry

    lax.fori_loop(0, n_chunks, scan, 0)

    gain = onorm_ref[...].astype(F32)

    def finish(c, carry):
        t0 = pl.multiple_of(lc + c * GC, GC)
        o = o_s[0, pl.ds(t0, GC), :] + o_s[1, pl.ds(t0, GC), :]
        z = z_ref[0, pl.ds(t0, GC), :].astype(F32)
        y = o * lax.rsqrt(jnp.mean(o * o, axis=-1, keepdims=True) + EPS) * gain
        y = y * (z * jax.nn.sigmoid(z))
        o_ref[0, pl.ds(pl.multiple_of(c * GC, GC), GC), :] = y.astype(o_ref.dtype)
        return carry

    lax.fori_loop(0, n_chunks - n_ctx_chunks, finish, 0)


def gdn_core(p_all, ab_rows, conv_w, a_log, dt_bias, out_norm, n_ctx):
    bsz, ltot, _ = p_all.shape
    n_heads = a_log.shape[1]
    n_chunks = ltot // GC
    n_ctx_chunks = n_ctx // GC
    lat = ltot - n_ctx
    n_rows = ab_rows.shape[2]
    unroll = min(GDN_UNROLL, n_chunks)
    body = functools.partial(_gdn_body, n_ctx_chunks=n_ctx_chunks, n_chunks=n_chunks, n_heads=n_heads,
                             unroll=unroll)
    smem = pl.BlockSpec(memory_space=pltpu.SMEM)

    def col(off):
        return pl.BlockSpec((1, ltot, GD), lambda b, h: (b, 0, off * n_heads + h))

    def cw(off):
        return pl.BlockSpec((CONV_K, GD), lambda b, h: (0, off * n_heads + h))

    return pl.pallas_call(
        body,
        grid=(bsz, n_heads),
        in_specs=[smem, smem, col(0), col(1), col(2), col(3),
                  pl.BlockSpec((1, 4 * n_heads, n_rows, GC), lambda b, h: (b, 0, 0, 0)),
                  cw(0), cw(1), cw(2),
                  pl.BlockSpec((1, GD), lambda b, h: (0, 0))],
        out_specs=pl.BlockSpec((1, lat, GD), lambda b, h: (b, 0, h)),
        out_shape=jax.ShapeDtypeStruct((bsz, lat, n_heads * GD), BF16),
        scratch_shapes=[
            pltpu.VMEM((ltot + 2 * PAD, GD), F32),
            pltpu.VMEM((ltot + 2 * PAD, GD), F32),
            pltpu.VMEM((ltot + 2 * PAD, GD), F32),
            pltpu.VMEM((2, ltot, GD), F32),
            pltpu.VMEM((2, ltot, GD), F32),
            pltpu.VMEM((2, 2 * ltot, GD), BF16),
            pltpu.VMEM((2, 3, n_rows, GC), F32),
            pltpu.VMEM((2, GD, GD), F32),
            pltpu.VMEM((N_MASKS, GC, GC), F32),
        ],
        compiler_params=pltpu.CompilerParams(
            dimension_semantics=("arbitrary", "arbitrary"),
            vmem_limit_bytes=56 * 1024 * 1024),
        name="gdn_core",
    )(a_log.astype(F32), dt_bias.astype(F32), p_all, p_all, p_all, p_all, ab_rows,
      conv_w, conv_w, conv_w, out_norm.reshape(1, GD))


def gdn_gate_rows(ab, n_heads):
    bsz, ltot, _ = ab.shape
    n_chunks = ltot // GC
    n_rows = -(-n_chunks // 8) * 8
    t = jnp.transpose(ab.astype(F32), (0, 2, 1)).reshape(bsz, 4 * n_heads, n_chunks, GC)
    return jnp.pad(t, ((0, 0), (0, 0), (0, n_rows - n_chunks), (0, 0)))


def gdn_mixer_pallas(h_lat, h_ctx, w_in, conv_w, a_log, dt_bias, out_norm, w_out):
    n_ctx = h_ctx.shape[1]
    n_heads = a_log.shape[1]
    p_all = jnp.concatenate([h_ctx, h_lat], axis=1) @ w_in
    ab_rows = gdn_gate_rows(p_all[..., 4 * n_heads * GD:], n_heads)
    y = gdn_core(p_all, ab_rows, conv_w, a_log, dt_bias, out_norm, n_ctx)
    return y.astype(F32) @ w_out


def rmsnorm(x, g):
    xf = x.astype(F32)
    y = xf * lax.rsqrt(jnp.mean(xf * xf, axis=-1, keepdims=True) + EPS)
    return y.astype(x.dtype) * g


def l2norm(x):
    xf = x.astype(F32)
    return (xf * lax.rsqrt(jnp.sum(xf * xf, axis=-1, keepdims=True) + EPS)).astype(x.dtype)


def swiglu(h, w1, w3, w2):
    return (jax.nn.silu(h @ w1) * (h @ w3)) @ w2


def multiscale_pool(u, pool_w, pool_scale):
    B, L, _ = u.shape
    cs = jnp.cumsum(u.astype(F32), axis=1)
    cs = jnp.concatenate([jnp.zeros_like(cs[:, :1]), cs], axis=1)
    t = jnp.arange(L)
    means = []
    for gi, w in enumerate(POOL_WINDOWS):
        lo = jnp.clip(t - w // 2, 0, L)
        hi = jnp.clip(t + w // 2, 0, L)
        seg = cs[:, :, gi * POOL_GROUP_DIM:(gi + 1) * POOL_GROUP_DIM]
        means.append((seg[:, hi] - seg[:, lo]) / (hi - lo).astype(F32)[None, :, None])
    pooled = jnp.stack(means, axis=2)
    delta = (pooled - u.reshape(B, L, POOL_GROUPS, POOL_GROUP_DIM).astype(F32)).astype(u.dtype)
    y = jnp.einsum('blgc,gcd->blgd', delta, pool_w) * pool_scale.reshape(POOL_GROUPS, POOL_GROUP_DIM)
    return y.reshape(B, L, POOL_DIM)


def axial_rope(x, rows, cols):
    half = HEAD_DIM // 2
    inv = ROPE_THETA ** (-jnp.arange(0, half, 2, dtype=F32) / half)

    def rot(xa, pos):
        ang = pos[:, None] * inv[None, :]
        cos = jnp.cos(ang)[None, :, None, :].astype(x.dtype)
        sin = jnp.sin(ang)[None, :, None, :].astype(x.dtype)
        x1, x2 = xa[..., :half // 2], xa[..., half // 2:]
        return jnp.concatenate([x1 * cos - x2 * sin, x2 * cos + x1 * sin], axis=-1)

    return jnp.concatenate([rot(x[..., :half], rows), rot(x[..., half:], cols)], axis=-1)


def split_ab(p):
    B, L, _ = p.shape
    u = p[..., :POOL_DIM]
    q = p[..., POOL_DIM:POOL_DIM + Q_DIM].reshape(B, L, N_HEADS, HEAD_DIM)
    k = p[..., POOL_DIM + Q_DIM:POOL_DIM + Q_DIM + KV_DIM].reshape(B, L, N_KV_HEADS, HEAD_DIM)
    v = p[..., POOL_DIM + Q_DIM + KV_DIM:].reshape(B, L, N_KV_HEADS, HEAD_DIM)
    return u, q, k, v


def banded_attention(q, k, v, k_ctx, v_ctx, sinks):
    B, L, _, _ = q.shape
    Lc = k_ctx.shape[1]
    nb = L // BLOCK
    scale = HEAD_DIM ** -0.5
    qb = q.reshape(B, nb, BLOCK, N_KV_HEADS, GQA_GROUP, HEAD_DIM)
    pad = ((0, 0), (BLOCK, BLOCK), (0, 0), (0, 0))
    kp = jnp.pad(k, pad).reshape(B, nb + 2, BLOCK, N_KV_HEADS, HEAD_DIM)
    vp = jnp.pad(v, pad).reshape(B, nb + 2, BLOCK, N_KV_HEADS, HEAD_DIM)
    k_band = jnp.concatenate([kp[:, :-2], kp[:, 1:-1], kp[:, 2:]], axis=2)
    v_band = jnp.concatenate([vp[:, :-2], vp[:, 1:-1], vp[:, 2:]], axis=2)
    s_band = jnp.einsum('bnqhgd,bnkhd->bnhgqk', qb, k_band, preferred_element_type=F32) * scale
    qpos = jnp.arange(nb)[:, None] * BLOCK + jnp.arange(BLOCK)[None, :]
    kpos = (jnp.arange(nb)[:, None] - 1) * BLOCK + jnp.arange(3 * BLOCK)[None, :]
    allowed = ((jnp.abs(kpos[:, None, :] - qpos[:, :, None]) <= WINDOW)
               & (kpos[:, None, :] >= 0) & (kpos[:, None, :] < L))
    s_band = jnp.where(allowed[None, :, None, None], s_band, NEG_INF)
    s_ctx = jnp.einsum('bnqhgd,bkhd->bnhgqk', qb, k_ctx, preferred_element_type=F32) * scale
    sink = sinks.astype(F32).reshape(N_KV_HEADS, GQA_GROUP)
    s_sink = jnp.broadcast_to(sink[None, None, :, :, None, None], s_ctx.shape[:-1] + (1,))
    p = jax.nn.softmax(jnp.concatenate([s_sink, s_ctx, s_band], axis=-1), axis=-1)
    p_ctx = p[..., 1:1 + Lc].astype(v.dtype)
    p_band = p[..., 1 + Lc:].astype(v.dtype)
    o = (jnp.einsum('bnhgqk,bkhd->bnqhgd', p_ctx, v_ctx)
         + jnp.einsum('bnhgqk,bnkhd->bnqhgd', p_band, v_band))
    return o.reshape(B, L, Q_DIM)


def context_attention(q, k, v, sinks):
    B, Lc, _, _ = q.shape
    qc = q.reshape(B, Lc, N_KV_HEADS, GQA_GROUP, HEAD_DIM)
    s = jnp.einsum('bqhgd,bkhd->bhgqk', qc, k, preferred_element_type=F32) * HEAD_DIM ** -0.5
    sink = sinks.astype(F32).reshape(N_KV_HEADS, GQA_GROUP)
    s_sink = jnp.broadcast_to(sink[None, :, :, None, None], s.shape[:-1] + (1,))
    p = jax.nn.softmax(jnp.concatenate([s_sink, s], axis=-1), axis=-1)[..., 1:].astype(v.dtype)
    o = jnp.einsum('bhgqk,bkhd->bqhgd', p, v)
    return o.reshape(B, Lc, Q_DIM)


def pool_swa_mixer(h_lat, h_ctx, w_in, pool_w, pool_scale, sinks, w_out, ctx_out):
    L = h_lat.shape[1]
    rows_n = L // GRID_W
    rows = jnp.repeat(jnp.arange(rows_n, dtype=F32), GRID_W)
    cols = jnp.tile(jnp.arange(GRID_W, dtype=F32), rows_n)
    u_l, q_l, k_l, v_l = split_ab(h_lat @ w_in)
    u_c, q_c, k_c, v_c = split_ab(h_ctx @ w_in)
    q_l = axial_rope(q_l, rows, cols)
    k_l = axial_rope(k_l, rows, cols)
    a_l = banded_attention(q_l, k_l, v_l, k_c, v_c, sinks)
    y_lat = jnp.concatenate([multiscale_pool(u_l, pool_w, pool_scale), a_l], axis=-1) @ w_out
    if not ctx_out:
        return y_lat, None
    a_c = context_attention(q_c, k_c, v_c, sinks)
    y_ctx = jnp.concatenate([multiscale_pool(u_c, pool_w, pool_scale), a_c], axis=-1) @ w_out
    return y_lat, y_ctx


def centred_dwconv(x, w):
    return lax.conv_general_dilated(x, w[:, None, :], window_strides=(1,),
                                    padding=[(CONV_K // 2, CONV_K // 2)],
                                    dimension_numbers=('NWC', 'WIO', 'NWC'),
                                    feature_group_count=x.shape[-1])


def gated_delta_chunked(q, k, v, g, beta, s0):
    B, H, L, _ = q.shape
    DV = v.shape[-1]
    n = L // CHUNK

    def chunks(t):
        return t.astype(F32).reshape((B, H, n, CHUNK) + t.shape[3:])

    q, k, v, g, beta = chunks(q), chunks(k), chunks(v), chunks(g), chunks(beta)
    gam = jnp.cumsum(g, axis=-1)
    idx = jnp.arange(CHUNK)
    incl = idx[:, None] >= idx[None, :]
    strict = idx[:, None] > idx[None, :]
    diff = gam[..., :, None] - gam[..., None, :]
    decay = jnp.where(incl, jnp.exp(jnp.where(incl, diff, 0.0)), 0.0)
    kb = k * beta[..., None]
    m = jnp.where(strict, jnp.einsum('bhnid,bhnjd->bhnij', kb, k) * decay, 0.0)
    a = m + jnp.eye(CHUNK, dtype=F32)
    u = lax.linalg.triangular_solve(a, v * beta[..., None], left_side=True, lower=True, unit_diagonal=True)
    w = lax.linalg.triangular_solve(a, kb * jnp.exp(gam)[..., None], left_side=True, lower=True,
                                    unit_diagonal=True)
    qk = jnp.einsum('bhnid,bhnjd->bhnij', q, k) * decay
    qd = q * jnp.exp(gam)[..., None]
    kd = k * jnp.exp(gam[..., -1:] - gam)[..., None]
    cd = jnp.exp(gam[..., -1])
    xs = tuple(jnp.moveaxis(t, 2, 0) for t in (u, w, qk, qd, kd, cd))

    def step(s, inp):
        u_c, w_c, qk_c, qd_c, kd_c, cd_c = inp
        v_new = u_c - jnp.einsum('bhik,bhkv->bhiv', w_c, s)
        o_c = jnp.einsum('bhik,bhkv->bhiv', qd_c, s) + jnp.einsum('bhij,bhjv->bhiv', qk_c, v_new)
        s = s * cd_c[..., None, None] + jnp.einsum('bhik,bhiv->bhkv', kd_c, v_new)
        return s, o_c

    s_fin, o = lax.scan(step, s0, xs)
    o = jnp.moveaxis(o, 0, 2).reshape(B, H, L, DV)
    return o, s_fin


def gdn_project(h, w_in, conv_w, a_log, dt_bias):
    B, L, _ = h.shape
    p = h @ w_in
    qkv = jax.nn.silu(centred_dwconv(p[..., :3 * GDN_DIM], conv_w))

    def heads(t):
        return t.reshape(B, L, GDN_HEADS, GDN_HEAD_DIM).transpose(0, 2, 1, 3)

    q = l2norm(heads(qkv[..., :GDN_DIM])) * (GDN_HEAD_DIM ** -0.5)
    k = l2norm(heads(qkv[..., GDN_DIM:2 * GDN_DIM]))
    v = heads(qkv[..., 2 * GDN_DIM:])
    z = p[..., 3 * GDN_DIM:4 * GDN_DIM].reshape(B, L, GDN_HEADS, GDN_HEAD_DIM)
    ab = p[..., 4 * GDN_DIM:].astype(F32).reshape(B, L, 2, 2, GDN_HEADS)
    g = -jnp.exp(a_log.astype(F32)) * jax.nn.softplus(ab[:, :, :, 0] + dt_bias.astype(F32))
    beta = jax.nn.sigmoid(ab[:, :, :, 1])
    g = jnp.transpose(g, (2, 0, 3, 1))
    beta = jnp.transpose(beta, (2, 0, 3, 1))
    return q, k, v, z, g, beta


def gdn_output(o, z, out_norm, w_out):
    B, H, L, DV = o.shape
    o = jnp.transpose(o, (0, 2, 1, 3))
    y = rmsnorm(o, out_norm.astype(F32)) * jax.nn.silu(z.astype(F32))
    return y.astype(z.dtype).reshape(B, L, H * DV) @ w_out


def gdn_mixer(h_lat, h_ctx, w_in, conv_w, a_log, dt_bias, out_norm, w_out, ctx_out):
    ql, kl, vl, zl, gl, bl = gdn_project(h_lat, w_in, conv_w, a_log, dt_bias)
    qc, kc, vc, zc, gc, bc = gdn_project(h_ctx, w_in, conv_w, a_log, dt_bias)
    B = h_lat.shape[0]
    s0 = jnp.zeros((B, GDN_HEADS, GDN_HEAD_DIM, GDN_HEAD_DIM), F32)
    outs_l, outs_c = [], []
    for d in range(2):
        rev = (lambda t: jnp.flip(t, axis=2)) if d == 1 else (lambda t: t)
        oc, sc = gated_delta_chunked(rev(qc), rev(kc), rev(vc), rev(gc[d]), rev(bc[d]), s0)
        ol, _ = gated_delta_chunked(rev(ql), rev(kl), rev(vl), rev(gl[d]), rev(bl[d]), sc)
        outs_l.append(rev(ol))
        outs_c.append(rev(oc))
    y_lat = gdn_output(outs_l[0] + outs_l[1], zl, out_norm, w_out)
    if not ctx_out:
        return y_lat, None
    y_ctx = gdn_output(outs_c[0] + outs_c[1], zc, out_norm, w_out)
    return y_lat, y_ctx


def run_layer(layer, x_lat, x_ctx, c, c_ctx, p, last):
    mod = jax.nn.silu(c) @ p['mod_w'] + p['mod_b']
    mod_c = jax.nn.silu(c_ctx) @ p['mod_w'] + p['mod_b']
    sh_m, sc_m, g_m, sh_f, sc_f, g_f = [t[:, None, :] for t in jnp.split(mod, 6, axis=-1)]
    csh_m, csc_m, cg_m, csh_f, csc_f, cg_f = jnp.split(mod_c, 6)
    h_lat = rmsnorm(x_lat, p['mix_pre']) * (1.0 + sc_m) + sh_m
    h_ctx = rmsnorm(x_ctx, p['mix_pre']) * (1.0 + csc_m) + csh_m
    if layer % 2 == 0:
        y_lat, y_ctx = pool_swa_mixer(h_lat, h_ctx, p['w_in'], p['pool_w'], p['pool_scale'], p['sinks'],
                                      p['w_out'], not last)

        def channel_mixer(h):
            return swiglu(h, p['ffn_w1'], p['ffn_w3'], p['ffn_w2'])
    else:
        assert last
        y_lat = gdn_mixer_pallas(h_lat, h_ctx, p['w_in'], p['conv_w'], p['a_log'], p['dt_bias'],
                                 p['out_norm'], p['w_out'])
        y_ctx = None

        def channel_mixer(h):
            return moe_swiglu(h, p['router'], p['moe_w1'], p['moe_w3'], p['moe_w2'])
    x_lat = x_lat + g_m * rmsnorm(y_lat, p['mix_post'])
    h = rmsnorm(x_lat, p['ffn_pre']) * (1.0 + sc_f) + sh_f
    x_lat = x_lat + g_f * rmsnorm(channel_mixer(h), p['ffn_post'])
    if not last:
        x_ctx = x_ctx + cg_m * rmsnorm(y_ctx, p['mix_post'])
        hc = rmsnorm(x_ctx, p['ffn_pre']) * (1.0 + csc_f) + csh_f
        x_ctx = x_ctx + cg_f * rmsnorm(channel_mixer(hc), p['ffn_post'])
    return x_lat, x_ctx


def kernel(x, c, ctx, c_ctx, l0_mod_w, l0_mod_b, l0_mix_pre, l0_mix_post, l0_ffn_pre, l0_ffn_post, l0_w_in, l0_pool_w, l0_pool_scale, l0_sinks, l0_w_out, l0_ffn_w1, l0_ffn_w3, l0_ffn_w2, l1_mod_w, l1_mod_b, l1_mix_pre, l1_mix_post, l1_ffn_pre, l1_ffn_post, l1_w_in, l1_conv_w, l1_a_log, l1_dt_bias, l1_out_norm, l1_w_out, l1_router, l1_moe_w1, l1_moe_w3, l1_moe_w2):
    layers = [
        dict(mod_w=l0_mod_w, mod_b=l0_mod_b, mix_pre=l0_mix_pre, mix_post=l0_mix_post,
             ffn_pre=l0_ffn_pre, ffn_post=l0_ffn_post, w_in=l0_w_in, pool_w=l0_pool_w,
             pool_scale=l0_pool_scale, sinks=l0_sinks, w_out=l0_w_out,
             ffn_w1=l0_ffn_w1, ffn_w3=l0_ffn_w3, ffn_w2=l0_ffn_w2),
        dict(mod_w=l1_mod_w, mod_b=l1_mod_b, mix_pre=l1_mix_pre, mix_post=l1_mix_post,
             ffn_pre=l1_ffn_pre, ffn_post=l1_ffn_post, w_in=l1_w_in, conv_w=l1_conv_w,
             a_log=l1_a_log, dt_bias=l1_dt_bias, out_norm=l1_out_norm, w_out=l1_w_out,
             router=l1_router, moe_w1=l1_moe_w1, moe_w3=l1_moe_w3, moe_w2=l1_moe_w2),
    ]
    x_lat, x_ctx = x, ctx
    for layer in range(2):
        x_lat, x_ctx = run_layer(layer, x_lat, x_ctx, c, c_ctx, layers[layer], layer == 1)
    return x_lat
```

```python
import functools
import math

import jax
import jax.numpy as jnp
from jax import lax
from jax.experimental import pallas as pl
from jax.experimental.pallas import tpu as pltpu

F32 = jnp.float32
BF16 = jnp.bfloat16
I32 = jnp.int32

LANES = 128
D_MODEL = 1024
GRID_W = 64
EPS = 1e-6
NEG_INF = -1e30

POOL_GROUPS = 4
POOL_GROUP_DIM = 128
POOL_DIM = POOL_GROUPS * POOL_GROUP_DIM
POOL_WINDOWS = (2, 4, 8, 16)
HEAD_DIM = 64
N_HEADS = 8
N_KV_HEADS = 2
GQA_GROUP = N_HEADS // N_KV_HEADS
Q_DIM = N_HEADS * HEAD_DIM
KV_DIM = N_KV_HEADS * HEAD_DIM
WINDOW = 128
BLOCK = 128
ROPE_THETA = 10000.0

GDN_HEADS = 8
GDN_HEAD_DIM = 128
GDN_DIM = GDN_HEADS * GDN_HEAD_DIM
CONV_K = 5
CHUNK = 64

N_EXPERTS = 8
TOP_K = 2

ROUTE_TILE = 512
MOE_TM = 512
MOE_TF = 512
ROW_TILE = 256


def _split_bf16(a):
    hi = a.astype(BF16)
    lo = (a - hi.astype(F32)).astype(BF16)
    return hi, lo


def _route_body(h_ref, rhi_ref, rlo_ref, route_ref, cnt_ref, carry_ref):
    i = pl.program_id(0)

    @pl.when(i == 0)
    def _():
        carry_ref[...] = jnp.zeros_like(carry_ref)

    h_hi, h_lo = _split_bf16(h_ref[...])
    r_hi = rhi_ref[...]
    r_lo = rlo_ref[...]
    logits = (jnp.dot(h_hi, r_hi, preferred_element_type=F32)
              + jnp.dot(h_hi, r_lo, preferred_element_type=F32)
              + jnp.dot(h_lo, r_hi, preferred_element_type=F32))
    tr = logits.shape[0]
    lane = lax.broadcasted_iota(I32, (tr, LANES), 1)
    logits = jnp.where(lane < N_EXPERTS, logits, -jnp.inf)
    m1 = jnp.max(logits, axis=-1, keepdims=True)
    i1 = jnp.min(jnp.where(logits == m1, lane, LANES), axis=-1, keepdims=True)
    rest = jnp.where(lane == i1, -jnp.inf, logits)
    m2 = jnp.max(rest, axis=-1, keepdims=True)
    i2 = jnp.min(jnp.where(rest == m2, lane, LANES), axis=-1, keepdims=True)
    e2 = jnp.exp(m2 - m1)
    g1 = 1.0 / (1.0 + e2)
    g2 = e2 / (1.0 + e2)

    onehot = ((lane == i1) | (lane == i2)).astype(F32)
    row = lax.broadcasted_iota(I32, (tr, tr), 0)
    col = lax.broadcasted_iota(I32, (tr, tr), 1)
    strict = (row > col).astype(BF16)
    before = jnp.dot(strict, onehot.astype(BF16), preferred_element_type=F32) + carry_ref[...]
    rank1 = jnp.sum(jnp.where(lane == i1, before, 0.0), axis=-1, keepdims=True)
    rank2 = jnp.sum(jnp.where(lane == i2, before, 0.0), axis=-1, keepdims=True)
    carry_ref[...] += jnp.sum(onehot, axis=0, keepdims=True)

    packed = jnp.where(lane == 0, i1.astype(F32), 0.0)
    packed = jnp.where(lane == 1, i2.astype(F32), packed)
    packed = jnp.where(lane == 2, g1, packed)
    packed = jnp.where(lane == 3, g2, packed)
    packed = jnp.where(lane == 4, rank1, packed)
    packed = jnp.where(lane == 5, rank2, packed)
    route_ref[...] = packed
    cnt_ref[...] = jnp.broadcast_to(carry_ref[...], cnt_ref.shape)


def moe_route(h, router):
    t, d = h.shape
    tr = min(ROUTE_TILE, t)
    r_pad = jnp.zeros((d, LANES), F32).at[:, :N_EXPERTS].set(router.astype(F32))
    r_hi, r_lo = _split_bf16(r_pad)
    route, cnt = pl.pallas_call(
        _route_body,
        grid=(t // tr,),
        in_specs=[
            pl.BlockSpec((tr, d), lambda i: (i, 0)),
            pl.BlockSpec((d, LANES), lambda i: (0, 0)),
            pl.BlockSpec((d, LANES), lambda i: (0, 0)),
        ],
        out_specs=[
            pl.BlockSpec((tr, LANES), lambda i: (i, 0)),
            pl.BlockSpec((8, LANES), lambda i: (0, 0)),
        ],
        out_shape=[
            jax.ShapeDtypeStruct((t, LANES), F32),
            jax.ShapeDtypeStruct((8, LANES), F32),
        ],
        scratch_shapes=[pltpu.VMEM((1, LANES), F32)],
        compiler_params=pltpu.CompilerParams(dimension_semantics=("arbitrary",)),
        name="moe_route",
    )(h, r_hi, r_lo)
    return route, cnt


def _row_copy(src_ref, src_row, dst_ref, dst_row, sem):
    return pltpu.make_async_copy(src_ref.at[pl.ds(src_row, 1)], dst_ref.at[pl.ds(dst_row, 1)], sem)


def _scatter_body(pos_ref, h_ref, xs_in_ref, xs_ref, sem, *, ts):
    del xs_in_ref

    def issue(r, c):
        for k in range(TOP_K):
            _row_copy(h_ref, r, xs_ref, pos_ref[0, 0, TOP_K * r + k], sem).start()
        return c

    lax.fori_loop(0, ts, issue, 0)

    def drain(r, c):
        _row_copy(h_ref, 0, xs_ref, 0, sem).wait()
        return c

    lax.fori_loop(0, TOP_K * ts, drain, 0)


def moe_scatter(h, pos, p_rows):
    t, d = h.shape
    ts = min(ROW_TILE, t)
    nt = t // ts
    xs0 = jnp.zeros((p_rows, d), h.dtype)
    return pl.pallas_call(
        functools.partial(_scatter_body, ts=ts),
        grid=(nt,),
        in_specs=[
            pl.BlockSpec((1, 1, TOP_K * ts), lambda i: (i, 0, 0), memory_space=pltpu.SMEM),
            pl.BlockSpec((ts, d), lambda i: (i, 0)),
            pl.BlockSpec(memory_space=pl.ANY),
        ],
        out_specs=pl.BlockSpec(memory_space=pl.ANY),
        out_shape=jax.ShapeDtypeStruct((p_rows, d), h.dtype),
        scratch_shapes=[pltpu.SemaphoreType.DMA(())],
        input_output_aliases={2: 0},
        compiler_params=pltpu.CompilerParams(dimension_semantics=("arbitrary",), disable_bounds_checks=True),
        name="moe_scatter",
    )(pos.reshape(nt, 1, TOP_K * ts), h, xs0)


def _expert_body(te_ref, nu_ref, x_ref, w1_ref, w3_ref, w2_ref, o_ref):
    i = pl.program_id(0)
    j = pl.program_id(1)

    @pl.when(j == 0)
    def _():
        o_ref[...] = jnp.zeros_like(o_ref)

    @pl.when(i < nu_ref[0])
    def _():
        x = x_ref[...].astype(BF16)
        a = jnp.dot(x, w1_ref[0], preferred_element_type=F32)
        b = jnp.dot(x, w3_ref[0], preferred_element_type=F32)
        mid = (a * jax.nn.sigmoid(a) * b).astype(BF16)
        o_ref[...] += jnp.dot(mid, w2_ref[0], preferred_element_type=F32)


def moe_experts(xs, w1, w3, w2, tile_expert, n_used, tm):
    p_rows, d = xs.shape
    n_exp, _, d_exp = w1.shape
    tf = MOE_TF if d_exp % MOE_TF == 0 else d_exp
    nj = d_exp // tf
    n_tiles = p_rows // tm

    def jj(i, j, nu):
        return jnp.where(i < nu[0], j, nj - 1)

    grid_spec = pltpu.PrefetchScalarGridSpec(
        num_scalar_prefetch=2,
        grid=(n_tiles, nj),
        in_specs=[
            pl.BlockSpec((tm, d), lambda i, j, te, nu: (i, 0)),
            pl.BlockSpec((1, d, tf), lambda i, j, te, nu: (te[i], 0, jj(i, j, nu))),
            pl.BlockSpec((1, d, tf), lambda i, j, te, nu: (te[i], 0, jj(i, j, nu))),
            pl.BlockSpec((1, tf, d), lambda i, j, te, nu: (te[i], jj(i, j, nu), 0)),
        ],
        out_specs=pl.BlockSpec((tm, d), lambda i, j, te, nu: (i, 0)),
    )
    return pl.pallas_call(
        _expert_body,
        grid_spec=grid_spec,
        out_shape=jax.ShapeDtypeStruct((p_rows, d), F32),
        compiler_params=pltpu.CompilerParams(
            dimension_semantics=("arbitrary", "arbitrary"),
            vmem_limit_bytes=48 * 1024 * 1024),
        name="moe_experts",
    )(tile_expert, n_used, xs, w1, w3, w2)


def _combine_body(pos_ref, gate_ref, ys_ref, x_ref, gain_ref, mgate_ref, y_ref, buf, sem, *, ts):
    def issue(r, c):
        for k in range(TOP_K):
            _row_copy(ys_ref, pos_ref[0, 0, TOP_K * r + k], buf.at[k], r, sem).start()
        return c

    lax.fori_loop(0, ts, issue, 0)

    def drain(r, c):
        _row_copy(ys_ref, 0, buf.at[0], 0, sem).wait()
        return c

    lax.fori_loop(0, TOP_K * ts, drain, 0)
    g = gate_ref[...]
    y = g[:, 2:3] * buf[0] + g[:, 3:4] * buf[1]
    y_ref[...] = x_ref[...] + mgate_ref[0] * _rms(y, gain_ref[...])


def moe_combine(ys, pos, route, x, gain, mod, tiles_per_batch):
    t = route.shape[0]
    d = ys.shape[1]
    ts = min(ROW_TILE, t)
    nt = t // ts
    return pl.pallas_call(
        functools.partial(_combine_body, ts=ts),
        grid=(nt,),
        in_specs=[
            pl.BlockSpec((1, 1, TOP_K * ts), lambda i: (i, 0, 0), memory_space=pltpu.SMEM),
            pl.BlockSpec((ts, LANES), lambda i: (i, 0)),
            pl.BlockSpec(memory_space=pl.ANY),
            pl.BlockSpec((ts, d), lambda i: (i, 0)),
            pl.BlockSpec((1, d), lambda i: (0, 0)),
            _mod_spec(MOD_GATE_F, tiles_per_batch, 0, d),
        ],
        out_specs=pl.BlockSpec((ts, d), lambda i: (i, 0)),
        out_shape=jax.ShapeDtypeStruct((t, d), F32),
        scratch_shapes=[pltpu.VMEM((TOP_K, ts, d), F32), pltpu.SemaphoreType.DMA(())],
        compiler_params=pltpu.CompilerParams(dimension_semantics=("arbitrary",), disable_bounds_checks=True),
        name="moe_combine",
    )(pos.reshape(nt, 1, TOP_K * ts), route, ys, x, gain.reshape(1, d), mod)


def moe_layer(tok, router, w1, w3, w2, x, gain, mod, tiles_per_batch):
    t = tok.shape[0]
    tm = min(MOE_TM, t)
    route, cnt = moe_route(tok, router)

    counts = cnt[0, :N_EXPERTS].astype(I32)
    padded = ((counts + tm - 1) // tm) * tm
    ends = jnp.cumsum(padded)
    starts = ends - padded
    expert = route[:, 0:TOP_K].astype(I32)
    rank = route[:, 4:4 + TOP_K].astype(I32)
    start_of = jnp.sum(jnp.where(expert[..., None] == jnp.arange(N_EXPERTS), starts, 0), axis=-1)
    pos = start_of + rank

    n_tiles = (TOP_K * t) // tm + N_EXPERTS
    n_used = (ends[-1] // tm).astype(I32)
    tile_idx = jnp.minimum(jnp.arange(n_tiles, dtype=I32), n_used - 1)
    tile_expert = jnp.sum(tile_idx[:, None] * tm >= ends[None, :], axis=-1).astype(I32)
    tile_expert = jnp.minimum(tile_expert, N_EXPERTS - 1)

    xs = moe_scatter(tok, pos, n_tiles * tm)
    ys = moe_experts(xs, w1.astype(BF16), w3.astype(BF16), w2.astype(BF16), tile_expert, n_used.reshape(1), tm)
    return moe_combine(ys, pos, route, x, gain, mod, tiles_per_batch)


GC = 128
GD = GDN_HEAD_DIM
PAD = 8
GDN_UNROLL = 4
GDN_BASE = 16
N_MERGE = 3
M_INCL_F, M_STRICT_F, M_INCL_B, M_STRICT_B, M_EYE, M_BASE, M_OFF0 = 0, 1, 2, 3, 4, 5, 6
N_MASKS = M_OFF0 + N_MERGE


def _bdot(a, b):
    return jnp.dot(a.astype(BF16), b.astype(BF16), preferred_element_type=F32)


def _bdot_nt(a, b):
    return lax.dot_general(a.astype(BF16), b.astype(BF16), (((1,), (1,)), ((), ())), preferred_element_type=F32)


def _gdn_body(alog_ref, dtb_ref, q_ref, k_ref, v_ref, z_ref, ab_ref, cq_ref, ck_ref, cv_ref, onorm_ref,
              o_ref, pq_s, pk_s, pv_s, o_s, b_s, gq_s, gate_s, state_s, mask_s,
              *, n_ctx_chunks, n_chunks, n_heads, unroll):
    h = pl.program_id(1)
    ltot = n_chunks * GC
    lc = n_ctx_chunks * GC

    row = lax.broadcasted_iota(I32, (GC, GC), 0)
    col = lax.broadcasted_iota(I32, (GC, GC), 1)
    mask_s[M_INCL_F] = (row >= col).astype(F32)
    mask_s[M_STRICT_F] = (row > col).astype(F32)
    mask_s[M_INCL_B] = (row <= col).astype(F32)
    mask_s[M_STRICT_B] = (row < col).astype(F32)
    mask_s[M_EYE] = (row == col).astype(F32)
    mask_s[M_BASE] = ((row // GDN_BASE) == (col // GDN_BASE)).astype(F32)
    for lvl in range(N_MERGE):
        s = GDN_BASE << lvl
        mask_s[M_OFF0 + lvl] = (((row // (2 * s)) == (col // (2 * s))) & ((row // s) != (col // s))).astype(F32)

    for src_ref, dst in ((q_ref, pq_s), (k_ref, pk_s), (v_ref, pv_s)):
        dst[pl.ds(0, PAD), :] = jnp.zeros((PAD, GD), F32)
        dst[pl.ds(PAD + ltot, PAD), :] = jnp.zeros((PAD, GD), F32)
        dst[pl.ds(PAD, ltot), :] = src_ref[0].astype(F32)

    for d in range(2):
        a = ab_ref[0, d * 2 * n_heads + h]
        b = ab_ref[0, d * 2 * n_heads + n_heads + h]
        xa = a + dtb_ref[d, h]
        softplus = jnp.maximum(xa, 0.0) + jnp.log1p(jnp.exp(-jnp.abs(xa)))
        g = -jnp.exp(alog_ref[d, h]) * softplus
        beta = 1.0 / (1.0 + jnp.exp(-b))
        tri = mask_s[M_INCL_B if d == 0 else M_INCL_F].astype(BF16)
        g_hi = g.astype(BF16)
        g_lo = (g - g_hi.astype(F32)).astype(BF16)
        gam = (jnp.dot(g_hi, tri, preferred_element_type=F32) + jnp.dot(g_lo, tri, preferred_element_type=F32))
        gate_s[d, 0] = gam
        gate_s[d, 1] = beta
        gate_s[d, 2] = jnp.broadcast_to(jnp.sum(g, axis=-1, keepdims=True), g.shape)

    rloc = lax.broadcasted_iota(I32, (GC, GD), 0)

    def conv_chunk(c, pad_ref, w_ref, normalise, scale):
        t0 = pl.multiple_of(c * GC, GC)
        win = pad_ref[pl.ds(t0, GC + 2 * PAD), :]
        first = (c == 0) | (c == n_ctx_chunks)
        last = (c == n_ctx_chunks - 1) | (c == n_chunks - 1)
        acc = jnp.zeros((GC, GD), F32)
        for j in range(CONV_K):
            sh = CONV_K // 2 - j
            rolled = win if sh == 0 else pltpu.roll(win, sh % (GC + 2 * PAD), axis=0)
            tap = rolled[PAD:PAD + GC]
            if sh > 0:
                tap = jnp.where(first & (rloc < sh), 0.0, tap)
            elif sh < 0:
                tap = jnp.where(last & (rloc >= GC + sh), 0.0, tap)
            acc = acc + tap * w_ref[j:j + 1, :].astype(F32)
        y = acc * jax.nn.sigmoid(acc)
        if normalise:
            y = y * (lax.rsqrt(jnp.sum(y * y, axis=-1, keepdims=True) + EPS) * scale)
        return y

    def prep(i, carry):
        chunks = [jnp.minimum(i * unroll + u, n_chunks - 1) for u in range(unroll)]
        qs = [conv_chunk(c, pq_s, cq_ref, True, GD ** -0.5) for c in chunks]
        ks = [conv_chunk(c, pk_s, ck_ref, True, 1.0) for c in chunks]
        vs = [conv_chunk(c, pv_s, cv_ref, False, 1.0) for c in chunks]
        kqs = [_bdot_nt(jnp.concatenate([k, q], axis=0), k) for k, q in zip(ks, qs)]
        kts = [k.T for k in ks]
        lanes = [(u, d) for u in range(unroll) for d in range(2)]
        lms, dbs, e_cols, kdts = [], [], [], []
        for u, d in lanes:
            c = chunks[u]
            gam_row = gate_s[d, 0, pl.ds(c, 1), :]
            beta_row = gate_s[d, 1, pl.ds(c, 1), :]
            tot_row = gate_s[d, 2, pl.ds(c, 1), :]
            gam_col = jnp.sum(mask_s[M_EYE] * gam_row, axis=1, keepdims=True)
            db = jnp.exp((gam_col - gam_row) * mask_s[M_INCL_F if d == 0 else M_INCL_B]) * beta_row
            dbs.append(db)
            lms.append(kqs[u][:GC] * db * mask_s[M_STRICT_F if d == 0 else M_STRICT_B])
            e_cols.append(jnp.exp(gam_col))
            kdts.append(kts[u] * (jnp.exp(tot_row - gam_row) * beta_row))
        xs = [-lm * mask_s[M_BASE] for lm in lms]
        ts = [mask_s[M_EYE] + x for x in xs]
        for _ in range(3):
            xs = [_bdot(x, x) for x in xs]
            ts = [t + _bdot(t, x) for t, x in zip(ts, xs)]
        for lvl in range(N_MERGE):
            ys = [_bdot(lm * mask_s[M_OFF0 + lvl], t) for lm, t in zip(lms, ts)]
            ts = [t - _bdot(t, y) for t, y in zip(ts, ys)]
        wus = [_bdot(t, jnp.concatenate([ks[u] * e, vs[u]], axis=1))
               for t, e, (u, d) in zip(ts, e_cols, lanes)]
        x12s = []
        for wu, db, kdt, (u, d) in zip(wus, dbs, kdts, lanes):
            qkb = kqs[u][GC:] * db * mask_s[M_INCL_F if d == 0 else M_INCL_B]
            x12s.append(_bdot(jnp.concatenate([kdt, qkb], axis=0), wu))
        for x12, e, (u, d) in zip(x12s, e_cols, lanes):
            t0 = pl.multiple_of(chunks[u] * GC, GC)
            b_s[d, pl.ds(t0, GC), :] = x12[:GD, GD:]
            o_s[d, pl.ds(t0, GC), :] = x12[GD:, GD:]
            q_eff = qs[u] * e - x12[GD:, :GD]
            gq_s[d, pl.ds(pl.multiple_of(2 * t0, 2 * GC), 2 * GC), :] = (
                jnp.concatenate([x12[:GD, :GD], q_eff], axis=0).astype(BF16))
        return carry

    lax.fori_loop(0, -(-n_chunks // unroll), prep, 0)

    state_s[...] = jnp.zeros_like(state_s)

    def scan(s, carry):
        for d in range(2):
            if d == 0:
                c = s
            else:
                c = jnp.where(s < n_ctx_chunks, n_ctx_chunks - 1 - s, n_chunks - 1 - (s - n_ctx_chunks))
            t0 = pl.multiple_of(c * GC, GC)
            st = state_s[d]
            r = jnp.dot(gq_s[d, pl.ds(pl.multiple_of(2 * t0, 2 * GC), 2 * GC), :], st.astype(BF16),
                        preferred_element_type=F32)
            cd = jnp.exp(gate_s[d, 2, pl.ds(c, 1), 0:1])
            state_s[d] = st * cd - r[:GD] + b_s[d, pl.ds(t0, GC), :]
            o_s[d, pl.ds(t0, GC), :] += r[GD:]
        return carry

    lax.fori_loop(0, n_chunks, scan, 0)

    gain = onorm_ref[...].astype(F32)

    def finish(c, carry):
        t0 = pl.multiple_of(lc + c * GC, GC)
        o = o_s[0, pl.ds(t0, GC), :] + o_s[1, pl.ds(t0, GC), :]
        z = z_ref[0, pl.ds(t0, GC), :].astype(F32)
        y = o * lax.rsqrt(jnp.mean(o * o, axis=-1, keepdims=True) + EPS) * gain
        y = y * (z * jax.nn.sigmoid(z))
        o_ref[0, pl.ds(pl.multiple_of(c * GC, GC), GC), :] = y.astype(o_ref.dtype)
        return carry

    lax.fori_loop(0, n_chunks - n_ctx_chunks, finish, 0)


def gdn_core(p_all, ab_rows, conv_w, a_log, dt_bias, out_norm, n_ctx):
    bsz, ltot, _ = p_all.shape
    n_heads = a_log.shape[1]
    n_chunks = ltot // GC
    n_ctx_chunks = n_ctx // GC
    lat = ltot - n_ctx
    n_rows = ab_rows.shape[2]
    unroll = min(GDN_UNROLL, n_chunks)
    body = functools.partial(_gdn_body, n_ctx_chunks=n_ctx_chunks, n_chunks=n_chunks, n_heads=n_heads,
                             unroll=unroll)
    smem = pl.BlockSpec(memory_space=pltpu.SMEM)

    def col(off):
        return pl.BlockSpec((1, ltot, GD), lambda b, h: (b, 0, off * n_heads + h))

    def cw(off):
        return pl.BlockSpec((CONV_K, GD), lambda b, h: (0, off * n_heads + h))

    return pl.pallas_call(
        body,
        grid=(bsz, n_heads),
        in_specs=[smem, smem, col(0), col(1), col(2), col(3),
                  pl.BlockSpec((1, 4 * n_heads, n_rows, GC), lambda b, h: (b, 0, 0, 0)),
                  cw(0), cw(1), cw(2),
                  pl.BlockSpec((1, GD), lambda b, h: (0, 0))],
        out_specs=pl.BlockSpec((1, lat, GD), lambda b, h: (b, 0, h)),
        out_shape=jax.ShapeDtypeStruct((bsz, lat, n_heads * GD), BF16),
        scratch_shapes=[
            pltpu.VMEM((ltot + 2 * PAD, GD), F32),
            pltpu.VMEM((ltot + 2 * PAD, GD), F32),
            pltpu.VMEM((ltot + 2 * PAD, GD), F32),
            pltpu.VMEM((2, ltot, GD), F32),
            pltpu.VMEM((2, ltot, GD), F32),
            pltpu.VMEM((2, 2 * ltot, GD), BF16),
            pltpu.VMEM((2, 3, n_rows, GC), F32),
            pltpu.VMEM((2, GD, GD), F32),
            pltpu.VMEM((N_MASKS, GC, GC), F32),
        ],
        compiler_params=pltpu.CompilerParams(
            dimension_semantics=("arbitrary", "arbitrary"),
            vmem_limit_bytes=56 * 1024 * 1024),
        name="gdn_core",
    )(a_log.astype(F32), dt_bias.astype(F32), p_all, p_all, p_all, p_all, ab_rows,
      conv_w, conv_w, conv_w, out_norm.reshape(1, GD))


def gdn_gate_rows(ab, n_heads):
    bsz, ltot, _ = ab.shape
    n_chunks = ltot // GC
    n_rows = -(-n_chunks // 8) * 8
    t = jnp.transpose(ab.astype(F32), (0, 2, 1)).reshape(bsz, 4 * n_heads, n_chunks, GC)
    return jnp.pad(t, ((0, 0), (0, 0), (0, n_rows - n_chunks), (0, 0)))


TM = 256
FFN_TF = 256
MOD_TN = 512
MOD_SHIFT_M, MOD_SCALE_M, MOD_GATE_M, MOD_SHIFT_F, MOD_SCALE_F, MOD_GATE_F = range(6)
N_MOD = 6
RESIDENT = dict(pipeline_mode=pl.Buffered(1))
VMEM_LIMIT = 56 * 1024 * 1024


def _rms(x, gain):
    return x * lax.rsqrt(jnp.mean(x * x, axis=-1, keepdims=True) + EPS) * gain


def _mod_spec(k, tiles_per_batch, ctx_tiles, d):
    def index(i, *_):
        seg = jnp.where(i % tiles_per_batch >= ctx_tiles, 1, 0)
        return ((i // tiles_per_batch) * 2 + seg) * N_MOD + k, 0, 0
    return pl.BlockSpec((1, 1, d), index)


def _const_spec(shape):
    return pl.BlockSpec(shape, lambda i, *_: (0,) * len(shape), **RESIDENT)


def _mod_body(c_ref, w_ref, b_ref, o_ref):
    c = c_ref[...]
    s = (c * jax.nn.sigmoid(c)).astype(BF16)
    o_ref[...] = jnp.dot(s, w_ref[...].astype(BF16), preferred_element_type=F32) + b_ref[...]


def modulation(c, c_ctx, mod_w, mod_b):
    bsz, d = c.shape
    n = mod_w.shape[1]
    rows = -(-(bsz + 1) // 8) * 8
    cc = jnp.zeros((rows, d), F32).at[:bsz].set(c).at[bsz].set(c_ctx)
    out = pl.pallas_call(
        _mod_body,
        grid=(n // MOD_TN,),
        in_specs=[pl.BlockSpec((rows, d), lambda j: (0, 0)),
                  pl.BlockSpec((d, MOD_TN), lambda j: (0, j)),
                  pl.BlockSpec((1, MOD_TN), lambda j: (0, j))],
        out_specs=pl.BlockSpec((rows, MOD_TN), lambda j: (0, j)),
        out_shape=jax.ShapeDtypeStruct((rows, n), F32),
        compiler_params=pltpu.CompilerParams(dimension_semantics=("arbitrary",)),
        name="modulation",
    )(cc, mod_w, mod_b.reshape(1, n))
    lat = out[:bsz].reshape(bsz, 1, N_MOD, d)
    ctx = jnp.broadcast_to(out[bsz].reshape(1, 1, N_MOD, d), (bsz, 1, N_MOD, d))
    return jnp.concatenate([ctx, lat], axis=1).reshape(bsz * 2 * N_MOD, 1, d)


def _pre_mm_body(x_ref, gain_ref, sh_ref, sc_ref, w_ref, cos_ref, sin_ref, o_ref, *, rope_lo, rope_hi, q_hi):
    h = (_rms(x_ref[...], gain_ref[...]) * (1.0 + sc_ref[0]) + sh_ref[0]).astype(BF16)
    acc = jnp.dot(h, w_ref[...], preferred_element_type=F32)
    lane = lax.broadcasted_iota(I32, (acc.shape[0], LANES), 1)
    first_half = (lane % (HEAD_DIM // 2)) < (HEAD_DIM // 4)
    for s in range(0, acc.shape[1], LANES):
        blk = acc[:, s:s + LANES]
        if rope_lo <= s < rope_hi:
            partner = jnp.where(first_half, pltpu.roll(blk, LANES - HEAD_DIM // 4, 1),
                                pltpu.roll(blk, HEAD_DIM // 4, 1))
            blk = blk * cos_ref[...] + partner * sin_ref[...]
            if s < q_hi:
                blk = blk * (HEAD_DIM ** -0.5)
        o_ref[:, s:s + LANES] = blk.astype(o_ref.dtype)


def rope_tables(n_ctx, n_lat):
    half = HEAD_DIM // 2
    inv = ROPE_THETA ** (-jnp.arange(0, half, 2, dtype=F32) / half)
    t = jnp.arange(n_lat)
    pos = jnp.stack([(t // GRID_W).astype(F32), (t % GRID_W).astype(F32)], axis=1)
    lane = jnp.arange(LANES)
    part = (lane % HEAD_DIM) // half
    freq = lane % (half // 2)
    ang = pos[:, part] * inv[freq][None, :]
    sign = jnp.where((lane % half) < half // 2, -1.0, 1.0)
    cos = jnp.concatenate([jnp.ones((n_ctx, LANES), F32), jnp.cos(ang)], axis=0)
    sin = jnp.concatenate([jnp.zeros((n_ctx, LANES), F32), jnp.sin(ang) * sign], axis=0)
    return cos, sin


def pre_mm(x, gain, mod, w, cos, sin, tiles_per_batch, ctx_tiles):
    r, d = x.shape
    n = w.shape[1]
    body = functools.partial(_pre_mm_body, rope_lo=POOL_DIM, rope_hi=POOL_DIM + Q_DIM + KV_DIM, q_hi=POOL_DIM + Q_DIM)
    return pl.pallas_call(
        body,
        grid=(r // TM,),
        in_specs=[pl.BlockSpec((TM, d), lambda i: (i, 0)),
                  _const_spec((1, d)),
                  _mod_spec(MOD_SHIFT_M, tiles_per_batch, ctx_tiles, d),
                  _mod_spec(MOD_SCALE_M, tiles_per_batch, ctx_tiles, d),
                  _const_spec((d, n)),
                  pl.BlockSpec((TM, LANES), lambda i: (i % tiles_per_batch, 0)),
                  pl.BlockSpec((TM, LANES), lambda i: (i % tiles_per_batch, 0))],
        out_specs=pl.BlockSpec((TM, n), lambda i: (i, 0)),
        out_shape=jax.ShapeDtypeStruct((r, n), BF16),
        compiler_params=pltpu.CompilerParams(dimension_semantics=("arbitrary",), vmem_limit_bytes=VMEM_LIMIT),
        name="l0_norm_proj_rope",
    )(x, gain.reshape(1, d), mod, mod, w, cos, sin)


def _post_mm_body(*refs, n_a):
    a_refs = refs[:n_a]
    w_refs = refs[n_a:2 * n_a]
    x_ref, gpost_ref, gate_ref, gpre_ref, sh_ref, sc_ref, xo_ref, ho_ref = refs[2 * n_a:]
    y = jnp.dot(a_refs[0][...], w_refs[0][...], preferred_element_type=F32)
    for a_ref, w_ref in zip(a_refs[1:], w_refs[1:]):
        y = y + jnp.dot(a_ref[...], w_ref[...], preferred_element_type=F32)
    xn = x_ref[...] + gate_ref[0] * _rms(y, gpost_ref[...])
    xo_ref[...] = xn
    ho_ref[...] = (_rms(xn, gpre_ref[...]) * (1.0 + sc_ref[0]) + sh_ref[0]).astype(ho_ref.dtype)


def post_mm(a_list, w_list, x, x_tile_offset, g_post, g_pre, mod, tiles_per_batch, ctx_tiles, x_tiles_per_batch,
            h_dtype, name):
    r = a_list[0].shape[0]
    d = w_list[0].shape[1]
    n_a = len(a_list)

    def x_index(i):
        return (i // tiles_per_batch) * x_tiles_per_batch + x_tile_offset + i % tiles_per_batch, 0

    in_specs = ([pl.BlockSpec((TM, a.shape[1]), lambda i: (i, 0)) for a in a_list]
                + [_const_spec(w.shape) for w in w_list]
                + [pl.BlockSpec((TM, d), x_index),
                   _const_spec((1, d)),
                   _mod_spec(MOD_GATE_M, tiles_per_batch, ctx_tiles, d),
                   _const_spec((1, d)),
                   _mod_spec(MOD_SHIFT_F, tiles_per_batch, ctx_tiles, d),
                   _mod_spec(MOD_SCALE_F, tiles_per_batch, ctx_tiles, d)])
    return pl.pallas_call(
        functools.partial(_post_mm_body, n_a=n_a),
        grid=(r // TM,),
        in_specs=in_specs,
        out_specs=[pl.BlockSpec((TM, d), lambda i: (i, 0)), pl.BlockSpec((TM, d), lambda i: (i, 0))],
        out_shape=[jax.ShapeDtypeStruct((r, d), F32), jax.ShapeDtypeStruct((r, d), h_dtype)],
        compiler_params=pltpu.CompilerParams(dimension_semantics=("arbitrary",), vmem_limit_bytes=VMEM_LIMIT),
        name=name,
    )(*a_list, *w_list, x, g_post.reshape(1, d), mod, g_pre.reshape(1, d), mod, mod)


def _ffn_body(h_ref, w1_ref, w3_ref, w2_ref, x_ref, gpost_ref, gate_ref, gpre_ref, sh_ref, sc_ref,
              xo_ref, ho_ref, acc_ref):
    h = h_ref[...]
    n_f = w1_ref.shape[1]
    for f0 in range(0, n_f, FFN_TF):
        a = jnp.dot(h, w1_ref[:, f0:f0 + FFN_TF], preferred_element_type=F32)
        b = jnp.dot(h, w3_ref[:, f0:f0 + FFN_TF], preferred_element_type=F32)
        mid = (a * jax.nn.sigmoid(a) * b).astype(BF16)
        part = jnp.dot(mid, w2_ref[f0:f0 + FFN_TF, :], preferred_element_type=F32)
        if f0 == 0:
            acc_ref[...] = part
        else:
            acc_ref[...] += part
    xn = x_ref[...] + gate_ref[0] * _rms(acc_ref[...], gpost_ref[...])
    xo_ref[...] = xn
    ho_ref[...] = (_rms(xn, gpre_ref[...]) * (1.0 + sc_ref[0]) + sh_ref[0]).astype(ho_ref.dtype)


def ffn_layer(h, w1, w3, w2, x, g_post, mod, g_pre_next, mod_next, tiles_per_batch, ctx_tiles):
    r, d = h.shape
    return pl.pallas_call(
        _ffn_body,
        grid=(r // TM,),
        in_specs=[pl.BlockSpec((TM, d), lambda i: (i, 0)),
                  _const_spec(w1.shape), _const_spec(w3.shape), _const_spec(w2.shape),
                  pl.BlockSpec((TM, d), lambda i: (i, 0)),
                  _const_spec((1, d)),
                  _mod_spec(MOD_GATE_F, tiles_per_batch, ctx_tiles, d),
                  _const_spec((1, d)),
                  _mod_spec(MOD_SHIFT_M, tiles_per_batch, ctx_tiles, d),
                  _mod_spec(MOD_SCALE_M, tiles_per_batch, ctx_tiles, d)],
        out_specs=[pl.BlockSpec((TM, d), lambda i: (i, 0)), pl.BlockSpec((TM, d), lambda i: (i, 0))],
        out_shape=[jax.ShapeDtypeStruct((r, d), F32), jax.ShapeDtypeStruct((r, d), BF16)],
        scratch_shapes=[pltpu.VMEM((TM, d), F32)],
        compiler_params=pltpu.CompilerParams(dimension_semantics=("arbitrary",), vmem_limit_bytes=VMEM_LIMIT),
        name="l0_ffn",
    )(h, w1, w3, w2, x, g_post.reshape(1, d), mod, g_pre_next.reshape(1, d), mod_next, mod_next)


def _proj_body(h_ref, w_ref, wg_ref, o_ref, og_ref):
    h = h_ref[...]
    o_ref[...] = jnp.dot(h, w_ref[...], preferred_element_type=F32).astype(o_ref.dtype)
    og_ref[...] = jnp.dot(h, wg_ref[...], preferred_element_type=F32)


def gdn_proj(h, w, w_gate):
    r, d = h.shape
    n = w.shape[1]
    return pl.pallas_call(
        _proj_body,
        grid=(r // TM,),
        in_specs=[pl.BlockSpec((TM, d), lambda i: (i, 0)), _const_spec(w.shape), _const_spec(w_gate.shape)],
        out_specs=[pl.BlockSpec((TM, n), lambda i: (i, 0)), pl.BlockSpec((TM, LANES), lambda i: (i, 0))],
        out_shape=[jax.ShapeDtypeStruct((r, n), BF16), jax.ShapeDtypeStruct((r, LANES), F32)],
        compiler_params=pltpu.CompilerParams(dimension_semantics=("arbitrary",), vmem_limit_bytes=VMEM_LIMIT),
        name="l1_proj",
    )(h, w, w_gate)


def _attn_body(sink_ref, q_ref, kc_ref, vc_ref, k0_ref, k1_ref, k2_ref, v0_ref, v1_ref, v2_ref, o_ref,
               *, n_ctx_blocks, n_blocks):
    n = pl.program_id(1)
    rows = GQA_GROUP * BLOCK
    qi = lax.broadcasted_iota(I32, (rows, 3 * BLOCK), 0) % BLOCK
    kj = lax.broadcasted_iota(I32, (rows, 3 * BLOCK), 1)
    first = jnp.where(n > n_ctx_blocks, 0, BLOCK)
    last = jnp.where(n < n_blocks - 1, 3 * BLOCK, 2 * BLOCK)
    last = jnp.where(n >= n_ctx_blocks, last, 0)
    band_ok = (kj >= qi) & (kj <= qi + 2 * WINDOW) & (kj >= first) & (kj < last)
    q = q_ref[0]
    outs = []
    for g in range(N_KV_HEADS):
        cols = slice(g * HEAD_DIM, (g + 1) * HEAD_DIM)
        k_c = kc_ref[0][:, cols]
        v_c = vc_ref[0][:, cols]
        k_b = jnp.concatenate([k0_ref[0][:, cols], k1_ref[0][:, cols], k2_ref[0][:, cols]], axis=0)
        v_b = jnp.concatenate([v0_ref[0][:, cols], v1_ref[0][:, cols], v2_ref[0][:, cols]], axis=0)
        heads = [g * GQA_GROUP + i for i in range(GQA_GROUP)]
        q_g = jnp.concatenate([q[:, hd * HEAD_DIM:(hd + 1) * HEAD_DIM] for hd in heads], axis=0)
        s_c = lax.dot_general(q_g, k_c, (((1,), (1,)), ((), ())), preferred_element_type=F32)
        s_b = lax.dot_general(q_g, k_b, (((1,), (1,)), ((), ())), preferred_element_type=F32)
        s_b = jnp.where(band_ok, s_b, NEG_INF)
        sink = jnp.concatenate([sink_ref[hd] + jnp.zeros((BLOCK, 1), F32) for hd in heads], axis=0)
        m = jnp.maximum(jnp.maximum(jnp.max(s_c, axis=-1, keepdims=True), jnp.max(s_b, axis=-1, keepdims=True)), sink)
        p_c = jnp.exp(s_c - m)
        p_b = jnp.exp(s_b - m)
        den = jnp.sum(p_c, axis=-1, keepdims=True) + jnp.sum(p_b, axis=-1, keepdims=True) + jnp.exp(sink - m)
        o = (jnp.dot(p_c.astype(BF16), v_c, preferred_element_type=F32)
             + jnp.dot(p_b.astype(BF16), v_b, preferred_element_type=F32)) / den
        outs += [o[i * BLOCK:(i + 1) * BLOCK] for i in range(GQA_GROUP)]
    o_ref[0] = jnp.concatenate(outs, axis=1).astype(o_ref.dtype)


def attention(p0, sinks, n_ctx):
    bsz, ltot, _ = p0.shape
    n_blocks = ltot // BLOCK
    n_ctx_blocks = n_ctx // BLOCK
    k_blk = (POOL_DIM + Q_DIM) // KV_DIM
    v_blk = k_blk + 1

    def band(dn, blk):
        return pl.BlockSpec((1, BLOCK, KV_DIM),
                            lambda b, n: (b, jnp.clip(n + dn, n_ctx_blocks, n_blocks - 1), blk))

    return pl.pallas_call(
        functools.partial(_attn_body, n_ctx_blocks=n_ctx_blocks, n_blocks=n_blocks),
        grid=(bsz, n_blocks),
        in_specs=[pl.BlockSpec(memory_space=pltpu.SMEM),
                  pl.BlockSpec((1, BLOCK, Q_DIM), lambda b, n: (b, n, POOL_DIM // Q_DIM)),
                  pl.BlockSpec((1, n_ctx, KV_DIM), lambda b, n: (b, 0, k_blk)),
                  pl.BlockSpec((1, n_ctx, KV_DIM), lambda b, n: (b, 0, v_blk)),
                  band(-1, k_blk), band(0, k_blk), band(1, k_blk),
                  band(-1, v_blk), band(0, v_blk), band(1, v_blk)],
        out_specs=pl.BlockSpec((1, BLOCK, Q_DIM), lambda b, n: (b, n, 0)),
        out_shape=jax.ShapeDtypeStruct((bsz, ltot, Q_DIM), BF16),
        compiler_params=pltpu.CompilerParams(dimension_semantics=("arbitrary", "arbitrary"),
                                             vmem_limit_bytes=VMEM_LIMIT),
        name="l0_attention",
    )(sinks.astype(F32), p0, p0, p0, p0, p0, p0, p0, p0, p0)


POOL_HALO = 16


def _pool_body(up_ref, uc_ref, un_ref, w_ref, scale_ref, o_ref, *, tiles_per_batch, ctx_tiles):
    j = pl.program_id(1)
    seg_lo = jnp.where(j < ctx_tiles, 0, ctx_tiles * TM)
    seg_hi = jnp.where(j < ctx_tiles, ctx_tiles * TM, tiles_per_batch * TM)
    halo = jnp.concatenate([up_ref[0][TM - POOL_HALO:], uc_ref[0], un_ref[0][:POOL_HALO]], axis=0)
    n_h = TM + 2 * POOL_HALO
    t = j * TM + lax.broadcasted_iota(I32, (TM, n_h), 0)
    pos = j * TM - POOL_HALO + lax.broadcasted_iota(I32, (TM, n_h), 1)
    t_col = j * TM + lax.broadcasted_iota(I32, (TM, 1), 0)
    cur = uc_ref[0]
    for g, w in enumerate(POOL_WINDOWS):
        cols = slice(g * POOL_GROUP_DIM, (g + 1) * POOL_GROUP_DIM)
        lo = jnp.maximum(t - w // 2, seg_lo)
        hi = jnp.minimum(t + w // 2, seg_hi)
        window = ((pos >= lo) & (pos < hi)).astype(BF16)
        cnt = (jnp.minimum(t_col + w // 2, seg_hi) - jnp.maximum(t_col - w // 2, seg_lo)).astype(F32)
        mean = jnp.dot(window, halo[:, cols], preferred_element_type=F32) / cnt
        delta = (mean - cur[:, cols].astype(F32)).astype(BF16)
        y = jnp.dot(delta, w_ref[g], preferred_element_type=F32) * scale_ref[:, cols]
        o_ref[0, :, cols] = y.astype(o_ref.dtype)


def pool_mixer(p0, pool_w, pool_scale, n_ctx):
    bsz, ltot, _ = p0.shape
    tiles_per_batch = ltot // TM
    ctx_tiles = n_ctx // TM

    def tile(dj):
        return pl.BlockSpec((1, TM, POOL_DIM), lambda b, j: (b, jnp.clip(j + dj, 0, tiles_per_batch - 1), 0))

    return pl.pallas_call(
        functools.partial(_pool_body, tiles_per_batch=tiles_per_batch, ctx_tiles=ctx_tiles),
        grid=(bsz, tiles_per_batch),
        in_specs=[tile(-1), tile(0), tile(1),
                  pl.BlockSpec(pool_w.shape, lambda b, j: (0, 0, 0)),
                  pl.BlockSpec((1, POOL_DIM), lambda b, j: (0, 0))],
        out_specs=pl.BlockSpec((1, TM, POOL_DIM), lambda b, j: (b, j, 0)),
        out_shape=jax.ShapeDtypeStruct((bsz, ltot, POOL_DIM), BF16),
        compiler_params=pltpu.CompilerParams(dimension_semantics=("arbitrary", "arbitrary")),
        name="l0_pool",
    )(p0, p0, p0, pool_w.astype(BF16), pool_scale.reshape(1, POOL_DIM).astype(F32))


def kernel(x, c, ctx, c_ctx, l0_mod_w, l0_mod_b, l0_mix_pre, l0_mix_post, l0_ffn_pre, l0_ffn_post, l0_w_in, l0_pool_w, l0_pool_scale, l0_sinks, l0_w_out, l0_ffn_w1, l0_ffn_w3, l0_ffn_w2, l1_mod_w, l1_mod_b, l1_mix_pre, l1_mix_post, l1_ffn_pre, l1_ffn_post, l1_w_in, l1_conv_w, l1_a_log, l1_dt_bias, l1_out_norm, l1_w_out, l1_router, l1_moe_w1, l1_moe_w3, l1_moe_w2):
    bsz, n_lat, d = x.shape
    n_ctx = ctx.shape[1]
    ltot = n_ctx + n_lat
    assert n_ctx % TM == 0 and n_lat % TM == 0 and n_ctx % GC == 0
    tiles = ltot // TM
    ctx_tiles = n_ctx // TM
    lat_tiles = n_lat // TM
    bf = lambda w: w.astype(BF16)

    x_all = jnp.concatenate([ctx, x], axis=1).reshape(bsz * ltot, d)
    mod0 = modulation(c, c_ctx, l0_mod_w, l0_mod_b)
    mod1 = modulation(c, c_ctx, l1_mod_w, l1_mod_b)
    cos, sin = rope_tables(n_ctx, n_lat)

    p0 = pre_mm(x_all, l0_mix_pre, mod0, bf(l0_w_in), cos, sin, tiles, ctx_tiles).reshape(bsz, ltot, -1)
    attn = attention(p0, l0_sinks, n_ctx).reshape(bsz * ltot, Q_DIM)
    pooled = pool_mixer(p0, l0_pool_w, l0_pool_scale, n_ctx).reshape(bsz * ltot, POOL_DIM)
    w_out0 = bf(l0_w_out)
    x1, h1 = post_mm([pooled, attn], [w_out0[:POOL_DIM], w_out0[POOL_DIM:]], x_all, 0, l0_mix_post, l0_ffn_pre,
                     mod0, tiles, ctx_tiles, tiles, BF16, "l0_out_proj")
    x2, h2 = ffn_layer(h1, bf(l0_ffn_w1), bf(l0_ffn_w3), bf(l0_ffn_w2), x1, l0_ffn_post, mod0, l1_mix_pre, mod1,
                       tiles, ctx_tiles)

    n_heads = l1_a_log.shape[1]
    qkvz = 4 * n_heads * GD
    w_in1 = bf(l1_w_in)
    w_gate = jnp.zeros((d, LANES), BF16).at[:, :4 * n_heads].set(w_in1[:, qkvz:])
    p1, ab = gdn_proj(h2, w_in1[:, :qkvz], w_gate)
    ab_rows = gdn_gate_rows(ab.reshape(bsz, ltot, LANES)[..., :4 * n_heads], n_heads)
    y = gdn_core(p1.reshape(bsz, ltot, qkvz), ab_rows, l1_conv_w, l1_a_log, l1_dt_bias, l1_out_norm, n_ctx)
    x3, h3 = post_mm([y.reshape(bsz * n_lat, n_heads * GD)], [bf(l1_w_out)], x2, ctx_tiles, l1_mix_post, l1_ffn_pre,
                     mod1, lat_tiles, 0, tiles, F32, "l1_out_proj")
    out = moe_layer(h3, l1_router, l1_moe_w1, l1_moe_w3, l1_moe_w2, x3, l1_ffn_post, mod1, lat_tiles)
    return out.reshape(bsz, n_lat, d)
```

```python
import functools
import math

import jax
import jax.numpy as jnp
from jax import lax
from jax.experimental import pallas as pl
from jax.experimental.pallas import tpu as pltpu

F32 = jnp.float32
BF16 = jnp.bfloat16
I32 = jnp.int32

LANES = 128
D_MODEL = 1024
GRID_W = 64
EPS = 1e-6
NEG_INF = -1e30

POOL_GROUPS = 4
POOL_GROUP_DIM = 128
POOL_DIM = POOL_GROUPS * POOL_GROUP_DIM
POOL_WINDOWS = (2, 4, 8, 16)
HEAD_DIM = 64
N_HEADS = 8
N_KV_HEADS = 2
GQA_GROUP = N_HEADS // N_KV_HEADS
Q_DIM = N_HEADS * HEAD_DIM
KV_DIM = N_KV_HEADS * HEAD_DIM
WINDOW = 128
BLOCK = 128
ROPE_THETA = 10000.0

GDN_HEADS = 8
GDN_HEAD_DIM = 128
GDN_DIM = GDN_HEADS * GDN_HEAD_DIM
CONV_K = 5
CHUNK = 64

N_EXPERTS = 8
TOP_K = 2

ROUTE_TILE = 512
MOE_TM = 512
MOE_TF = 1792
ROW_TILE = 256
ISSUE_UNROLL = 8


def _split_bf16(a):
    hi = a.astype(BF16)
    lo = (a - hi.astype(F32)).astype(BF16)
    return hi, lo


def _route_body(h_ref, rhi_ref, rlo_ref, route_ref, cnt_ref, carry_ref):
    i = pl.program_id(0)

    @pl.when(i == 0)
    def _():
        carry_ref[...] = jnp.zeros_like(carry_ref)

    h_hi, h_lo = _split_bf16(h_ref[...])
    r_hi = rhi_ref[...]
    r_lo = rlo_ref[...]
    logits = (jnp.dot(h_hi, r_hi, preferred_element_type=F32)
              + jnp.dot(h_hi, r_lo, preferred_element_type=F32)
              + jnp.dot(h_lo, r_hi, preferred_element_type=F32))
    tr = logits.shape[0]
    lane = lax.broadcasted_iota(I32, (tr, LANES), 1)
    logits = jnp.where(lane < N_EXPERTS, logits, -jnp.inf)
    m1 = jnp.max(logits, axis=-1, keepdims=True)
    i1 = jnp.min(jnp.where(logits == m1, lane, LANES), axis=-1, keepdims=True)
    rest = jnp.where(lane == i1, -jnp.inf, logits)
    m2 = jnp.max(rest, axis=-1, keepdims=True)
    i2 = jnp.min(jnp.where(rest == m2, lane, LANES), axis=-1, keepdims=True)
    e2 = jnp.exp(m2 - m1)
    g1 = 1.0 / (1.0 + e2)
    g2 = e2 / (1.0 + e2)

    onehot = ((lane == i1) | (lane == i2)).astype(F32)
    row = lax.broadcasted_iota(I32, (tr, tr), 0)
    col = lax.broadcasted_iota(I32, (tr, tr), 1)
    strict = (row > col).astype(BF16)
    before = jnp.dot(strict, onehot.astype(BF16), preferred_element_type=F32) + carry_ref[...]
    rank1 = jnp.sum(jnp.where(lane == i1, before, 0.0), axis=-1, keepdims=True)
    rank2 = jnp.sum(jnp.where(lane == i2, before, 0.0), axis=-1, keepdims=True)
    carry_ref[...] += jnp.sum(onehot, axis=0, keepdims=True)

    packed = jnp.where(lane == 0, i1.astype(F32), 0.0)
    packed = jnp.where(lane == 1, i2.astype(F32), packed)
    packed = jnp.where(lane == 2, g1, packed)
    packed = jnp.where(lane == 3, g2, packed)
    packed = jnp.where(lane == 4, rank1, packed)
    packed = jnp.where(lane == 5, rank2, packed)
    route_ref[...] = packed
    cnt_ref[...] = jnp.broadcast_to(carry_ref[...], cnt_ref.shape)


def moe_route(h, router):
    t, d = h.shape
    tr = min(ROUTE_TILE, t)
    r_pad = jnp.zeros((d, LANES), F32).at[:, :N_EXPERTS].set(router.astype(F32))
    r_hi, r_lo = _split_bf16(r_pad)
    route, cnt = pl.pallas_call(
        _route_body,
        grid=(t // tr,),
        in_specs=[
            pl.BlockSpec((tr, d), lambda i: (i, 0)),
            pl.BlockSpec((d, LANES), lambda i: (0, 0)),
            pl.BlockSpec((d, LANES), lambda i: (0, 0)),
        ],
        out_specs=[
            pl.BlockSpec((tr, LANES), lambda i: (i, 0)),
            pl.BlockSpec((8, LANES), lambda i: (0, 0)),
        ],
        out_shape=[
            jax.ShapeDtypeStruct((t, LANES), F32),
            jax.ShapeDtypeStruct((8, LANES), F32),
        ],
        scratch_shapes=[pltpu.VMEM((1, LANES), F32)],
        compiler_params=pltpu.CompilerParams(dimension_semantics=("arbitrary",)),
        name="moe_route",
    )(h, r_hi, r_lo)
    return route, cnt


def _row_copy(src_ref, src_row, dst_ref, dst_row, sem):
    return pltpu.make_async_copy(src_ref.at[pl.ds(src_row, 1)], dst_ref.at[pl.ds(dst_row, 1)], sem)


def _scatter_body(pos_ref, h_ref, xs_in_ref, xs_ref, sem, *, ts):
    del xs_in_ref

    def issue(r, c):
        for k in range(TOP_K):
            _row_copy(h_ref, r, xs_ref, pos_ref[0, 0, TOP_K * r + k], sem).start()
        return c

    lax.fori_loop(0, ts, issue, 0, unroll=ISSUE_UNROLL)
    for _ in range(TOP_K):
        pltpu.make_async_copy(h_ref, xs_ref.at[pl.ds(0, ts)], sem).wait()


def moe_scatter(h, pos, p_rows):
    t, d = h.shape
    ts = min(ROW_TILE, t)
    nt = t // ts
    xs0 = jnp.zeros((p_rows, d), h.dtype)
    return pl.pallas_call(
        functools.partial(_scatter_body, ts=ts),
        grid=(nt,),
        in_specs=[
            pl.BlockSpec((1, 1, TOP_K * ts), lambda i: (i, 0, 0), memory_space=pltpu.SMEM),
            pl.BlockSpec((ts, d), lambda i: (i, 0)),
            pl.BlockSpec(memory_space=pl.ANY),
        ],
        out_specs=pl.BlockSpec(memory_space=pl.ANY),
        out_shape=jax.ShapeDtypeStruct((p_rows, d), h.dtype),
        scratch_shapes=[pltpu.SemaphoreType.DMA(())],
        input_output_aliases={2: 0},
        compiler_params=pltpu.CompilerParams(dimension_semantics=("arbitrary",), disable_bounds_checks=True),
        name="moe_scatter",
    )(pos.reshape(nt, 1, TOP_K * ts), h, xs0)


def _expert_body(te_ref, nu_ref, x_ref, w1_ref, w3_ref, w2_ref, o_ref):
    i = pl.program_id(0)
    j = pl.program_id(1)

    @pl.when(j == 0)
    def _():
        o_ref[...] = jnp.zeros_like(o_ref)

    @pl.when(i < nu_ref[0])
    def _():
        x = x_ref[...].astype(BF16)
        a = jnp.dot(x, w1_ref[0], preferred_element_type=F32)
        b = jnp.dot(x, w3_ref[0], preferred_element_type=F32)
        mid = (a * jax.nn.sigmoid(a) * b).astype(BF16)
        o_ref[...] += jnp.dot(mid, w2_ref[0], preferred_element_type=F32)


def moe_experts(xs, w1, w3, w2, tile_expert, n_used, tm):
    p_rows, d = xs.shape
    n_exp, _, d_exp = w1.shape
    tf = MOE_TF if d_exp % MOE_TF == 0 else d_exp
    nj = d_exp // tf
    n_tiles = p_rows // tm

    def jj(i, j, nu):
        return jnp.where(i < nu[0], j, nj - 1)

    grid_spec = pltpu.PrefetchScalarGridSpec(
        num_scalar_prefetch=2,
        grid=(n_tiles, nj),
        in_specs=[
            pl.BlockSpec((tm, d), lambda i, j, te, nu: (i, 0)),
            pl.BlockSpec((1, d, tf), lambda i, j, te, nu: (te[i], 0, jj(i, j, nu))),
            pl.BlockSpec((1, d, tf), lambda i, j, te, nu: (te[i], 0, jj(i, j, nu))),
            pl.BlockSpec((1, tf, d), lambda i, j, te, nu: (te[i], jj(i, j, nu), 0)),
        ],
        out_specs=pl.BlockSpec((tm, d), lambda i, j, te, nu: (i, 0)),
    )
    return pl.pallas_call(
        _expert_body,
        grid_spec=grid_spec,
        out_shape=jax.ShapeDtypeStruct((p_rows, d), F32),
        compiler_params=pltpu.CompilerParams(
            dimension_semantics=("arbitrary", "arbitrary"),
            vmem_limit_bytes=56 * 1024 * 1024),
        name="moe_experts",
    )(tile_expert, n_used, xs, w1, w3, w2)


def _combine_body(pos_ref, pos_next_ref, gate_ref, ys_ref, x_ref, gain_ref, mgate_ref, y_ref, buf, sem, *, ts):
    i = pl.program_id(0)
    slot = i % 2

    def gather(p_ref, s):
        def issue(r, c):
            for k in range(TOP_K):
                _row_copy(ys_ref, p_ref[0, 0, TOP_K * r + k], buf.at[s, k], r, sem.at[s]).start()
            return c
        lax.fori_loop(0, ts, issue, 0, unroll=ISSUE_UNROLL)

    @pl.when(i == 0)
    def _():
        gather(pos_ref, slot)

    @pl.when(i + 1 < pl.num_programs(0))
    def _():
        gather(pos_next_ref, 1 - slot)

    for k in range(TOP_K):
        pltpu.make_async_copy(ys_ref.at[pl.ds(0, ts)], buf.at[slot, k], sem.at[slot]).wait()
    g = gate_ref[...]
    y = g[:, 2:3] * buf[slot, 0] + g[:, 3:4] * buf[slot, 1]
    y_ref[...] = x_ref[...] + mgate_ref[0] * _rms(y, gain_ref[...])


def moe_combine(ys, pos, route, x, gain, mod, tiles_per_batch):
    t = route.shape[0]
    d = ys.shape[1]
    ts = min(ROW_TILE, t)
    nt = t // ts
    pos3 = pos.reshape(nt, 1, TOP_K * ts)
    return pl.pallas_call(
        functools.partial(_combine_body, ts=ts),
        grid=(nt,),
        in_specs=[
            pl.BlockSpec((1, 1, TOP_K * ts), lambda i: (i, 0, 0), memory_space=pltpu.SMEM),
            pl.BlockSpec((1, 1, TOP_K * ts), lambda i: (jnp.minimum(i + 1, nt - 1), 0, 0), memory_space=pltpu.SMEM),
            pl.BlockSpec((ts, LANES), lambda i: (i, 0)),
            pl.BlockSpec(memory_space=pl.ANY),
            pl.BlockSpec((ts, d), lambda i: (i, 0)),
            pl.BlockSpec((1, d), lambda i: (0, 0)),
            _mod_spec(MOD_GATE_F, tiles_per_batch, 0, d),
        ],
        out_specs=pl.BlockSpec((ts, d), lambda i: (i, 0)),
        out_shape=jax.ShapeDtypeStruct((t, d), F32),
        scratch_shapes=[pltpu.VMEM((2, TOP_K, ts, d), F32), pltpu.SemaphoreType.DMA((2,))],
        compiler_params=pltpu.CompilerParams(dimension_semantics=("arbitrary",), disable_bounds_checks=True),
        name="moe_combine",
    )(pos3, pos3, route, ys, x, gain.reshape(1, d), mod)


def moe_layer(tok, router, w1, w3, w2, x, gain, mod, tiles_per_batch):
    t = tok.shape[0]
    tm = min(MOE_TM, t)
    route, cnt = moe_route(tok, router)

    counts = cnt[0, :N_EXPERTS].astype(I32)
    padded = ((counts + tm - 1) // tm) * tm
    ends = jnp.cumsum(padded)
    starts = ends - padded
    expert = route[:, 0:TOP_K].astype(I32)
    rank = route[:, 4:4 + TOP_K].astype(I32)
    start_of = jnp.sum(jnp.where(expert[..., None] == jnp.arange(N_EXPERTS), starts, 0), axis=-1)
    pos = start_of + rank

    n_tiles = (TOP_K * t) // tm + N_EXPERTS
    n_used = (ends[-1] // tm).astype(I32)
    tile_idx = jnp.minimum(jnp.arange(n_tiles, dtype=I32), n_used - 1)
    tile_expert = jnp.sum(tile_idx[:, None] * tm >= ends[None, :], axis=-1).astype(I32)
    tile_expert = jnp.minimum(tile_expert, N_EXPERTS - 1)

    xs = moe_scatter(tok, pos, n_tiles * tm)
    ys = moe_experts(xs, w1.astype(BF16), w3.astype(BF16), w2.astype(BF16), tile_expert, n_used.reshape(1), tm)
    return moe_combine(ys, pos, route, x, gain, mod, tiles_per_batch)


GC = 128
GD = GDN_HEAD_DIM
PAD = 8
GDN_UNROLL = 4
GDN_BASE = 16
N_MERGE = 3
M_INCL_F, M_STRICT_F, M_INCL_B, M_STRICT_B, M_EYE, M_BASE, M_OFF0 = 0, 1, 2, 3, 4, 5, 6
N_MASKS = M_OFF0 + N_MERGE


def _bdot(a, b):
    return jnp.dot(a.astype(BF16), b.astype(BF16), preferred_element_type=F32)


def _bdot_nt(a, b):
    return lax.dot_general(a.astype(BF16), b.astype(BF16), (((1,), (1,)), ((), ())), preferred_element_type=F32)


def _gdn_body(alog_ref, dtb_ref, q_ref, k_ref, v_ref, z_ref, ab_ref, cq_ref, ck_ref, cv_ref, onorm_ref,
              o_ref, pq_s, pk_s, pv_s, o_s, b_s, gq_s, gate_s, state_s, mask_s,
              *, n_ctx_chunks, n_chunks, n_heads, unroll):
    h = pl.program_id(1)
    ltot = n_chunks * GC
    lc = n_ctx_chunks * GC

    row = lax.broadcasted_iota(I32, (GC, GC), 0)
    col = lax.broadcasted_iota(I32, (GC, GC), 1)
    mask_s[M_INCL_F] = (row >= col).astype(F32)
    mask_s[M_STRICT_F] = (row > col).astype(F32)
    mask_s[M_INCL_B] = (row <= col).astype(F32)
    mask_s[M_STRICT_B] = (row < col).astype(F32)
    mask_s[M_EYE] = (row == col).astype(F32)
    mask_s[M_BASE] = ((row // GDN_BASE) == (col // GDN_BASE)).astype(F32)
    for lvl in range(N_MERGE):
        s = GDN_BASE << lvl
        mask_s[M_OFF0 + lvl] = (((row // (2 * s)) == (col // (2 * s))) & ((row // s) != (col // s))).astype(F32)

    for src_ref, dst in ((q_ref, pq_s), (k_ref, pk_s), (v_ref, pv_s)):
        for off in (0, PAD + lc, 2 * PAD + ltot):
            dst[pl.ds(off, PAD), :] = jnp.zeros((PAD, GD), F32)
        dst[pl.ds(PAD, lc), :] = src_ref[0, pl.ds(0, lc), :].astype(F32)
        dst[pl.ds(2 * PAD + lc, ltot - lc), :] = src_ref[0, pl.ds(lc, ltot - lc), :].astype(F32)

    for d in range(2):
        a = ab_ref[0, d * 2 * n_heads + h]
        b = ab_ref[0, d * 2 * n_heads + n_heads + h]
        xa = a + dtb_ref[d, h]
        softplus = jnp.maximum(xa, 0.0) + jnp.log1p(jnp.exp(-jnp.abs(xa)))
        g = -jnp.exp(alog_ref[d, h]) * softplus
        beta = 1.0 / (1.0 + jnp.exp(-b))
        tri = mask_s[M_INCL_B if d == 0 else M_INCL_F].astype(BF16)
        g_hi = g.astype(BF16)
        g_lo = (g - g_hi.astype(F32)).astype(BF16)
        gam = (jnp.dot(g_hi, tri, preferred_element_type=F32) + jnp.dot(g_lo, tri, preferred_element_type=F32))
        gate_s[d, 0] = gam
        gate_s[d, 1] = beta
        gate_s[d, 2] = jnp.broadcast_to(jnp.sum(g, axis=-1, keepdims=True), g.shape)

    def conv_chunk(c, pad_ref, w_ref, normalise, scale):
        start = pl.multiple_of(c * GC + jnp.where(c >= n_ctx_chunks, PAD, 0), PAD)
        win = pad_ref[pl.ds(start, GC + 2 * PAD), :]
        acc = jnp.zeros((GC, GD), F32)
        for j in range(CONV_K):
            sh = CONV_K // 2 - j
            rolled = win if sh == 0 else pltpu.roll(win, sh % (GC + 2 * PAD), axis=0)
            acc = acc + rolled[PAD:PAD + GC] * w_ref[j:j + 1, :].astype(F32)
        y = acc * jax.nn.sigmoid(acc)
        if normalise:
            y = y * (lax.rsqrt(jnp.sum(y * y, axis=-1, keepdims=True) + EPS) * scale)
        return y

    def group_chunks(i):
        return [jnp.minimum(i * unroll + u, n_chunks - 1) for u in range(unroll)]

    def prep(i, carry):
        chunks = group_chunks(i)
        qs = [conv_chunk(c, pq_s, cq_ref, True, GD ** -0.5) for c in chunks]
        ks = [conv_chunk(c, pk_s, ck_ref, True, 1.0) for c in chunks]
        vs = [conv_chunk(c, pv_s, cv_ref, False, 1.0) for c in chunks]
        kqs = [_bdot_nt(jnp.concatenate([k, q], axis=0), k) for k, q in zip(ks, qs)]
        kts = [k.T for k in ks]
        lanes = [(u, d) for u in range(unroll) for d in range(2)]
        lms, dbs, e_cols, kdts = [], [], [], []
        for u, d in lanes:
            c = chunks[u]
            gam_row = gate_s[d, 0, pl.ds(c, 1), :]
            beta_row = gate_s[d, 1, pl.ds(c, 1), :]
            tot_row = gate_s[d, 2, pl.ds(c, 1), :]
            gam_col = jnp.sum(mask_s[M_EYE] * gam_row, axis=1, keepdims=True)
            db = jnp.exp((gam_col - gam_row) * mask_s[M_INCL_F if d == 0 else M_INCL_B]) * beta_row
            dbs.append(db)
            lms.append(kqs[u][:GC] * db * mask_s[M_STRICT_F if d == 0 else M_STRICT_B])
            e_cols.append(jnp.exp(gam_col))
            kdts.append(kts[u] * (jnp.exp(tot_row - gam_row) * beta_row))
        xs = [-lm * mask_s[M_BASE] for lm in lms]
        ts = [mask_s[M_EYE] + x for x in xs]
        for _ in range(3):
            xs = [_bdot(x, x) for x in xs]
            ts = [t + _bdot(t, x) for t, x in zip(ts, xs)]
        for lvl in range(N_MERGE):
            ys = [_bdot(lm * mask_s[M_OFF0 + lvl], t) for lm, t in zip(lms, ts)]
            ts = [t - _bdot(t, y) for t, y in zip(ts, ys)]
        wus = [_bdot(t, jnp.concatenate([ks[u] * e, vs[u]], axis=1))
               for t, e, (u, d) in zip(ts, e_cols, lanes)]
        x12s = []
        for wu, db, kdt, (u, d) in zip(wus, dbs, kdts, lanes):
            qkb = kqs[u][GC:] * db * mask_s[M_INCL_F if d == 0 else M_INCL_B]
            x12s.append(_bdot(jnp.concatenate([kdt, qkb], axis=0), wu))
        for x12, e, (u, d) in zip(x12s, e_cols, lanes):
            t0 = pl.multiple_of(chunks[u] * GC, GC)
            b_s[d, pl.ds(t0, GC), :] = x12[:GD, GD:]
            o_s[d, pl.ds(t0, GC), :] = x12[GD:, GD:]
            q_eff = qs[u] * e - x12[GD:, :GD]
            gq_s[d, pl.ds(pl.multiple_of(2 * t0, 2 * GC), 2 * GC), :] = (
                jnp.concatenate([x12[:GD, :GD], q_eff], axis=0).astype(BF16))
        return carry

    lax.fori_loop(0, -(-n_chunks // unroll), prep, 0)

    state_s[...] = jnp.zeros_like(state_s)

    def scan(s, carry):
        for d in range(2):
            if d == 0:
                c = s
            else:
                c = jnp.where(s < n_ctx_chunks, n_ctx_chunks - 1 - s, n_chunks - 1 - (s - n_ctx_chunks))
            t0 = pl.multiple_of(c * GC, GC)
            st = state_s[d]
            r = jnp.dot(gq_s[d, pl.ds(pl.multiple_of(2 * t0, 2 * GC), 2 * GC), :], st.astype(BF16),
                        preferred_element_type=F32)
            cd = jnp.exp(gate_s[d, 2, pl.ds(c, 1), 0:1])
            state_s[d] = st * cd - r[:GD] + b_s[d, pl.ds(t0, GC), :]
            o_s[d, pl.ds(t0, GC), :] += r[GD:]
        return carry

    lax.fori_loop(0, n_chunks, scan, 0)

    gain = onorm_ref[...].astype(F32)

    def finish(c, carry):
        t0 = pl.multiple_of(lc + c * GC, GC)
        o = o_s[0, pl.ds(t0, GC), :] + o_s[1, pl.ds(t0, GC), :]
        z = z_ref[0, pl.ds(t0, GC), :].astype(F32)
        y = o * lax.rsqrt(jnp.mean(o * o, axis=-1, keepdims=True) + EPS) * gain
        y = y * (z * jax.nn.sigmoid(z))
        o_ref[0, pl.ds(pl.multiple_of(c * GC, GC), GC), :] = y.astype(o_ref.dtype)
        return carry

    lax.fori_loop(0, n_chunks - n_ctx_chunks, finish, 0, unroll=4)


def gdn_core(p_all, ab_rows, conv_w, a_log, dt_bias, out_norm, n_ctx):
    bsz, ltot, _ = p_all.shape
    n_heads = a_log.shape[1]
    n_chunks = ltot // GC
    n_ctx_chunks = n_ctx // GC
    lat = ltot - n_ctx
    n_rows = ab_rows.shape[2]
    unroll = min(GDN_UNROLL, n_chunks)
    body = functools.partial(_gdn_body, n_ctx_chunks=n_ctx_chunks, n_chunks=n_chunks, n_heads=n_heads,
                             unroll=unroll)
    smem = pl.BlockSpec(memory_space=pltpu.SMEM)

    def col(off):
        return pl.BlockSpec((1, ltot, GD), lambda b, h: (b, 0, off * n_heads + h))

    def cw(off):
        return pl.BlockSpec((CONV_K, GD), lambda b, h: (0, off * n_heads + h))

    return pl.pallas_call(
        body,
        grid=(bsz, n_heads),
        in_specs=[smem, smem, col(0), col(1), col(2), col(3),
                  pl.BlockSpec((1, 4 * n_heads, n_rows, GC), lambda b, h: (b, 0, 0, 0)),
                  cw(0), cw(1), cw(2),
                  pl.BlockSpec((1, GD), lambda b, h: (0, 0))],
        out_specs=pl.BlockSpec((1, lat, GD), lambda b, h: (b, 0, h)),
        out_shape=jax.ShapeDtypeStruct((bsz, lat, n_heads * GD), BF16),
        scratch_shapes=[
            pltpu.VMEM((ltot + 3 * PAD, GD), F32),
            pltpu.VMEM((ltot + 3 * PAD, GD), F32),
            pltpu.VMEM((ltot + 3 * PAD, GD), F32),
            pltpu.VMEM((2, ltot, GD), F32),
            pltpu.VMEM((2, ltot, GD), F32),
            pltpu.VMEM((2, 2 * ltot, GD), BF16),
            pltpu.VMEM((2, 3, n_rows, GC), F32),
            pltpu.VMEM((2, GD, GD), F32),
            pltpu.VMEM((N_MASKS, GC, GC), F32),
        ],
        compiler_params=pltpu.CompilerParams(
            dimension_semantics=("arbitrary", "arbitrary"),
            vmem_limit_bytes=56 * 1024 * 1024),
        name="gdn_core",
    )(a_log.astype(F32), dt_bias.astype(F32), p_all, p_all, p_all, p_all, ab_rows,
      conv_w, conv_w, conv_w, out_norm.reshape(1, GD))


def gdn_gate_rows(ab, n_heads):
    bsz, ltot, _ = ab.shape
    n_chunks = ltot // GC
    n_rows = -(-n_chunks // 8) * 8
    t = jnp.transpose(ab.astype(F32), (0, 2, 1)).reshape(bsz, 4 * n_heads, n_chunks, GC)
    return jnp.pad(t, ((0, 0), (0, 0), (0, n_rows - n_chunks), (0, 0)))


TM = 256
FFN_TF = 256
MOD_TN = 512
MOD_SHIFT_M, MOD_SCALE_M, MOD_GATE_M, MOD_SHIFT_F, MOD_SCALE_F, MOD_GATE_F = range(6)
N_MOD = 6
RESIDENT = dict(pipeline_mode=pl.Buffered(1))
VMEM_LIMIT = 56 * 1024 * 1024


def _rms(x, gain):
    return x * lax.rsqrt(jnp.mean(x * x, axis=-1, keepdims=True) + EPS) * gain


def _mod_spec(k, tiles_per_batch, ctx_tiles, d):
    def index(i, *_):
        seg = jnp.where(i % tiles_per_batch >= ctx_tiles, 1, 0)
        return ((i // tiles_per_batch) * 2 + seg) * N_MOD + k, 0, 0
    return pl.BlockSpec((1, 1, d), index)


def _const_spec(shape):
    return pl.BlockSpec(shape, lambda i, *_: (0,) * len(shape), **RESIDENT)


def _mod_body(c_ref, w_ref, b_ref, o_ref):
    c = c_ref[...]
    s = (c * jax.nn.sigmoid(c)).astype(BF16)
    o_ref[...] = jnp.dot(s, w_ref[...].astype(BF16), preferred_element_type=F32) + b_ref[...]


def modulation(c, c_ctx, mod_w, mod_b):
    bsz, d = c.shape
    n = mod_w.shape[1]
    rows = -(-(bsz + 1) // 8) * 8
    cc = jnp.zeros((rows, d), F32).at[:bsz].set(c).at[bsz].set(c_ctx)
    out = pl.pallas_call(
        _mod_body,
        grid=(n // MOD_TN,),
        in_specs=[pl.BlockSpec((rows, d), lambda j: (0, 0)),
                  pl.BlockSpec((d, MOD_TN), lambda j: (0, j)),
                  pl.BlockSpec((1, MOD_TN), lambda j: (0, j))],
        out_specs=pl.BlockSpec((rows, MOD_TN), lambda j: (0, j)),
        out_shape=jax.ShapeDtypeStruct((rows, n), F32),
        compiler_params=pltpu.CompilerParams(dimension_semantics=("arbitrary",)),
        name="modulation",
    )(cc, mod_w, mod_b.reshape(1, n))
    lat = out[:bsz].reshape(bsz, 1, N_MOD, d)
    ctx = jnp.broadcast_to(out[bsz].reshape(1, 1, N_MOD, d), (bsz, 1, N_MOD, d))
    return jnp.concatenate([ctx, lat], axis=1).reshape(bsz * 2 * N_MOD, 1, d)


def _pre_mm_body(x_ref, gain_ref, sh_ref, sc_ref, w_ref, cos_ref, sin_ref, o_ref, *, rope_lo, rope_hi, q_hi):
    h = (_rms(x_ref[...], gain_ref[...]) * (1.0 + sc_ref[0]) + sh_ref[0]).astype(BF16)
    acc = jnp.dot(h, w_ref[...], preferred_element_type=F32)
    lane = lax.broadcasted_iota(I32, (acc.shape[0], LANES), 1)
    first_half = (lane % (HEAD_DIM // 2)) < (HEAD_DIM // 4)
    for s in range(0, acc.shape[1], LANES):
        blk = acc[:, s:s + LANES]
        if rope_lo <= s < rope_hi:
            partner = jnp.where(first_half, pltpu.roll(blk, LANES - HEAD_DIM // 4, 1),
                                pltpu.roll(blk, HEAD_DIM // 4, 1))
            blk = blk * cos_ref[...] + partner * sin_ref[...]
            if s < q_hi:
                blk = blk * (HEAD_DIM ** -0.5)
        o_ref[:, s:s + LANES] = blk.astype(o_ref.dtype)


def rope_tables(n_ctx, n_lat):
    half = HEAD_DIM // 2
    inv = ROPE_THETA ** (-jnp.arange(0, half, 2, dtype=F32) / half)
    t = jnp.arange(n_lat)
    pos = jnp.stack([(t // GRID_W).astype(F32), (t % GRID_W).astype(F32)], axis=1)
    lane = jnp.arange(LANES)
    part = (lane % HEAD_DIM) // half
    freq = lane % (half // 2)
    ang = pos[:, part] * inv[freq][None, :]
    sign = jnp.where((lane % half) < half // 2, -1.0, 1.0)
    cos = jnp.concatenate([jnp.ones((n_ctx, LANES), F32), jnp.cos(ang)], axis=0)
    sin = jnp.concatenate([jnp.zeros((n_ctx, LANES), F32), jnp.sin(ang) * sign], axis=0)
    return cos, sin


def pre_mm(x, gain, mod, w, cos, sin, tiles_per_batch, ctx_tiles):
    r, d = x.shape
    n = w.shape[1]
    body = functools.partial(_pre_mm_body, rope_lo=POOL_DIM, rope_hi=POOL_DIM + Q_DIM + KV_DIM, q_hi=POOL_DIM + Q_DIM)
    return pl.pallas_call(
        body,
        grid=(r // TM,),
        in_specs=[pl.BlockSpec((TM, d), lambda i: (i, 0)),
                  _const_spec((1, d)),
                  _mod_spec(MOD_SHIFT_M, tiles_per_batch, ctx_tiles, d),
                  _mod_spec(MOD_SCALE_M, tiles_per_batch, ctx_tiles, d),
                  _const_spec((d, n)),
                  pl.BlockSpec((TM, LANES), lambda i: (i % tiles_per_batch, 0)),
                  pl.BlockSpec((TM, LANES), lambda i: (i % tiles_per_batch, 0))],
        out_specs=pl.BlockSpec((TM, n), lambda i: (i, 0)),
        out_shape=jax.ShapeDtypeStruct((r, n), BF16),
        compiler_params=pltpu.CompilerParams(dimension_semantics=("arbitrary",), vmem_limit_bytes=VMEM_LIMIT),
        name="l0_norm_proj_rope",
    )(x, gain.reshape(1, d), mod, mod, w, cos, sin)


def _post_mm_body(*refs, n_a):
    a_refs = refs[:n_a]
    w_refs = refs[n_a:2 * n_a]
    x_ref, gpost_ref, gate_ref, gpre_ref, sh_ref, sc_ref, xo_ref, ho_ref = refs[2 * n_a:]
    y = jnp.dot(a_refs[0][...], w_refs[0][...], preferred_element_type=F32)
    for a_ref, w_ref in zip(a_refs[1:], w_refs[1:]):
        y = y + jnp.dot(a_ref[...], w_ref[...], preferred_element_type=F32)
    xn = x_ref[...] + gate_ref[0] * _rms(y, gpost_ref[...])
    xo_ref[...] = xn
    ho_ref[...] = (_rms(xn, gpre_ref[...]) * (1.0 + sc_ref[0]) + sh_ref[0]).astype(ho_ref.dtype)


def post_mm(a_list, w_list, x, x_tile_offset, g_post, g_pre, mod, tiles_per_batch, ctx_tiles, x_tiles_per_batch,
            h_dtype, name):
    r = a_list[0].shape[0]
    d = w_list[0].shape[1]
    n_a = len(a_list)

    def x_index(i):
        return (i // tiles_per_batch) * x_tiles_per_batch + x_tile_offset + i % tiles_per_batch, 0

    in_specs = ([pl.BlockSpec((TM, a.shape[1]), lambda i: (i, 0)) for a in a_list]
                + [_const_spec(w.shape) for w in w_list]
                + [pl.BlockSpec((TM, d), x_index),
                   _const_spec((1, d)),
                   _mod_spec(MOD_GATE_M, tiles_per_batch, ctx_tiles, d),
                   _const_spec((1, d)),
                   _mod_spec(MOD_SHIFT_F, tiles_per_batch, ctx_tiles, d),
                   _mod_spec(MOD_SCALE_F, tiles_per_batch, ctx_tiles, d)])
    return pl.pallas_call(
        functools.partial(_post_mm_body, n_a=n_a),
        grid=(r // TM,),
        in_specs=in_specs,
        out_specs=[pl.BlockSpec((TM, d), lambda i: (i, 0)), pl.BlockSpec((TM, d), lambda i: (i, 0))],
        out_shape=[jax.ShapeDtypeStruct((r, d), F32), jax.ShapeDtypeStruct((r, d), h_dtype)],
        compiler_params=pltpu.CompilerParams(dimension_semantics=("arbitrary",), vmem_limit_bytes=VMEM_LIMIT),
        name=name,
    )(*a_list, *w_list, x, g_post.reshape(1, d), mod, g_pre.reshape(1, d), mod, mod)


def _ffn_body(h_ref, w1_ref, w3_ref, w2_ref, x_ref, gpost_ref, gate_ref, gpre_ref, sh_ref, sc_ref,
              xo_ref, ho_ref, acc_ref):
    h = h_ref[...]
    n_f = w1_ref.shape[1]
    for f0 in range(0, n_f, FFN_TF):
        a = jnp.dot(h, w1_ref[:, f0:f0 + FFN_TF], preferred_element_type=F32)
        b = jnp.dot(h, w3_ref[:, f0:f0 + FFN_TF], preferred_element_type=F32)
        mid = (a * jax.nn.sigmoid(a) * b).astype(BF16)
        part = jnp.dot(mid, w2_ref[f0:f0 + FFN_TF, :], preferred_element_type=F32)
        if f0 == 0:
            acc_ref[...] = part
        else:
            acc_ref[...] += part
    xn = x_ref[...] + gate_ref[0] * _rms(acc_ref[...], gpost_ref[...])
    xo_ref[...] = xn
    ho_ref[...] = (_rms(xn, gpre_ref[...]) * (1.0 + sc_ref[0]) + sh_ref[0]).astype(ho_ref.dtype)


def ffn_layer(h, w1, w3, w2, x, g_post, mod, g_pre_next, mod_next, tiles_per_batch, ctx_tiles):
    r, d = h.shape
    return pl.pallas_call(
        _ffn_body,
        grid=(r // TM,),
        in_specs=[pl.BlockSpec((TM, d), lambda i: (i, 0)),
                  _const_spec(w1.shape), _const_spec(w3.shape), _const_spec(w2.shape),
                  pl.BlockSpec((TM, d), lambda i: (i, 0)),
                  _const_spec((1, d)),
                  _mod_spec(MOD_GATE_F, tiles_per_batch, ctx_tiles, d),
                  _const_spec((1, d)),
                  _mod_spec(MOD_SHIFT_M, tiles_per_batch, ctx_tiles, d),
                  _mod_spec(MOD_SCALE_M, tiles_per_batch, ctx_tiles, d)],
        out_specs=[pl.BlockSpec((TM, d), lambda i: (i, 0)), pl.BlockSpec((TM, d), lambda i: (i, 0))],
        out_shape=[jax.ShapeDtypeStruct((r, d), F32), jax.ShapeDtypeStruct((r, d), BF16)],
        scratch_shapes=[pltpu.VMEM((TM, d), F32)],
        compiler_params=pltpu.CompilerParams(dimension_semantics=("arbitrary",), vmem_limit_bytes=VMEM_LIMIT),
        name="l0_ffn",
    )(h, w1, w3, w2, x, g_post.reshape(1, d), mod, g_pre_next.reshape(1, d), mod_next, mod_next)


def _proj_body(h_ref, w_ref, wg_ref, o_ref, og_ref):
    h = h_ref[...]
    o_ref[...] = jnp.dot(h, w_ref[...], preferred_element_type=F32).astype(o_ref.dtype)
    og_ref[...] = jnp.dot(h, wg_ref[...], preferred_element_type=F32)


def gdn_proj(h, w, w_gate):
    r, d = h.shape
    n = w.shape[1]
    return pl.pallas_call(
        _proj_body,
        grid=(r // TM,),
        in_specs=[pl.BlockSpec((TM, d), lambda i: (i, 0)), _const_spec(w.shape), _const_spec(w_gate.shape)],
        out_specs=[pl.BlockSpec((TM, n), lambda i: (i, 0)), pl.BlockSpec((TM, LANES), lambda i: (i, 0))],
        out_shape=[jax.ShapeDtypeStruct((r, n), BF16), jax.ShapeDtypeStruct((r, LANES), F32)],
        compiler_params=pltpu.CompilerParams(dimension_semantics=("arbitrary",), vmem_limit_bytes=VMEM_LIMIT),
        name="l1_proj",
    )(h, w, w_gate)


def _attn_body(sink_ref, q_ref, kc_ref, vc_ref, k0_ref, k1_ref, k2_ref, v0_ref, v1_ref, v2_ref, o_ref,
               *, n_ctx_blocks, n_blocks):
    n = pl.program_id(1)
    rows = GQA_GROUP * BLOCK
    qi = lax.broadcasted_iota(I32, (rows, 3 * BLOCK), 0) % BLOCK
    kj = lax.broadcasted_iota(I32, (rows, 3 * BLOCK), 1)
    first = jnp.where(n > n_ctx_blocks, 0, BLOCK)
    last = jnp.where(n < n_blocks - 1, 3 * BLOCK, 2 * BLOCK)
    last = jnp.where(n >= n_ctx_blocks, last, 0)
    band_ok = (kj >= qi) & (kj <= qi + 2 * WINDOW) & (kj >= first) & (kj < last)
    q = q_ref[0]
    outs = []
    for g in range(N_KV_HEADS):
        cols = slice(g * HEAD_DIM, (g + 1) * HEAD_DIM)
        k_c = kc_ref[0][:, cols]
        v_c = vc_ref[0][:, cols]
        k_b = jnp.concatenate([k0_ref[0][:, cols], k1_ref[0][:, cols], k2_ref[0][:, cols]], axis=0)
        v_b = jnp.concatenate([v0_ref[0][:, cols], v1_ref[0][:, cols], v2_ref[0][:, cols]], axis=0)
        heads = [g * GQA_GROUP + i for i in range(GQA_GROUP)]
        q_g = jnp.concatenate([q[:, hd * HEAD_DIM:(hd + 1) * HEAD_DIM] for hd in heads], axis=0)
        s_c = lax.dot_general(q_g, k_c, (((1,), (1,)), ((), ())), preferred_element_type=F32)
        s_b = lax.dot_general(q_g, k_b, (((1,), (1,)), ((), ())), preferred_element_type=F32)
        s_b = jnp.where(band_ok, s_b, NEG_INF)
        sink = jnp.concatenate([sink_ref[hd] + jnp.zeros((BLOCK, 1), F32) for hd in heads], axis=0)
        m = jnp.maximum(jnp.maximum(jnp.max(s_c, axis=-1, keepdims=True), jnp.max(s_b, axis=-1, keepdims=True)), sink)
        p_c = jnp.exp(s_c - m)
        p_b = jnp.exp(s_b - m)
        den = jnp.sum(p_c, axis=-1, keepdims=True) + jnp.sum(p_b, axis=-1, keepdims=True) + jnp.exp(sink - m)
        o = (jnp.dot(p_c.astype(BF16), v_c, preferred_element_type=F32)
             + jnp.dot(p_b.astype(BF16), v_b, preferred_element_type=F32)) / den
        outs += [o[i * BLOCK:(i + 1) * BLOCK] for i in range(GQA_GROUP)]
    o_ref[0] = jnp.concatenate(outs, axis=1).astype(o_ref.dtype)


def attention(p0, sinks, n_ctx):
    bsz, ltot, _ = p0.shape
    n_blocks = ltot // BLOCK
    n_ctx_blocks = n_ctx // BLOCK
    k_blk = (POOL_DIM + Q_DIM) // KV_DIM
    v_blk = k_blk + 1

    def band(dn, blk):
        return pl.BlockSpec((1, BLOCK, KV_DIM),
                            lambda b, n: (b, jnp.clip(n + dn, n_ctx_blocks, n_blocks - 1), blk))

    return pl.pallas_call(
        functools.partial(_attn_body, n_ctx_blocks=n_ctx_blocks, n_blocks=n_blocks),
        grid=(bsz, n_blocks),
        in_specs=[pl.BlockSpec(memory_space=pltpu.SMEM),
                  pl.BlockSpec((1, BLOCK, Q_DIM), lambda b, n: (b, n, POOL_DIM // Q_DIM)),
                  pl.BlockSpec((1, n_ctx, KV_DIM), lambda b, n: (b, 0, k_blk)),
                  pl.BlockSpec((1, n_ctx, KV_DIM), lambda b, n: (b, 0, v_blk)),
                  band(-1, k_blk), band(0, k_blk), band(1, k_blk),
                  band(-1, v_blk), band(0, v_blk), band(1, v_blk)],
        out_specs=pl.BlockSpec((1, BLOCK, Q_DIM), lambda b, n: (b, n, 0)),
        out_shape=jax.ShapeDtypeStruct((bsz, ltot, Q_DIM), BF16),
        compiler_params=pltpu.CompilerParams(dimension_semantics=("arbitrary", "arbitrary"),
                                             vmem_limit_bytes=VMEM_LIMIT),
        name="l0_attention",
    )(sinks.astype(F32), p0, p0, p0, p0, p0, p0, p0, p0, p0)


POOL_HALO = 16


def _pool_body(up_ref, uc_ref, un_ref, w_ref, scale_ref, o_ref, *, tiles_per_batch, ctx_tiles):
    j = pl.program_id(1)
    seg_lo = jnp.where(j < ctx_tiles, 0, ctx_tiles * TM)
    seg_hi = jnp.where(j < ctx_tiles, ctx_tiles * TM, tiles_per_batch * TM)
    halo = jnp.concatenate([up_ref[0][TM - POOL_HALO:], uc_ref[0], un_ref[0][:POOL_HALO]], axis=0)
    n_h = TM + 2 * POOL_HALO
    t = j * TM + lax.broadcasted_iota(I32, (TM, n_h), 0)
    pos = j * TM - POOL_HALO + lax.broadcasted_iota(I32, (TM, n_h), 1)
    t_col = j * TM + lax.broadcasted_iota(I32, (TM, 1), 0)
    cur = uc_ref[0]
    for g, w in enumerate(POOL_WINDOWS):
        cols = slice(g * POOL_GROUP_DIM, (g + 1) * POOL_GROUP_DIM)
        lo = jnp.maximum(t - w // 2, seg_lo)
        hi = jnp.minimum(t + w // 2, seg_hi)
        window = ((pos >= lo) & (pos < hi)).astype(BF16)
        cnt = (jnp.minimum(t_col + w // 2, seg_hi) - jnp.maximum(t_col - w // 2, seg_lo)).astype(F32)
        mean = jnp.dot(window, halo[:, cols], preferred_element_type=F32) / cnt
        delta = (mean - cur[:, cols].astype(F32)).astype(BF16)
        y = jnp.dot(delta, w_ref[g], preferred_element_type=F32) * scale_ref[:, cols]
        o_ref[0, :, cols] = y.astype(o_ref.dtype)


def pool_mixer(p0, pool_w, pool_scale, n_ctx):
    bsz, ltot, _ = p0.shape
    tiles_per_batch = ltot // TM
    ctx_tiles = n_ctx // TM

    def tile(dj):
        return pl.BlockSpec((1, TM, POOL_DIM), lambda b, j: (b, jnp.clip(j + dj, 0, tiles_per_batch - 1), 0))

    return pl.pallas_call(
        functools.partial(_pool_body, tiles_per_batch=tiles_per_batch, ctx_tiles=ctx_tiles),
        grid=(bsz, tiles_per_batch),
        in_specs=[tile(-1), tile(0), tile(1),
                  pl.BlockSpec(pool_w.shape, lambda b, j: (0, 0, 0)),
                  pl.BlockSpec((1, POOL_DIM), lambda b, j: (0, 0))],
        out_specs=pl.BlockSpec((1, TM, POOL_DIM), lambda b, j: (b, j, 0)),
        out_shape=jax.ShapeDtypeStruct((bsz, ltot, POOL_DIM), BF16),
        compiler_params=pltpu.CompilerParams(dimension_semantics=("arbitrary", "arbitrary")),
        name="l0_pool",
    )(p0, p0, p0, pool_w.astype(BF16), pool_scale.reshape(1, POOL_DIM).astype(F32))


def kernel(x, c, ctx, c_ctx, l0_mod_w, l0_mod_b, l0_mix_pre, l0_mix_post, l0_ffn_pre, l0_ffn_post, l0_w_in, l0_pool_w, l0_pool_scale, l0_sinks, l0_w_out, l0_ffn_w1, l0_ffn_w3, l0_ffn_w2, l1_mod_w, l1_mod_b, l1_mix_pre, l1_mix_post, l1_ffn_pre, l1_ffn_post, l1_w_in, l1_conv_w, l1_a_log, l1_dt_bias, l1_out_norm, l1_w_out, l1_router, l1_moe_w1, l1_moe_w3, l1_moe_w2):
    bsz, n_lat, d = x.shape
    n_ctx = ctx.shape[1]
    ltot = n_ctx + n_lat
    assert n_ctx % TM == 0 and n_lat % TM == 0 and n_ctx % GC == 0
    tiles = ltot // TM
    ctx_tiles = n_ctx // TM
    lat_tiles = n_lat // TM
    bf = lambda w: w.astype(BF16)

    x_all = jnp.concatenate([ctx, x], axis=1).reshape(bsz * ltot, d)
    mod0 = modulation(c, c_ctx, l0_mod_w, l0_mod_b)
    mod1 = modulation(c, c_ctx, l1_mod_w, l1_mod_b)
    cos, sin = rope_tables(n_ctx, n_lat)

    p0 = pre_mm(x_all, l0_mix_pre, mod0, bf(l0_w_in), cos, sin, tiles, ctx_tiles).reshape(bsz, ltot, -1)
    attn = attention(p0, l0_sinks, n_ctx).reshape(bsz * ltot, Q_DIM)
    pooled = pool_mixer(p0, l0_pool_w, l0_pool_scale, n_ctx).reshape(bsz * ltot, POOL_DIM)
    w_out0 = bf(l0_w_out)
    x1, h1 = post_mm([pooled, attn], [w_out0[:POOL_DIM], w_out0[POOL_DIM:]], x_all, 0, l0_mix_post, l0_ffn_pre,
                     mod0, tiles, ctx_tiles, tiles, BF16, "l0_out_proj")
    x2, h2 = ffn_layer(h1, bf(l0_ffn_w1), bf(l0_ffn_w3), bf(l0_ffn_w2), x1, l0_ffn_post, mod0, l1_mix_pre, mod1,
                       tiles, ctx_tiles)

    n_heads = l1_a_log.shape[1]
    qkvz = 4 * n_heads * GD
    w_in1 = bf(l1_w_in)
    w_gate = jnp.zeros((d, LANES), BF16).at[:, :4 * n_heads].set(w_in1[:, qkvz:])
    p1, ab = gdn_proj(h2, w_in1[:, :qkvz], w_gate)
    ab_rows = gdn_gate_rows(ab.reshape(bsz, ltot, LANES)[..., :4 * n_heads], n_heads)
    y = gdn_core(p1.reshape(bsz, ltot, qkvz), ab_rows, l1_conv_w, l1_a_log, l1_dt_bias, l1_out_norm, n_ctx)
    x3, h3 = post_mm([y.reshape(bsz * n_lat, n_heads * GD)], [bf(l1_w_out)], x2, ctx_tiles, l1_mix_post, l1_ffn_pre,
                     mod1, lat_tiles, 0, tiles, F32, "l1_out_proj")
    out = moe_layer(h3, l1_router, l1_moe_w1, l1_moe_w3, l1_moe_w2, x3, l1_ffn_post, mod1, lat_tiles)
    return out.reshape(bsz, n_lat, d)
```

```python
import functools
import math

import jax
import jax.numpy as jnp
from jax import lax
from jax.experimental import pallas as pl
from jax.experimental.pallas import tpu as pltpu

F32 = jnp.float32
BF16 = jnp.bfloat16
I32 = jnp.int32

LANES = 128
D_MODEL = 1024
GRID_W = 64
EPS = 1e-6
NEG_INF = -1e30

POOL_GROUPS = 4
POOL_GROUP_DIM = 128
POOL_DIM = POOL_GROUPS * POOL_GROUP_DIM
POOL_WINDOWS = (2, 4, 8, 16)
HEAD_DIM = 64
N_HEADS = 8
N_KV_HEADS = 2
GQA_GROUP = N_HEADS // N_KV_HEADS
Q_DIM = N_HEADS * HEAD_DIM
KV_DIM = N_KV_HEADS * HEAD_DIM
WINDOW = 128
BLOCK = 128
ROPE_THETA = 10000.0

GDN_HEADS = 8
GDN_HEAD_DIM = 128
GDN_DIM = GDN_HEADS * GDN_HEAD_DIM
CONV_K = 5
CHUNK = 64

N_EXPERTS = 8
TOP_K = 2

ROUTE_TILE = 512
MOE_TM = 512
MOE_TF = 1792
ROW_TILE = 256
ISSUE_UNROLL = 8


def _split_bf16(a):
    hi = a.astype(BF16)
    lo = (a - hi.astype(F32)).astype(BF16)
    return hi, lo


def _route_body(h_ref, rhi_ref, rlo_ref, route_ref, cnt_ref, carry_ref):
    i = pl.program_id(0)

    @pl.when(i == 0)
    def _():
        carry_ref[...] = jnp.zeros_like(carry_ref)

    h_hi, h_lo = _split_bf16(h_ref[...])
    r_hi = rhi_ref[...]
    r_lo = rlo_ref[...]
    logits = (jnp.dot(h_hi, r_hi, preferred_element_type=F32)
              + jnp.dot(h_hi, r_lo, preferred_element_type=F32)
              + jnp.dot(h_lo, r_hi, preferred_element_type=F32))
    tr = logits.shape[0]
    lane = lax.broadcasted_iota(I32, (tr, LANES), 1)
    logits = jnp.where(lane < N_EXPERTS, logits, -jnp.inf)
    m1 = jnp.max(logits, axis=-1, keepdims=True)
    i1 = jnp.min(jnp.where(logits == m1, lane, LANES), axis=-1, keepdims=True)
    rest = jnp.where(lane == i1, -jnp.inf, logits)
    m2 = jnp.max(rest, axis=-1, keepdims=True)
    i2 = jnp.min(jnp.where(rest == m2, lane, LANES), axis=-1, keepdims=True)
    e2 = jnp.exp(m2 - m1)
    g1 = 1.0 / (1.0 + e2)
    g2 = e2 / (1.0 + e2)

    onehot = ((lane == i1) | (lane == i2)).astype(F32)
    row = lax.broadcasted_iota(I32, (tr, tr), 0)
    col = lax.broadcasted_iota(I32, (tr, tr), 1)
    strict = (row > col).astype(BF16)
    before = jnp.dot(strict, onehot.astype(BF16), preferred_element_type=F32) + carry_ref[...]
    rank1 = jnp.sum(jnp.where(lane == i1, before, 0.0), axis=-1, keepdims=True)
    rank2 = jnp.sum(jnp.where(lane == i2, before, 0.0), axis=-1, keepdims=True)
    carry_ref[...] += jnp.sum(onehot, axis=0, keepdims=True)

    packed = jnp.where(lane == 0, i1.astype(F32), 0.0)
    packed = jnp.where(lane == 1, i2.astype(F32), packed)
    packed = jnp.where(lane == 2, g1, packed)
    packed = jnp.where(lane == 3, g2, packed)
    packed = jnp.where(lane == 4, rank1, packed)
    packed = jnp.where(lane == 5, rank2, packed)
    route_ref[...] = packed
    cnt_ref[...] = jnp.broadcast_to(carry_ref[...], cnt_ref.shape)


def moe_route(h, router):
    t, d = h.shape
    tr = min(ROUTE_TILE, t)
    r_pad = jnp.zeros((d, LANES), F32).at[:, :N_EXPERTS].set(router.astype(F32))
    r_hi, r_lo = _split_bf16(r_pad)
    route, cnt = pl.pallas_call(
        _route_body,
        grid=(t // tr,),
        in_specs=[
            pl.BlockSpec((tr, d), lambda i: (i, 0)),
            pl.BlockSpec((d, LANES), lambda i: (0, 0)),
            pl.BlockSpec((d, LANES), lambda i: (0, 0)),
        ],
        out_specs=[
            pl.BlockSpec((tr, LANES), lambda i: (i, 0)),
            pl.BlockSpec((8, LANES), lambda i: (0, 0)),
        ],
        out_shape=[
            jax.ShapeDtypeStruct((t, LANES), F32),
            jax.ShapeDtypeStruct((8, LANES), F32),
        ],
        scratch_shapes=[pltpu.VMEM((1, LANES), F32)],
        compiler_params=pltpu.CompilerParams(dimension_semantics=("arbitrary",)),
        name="moe_route",
    )(h, r_hi, r_lo)
    return route, cnt


def _row_copy(src_ref, src_row, dst_ref, dst_row, sem):
    return pltpu.make_async_copy(src_ref.at[pl.ds(src_row, 1)], dst_ref.at[pl.ds(dst_row, 1)], sem)


def _scatter_body(pos_ref, h_ref, xs_in_ref, xs_ref, sem, *, ts):
    del xs_in_ref

    def issue(r, c):
        for k in range(TOP_K):
            _row_copy(h_ref, r, xs_ref, pos_ref[0, 0, TOP_K * r + k], sem).start()
        return c

    lax.fori_loop(0, ts, issue, 0, unroll=ISSUE_UNROLL)
    for _ in range(TOP_K):
        pltpu.make_async_copy(h_ref, xs_ref.at[pl.ds(0, ts)], sem).wait()


def moe_scatter(h, pos, p_rows):
    t, d = h.shape
    ts = min(ROW_TILE, t)
    nt = t // ts
    xs0 = jnp.zeros((p_rows, d), h.dtype)
    return pl.pallas_call(
        functools.partial(_scatter_body, ts=ts),
        grid=(nt,),
        in_specs=[
            pl.BlockSpec((1, 1, TOP_K * ts), lambda i: (i, 0, 0), memory_space=pltpu.SMEM),
            pl.BlockSpec((ts, d), lambda i: (i, 0)),
            pl.BlockSpec(memory_space=pl.ANY),
        ],
        out_specs=pl.BlockSpec(memory_space=pl.ANY),
        out_shape=jax.ShapeDtypeStruct((p_rows, d), h.dtype),
        scratch_shapes=[pltpu.SemaphoreType.DMA(())],
        input_output_aliases={2: 0},
        compiler_params=pltpu.CompilerParams(dimension_semantics=("arbitrary",), disable_bounds_checks=True),
        name="moe_scatter",
    )(pos.reshape(nt, 1, TOP_K * ts), h, xs0)


def _expert_body(te_ref, nu_ref, x_ref, w1_ref, w3_ref, w2_ref, o_ref):
    i = pl.program_id(0)
    j = pl.program_id(1)

    @pl.when(j == 0)
    def _():
        o_ref[...] = jnp.zeros_like(o_ref)

    @pl.when(i < nu_ref[0])
    def _():
        x = x_ref[...].astype(BF16)
        a = jnp.dot(x, w1_ref[0], preferred_element_type=F32)
        b = jnp.dot(x, w3_ref[0], preferred_element_type=F32)
        mid = (a * jax.nn.sigmoid(a) * b).astype(BF16)
        o_ref[...] += jnp.dot(mid, w2_ref[0], preferred_element_type=F32)


def moe_experts(xs, w1, w3, w2, tile_expert, n_used, tm):
    p_rows, d = xs.shape
    n_exp, _, d_exp = w1.shape
    tf = MOE_TF if d_exp % MOE_TF == 0 else d_exp
    nj = d_exp // tf
    n_tiles = p_rows // tm

    def jj(i, j, nu):
        return jnp.where(i < nu[0], j, nj - 1)

    grid_spec = pltpu.PrefetchScalarGridSpec(
        num_scalar_prefetch=2,
        grid=(n_tiles, nj),
        in_specs=[
            pl.BlockSpec((tm, d), lambda i, j, te, nu: (i, 0)),
            pl.BlockSpec((1, d, tf), lambda i, j, te, nu: (te[i], 0, jj(i, j, nu))),
            pl.BlockSpec((1, d, tf), lambda i, j, te, nu: (te[i], 0, jj(i, j, nu))),
            pl.BlockSpec((1, tf, d), lambda i, j, te, nu: (te[i], jj(i, j, nu), 0)),
        ],
        out_specs=pl.BlockSpec((tm, d), lambda i, j, te, nu: (i, 0)),
    )
    return pl.pallas_call(
        _expert_body,
        grid_spec=grid_spec,
        out_shape=jax.ShapeDtypeStruct((p_rows, d), F32),
        compiler_params=pltpu.CompilerParams(
            dimension_semantics=("arbitrary", "arbitrary"),
            vmem_limit_bytes=56 * 1024 * 1024),
        name="moe_experts",
    )(tile_expert, n_used, xs, w1, w3, w2)


def _combine_body(pos_ref, pos_next_ref, gate_ref, ys_ref, x_ref, gain_ref, mgate_ref, y_ref, buf, sem, *, ts):
    i = pl.program_id(0)
    slot = i % 2

    def gather(p_ref, s):
        def issue(r, c):
            for k in range(TOP_K):
                _row_copy(ys_ref, p_ref[0, 0, TOP_K * r + k], buf.at[s, k], r, sem.at[s]).start()
            return c
        lax.fori_loop(0, ts, issue, 0, unroll=ISSUE_UNROLL)

    @pl.when(i == 0)
    def _():
        gather(pos_ref, slot)

    @pl.when(i + 1 < pl.num_programs(0))
    def _():
        gather(pos_next_ref, 1 - slot)

    for k in range(TOP_K):
        pltpu.make_async_copy(ys_ref.at[pl.ds(0, ts)], buf.at[slot, k], sem.at[slot]).wait()
    g = gate_ref[...]
    y = g[:, 2:3] * buf[slot, 0] + g[:, 3:4] * buf[slot, 1]
    y_ref[...] = x_ref[...] + mgate_ref[0] * _rms(y, gain_ref[...])


def moe_combine(ys, pos, route, x, gain, mod, tiles_per_batch):
    t = route.shape[0]
    d = ys.shape[1]
    ts = min(ROW_TILE, t)
    nt = t // ts
    pos3 = pos.reshape(nt, 1, TOP_K * ts)
    return pl.pallas_call(
        functools.partial(_combine_body, ts=ts),
        grid=(nt,),
        in_specs=[
            pl.BlockSpec((1, 1, TOP_K * ts), lambda i: (i, 0, 0), memory_space=pltpu.SMEM),
            pl.BlockSpec((1, 1, TOP_K * ts), lambda i: (jnp.minimum(i + 1, nt - 1), 0, 0), memory_space=pltpu.SMEM),
            pl.BlockSpec((ts, LANES), lambda i: (i, 0)),
            pl.BlockSpec(memory_space=pl.ANY),
            pl.BlockSpec((ts, d), lambda i: (i, 0)),
            pl.BlockSpec((1, d), lambda i: (0, 0)),
            _mod_spec(MOD_GATE_F, tiles_per_batch, 0, d),
        ],
        out_specs=pl.BlockSpec((ts, d), lambda i: (i, 0)),
        out_shape=jax.ShapeDtypeStruct((t, d), F32),
        scratch_shapes=[pltpu.VMEM((2, TOP_K, ts, d), F32), pltpu.SemaphoreType.DMA((2,))],
        compiler_params=pltpu.CompilerParams(dimension_semantics=("arbitrary",), disable_bounds_checks=True),
        name="moe_combine",
    )(pos3, pos3, route, ys, x, gain.reshape(1, d), mod)


def moe_layer(tok, router, w1, w3, w2, x, gain, mod, tiles_per_batch):
    t = tok.shape[0]
    tm = min(MOE_TM, t)
    route, cnt = moe_route(tok, router)

    counts = cnt[0, :N_EXPERTS].astype(I32)
    padded = ((counts + tm - 1) // tm) * tm
    ends = jnp.cumsum(padded)
    starts = ends - padded
    expert = route[:, 0:TOP_K].astype(I32)
    rank = route[:, 4:4 + TOP_K].astype(I32)
    start_of = jnp.sum(jnp.where(expert[..., None] == jnp.arange(N_EXPERTS), starts, 0), axis=-1)
    pos = start_of + rank

    n_tiles = (TOP_K * t) // tm + N_EXPERTS
    n_used = (ends[-1] // tm).astype(I32)
    tile_idx = jnp.minimum(jnp.arange(n_tiles, dtype=I32), n_used - 1)
    tile_expert = jnp.sum(tile_idx[:, None] * tm >= ends[None, :], axis=-1).astype(I32)
    tile_expert = jnp.minimum(tile_expert, N_EXPERTS - 1)

    xs = moe_scatter(tok, pos, n_tiles * tm)
    ys = moe_experts(xs, w1.astype(BF16), w3.astype(BF16), w2.astype(BF16), tile_expert, n_used.reshape(1), tm)
    return moe_combine(ys, pos, route, x, gain, mod, tiles_per_batch)


GC = 128
GD = GDN_HEAD_DIM
PAD = 8
GDN_UNROLL = 9
GDN_BASE = 16
N_MERGE = 3
M_INCL_F, M_STRICT_F, M_INCL_B, M_STRICT_B, M_EYE, M_BASE, M_OFF0 = 0, 1, 2, 3, 4, 5, 6
N_MASKS = M_OFF0 + N_MERGE


def _bdot(a, b):
    return jnp.dot(a.astype(BF16), b.astype(BF16), preferred_element_type=F32)


def _bdot_nt(a, b):
    return lax.dot_general(a.astype(BF16), b.astype(BF16), (((1,), (1,)), ((), ())), preferred_element_type=F32)


def _gdn_body(alog_ref, dtb_ref, q_ref, k_ref, v_ref, z_ref, ab_ref, cq_ref, ck_ref, cv_ref, onorm_ref,
              o_ref, pq_s, pk_s, pv_s, o_s, b_s, gq_s, gate_s, state_s, mask_s,
              *, n_ctx_chunks, n_chunks, n_heads, unroll):
    h = pl.program_id(1)
    ltot = n_chunks * GC
    lc = n_ctx_chunks * GC

    row = lax.broadcasted_iota(I32, (GC, GC), 0)
    col = lax.broadcasted_iota(I32, (GC, GC), 1)
    mask_s[M_INCL_F] = (row >= col).astype(F32)
    mask_s[M_STRICT_F] = (row > col).astype(F32)
    mask_s[M_INCL_B] = (row <= col).astype(F32)
    mask_s[M_STRICT_B] = (row < col).astype(F32)
    mask_s[M_EYE] = (row == col).astype(F32)
    mask_s[M_BASE] = ((row // GDN_BASE) == (col // GDN_BASE)).astype(F32)
    for lvl in range(N_MERGE):
        s = GDN_BASE << lvl
        mask_s[M_OFF0 + lvl] = (((row // (2 * s)) == (col // (2 * s))) & ((row // s) != (col // s))).astype(F32)

    for src_ref, dst in ((q_ref, pq_s), (k_ref, pk_s), (v_ref, pv_s)):
        for off in (0, PAD + lc, 2 * PAD + ltot):
            dst[pl.ds(off, PAD), :] = jnp.zeros((PAD, GD), F32)
        dst[pl.ds(PAD, lc), :] = src_ref[0, pl.ds(0, lc), :].astype(F32)
        dst[pl.ds(2 * PAD + lc, ltot - lc), :] = src_ref[0, pl.ds(lc, ltot - lc), :].astype(F32)

    for d in range(2):
        a = ab_ref[0, d * 2 * n_heads + h]
        b = ab_ref[0, d * 2 * n_heads + n_heads + h]
        xa = a + dtb_ref[d, h]
        softplus = jnp.maximum(xa, 0.0) + jnp.log1p(jnp.exp(-jnp.abs(xa)))
        g = -jnp.exp(alog_ref[d, h]) * softplus
        beta = 1.0 / (1.0 + jnp.exp(-b))
        tri = mask_s[M_INCL_B if d == 0 else M_INCL_F].astype(BF16)
        g_hi = g.astype(BF16)
        g_lo = (g - g_hi.astype(F32)).astype(BF16)
        gam = (jnp.dot(g_hi, tri, preferred_element_type=F32) + jnp.dot(g_lo, tri, preferred_element_type=F32))
        gate_s[d, 0] = gam
        gate_s[d, 1] = beta
        gate_s[d, 2] = jnp.broadcast_to(jnp.sum(g, axis=-1, keepdims=True), g.shape)

    def conv_chunk(c, pad_ref, w_ref, normalise, scale):
        start = pl.multiple_of(c * GC + jnp.where(c >= n_ctx_chunks, PAD, 0), PAD)
        acc = jnp.zeros((GC, GD), F32)
        for j in range(CONV_K):
            tap = pad_ref[pl.ds(start + (PAD - CONV_K // 2 + j), GC), :]
            acc = acc + tap * w_ref[j:j + 1, :].astype(F32)
        y = acc * jax.nn.sigmoid(acc)
        if normalise:
            y = y * (lax.rsqrt(jnp.sum(y * y, axis=-1, keepdims=True) + EPS) * scale)
        return y

    def group_chunks(i):
        return [jnp.minimum(i * unroll + u, n_chunks - 1) for u in range(unroll)]

    def prep(i, carry):
        chunks = group_chunks(i)
        qs = [conv_chunk(c, pq_s, cq_ref, True, GD ** -0.5) for c in chunks]
        ks = [conv_chunk(c, pk_s, ck_ref, True, 1.0) for c in chunks]
        vs = [conv_chunk(c, pv_s, cv_ref, False, 1.0) for c in chunks]
        kqs = [_bdot_nt(jnp.concatenate([k, q], axis=0), k) for k, q in zip(ks, qs)]
        kts = [k.T for k in ks]
        lanes = [(u, d) for u in range(unroll) for d in range(2)]
        lms, dbs, e_cols, kdts = [], [], [], []
        for u, d in lanes:
            c = chunks[u]
            gam_row = gate_s[d, 0, pl.ds(c, 1), :]
            beta_row = gate_s[d, 1, pl.ds(c, 1), :]
            tot_row = gate_s[d, 2, pl.ds(c, 1), :]
            gam_col = jnp.sum(mask_s[M_EYE] * gam_row, axis=1, keepdims=True)
            db = jnp.exp((gam_col - gam_row) * mask_s[M_INCL_F if d == 0 else M_INCL_B]) * beta_row
            dbs.append(db)
            lms.append(kqs[u][:GC] * db * mask_s[M_STRICT_F if d == 0 else M_STRICT_B])
            e_cols.append(jnp.exp(gam_col))
            kdts.append(kts[u] * (jnp.exp(tot_row - gam_row) * beta_row))
        xs = [-lm * mask_s[M_BASE] for lm in lms]
        ts = [mask_s[M_EYE] + x for x in xs]
        for _ in range(3):
            xs = [_bdot(x, x) for x in xs]
            ts = [t + _bdot(t, x) for t, x in zip(ts, xs)]
        for lvl in range(N_MERGE):
            ys = [_bdot(lm * mask_s[M_OFF0 + lvl], t) for lm, t in zip(lms, ts)]
            ts = [t - _bdot(t, y) for t, y in zip(ts, ys)]
        wus = [_bdot(t, jnp.concatenate([ks[u] * e, vs[u]], axis=1))
               for t, e, (u, d) in zip(ts, e_cols, lanes)]
        x12s = []
        for wu, db, kdt, (u, d) in zip(wus, dbs, kdts, lanes):
            qkb = kqs[u][GC:] * db * mask_s[M_INCL_F if d == 0 else M_INCL_B]
            x12s.append(_bdot(jnp.concatenate([kdt, qkb], axis=0), wu))
        for x12, e, (u, d) in zip(x12s, e_cols, lanes):
            t0 = pl.multiple_of(chunks[u] * GC, GC)
            b_s[d, pl.ds(t0, GC), :] = x12[:GD, GD:]
            o_s[d, pl.ds(t0, GC), :] = x12[GD:, GD:]
            q_eff = qs[u] * e - x12[GD:, :GD]
            gq_s[d, pl.ds(pl.multiple_of(2 * t0, 2 * GC), 2 * GC), :] = (
                jnp.concatenate([x12[:GD, :GD], q_eff], axis=0).astype(BF16))
        return carry

    lax.fori_loop(0, -(-n_chunks // unroll), prep, 0)

    state_s[...] = jnp.zeros_like(state_s)

    def scan(s, carry):
        for d in range(2):
            if d == 0:
                c = s
            else:
                c = jnp.where(s < n_ctx_chunks, n_ctx_chunks - 1 - s, n_chunks - 1 - (s - n_ctx_chunks))
            t0 = pl.multiple_of(c * GC, GC)
            st = state_s[d]
            r = jnp.dot(gq_s[d, pl.ds(pl.multiple_of(2 * t0, 2 * GC), 2 * GC), :], st.astype(BF16),
                        preferred_element_type=F32)
            cd = jnp.exp(gate_s[d, 2, pl.ds(c, 1), 0:1])
            state_s[d] = st * cd - r[:GD] + b_s[d, pl.ds(t0, GC), :]
            o_s[d, pl.ds(t0, GC), :] += r[GD:]
        return carry

    lax.fori_loop(0, n_chunks, scan, 0)

    gain = onorm_ref[...].astype(F32)

    def finish(c, carry):
        t0 = pl.multiple_of(lc + c * GC, GC)
        o = o_s[0, pl.ds(t0, GC), :] + o_s[1, pl.ds(t0, GC), :]
        z = z_ref[0, pl.ds(t0, GC), :].astype(F32)
        y = o * lax.rsqrt(jnp.mean(o * o, axis=-1, keepdims=True) + EPS) * gain
        y = y * (z * jax.nn.sigmoid(z))
        o_ref[0, pl.ds(pl.multiple_of(c * GC, GC), GC), :] = y.astype(o_ref.dtype)
        return carry

    lax.fori_loop(0, n_chunks - n_ctx_chunks, finish, 0, unroll=4)


def gdn_core(p_all, ab_rows, conv_w, a_log, dt_bias, out_norm, n_ctx):
    bsz, ltot, _ = p_all.shape
    n_heads = a_log.shape[1]
    n_chunks = ltot // GC
    n_ctx_chunks = n_ctx // GC
    lat = ltot - n_ctx
    n_rows = ab_rows.shape[2]
    unroll = min(GDN_UNROLL, n_chunks)
    body = functools.partial(_gdn_body, n_ctx_chunks=n_ctx_chunks, n_chunks=n_chunks, n_heads=n_heads,
                             unroll=unroll)
    smem = pl.BlockSpec(memory_space=pltpu.SMEM)

    def col(off):
        return pl.BlockSpec((1, ltot, GD), lambda b, h: (b, 0, off * n_heads + h))

    def cw(off):
        return pl.BlockSpec((CONV_K, GD), lambda b, h: (0, off * n_heads + h))

    return pl.pallas_call(
        body,
        grid=(bsz, n_heads),
        in_specs=[smem, smem, col(0), col(1), col(2), col(3),
                  pl.BlockSpec((1, 4 * n_heads, n_rows, GC), lambda b, h: (b, 0, 0, 0)),
                  cw(0), cw(1), cw(2),
                  pl.BlockSpec((1, GD), lambda b, h: (0, 0))],
        out_specs=pl.BlockSpec((1, lat, GD), lambda b, h: (b, 0, h)),
        out_shape=jax.ShapeDtypeStruct((bsz, lat, n_heads * GD), BF16),
        scratch_shapes=[
            pltpu.VMEM((ltot + 3 * PAD, GD), F32),
            pltpu.VMEM((ltot + 3 * PAD, GD), F32),
            pltpu.VMEM((ltot + 3 * PAD, GD), F32),
            pltpu.VMEM((2, ltot, GD), F32),
            pltpu.VMEM((2, ltot, GD), F32),
            pltpu.VMEM((2, 2 * ltot, GD), BF16),
            pltpu.VMEM((2, 3, n_rows, GC), F32),
            pltpu.VMEM((2, GD, GD), F32),
            pltpu.VMEM((N_MASKS, GC, GC), F32),
        ],
        compiler_params=pltpu.CompilerParams(
            dimension_semantics=("arbitrary", "arbitrary"),
            vmem_limit_bytes=56 * 1024 * 1024),
        name="gdn_core",
    )(a_log.astype(F32), dt_bias.astype(F32), p_all, p_all, p_all, p_all, ab_rows,
      conv_w, conv_w, conv_w, out_norm.reshape(1, GD))


def gdn_gate_rows(ab, n_heads):
    bsz, ltot, _ = ab.shape
    n_chunks = ltot // GC
    n_rows = -(-n_chunks // 8) * 8
    t = jnp.transpose(ab.astype(F32), (0, 2, 1)).reshape(bsz, 4 * n_heads, n_chunks, GC)
    return jnp.pad(t, ((0, 0), (0, 0), (0, n_rows - n_chunks), (0, 0)))


TM = 256
FFN_TF = 256
MOD_TN = 512
MOD_SHIFT_M, MOD_SCALE_M, MOD_GATE_M, MOD_SHIFT_F, MOD_SCALE_F, MOD_GATE_F = range(6)
N_MOD = 6
RESIDENT = dict(pipeline_mode=pl.Buffered(1))
VMEM_LIMIT = 56 * 1024 * 1024


def _rms(x, gain):
    return x * lax.rsqrt(jnp.mean(x * x, axis=-1, keepdims=True) + EPS) * gain


def _mod_spec(k, tiles_per_batch, ctx_tiles, d):
    def index(i, *_):
        seg = jnp.where(i % tiles_per_batch >= ctx_tiles, 1, 0)
        return ((i // tiles_per_batch) * 2 + seg) * N_MOD + k, 0, 0
    return pl.BlockSpec((1, 1, d), index)


def _const_spec(shape):
    return pl.BlockSpec(shape, lambda i, *_: (0,) * len(shape), **RESIDENT)


def _mod_body(c_ref, w_ref, b_ref, o_ref):
    c = c_ref[...]
    s = (c * jax.nn.sigmoid(c)).astype(BF16)
    o_ref[...] = jnp.dot(s, w_ref[...].astype(BF16), preferred_element_type=F32) + b_ref[...]


def modulation(c, c_ctx, mod_w, mod_b):
    bsz, d = c.shape
    n = mod_w.shape[1]
    rows = -(-(bsz + 1) // 8) * 8
    cc = jnp.zeros((rows, d), F32).at[:bsz].set(c).at[bsz].set(c_ctx)
    out = pl.pallas_call(
        _mod_body,
        grid=(n // MOD_TN,),
        in_specs=[pl.BlockSpec((rows, d), lambda j: (0, 0)),
                  pl.BlockSpec((d, MOD_TN), lambda j: (0, j)),
                  pl.BlockSpec((1, MOD_TN), lambda j: (0, j))],
        out_specs=pl.BlockSpec((rows, MOD_TN), lambda j: (0, j)),
        out_shape=jax.ShapeDtypeStruct((rows, n), F32),
        compiler_params=pltpu.CompilerParams(dimension_semantics=("arbitrary",)),
        name="modulation",
    )(cc, mod_w, mod_b.reshape(1, n))
    lat = out[:bsz].reshape(bsz, 1, N_MOD, d)
    ctx = jnp.broadcast_to(out[bsz].reshape(1, 1, N_MOD, d), (bsz, 1, N_MOD, d))
    return jnp.concatenate([ctx, lat], axis=1).reshape(bsz * 2 * N_MOD, 1, d)


def _pre_mm_body(x_ref, gain_ref, sh_ref, sc_ref, w_ref, cos_ref, sin_ref, o_ref, *, rope_lo, rope_hi, q_hi):
    h = (_rms(x_ref[...], gain_ref[...]) * (1.0 + sc_ref[0]) + sh_ref[0]).astype(BF16)
    acc = jnp.dot(h, w_ref[...], preferred_element_type=F32)
    lane = lax.broadcasted_iota(I32, (acc.shape[0], LANES), 1)
    first_half = (lane % (HEAD_DIM // 2)) < (HEAD_DIM // 4)
    for s in range(0, acc.shape[1], LANES):
        blk = acc[:, s:s + LANES]
        if rope_lo <= s < rope_hi:
            partner = jnp.where(first_half, pltpu.roll(blk, LANES - HEAD_DIM // 4, 1),
                                pltpu.roll(blk, HEAD_DIM // 4, 1))
            blk = blk * cos_ref[...] + partner * sin_ref[...]
            if s < q_hi:
                blk = blk * (HEAD_DIM ** -0.5)
        o_ref[:, s:s + LANES] = blk.astype(o_ref.dtype)


def rope_tables(n_ctx, n_lat):
    half = HEAD_DIM // 2
    inv = ROPE_THETA ** (-jnp.arange(0, half, 2, dtype=F32) / half)
    t = jnp.arange(n_lat)
    pos = jnp.stack([(t // GRID_W).astype(F32), (t % GRID_W).astype(F32)], axis=1)
    lane = jnp.arange(LANES)
    part = (lane % HEAD_DIM) // half
    freq = lane % (half // 2)
    ang = pos[:, part] * inv[freq][None, :]
    sign = jnp.where((lane % half) < half // 2, -1.0, 1.0)
    cos = jnp.concatenate([jnp.ones((n_ctx, LANES), F32), jnp.cos(ang)], axis=0)
    sin = jnp.concatenate([jnp.zeros((n_ctx, LANES), F32), jnp.sin(ang) * sign], axis=0)
    return cos, sin


def pre_mm(x, gain, mod, w, cos, sin, tiles_per_batch, ctx_tiles):
    r, d = x.shape
    n = w.shape[1]
    body = functools.partial(_pre_mm_body, rope_lo=POOL_DIM, rope_hi=POOL_DIM + Q_DIM + KV_DIM, q_hi=POOL_DIM + Q_DIM)
    return pl.pallas_call(
        body,
        grid=(r // TM,),
        in_specs=[pl.BlockSpec((TM, d), lambda i: (i, 0)),
                  _const_spec((1, d)),
                  _mod_spec(MOD_SHIFT_M, tiles_per_batch, ctx_tiles, d),
                  _mod_spec(MOD_SCALE_M, tiles_per_batch, ctx_tiles, d),
                  _const_spec((d, n)),
                  pl.BlockSpec((TM, LANES), lambda i: (i % tiles_per_batch, 0)),
                  pl.BlockSpec((TM, LANES), lambda i: (i % tiles_per_batch, 0))],
        out_specs=pl.BlockSpec((TM, n), lambda i: (i, 0)),
        out_shape=jax.ShapeDtypeStruct((r, n), BF16),
        compiler_params=pltpu.CompilerParams(dimension_semantics=("arbitrary",), vmem_limit_bytes=VMEM_LIMIT),
        name="l0_norm_proj_rope",
    )(x, gain.reshape(1, d), mod, mod, w, cos, sin)


def _post_mm_body(*refs, n_a):
    a_refs = refs[:n_a]
    w_refs = refs[n_a:2 * n_a]
    x_ref, gpost_ref, gate_ref, gpre_ref, sh_ref, sc_ref, xo_ref, ho_ref = refs[2 * n_a:]
    y = jnp.dot(a_refs[0][...], w_refs[0][...], preferred_element_type=F32)
    for a_ref, w_ref in zip(a_refs[1:], w_refs[1:]):
        y = y + jnp.dot(a_ref[...], w_ref[...], preferred_element_type=F32)
    xn = x_ref[...] + gate_ref[0] * _rms(y, gpost_ref[...])
    xo_ref[...] = xn
    ho_ref[...] = (_rms(xn, gpre_ref[...]) * (1.0 + sc_ref[0]) + sh_ref[0]).astype(ho_ref.dtype)


def post_mm(a_list, w_list, x, x_tile_offset, g_post, g_pre, mod, tiles_per_batch, ctx_tiles, x_tiles_per_batch,
            h_dtype, name):
    r = a_list[0].shape[0]
    d = w_list[0].shape[1]
    n_a = len(a_list)

    def x_index(i):
        return (i // tiles_per_batch) * x_tiles_per_batch + x_tile_offset + i % tiles_per_batch, 0

    in_specs = ([pl.BlockSpec((TM, a.shape[1]), lambda i: (i, 0)) for a in a_list]
                + [_const_spec(w.shape) for w in w_list]
                + [pl.BlockSpec((TM, d), x_index),
                   _const_spec((1, d)),
                   _mod_spec(MOD_GATE_M, tiles_per_batch, ctx_tiles, d),
                   _const_spec((1, d)),
                   _mod_spec(MOD_SHIFT_F, tiles_per_batch, ctx_tiles, d),
                   _mod_spec(MOD_SCALE_F, tiles_per_batch, ctx_tiles, d)])
    return pl.pallas_call(
        functools.partial(_post_mm_body, n_a=n_a),
        grid=(r // TM,),
        in_specs=in_specs,
        out_specs=[pl.BlockSpec((TM, d), lambda i: (i, 0)), pl.BlockSpec((TM, d), lambda i: (i, 0))],
        out_shape=[jax.ShapeDtypeStruct((r, d), F32), jax.ShapeDtypeStruct((r, d), h_dtype)],
        compiler_params=pltpu.CompilerParams(dimension_semantics=("arbitrary",), vmem_limit_bytes=VMEM_LIMIT),
        name=name,
    )(*a_list, *w_list, x, g_post.reshape(1, d), mod, g_pre.reshape(1, d), mod, mod)


def _ffn_body(h_ref, w1_ref, w3_ref, w2_ref, x_ref, gpost_ref, gate_ref, gpre_ref, sh_ref, sc_ref,
              xo_ref, ho_ref, acc_ref):
    h = h_ref[...]
    n_f = w1_ref.shape[1]
    for f0 in range(0, n_f, FFN_TF):
        a = jnp.dot(h, w1_ref[:, f0:f0 + FFN_TF], preferred_element_type=F32)
        b = jnp.dot(h, w3_ref[:, f0:f0 + FFN_TF], preferred_element_type=F32)
        mid = (a * jax.nn.sigmoid(a) * b).astype(BF16)
        part = jnp.dot(mid, w2_ref[f0:f0 + FFN_TF, :], preferred_element_type=F32)
        if f0 == 0:
            acc_ref[...] = part
        else:
            acc_ref[...] += part
    xn = x_ref[...] + gate_ref[0] * _rms(acc_ref[...], gpost_ref[...])
    xo_ref[...] = xn
    ho_ref[...] = (_rms(xn, gpre_ref[...]) * (1.0 + sc_ref[0]) + sh_ref[0]).astype(ho_ref.dtype)


def ffn_layer(h, w1, w3, w2, x, g_post, mod, g_pre_next, mod_next, tiles_per_batch, ctx_tiles):
    r, d = h.shape
    return pl.pallas_call(
        _ffn_body,
        grid=(r // TM,),
        in_specs=[pl.BlockSpec((TM, d), lambda i: (i, 0)),
                  _const_spec(w1.shape), _const_spec(w3.shape), _const_spec(w2.shape),
                  pl.BlockSpec((TM, d), lambda i: (i, 0)),
                  _const_spec((1, d)),
                  _mod_spec(MOD_GATE_F, tiles_per_batch, ctx_tiles, d),
                  _const_spec((1, d)),
                  _mod_spec(MOD_SHIFT_M, tiles_per_batch, ctx_tiles, d),
                  _mod_spec(MOD_SCALE_M, tiles_per_batch, ctx_tiles, d)],
        out_specs=[pl.BlockSpec((TM, d), lambda i: (i, 0)), pl.BlockSpec((TM, d), lambda i: (i, 0))],
        out_shape=[jax.ShapeDtypeStruct((r, d), F32), jax.ShapeDtypeStruct((r, d), BF16)],
        scratch_shapes=[pltpu.VMEM((TM, d), F32)],
        compiler_params=pltpu.CompilerParams(dimension_semantics=("arbitrary",), vmem_limit_bytes=VMEM_LIMIT),
        name="l0_ffn",
    )(h, w1, w3, w2, x, g_post.reshape(1, d), mod, g_pre_next.reshape(1, d), mod_next, mod_next)


def _proj_body(h_ref, w_ref, wg_ref, o_ref, og_ref):
    h = h_ref[...]
    o_ref[...] = jnp.dot(h, w_ref[...], preferred_element_type=F32).astype(o_ref.dtype)
    og_ref[...] = jnp.dot(h, wg_ref[...], preferred_element_type=F32)


def gdn_proj(h, w, w_gate):
    r, d = h.shape
    n = w.shape[1]
    return pl.pallas_call(
        _proj_body,
        grid=(r // TM,),
        in_specs=[pl.BlockSpec((TM, d), lambda i: (i, 0)), _const_spec(w.shape), _const_spec(w_gate.shape)],
        out_specs=[pl.BlockSpec((TM, n), lambda i: (i, 0)), pl.BlockSpec((TM, LANES), lambda i: (i, 0))],
        out_shape=[jax.ShapeDtypeStruct((r, n), BF16), jax.ShapeDtypeStruct((r, LANES), F32)],
        compiler_params=pltpu.CompilerParams(dimension_semantics=("arbitrary",), vmem_limit_bytes=VMEM_LIMIT),
        name="l1_proj",
    )(h, w, w_gate)


def _attn_body(sink_ref, q_ref, kc_ref, vc_ref, k0_ref, k1_ref, k2_ref, v0_ref, v1_ref, v2_ref, o_ref,
               *, n_ctx_blocks, n_blocks):
    n = pl.program_id(1)
    rows = GQA_GROUP * BLOCK
    qi = lax.broadcasted_iota(I32, (rows, 3 * BLOCK), 0) % BLOCK
    kj = lax.broadcasted_iota(I32, (rows, 3 * BLOCK), 1)
    first = jnp.where(n > n_ctx_blocks, 0, BLOCK)
    last = jnp.where(n < n_blocks - 1, 3 * BLOCK, 2 * BLOCK)
    last = jnp.where(n >= n_ctx_blocks, last, 0)
    band_ok = (kj >= qi) & (kj <= qi + 2 * WINDOW) & (kj >= first) & (kj < last)
    q = q_ref[0]
    nt = (((1,), (1,)), ((), ()))
    groups = range(N_KV_HEADS)
    cols = [slice(g * HEAD_DIM, (g + 1) * HEAD_DIM) for g in groups]
    heads = [[g * GQA_GROUP + i for i in range(GQA_GROUP)] for g in groups]
    q_g = [jnp.concatenate([q[:, hd * HEAD_DIM:(hd + 1) * HEAD_DIM] for hd in heads[g]], axis=0) for g in groups]
    k_b = [jnp.concatenate([k0_ref[0][:, c], k1_ref[0][:, c], k2_ref[0][:, c]], axis=0) for c in cols]
    ones_b = jnp.ones((3 * BLOCK, HEAD_DIM), BF16)
    ones_c = jnp.ones((kc_ref.shape[1], HEAD_DIM), BF16)
    v_b = [jnp.concatenate([jnp.concatenate([v0_ref[0][:, c], v1_ref[0][:, c], v2_ref[0][:, c]], axis=0), ones_b],
                           axis=1) for c in cols]
    v_c = [jnp.concatenate([vc_ref[0][:, c], ones_c], axis=1) for c in cols]
    s_c = [lax.dot_general(q_g[g], kc_ref[0][:, cols[g]], nt, preferred_element_type=F32) for g in groups]
    s_b = [lax.dot_general(q_g[g], k_b[g], nt, preferred_element_type=F32) for g in groups]
    s_b = [jnp.where(band_ok, s, NEG_INF) for s in s_b]
    sink = [jnp.concatenate([sink_ref[hd] + jnp.zeros((BLOCK, 1), F32) for hd in heads[g]], axis=0) for g in groups]
    m = [jnp.maximum(jnp.maximum(jnp.max(s_c[g], axis=-1, keepdims=True), jnp.max(s_b[g], axis=-1, keepdims=True)),
                     sink[g]) for g in groups]
    p_c = [jnp.exp(s_c[g] - m[g]) for g in groups]
    p_b = [jnp.exp(s_b[g] - m[g]) for g in groups]
    o_c = [jnp.dot(p_c[g].astype(BF16), v_c[g], preferred_element_type=F32) for g in groups]
    o_b = [jnp.dot(p_b[g].astype(BF16), v_b[g], preferred_element_type=F32) for g in groups]
    outs = []
    for g in groups:
        acc = o_c[g] + o_b[g]
        den = acc[:, HEAD_DIM:HEAD_DIM + 1] + jnp.exp(sink[g] - m[g])
        o = acc[:, :HEAD_DIM] / den
        outs += [o[i * BLOCK:(i + 1) * BLOCK] for i in range(GQA_GROUP)]
    o_ref[0] = jnp.concatenate(outs, axis=1).astype(o_ref.dtype)


def attention(p0, sinks, n_ctx):
    bsz, ltot, _ = p0.shape
    n_blocks = ltot // BLOCK
    n_ctx_blocks = n_ctx // BLOCK
    k_blk = (POOL_DIM + Q_DIM) // KV_DIM
    v_blk = k_blk + 1

    def band(dn, blk):
        return pl.BlockSpec((1, BLOCK, KV_DIM),
                            lambda b, n: (b, jnp.clip(n + dn, n_ctx_blocks, n_blocks - 1), blk))

    return pl.pallas_call(
        functools.partial(_attn_body, n_ctx_blocks=n_ctx_blocks, n_blocks=n_blocks),
        grid=(bsz, n_blocks),
        in_specs=[pl.BlockSpec(memory_space=pltpu.SMEM),
                  pl.BlockSpec((1, BLOCK, Q_DIM), lambda b, n: (b, n, POOL_DIM // Q_DIM)),
                  pl.BlockSpec((1, n_ctx, KV_DIM), lambda b, n: (b, 0, k_blk)),
                  pl.BlockSpec((1, n_ctx, KV_DIM), lambda b, n: (b, 0, v_blk)),
                  band(-1, k_blk), band(0, k_blk), band(1, k_blk),
                  band(-1, v_blk), band(0, v_blk), band(1, v_blk)],
        out_specs=pl.BlockSpec((1, BLOCK, Q_DIM), lambda b, n: (b, n, 0)),
        out_shape=jax.ShapeDtypeStruct((bsz, ltot, Q_DIM), BF16),
        compiler_params=pltpu.CompilerParams(dimension_semantics=("arbitrary", "arbitrary"),
                                             vmem_limit_bytes=VMEM_LIMIT),
        name="l0_attention",
    )(sinks.astype(F32), p0, p0, p0, p0, p0, p0, p0, p0, p0)


POOL_HALO = 16


def _pool_body(up_ref, uc_ref, un_ref, w_ref, scale_ref, o_ref, *, tiles_per_batch, ctx_tiles):
    j = pl.program_id(1)
    seg_lo = jnp.where(j < ctx_tiles, 0, ctx_tiles * TM)
    seg_hi = jnp.where(j < ctx_tiles, ctx_tiles * TM, tiles_per_batch * TM)
    halo = jnp.concatenate([up_ref[0][TM - POOL_HALO:], uc_ref[0], un_ref[0][:POOL_HALO]], axis=0)
    n_h = TM + 2 * POOL_HALO
    t = j * TM + lax.broadcasted_iota(I32, (TM, n_h), 0)
    pos = j * TM - POOL_HALO + lax.broadcasted_iota(I32, (TM, n_h), 1)
    t_col = j * TM + lax.broadcasted_iota(I32, (TM, 1), 0)
    cur = uc_ref[0]
    for g, w in enumerate(POOL_WINDOWS):
        cols = slice(g * POOL_GROUP_DIM, (g + 1) * POOL_GROUP_DIM)
        lo = jnp.maximum(t - w // 2, seg_lo)
        hi = jnp.minimum(t + w // 2, seg_hi)
        window = ((pos >= lo) & (pos < hi)).astype(BF16)
        cnt = (jnp.minimum(t_col + w // 2, seg_hi) - jnp.maximum(t_col - w // 2, seg_lo)).astype(F32)
        mean = jnp.dot(window, halo[:, cols], preferred_element_type=F32) / cnt
        delta = (mean - cur[:, cols].astype(F32)).astype(BF16)
        y = jnp.dot(delta, w_ref[g], preferred_element_type=F32) * scale_ref[:, cols]
        o_ref[0, :, cols] = y.astype(o_ref.dtype)


def pool_mixer(p0, pool_w, pool_scale, n_ctx):
    bsz, ltot, _ = p0.shape
    tiles_per_batch = ltot // TM
    ctx_tiles = n_ctx // TM

    def tile(dj):
        return pl.BlockSpec((1, TM, POOL_DIM), lambda b, j: (b, jnp.clip(j + dj, 0, tiles_per_batch - 1), 0))

    return pl.pallas_call(
        functools.partial(_pool_body, tiles_per_batch=tiles_per_batch, ctx_tiles=ctx_tiles),
        grid=(bsz, tiles_per_batch),
        in_specs=[tile(-1), tile(0), tile(1),
                  pl.BlockSpec(pool_w.shape, lambda b, j: (0, 0, 0)),
                  pl.BlockSpec((1, POOL_DIM), lambda b, j: (0, 0))],
        out_specs=pl.BlockSpec((1, TM, POOL_DIM), lambda b, j: (b, j, 0)),
        out_shape=jax.ShapeDtypeStruct((bsz, ltot, POOL_DIM), BF16),
        compiler_params=pltpu.CompilerParams(dimension_semantics=("arbitrary", "arbitrary")),
        name="l0_pool",
    )(p0, p0, p0, pool_w.astype(BF16), pool_scale.reshape(1, POOL_DIM).astype(F32))


def kernel(x, c, ctx, c_ctx, l0_mod_w, l0_mod_b, l0_mix_pre, l0_mix_post, l0_ffn_pre, l0_ffn_post, l0_w_in, l0_pool_w, l0_pool_scale, l0_sinks, l0_w_out, l0_ffn_w1, l0_ffn_w3, l0_ffn_w2, l1_mod_w, l1_mod_b, l1_mix_pre, l1_mix_post, l1_ffn_pre, l1_ffn_post, l1_w_in, l1_conv_w, l1_a_log, l1_dt_bias, l1_out_norm, l1_w_out, l1_router, l1_moe_w1, l1_moe_w3, l1_moe_w2):
    bsz, n_lat, d = x.shape
    n_ctx = ctx.shape[1]
    ltot = n_ctx + n_lat
    assert n_ctx % TM == 0 and n_lat % TM == 0 and n_ctx % GC == 0
    tiles = ltot // TM
    ctx_tiles = n_ctx // TM
    lat_tiles = n_lat // TM
    bf = lambda w: w.astype(BF16)

    x_all = jnp.concatenate([ctx, x], axis=1).reshape(bsz * ltot, d)
    mod0 = modulation(c, c_ctx, l0_mod_w, l0_mod_b)
    mod1 = modulation(c, c_ctx, l1_mod_w, l1_mod_b)
    cos, sin = rope_tables(n_ctx, n_lat)

    p0 = pre_mm(x_all, l0_mix_pre, mod0, bf(l0_w_in), cos, sin, tiles, ctx_tiles).reshape(bsz, ltot, -1)
    attn = attention(p0, l0_sinks, n_ctx).reshape(bsz * ltot, Q_DIM)
    pooled = pool_mixer(p0, l0_pool_w, l0_pool_scale, n_ctx).reshape(bsz * ltot, POOL_DIM)
    w_out0 = bf(l0_w_out)
    x1, h1 = post_mm([pooled, attn], [w_out0[:POOL_DIM], w_out0[POOL_DIM:]], x_all, 0, l0_mix_post, l0_ffn_pre,
                     mod0, tiles, ctx_tiles, tiles, BF16, "l0_out_proj")
    x2, h2 = ffn_layer(h1, bf(l0_ffn_w1), bf(l0_ffn_w3), bf(l0_ffn_w2), x1, l0_ffn_post, mod0, l1_mix_pre, mod1,
                       tiles, ctx_tiles)

    n_heads = l1_a_log.shape[1]
    qkvz = 4 * n_heads * GD
    w_in1 = bf(l1_w_in)
    w_gate = jnp.zeros((d, LANES), BF16).at[:, :4 * n_heads].set(w_in1[:, qkvz:])
    p1, ab = gdn_proj(h2, w_in1[:, :qkvz], w_gate)
    ab_rows = gdn_gate_rows(ab.reshape(bsz, ltot, LANES)[..., :4 * n_heads], n_heads)
    y = gdn_core(p1.reshape(bsz, ltot, qkvz), ab_rows, l1_conv_w, l1_a_log, l1_dt_bias, l1_out_norm, n_ctx)
    x3, h3 = post_mm([y.reshape(bsz * n_lat, n_heads * GD)], [bf(l1_w_out)], x2, ctx_tiles, l1_mix_post, l1_ffn_pre,
                     mod1, lat_tiles, 0, tiles, F32, "l1_out_proj")
    out = moe_layer(h3, l1_router, l1_moe_w1, l1_moe_w3, l1_moe_w2, x3, l1_ffn_post, mod1, lat_tiles)
    return out.reshape(bsz, n_lat, d)
```

```python
import functools
import math

import jax
import jax.numpy as jnp
from jax import lax
from jax.experimental import pallas as pl
from jax.experimental.pallas import tpu as pltpu

F32 = jnp.float32
BF16 = jnp.bfloat16
I32 = jnp.int32

LANES = 128
D_MODEL = 1024
GRID_W = 64
EPS = 1e-6
NEG_INF = -1e30

POOL_GROUPS = 4
POOL_GROUP_DIM = 128
POOL_DIM = POOL_GROUPS * POOL_GROUP_DIM
POOL_WINDOWS = (2, 4, 8, 16)
HEAD_DIM = 64
N_HEADS = 8
N_KV_HEADS = 2
GQA_GROUP = N_HEADS // N_KV_HEADS
Q_DIM = N_HEADS * HEAD_DIM
KV_DIM = N_KV_HEADS * HEAD_DIM
WINDOW = 128
BLOCK = 128
ROPE_THETA = 10000.0

GDN_HEADS = 8
GDN_HEAD_DIM = 128
GDN_DIM = GDN_HEADS * GDN_HEAD_DIM
CONV_K = 5
CHUNK = 64

N_EXPERTS = 8
TOP_K = 2

ROUTE_TILE = 512
MOE_TM = 512
MOE_TF = 1792
ROW_TILE = 256
ISSUE_UNROLL = 8


def _split_bf16(a):
    hi = a.astype(BF16)
    lo = (a - hi.astype(F32)).astype(BF16)
    return hi, lo


def _route_body(h_ref, rhi_ref, rlo_ref, route_ref, cnt_ref, carry_ref):
    i = pl.program_id(0)

    @pl.when(i == 0)
    def _():
        carry_ref[...] = jnp.zeros_like(carry_ref)

    h_hi, h_lo = _split_bf16(h_ref[...])
    r_hi = rhi_ref[...]
    r_lo = rlo_ref[...]
    logits = (jnp.dot(h_hi, r_hi, preferred_element_type=F32)
              + jnp.dot(h_hi, r_lo, preferred_element_type=F32)
              + jnp.dot(h_lo, r_hi, preferred_element_type=F32))
    tr = logits.shape[0]
    lane = lax.broadcasted_iota(I32, (tr, LANES), 1)
    logits = jnp.where(lane < N_EXPERTS, logits, -jnp.inf)
    m1 = jnp.max(logits, axis=-1, keepdims=True)
    i1 = jnp.min(jnp.where(logits == m1, lane, LANES), axis=-1, keepdims=True)
    rest = jnp.where(lane == i1, -jnp.inf, logits)
    m2 = jnp.max(rest, axis=-1, keepdims=True)
    i2 = jnp.min(jnp.where(rest == m2, lane, LANES), axis=-1, keepdims=True)
    e2 = jnp.exp(m2 - m1)
    g1 = 1.0 / (1.0 + e2)
    g2 = e2 / (1.0 + e2)

    onehot = ((lane == i1) | (lane == i2)).astype(F32)
    row = lax.broadcasted_iota(I32, (tr, tr), 0)
    col = lax.broadcasted_iota(I32, (tr, tr), 1)
    strict = (row > col).astype(BF16)
    before = jnp.dot(strict, onehot.astype(BF16), preferred_element_type=F32) + carry_ref[...]
    rank1 = jnp.sum(jnp.where(lane == i1, before, 0.0), axis=-1, keepdims=True)
    rank2 = jnp.sum(jnp.where(lane == i2, before, 0.0), axis=-1, keepdims=True)
    carry_ref[...] += jnp.sum(onehot, axis=0, keepdims=True)

    packed = jnp.where(lane == 0, i1.astype(F32), 0.0)
    packed = jnp.where(lane == 1, i2.astype(F32), packed)
    packed = jnp.where(lane == 2, g1, packed)
    packed = jnp.where(lane == 3, g2, packed)
    packed = jnp.where(lane == 4, rank1, packed)
    packed = jnp.where(lane == 5, rank2, packed)
    route_ref[...] = packed
    cnt_ref[...] = jnp.broadcast_to(carry_ref[...], cnt_ref.shape)


def moe_route(h, router):
    t, d = h.shape
    tr = min(ROUTE_TILE, t)
    r_pad = jnp.zeros((d, LANES), F32).at[:, :N_EXPERTS].set(router.astype(F32))
    r_hi, r_lo = _split_bf16(r_pad)
    route, cnt = pl.pallas_call(
        _route_body,
        grid=(t // tr,),
        in_specs=[
            pl.BlockSpec((tr, d), lambda i: (i, 0)),
            pl.BlockSpec((d, LANES), lambda i: (0, 0)),
            pl.BlockSpec((d, LANES), lambda i: (0, 0)),
        ],
        out_specs=[
            pl.BlockSpec((tr, LANES), lambda i: (i, 0)),
            pl.BlockSpec((8, LANES), lambda i: (0, 0)),
        ],
        out_shape=[
            jax.ShapeDtypeStruct((t, LANES), F32),
            jax.ShapeDtypeStruct((8, LANES), F32),
        ],
        scratch_shapes=[pltpu.VMEM((1, LANES), F32)],
        compiler_params=pltpu.CompilerParams(dimension_semantics=("arbitrary",)),
        name="moe_route",
    )(h, r_hi, r_lo)
    return route, cnt


def _row_copy(src_ref, src_row, dst_ref, dst_row, sem):
    return pltpu.make_async_copy(src_ref.at[pl.ds(src_row, 1)], dst_ref.at[pl.ds(dst_row, 1)], sem)


def _scatter_body(pos_ref, h_ref, xs_in_ref, xs_ref, sem, *, ts):
    del xs_in_ref

    def issue(r, c):
        for k in range(TOP_K):
            _row_copy(h_ref, r, xs_ref, pos_ref[0, 0, TOP_K * r + k], sem).start()
        return c

    lax.fori_loop(0, ts, issue, 0, unroll=ISSUE_UNROLL)
    for _ in range(TOP_K):
        pltpu.make_async_copy(h_ref, xs_ref.at[pl.ds(0, ts)], sem).wait()


def moe_scatter(h, pos, p_rows):
    t, d = h.shape
    ts = min(ROW_TILE, t)
    nt = t // ts
    xs0 = jnp.zeros((p_rows, d), h.dtype)
    return pl.pallas_call(
        functools.partial(_scatter_body, ts=ts),
        grid=(nt,),
        in_specs=[
            pl.BlockSpec((1, 1, TOP_K * ts), lambda i: (i, 0, 0), memory_space=pltpu.SMEM),
            pl.BlockSpec((ts, d), lambda i: (i, 0)),
            pl.BlockSpec(memory_space=pl.ANY),
        ],
        out_specs=pl.BlockSpec(memory_space=pl.ANY),
        out_shape=jax.ShapeDtypeStruct((p_rows, d), h.dtype),
        scratch_shapes=[pltpu.SemaphoreType.DMA(())],
        input_output_aliases={2: 0},
        compiler_params=pltpu.CompilerParams(dimension_semantics=("arbitrary",), disable_bounds_checks=True),
        name="moe_scatter",
    )(pos.reshape(nt, 1, TOP_K * ts), h, xs0)


def _expert_body(te_ref, nu_ref, x_ref, w1_ref, w3_ref, w2_ref, o_ref):
    i = pl.program_id(0)
    j = pl.program_id(1)

    @pl.when(j == 0)
    def _():
        o_ref[...] = jnp.zeros_like(o_ref)

    @pl.when(i < nu_ref[0])
    def _():
        x = x_ref[...].astype(BF16)
        a = jnp.dot(x, w1_ref[0], preferred_element_type=F32)
        b = jnp.dot(x, w3_ref[0], preferred_element_type=F32)
        mid = (a * jax.nn.sigmoid(a) * b).astype(BF16)
        o_ref[...] += jnp.dot(mid, w2_ref[0], preferred_element_type=F32)


def moe_experts(xs, w1, w3, w2, tile_expert, n_used, tm):
    p_rows, d = xs.shape
    n_exp, _, d_exp = w1.shape
    tf = MOE_TF if d_exp % MOE_TF == 0 else d_exp
    nj = d_exp // tf
    n_tiles = p_rows // tm

    def jj(i, j, nu):
        return jnp.where(i < nu[0], j, nj - 1)

    grid_spec = pltpu.PrefetchScalarGridSpec(
        num_scalar_prefetch=2,
        grid=(n_tiles, nj),
        in_specs=[
            pl.BlockSpec((tm, d), lambda i, j, te, nu: (i, 0)),
            pl.BlockSpec((1, d, tf), lambda i, j, te, nu: (te[i], 0, jj(i, j, nu))),
            pl.BlockSpec((1, d, tf), lambda i, j, te, nu: (te[i], 0, jj(i, j, nu))),
            pl.BlockSpec((1, tf, d), lambda i, j, te, nu: (te[i], jj(i, j, nu), 0)),
        ],
        out_specs=pl.BlockSpec((tm, d), lambda i, j, te, nu: (i, 0)),
    )
    return pl.pallas_call(
        _expert_body,
        grid_spec=grid_spec,
        out_shape=jax.ShapeDtypeStruct((p_rows, d), F32),
        compiler_params=pltpu.CompilerParams(
            dimension_semantics=("arbitrary", "arbitrary"),
            vmem_limit_bytes=56 * 1024 * 1024),
        name="moe_experts",
    )(tile_expert, n_used, xs, w1, w3, w2)


def _combine_body(pos_ref, pos_next_ref, gate_ref, ys_ref, x_ref, gain_ref, mgate_ref, y_ref, buf, sem, *, ts):
    i = pl.program_id(0)
    slot = i % 2

    def gather(p_ref, s):
        def issue(r, c):
            for k in range(TOP_K):
                _row_copy(ys_ref, p_ref[0, 0, TOP_K * r + k], buf.at[s, k], r, sem.at[s]).start()
            return c
        lax.fori_loop(0, ts, issue, 0, unroll=ISSUE_UNROLL)

    @pl.when(i == 0)
    def _():
        gather(pos_ref, slot)

    @pl.when(i + 1 < pl.num_programs(0))
    def _():
        gather(pos_next_ref, 1 - slot)

    for k in range(TOP_K):
        pltpu.make_async_copy(ys_ref.at[pl.ds(0, ts)], buf.at[slot, k], sem.at[slot]).wait()
    g = gate_ref[...]
    y = g[:, 2:3] * buf[slot, 0] + g[:, 3:4] * buf[slot, 1]
    y_ref[...] = x_ref[...] + mgate_ref[0] * _rms(y, gain_ref[...])


def moe_combine(ys, pos, route, x, gain, mod, tiles_per_batch):
    t = route.shape[0]
    d = ys.shape[1]
    ts = min(ROW_TILE, t)
    nt = t // ts
    pos3 = pos.reshape(nt, 1, TOP_K * ts)
    return pl.pallas_call(
        functools.partial(_combine_body, ts=ts),
        grid=(nt,),
        in_specs=[
            pl.BlockSpec((1, 1, TOP_K * ts), lambda i: (i, 0, 0), memory_space=pltpu.SMEM),
            pl.BlockSpec((1, 1, TOP_K * ts), lambda i: (jnp.minimum(i + 1, nt - 1), 0, 0), memory_space=pltpu.SMEM),
            pl.BlockSpec((ts, LANES), lambda i: (i, 0)),
            pl.BlockSpec(memory_space=pl.ANY),
            pl.BlockSpec((ts, d), lambda i: (i, 0)),
            pl.BlockSpec((1, d), lambda i: (0, 0)),
            _mod_spec(MOD_GATE_F, tiles_per_batch, 0, d),
        ],
        out_specs=pl.BlockSpec((ts, d), lambda i: (i, 0)),
        out_shape=jax.ShapeDtypeStruct((t, d), F32),
        scratch_shapes=[pltpu.VMEM((2, TOP_K, ts, d), F32), pltpu.SemaphoreType.DMA((2,))],
        compiler_params=pltpu.CompilerParams(dimension_semantics=("arbitrary",), disable_bounds_checks=True),
        name="moe_combine",
    )(pos3, pos3, route, ys, x, gain.reshape(1, d), mod)


def moe_layer(tok, router, w1, w3, w2, x, gain, mod, tiles_per_batch):
    t = tok.shape[0]
    tm = min(MOE_TM, t)
    route, cnt = moe_route(tok, router)

    counts = cnt[0, :N_EXPERTS].astype(I32)
    padded = ((counts + tm - 1) // tm) * tm
    ends = jnp.cumsum(padded)
    starts = ends - padded
    expert = route[:, 0:TOP_K].astype(I32)
    rank = route[:, 4:4 + TOP_K].astype(I32)
    start_of = jnp.sum(jnp.where(expert[..., None] == jnp.arange(N_EXPERTS), starts, 0), axis=-1)
    pos = start_of + rank

    n_tiles = (TOP_K * t) // tm + N_EXPERTS
    n_used = (ends[-1] // tm).astype(I32)
    tile_idx = jnp.minimum(jnp.arange(n_tiles, dtype=I32), n_used - 1)
    tile_expert = jnp.sum(tile_idx[:, None] * tm >= ends[None, :], axis=-1).astype(I32)
    tile_expert = jnp.minimum(tile_expert, N_EXPERTS - 1)

    xs = moe_scatter(tok, pos, n_tiles * tm)
    ys = moe_experts(xs, w1.astype(BF16), w3.astype(BF16), w2.astype(BF16), tile_expert, n_used.reshape(1), tm)
    return moe_combine(ys, pos, route, x, gain, mod, tiles_per_batch)


GC = 128
GD = GDN_HEAD_DIM
PAD = 8
GDN_UNROLL = 9
GDN_BASE = 16
N_MERGE = 3
M_INCL_F, M_STRICT_F, M_INCL_B, M_STRICT_B, M_EYE, M_BASE, M_OFF0 = 0, 1, 2, 3, 4, 5, 6
N_MASKS = M_OFF0 + N_MERGE


def _bdot(a, b):
    return jnp.dot(a.astype(BF16), b.astype(BF16), preferred_element_type=F32)


def _bdot_nt(a, b):
    return lax.dot_general(a.astype(BF16), b.astype(BF16), (((1,), (1,)), ((), ())), preferred_element_type=F32)


def _gdn_body(alog_ref, dtb_ref, q_ref, k_ref, v_ref, z_ref, ab_ref, cq_ref, ck_ref, cv_ref, onorm_ref,
              o_ref, pq_s, pk_s, pv_s, o_s, b_s, gq_s, gate_s, state_s, mask_s,
              *, n_ctx_chunks, n_chunks, n_heads, unroll):
    h = pl.program_id(1)
    ltot = n_chunks * GC
    lc = n_ctx_chunks * GC

    row = lax.broadcasted_iota(I32, (GC, GC), 0)
    col = lax.broadcasted_iota(I32, (GC, GC), 1)
    mask_s[M_INCL_F] = (row >= col).astype(F32)
    mask_s[M_STRICT_F] = (row > col).astype(F32)
    mask_s[M_INCL_B] = (row <= col).astype(F32)
    mask_s[M_STRICT_B] = (row < col).astype(F32)
    mask_s[M_EYE] = (row == col).astype(F32)
    mask_s[M_BASE] = ((row // GDN_BASE) == (col // GDN_BASE)).astype(F32)
    for lvl in range(N_MERGE):
        s = GDN_BASE << lvl
        mask_s[M_OFF0 + lvl] = (((row // (2 * s)) == (col // (2 * s))) & ((row // s) != (col // s))).astype(F32)

    for src_ref, dst in ((q_ref, pq_s), (k_ref, pk_s), (v_ref, pv_s)):
        for off in (0, PAD + lc, 2 * PAD + ltot):
            dst[pl.ds(off, PAD), :] = jnp.zeros((PAD, GD), F32)
        dst[pl.ds(PAD, lc), :] = src_ref[0, pl.ds(0, lc), :].astype(F32)
        dst[pl.ds(2 * PAD + lc, ltot - lc), :] = src_ref[0, pl.ds(lc, ltot - lc), :].astype(F32)

    for d in range(2):
        a = ab_ref[0, d * 2 * n_heads + h]
        b = ab_ref[0, d * 2 * n_heads + n_heads + h]
        xa = a + dtb_ref[d, h]
        softplus = jnp.maximum(xa, 0.0) + jnp.log1p(jnp.exp(-jnp.abs(xa)))
        g = -jnp.exp(alog_ref[d, h]) * softplus
        beta = 1.0 / (1.0 + jnp.exp(-b))
        tri = mask_s[M_INCL_B if d == 0 else M_INCL_F].astype(BF16)
        g_hi = g.astype(BF16)
        g_lo = (g - g_hi.astype(F32)).astype(BF16)
        gam = (jnp.dot(g_hi, tri, preferred_element_type=F32) + jnp.dot(g_lo, tri, preferred_element_type=F32))
        gate_s[d, 0] = gam
        gate_s[d, 1] = beta
        gate_s[d, 2] = jnp.broadcast_to(jnp.sum(g, axis=-1, keepdims=True), g.shape)

    def conv_chunk(c, pad_ref, w_ref, normalise, scale):
        start = pl.multiple_of(c * GC + jnp.where(c >= n_ctx_chunks, PAD, 0), PAD)
        acc = jnp.zeros((GC, GD), F32)
        for j in range(CONV_K):
            tap = pad_ref[pl.ds(start + (PAD - CONV_K // 2 + j), GC), :]
            acc = acc + tap * w_ref[j:j + 1, :].astype(F32)
        y = acc * jax.nn.sigmoid(acc)
        if normalise:
            y = y * (lax.rsqrt(jnp.sum(y * y, axis=-1, keepdims=True) + EPS) * scale)
        return y

    def group_chunks(i):
        return [jnp.minimum(i * unroll + u, n_chunks - 1) for u in range(unroll)]

    def prep(i, carry):
        chunks = group_chunks(i)
        qs = [conv_chunk(c, pq_s, cq_ref, True, GD ** -0.5) for c in chunks]
        ks = [conv_chunk(c, pk_s, ck_ref, True, 1.0) for c in chunks]
        vs = [conv_chunk(c, pv_s, cv_ref, False, 1.0) for c in chunks]
        kqs = [_bdot_nt(jnp.concatenate([k, q], axis=0), k) for k, q in zip(ks, qs)]
        kts = [k.T for k in ks]
        lanes = [(u, d) for u in range(unroll) for d in range(2)]
        lms, dbs, e_cols, kdts = [], [], [], []
        for u, d in lanes:
            c = chunks[u]
            gam_row = gate_s[d, 0, pl.ds(c, 1), :]
            beta_row = gate_s[d, 1, pl.ds(c, 1), :]
            tot_row = gate_s[d, 2, pl.ds(c, 1), :]
            gam_col = jnp.sum(mask_s[M_EYE] * gam_row, axis=1, keepdims=True)
            db = jnp.exp((gam_col - gam_row) * mask_s[M_INCL_F if d == 0 else M_INCL_B]) * beta_row
            dbs.append(db)
            lms.append(kqs[u][:GC] * db * mask_s[M_STRICT_F if d == 0 else M_STRICT_B])
            e_cols.append(jnp.exp(gam_col))
            kdts.append(kts[u] * (jnp.exp(tot_row - gam_row) * beta_row))
        xs = [-lm * mask_s[M_BASE] for lm in lms]
        ts = [mask_s[M_EYE] + x for x in xs]
        for _ in range(3):
            xs = [_bdot(x, x) for x in xs]
            ts = [t + _bdot(t, x) for t, x in zip(ts, xs)]
        for lvl in range(N_MERGE):
            ys = [_bdot(lm * mask_s[M_OFF0 + lvl], t) for lm, t in zip(lms, ts)]
            ts = [t - _bdot(t, y) for t, y in zip(ts, ys)]
        wus = [_bdot(t, jnp.concatenate([ks[u] * e, vs[u]], axis=1))
               for t, e, (u, d) in zip(ts, e_cols, lanes)]
        x12s = []
        for wu, db, kdt, (u, d) in zip(wus, dbs, kdts, lanes):
            qkb = kqs[u][GC:] * db * mask_s[M_INCL_F if d == 0 else M_INCL_B]
            x12s.append(_bdot(jnp.concatenate([kdt, qkb], axis=0), wu))
        for x12, e, (u, d) in zip(x12s, e_cols, lanes):
            t0 = pl.multiple_of(chunks[u] * GC, GC)
            b_s[d, pl.ds(t0, GC), :] = x12[:GD, GD:]
            o_s[d, pl.ds(t0, GC), :] = x12[GD:, GD:]
            q_eff = qs[u] * e - x12[GD:, :GD]
            gq_s[d, pl.ds(pl.multiple_of(2 * t0, 2 * GC), 2 * GC), :] = (
                jnp.concatenate([x12[:GD, :GD], q_eff], axis=0).astype(BF16))
        return carry

    lax.fori_loop(0, -(-n_chunks // unroll), prep, 0)

    state_s[...] = jnp.zeros_like(state_s)

    def scan(s, carry):
        for d in range(2):
            if d == 0:
                c = s
            else:
                c = jnp.where(s < n_ctx_chunks, n_ctx_chunks - 1 - s, n_chunks - 1 - (s - n_ctx_chunks))
            t0 = pl.multiple_of(c * GC, GC)
            st = state_s[d]
            r = jnp.dot(gq_s[d, pl.ds(pl.multiple_of(2 * t0, 2 * GC), 2 * GC), :], st.astype(BF16),
                        preferred_element_type=F32)
            cd = jnp.exp(gate_s[d, 2, pl.ds(c, 1), 0:1])
            state_s[d] = st * cd - r[:GD] + b_s[d, pl.ds(t0, GC), :]
            o_s[d, pl.ds(t0, GC), :] += r[GD:]
        return carry

    lax.fori_loop(0, n_chunks, scan, 0)

    gain = onorm_ref[...].astype(F32)

    def finish(c, carry):
        t0 = pl.multiple_of(lc + c * GC, GC)
        o = o_s[0, pl.ds(t0, GC), :] + o_s[1, pl.ds(t0, GC), :]
        z = z_ref[0, pl.ds(t0, GC), :].astype(F32)
        y = o * lax.rsqrt(jnp.mean(o * o, axis=-1, keepdims=True) + EPS) * gain
        y = y * (z * jax.nn.sigmoid(z))
        o_ref[0, pl.ds(pl.multiple_of(c * GC, GC), GC), :] = y.astype(o_ref.dtype)
        return carry

    lax.fori_loop(0, n_chunks - n_ctx_chunks, finish, 0, unroll=4)


def gdn_core(p_all, ab_rows, conv_w, a_log, dt_bias, out_norm, n_ctx):
    bsz, ltot, _ = p_all.shape
    n_heads = a_log.shape[1]
    n_chunks = ltot // GC
    n_ctx_chunks = n_ctx // GC
    lat = ltot - n_ctx
    n_rows = ab_rows.shape[2]
    unroll = min(GDN_UNROLL, n_chunks)
    body = functools.partial(_gdn_body, n_ctx_chunks=n_ctx_chunks, n_chunks=n_chunks, n_heads=n_heads,
                             unroll=unroll)
    smem = pl.BlockSpec(memory_space=pltpu.SMEM)

    def col(off):
        return pl.BlockSpec((1, ltot, GD), lambda b, h: (b, 0, off * n_heads + h))

    def cw(off):
        return pl.BlockSpec((CONV_K, GD), lambda b, h: (0, off * n_heads + h))

    return pl.pallas_call(
        body,
        grid=(bsz, n_heads),
        in_specs=[smem, smem, col(0), col(1), col(2), col(3),
                  pl.BlockSpec((1, 4 * n_heads, n_rows, GC), lambda b, h: (b, 0, 0, 0)),
                  cw(0), cw(1), cw(2),
                  pl.BlockSpec((1, GD), lambda b, h: (0, 0))],
        out_specs=pl.BlockSpec((1, lat, GD), lambda b, h: (b, 0, h)),
        out_shape=jax.ShapeDtypeStruct((bsz, lat, n_heads * GD), BF16),
        scratch_shapes=[
            pltpu.VMEM((ltot + 3 * PAD, GD), F32),
            pltpu.VMEM((ltot + 3 * PAD, GD), F32),
            pltpu.VMEM((ltot + 3 * PAD, GD), F32),
            pltpu.VMEM((2, ltot, GD), F32),
            pltpu.VMEM((2, ltot, GD), F32),
            pltpu.VMEM((2, 2 * ltot, GD), BF16),
            pltpu.VMEM((2, 3, n_rows, GC), F32),
            pltpu.VMEM((2, GD, GD), F32),
            pltpu.VMEM((N_MASKS, GC, GC), F32),
        ],
        compiler_params=pltpu.CompilerParams(
            dimension_semantics=("arbitrary", "arbitrary"),
            vmem_limit_bytes=56 * 1024 * 1024),
        name="gdn_core",
    )(a_log.astype(F32), dt_bias.astype(F32), p_all, p_all, p_all, p_all, ab_rows,
      conv_w, conv_w, conv_w, out_norm.reshape(1, GD))


def gdn_gate_rows(ab, n_heads):
    bsz, ltot, _ = ab.shape
    n_chunks = ltot // GC
    n_rows = -(-n_chunks // 8) * 8
    t = jnp.transpose(ab.astype(F32), (0, 2, 1)).reshape(bsz, 4 * n_heads, n_chunks, GC)
    return jnp.pad(t, ((0, 0), (0, 0), (0, n_rows - n_chunks), (0, 0)))


TM = 256
PAIR = 2
FFN_TF = 256
MOD_TN = 512
MOD_SHIFT_M, MOD_SCALE_M, MOD_GATE_M, MOD_SHIFT_F, MOD_SCALE_F, MOD_GATE_F = range(6)
N_MOD = 6
RESIDENT = dict(pipeline_mode=pl.Buffered(1))
VMEM_LIMIT = 56 * 1024 * 1024


def _rms(x, gain):
    return x * lax.rsqrt(jnp.mean(x * x, axis=-1, keepdims=True) + EPS) * gain


def _mod_spec(k, tiles_per_batch, ctx_tiles, d, half=None):
    def index(i, *_):
        t = i if half is None else PAIR * i + half
        seg = jnp.where(t % tiles_per_batch >= ctx_tiles, 1, 0)
        return ((t // tiles_per_batch) * 2 + seg) * N_MOD + k, 0, 0
    return pl.BlockSpec((1, 1, d), index)


def _const_spec(shape):
    return pl.BlockSpec(shape, lambda i, *_: (0,) * len(shape), **RESIDENT)


def _mod_body(c_ref, w_ref, b_ref, o_ref):
    c = c_ref[...]
    s = (c * jax.nn.sigmoid(c)).astype(BF16)
    o_ref[...] = jnp.dot(s, w_ref[...].astype(BF16), preferred_element_type=F32) + b_ref[...]


def modulation(c, c_ctx, mod_w, mod_b):
    bsz, d = c.shape
    n = mod_w.shape[1]
    rows = -(-(bsz + 1) // 8) * 8
    cc = jnp.zeros((rows, d), F32).at[:bsz].set(c).at[bsz].set(c_ctx)
    out = pl.pallas_call(
        _mod_body,
        grid=(n // MOD_TN,),
        in_specs=[pl.BlockSpec((rows, d), lambda j: (0, 0)),
                  pl.BlockSpec((d, MOD_TN), lambda j: (0, j)),
                  pl.BlockSpec((1, MOD_TN), lambda j: (0, j))],
        out_specs=pl.BlockSpec((rows, MOD_TN), lambda j: (0, j)),
        out_shape=jax.ShapeDtypeStruct((rows, n), F32),
        compiler_params=pltpu.CompilerParams(dimension_semantics=("arbitrary",)),
        name="modulation",
    )(cc, mod_w, mod_b.reshape(1, n))
    lat = out[:bsz].reshape(bsz, 1, N_MOD, d)
    ctx = jnp.broadcast_to(out[bsz].reshape(1, 1, N_MOD, d), (bsz, 1, N_MOD, d))
    return jnp.concatenate([ctx, lat], axis=1).reshape(bsz * 2 * N_MOD, 1, d)


def _pre_mm_body(x_ref, gain_ref, sh_ref, sc_ref, w_ref, cos_ref, sin_ref, o_ref, *, rope_lo, rope_hi, q_hi):
    h = (_rms(x_ref[...], gain_ref[...]) * (1.0 + sc_ref[0]) + sh_ref[0]).astype(BF16)
    acc = jnp.dot(h, w_ref[...], preferred_element_type=F32)
    lane = lax.broadcasted_iota(I32, (acc.shape[0], LANES), 1)
    first_half = (lane % (HEAD_DIM // 2)) < (HEAD_DIM // 4)
    for s in range(0, acc.shape[1], LANES):
        blk = acc[:, s:s + LANES]
        if rope_lo <= s < rope_hi:
            partner = jnp.where(first_half, pltpu.roll(blk, LANES - HEAD_DIM // 4, 1),
                                pltpu.roll(blk, HEAD_DIM // 4, 1))
            blk = blk * cos_ref[...] + partner * sin_ref[...]
            if s < q_hi:
                blk = blk * (HEAD_DIM ** -0.5)
        o_ref[:, s:s + LANES] = blk.astype(o_ref.dtype)


def rope_tables(n_ctx, n_lat):
    half = HEAD_DIM // 2
    inv = ROPE_THETA ** (-jnp.arange(0, half, 2, dtype=F32) / half)
    t = jnp.arange(n_lat)
    pos = jnp.stack([(t // GRID_W).astype(F32), (t % GRID_W).astype(F32)], axis=1)
    lane = jnp.arange(LANES)
    part = (lane % HEAD_DIM) // half
    freq = lane % (half // 2)
    ang = pos[:, part] * inv[freq][None, :]
    sign = jnp.where((lane % half) < half // 2, -1.0, 1.0)
    cos = jnp.concatenate([jnp.ones((n_ctx, LANES), F32), jnp.cos(ang)], axis=0)
    sin = jnp.concatenate([jnp.zeros((n_ctx, LANES), F32), jnp.sin(ang) * sign], axis=0)
    return cos, sin


def pre_mm(x, gain, mod, w, cos, sin, tiles_per_batch, ctx_tiles):
    r, d = x.shape
    n = w.shape[1]
    body = functools.partial(_pre_mm_body, rope_lo=POOL_DIM, rope_hi=POOL_DIM + Q_DIM + KV_DIM, q_hi=POOL_DIM + Q_DIM)
    return pl.pallas_call(
        body,
        grid=(r // TM,),
        in_specs=[pl.BlockSpec((TM, d), lambda i: (i, 0)),
                  _const_spec((1, d)),
                  _mod_spec(MOD_SHIFT_M, tiles_per_batch, ctx_tiles, d),
                  _mod_spec(MOD_SCALE_M, tiles_per_batch, ctx_tiles, d),
                  _const_spec((d, n)),
                  pl.BlockSpec((TM, LANES), lambda i: (i % tiles_per_batch, 0)),
                  pl.BlockSpec((TM, LANES), lambda i: (i % tiles_per_batch, 0))],
        out_specs=pl.BlockSpec((TM, n), lambda i: (i, 0)),
        out_shape=jax.ShapeDtypeStruct((r, n), BF16),
        compiler_params=pltpu.CompilerParams(dimension_semantics=("arbitrary",), vmem_limit_bytes=VMEM_LIMIT),
        name="l0_norm_proj_rope",
    )(x, gain.reshape(1, d), mod, mod, w, cos, sin)


def _post_mm_body(*refs, n_a):
    a_refs = refs[:n_a]
    w_refs = refs[n_a:2 * n_a]
    x_refs = refs[2 * n_a:2 * n_a + PAIR]
    gpost_ref, gpre_ref = refs[2 * n_a + PAIR:2 * n_a + PAIR + 2]
    mod_refs = refs[2 * n_a + PAIR + 2:2 * n_a + 4 * PAIR + 2]
    xo_ref, ho_ref = refs[2 * n_a + 4 * PAIR + 2:]
    y = jnp.dot(a_refs[0][...], w_refs[0][...], preferred_element_type=F32)
    for a_ref, w_ref in zip(a_refs[1:], w_refs[1:]):
        y = y + jnp.dot(a_ref[...], w_ref[...], preferred_element_type=F32)
    _residual_epilogue(y, x_refs, gpost_ref, gpre_ref, mod_refs, xo_ref, ho_ref)


def _residual_epilogue(y, x_refs, gpost_ref, gpre_ref, mod_refs, xo_ref, ho_ref):
    for t in range(PAIR):
        rows = slice(t * TM, (t + 1) * TM)
        gate_ref, sh_ref, sc_ref = mod_refs[3 * t:3 * t + 3]
        x = x_refs[t][...] if len(x_refs) == PAIR else x_refs[0][rows, :]
        xn = x + gate_ref[0] * _rms(y[rows], gpost_ref[...])
        xo_ref[rows, :] = xn
        ho_ref[rows, :] = (_rms(xn, gpre_ref[...]) * (1.0 + sc_ref[0]) + sh_ref[0]).astype(ho_ref.dtype)


def _pair_mod_specs(kinds, tiles_per_batch, ctx_tiles, d):
    return [_mod_spec(k, tiles_per_batch, ctx_tiles, d, half=t) for t in range(PAIR) for k in kinds]


def post_mm(a_list, w_list, x, x_tile_offset, g_post, g_pre, mod, tiles_per_batch, ctx_tiles, x_tiles_per_batch,
            h_dtype, name):
    r = a_list[0].shape[0]
    d = w_list[0].shape[1]
    n_a = len(a_list)
    tmd = PAIR * TM

    def x_spec(half):
        def index(i):
            t = PAIR * i + half
            return (t // tiles_per_batch) * x_tiles_per_batch + x_tile_offset + t % tiles_per_batch, 0
        return pl.BlockSpec((TM, d), index)

    in_specs = ([pl.BlockSpec((tmd, a.shape[1]), lambda i: (i, 0)) for a in a_list]
                + [_const_spec(w.shape) for w in w_list]
                + [x_spec(t) for t in range(PAIR)]
                + [_const_spec((1, d)), _const_spec((1, d))]
                + _pair_mod_specs((MOD_GATE_M, MOD_SHIFT_F, MOD_SCALE_F), tiles_per_batch, ctx_tiles, d))
    return pl.pallas_call(
        functools.partial(_post_mm_body, n_a=n_a),
        grid=(r // tmd,),
        in_specs=in_specs,
        out_specs=[pl.BlockSpec((tmd, d), lambda i: (i, 0)), pl.BlockSpec((tmd, d), lambda i: (i, 0))],
        out_shape=[jax.ShapeDtypeStruct((r, d), F32), jax.ShapeDtypeStruct((r, d), h_dtype)],
        compiler_params=pltpu.CompilerParams(dimension_semantics=("arbitrary",), vmem_limit_bytes=VMEM_LIMIT),
        name=name,
    )(*a_list, *w_list, *([x] * PAIR), g_post.reshape(1, d), g_pre.reshape(1, d), *([mod] * (3 * PAIR)))


def _ffn_body(h_ref, w1_ref, w3_ref, w2_ref, x_ref, gpost_ref, gpre_ref, *rest):
    mod_refs = rest[:3 * PAIR]
    xo_ref, ho_ref, acc_ref = rest[3 * PAIR:]
    h = h_ref[...]
    n_f = w1_ref.shape[1]
    for f0 in range(0, n_f, FFN_TF):
        a = jnp.dot(h, w1_ref[:, f0:f0 + FFN_TF], preferred_element_type=F32)
        b = jnp.dot(h, w3_ref[:, f0:f0 + FFN_TF], preferred_element_type=F32)
        mid = (a * jax.nn.sigmoid(a) * b).astype(BF16)
        part = jnp.dot(mid, w2_ref[f0:f0 + FFN_TF, :], preferred_element_type=F32)
        if f0 == 0:
            acc_ref[...] = part
        else:
            acc_ref[...] += part
    _residual_epilogue(acc_ref[...], (x_ref,), gpost_ref, gpre_ref, mod_refs, xo_ref, ho_ref)


def ffn_layer(h, w1, w3, w2, x, g_post, mod, g_pre_next, mod_next, tiles_per_batch, ctx_tiles):
    r, d = h.shape
    tmd = PAIR * TM
    gate_specs = _pair_mod_specs((MOD_GATE_F,), tiles_per_batch, ctx_tiles, d)
    next_specs = _pair_mod_specs((MOD_SHIFT_M, MOD_SCALE_M), tiles_per_batch, ctx_tiles, d)
    mod_specs, mod_args = [], []
    for t in range(PAIR):
        mod_specs += [gate_specs[t], next_specs[2 * t], next_specs[2 * t + 1]]
        mod_args += [mod, mod_next, mod_next]
    return pl.pallas_call(
        _ffn_body,
        grid=(r // tmd,),
        in_specs=[pl.BlockSpec((tmd, d), lambda i: (i, 0)),
                  _const_spec(w1.shape), _const_spec(w3.shape), _const_spec(w2.shape),
                  pl.BlockSpec((tmd, d), lambda i: (i, 0)),
                  _const_spec((1, d)), _const_spec((1, d))] + mod_specs,
        out_specs=[pl.BlockSpec((tmd, d), lambda i: (i, 0)), pl.BlockSpec((tmd, d), lambda i: (i, 0))],
        out_shape=[jax.ShapeDtypeStruct((r, d), F32), jax.ShapeDtypeStruct((r, d), BF16)],
        scratch_shapes=[pltpu.VMEM((tmd, d), F32)],
        compiler_params=pltpu.CompilerParams(dimension_semantics=("arbitrary",), vmem_limit_bytes=VMEM_LIMIT),
        name="l0_ffn",
    )(h, w1, w3, w2, x, g_post.reshape(1, d), g_pre_next.reshape(1, d), *mod_args)


def _proj_body(h_ref, w_ref, wg_ref, o_ref, og_ref):
    h = h_ref[...]
    o_ref[...] = jnp.dot(h, w_ref[...], preferred_element_type=F32).astype(o_ref.dtype)
    og_ref[...] = jnp.dot(h, wg_ref[...], preferred_element_type=F32)


def gdn_proj(h, w, w_gate):
    r, d = h.shape
    n = w.shape[1]
    tmd = PAIR * TM
    return pl.pallas_call(
        _proj_body,
        grid=(r // tmd,),
        in_specs=[pl.BlockSpec((tmd, d), lambda i: (i, 0)), _const_spec(w.shape), _const_spec(w_gate.shape)],
        out_specs=[pl.BlockSpec((tmd, n), lambda i: (i, 0)), pl.BlockSpec((tmd, LANES), lambda i: (i, 0))],
        out_shape=[jax.ShapeDtypeStruct((r, n), BF16), jax.ShapeDtypeStruct((r, LANES), F32)],
        compiler_params=pltpu.CompilerParams(dimension_semantics=("arbitrary",), vmem_limit_bytes=VMEM_LIMIT),
        name="l1_proj",
    )(h, w, w_gate)


def _attn_body(sink_ref, q_ref, kc_ref, vc_ref, k0_ref, k1_ref, k2_ref, v0_ref, v1_ref, v2_ref, o_ref,
               *, n_ctx_blocks, n_blocks):
    n = pl.program_id(1)
    rows = GQA_GROUP * BLOCK
    qi = lax.broadcasted_iota(I32, (rows, 3 * BLOCK), 0) % BLOCK
    kj = lax.broadcasted_iota(I32, (rows, 3 * BLOCK), 1)
    first = jnp.where(n > n_ctx_blocks, 0, BLOCK)
    last = jnp.where(n < n_blocks - 1, 3 * BLOCK, 2 * BLOCK)
    last = jnp.where(n >= n_ctx_blocks, last, 0)
    band_ok = (kj >= qi) & (kj <= qi + 2 * WINDOW) & (kj >= first) & (kj < last)
    q = q_ref[0]
    nt = (((1,), (1,)), ((), ()))
    groups = range(N_KV_HEADS)
    cols = [slice(g * HEAD_DIM, (g + 1) * HEAD_DIM) for g in groups]
    heads = [[g * GQA_GROUP + i for i in range(GQA_GROUP)] for g in groups]
    q_g = [jnp.concatenate([q[:, hd * HEAD_DIM:(hd + 1) * HEAD_DIM] for hd in heads[g]], axis=0) for g in groups]
    k_b = [jnp.concatenate([k0_ref[0][:, c], k1_ref[0][:, c], k2_ref[0][:, c]], axis=0) for c in cols]
    ones_b = jnp.ones((3 * BLOCK, HEAD_DIM), BF16)
    ones_c = jnp.ones((kc_ref.shape[1], HEAD_DIM), BF16)
    v_b = [jnp.concatenate([jnp.concatenate([v0_ref[0][:, c], v1_ref[0][:, c], v2_ref[0][:, c]], axis=0), ones_b],
                           axis=1) for c in cols]
    v_c = [jnp.concatenate([vc_ref[0][:, c], ones_c], axis=1) for c in cols]
    s_c = [lax.dot_general(q_g[g], kc_ref[0][:, cols[g]], nt, preferred_element_type=F32) for g in groups]
    s_b = [lax.dot_general(q_g[g], k_b[g], nt, preferred_element_type=F32) for g in groups]
    s_b = [jnp.where(band_ok, s, NEG_INF) for s in s_b]
    sink = [jnp.concatenate([sink_ref[hd] + jnp.zeros((BLOCK, 1), F32) for hd in heads[g]], axis=0) for g in groups]
    m = [jnp.maximum(jnp.maximum(jnp.max(s_c[g], axis=-1, keepdims=True), jnp.max(s_b[g], axis=-1, keepdims=True)),
                     sink[g]) for g in groups]
    p_c = [jnp.exp(s_c[g] - m[g]) for g in groups]
    p_b = [jnp.exp(s_b[g] - m[g]) for g in groups]
    o_c = [jnp.dot(p_c[g].astype(BF16), v_c[g], preferred_element_type=F32) for g in groups]
    o_b = [jnp.dot(p_b[g].astype(BF16), v_b[g], preferred_element_type=F32) for g in groups]
    outs = []
    for g in groups:
        acc = o_c[g] + o_b[g]
        den = acc[:, HEAD_DIM:HEAD_DIM + 1] + jnp.exp(sink[g] - m[g])
        o = acc[:, :HEAD_DIM] / den
        outs += [o[i * BLOCK:(i + 1) * BLOCK] for i in range(GQA_GROUP)]
    o_ref[0] = jnp.concatenate(outs, axis=1).astype(o_ref.dtype)


def attention(p0, sinks, n_ctx):
    bsz, ltot, _ = p0.shape
    n_blocks = ltot // BLOCK
    n_ctx_blocks = n_ctx // BLOCK
    k_blk = (POOL_DIM + Q_DIM) // KV_DIM
    v_blk = k_blk + 1

    def band(dn, blk):
        return pl.BlockSpec((1, BLOCK, KV_DIM),
                            lambda b, n: (b, jnp.clip(n + dn, n_ctx_blocks, n_blocks - 1), blk))

    return pl.pallas_call(
        functools.partial(_attn_body, n_ctx_blocks=n_ctx_blocks, n_blocks=n_blocks),
        grid=(bsz, n_blocks),
        in_specs=[pl.BlockSpec(memory_space=pltpu.SMEM),
                  pl.BlockSpec((1, BLOCK, Q_DIM), lambda b, n: (b, n, POOL_DIM // Q_DIM)),
                  pl.BlockSpec((1, n_ctx, KV_DIM), lambda b, n: (b, 0, k_blk)),
                  pl.BlockSpec((1, n_ctx, KV_DIM), lambda b, n: (b, 0, v_blk)),
                  band(-1, k_blk), band(0, k_blk), band(1, k_blk),
                  band(-1, v_blk), band(0, v_blk), band(1, v_blk)],
        out_specs=pl.BlockSpec((1, BLOCK, Q_DIM), lambda b, n: (b, n, 0)),
        out_shape=jax.ShapeDtypeStruct((bsz, ltot, Q_DIM), BF16),
        compiler_params=pltpu.CompilerParams(dimension_semantics=("arbitrary", "arbitrary"),
                                             vmem_limit_bytes=VMEM_LIMIT),
        name="l0_attention",
    )(sinks.astype(F32), p0, p0, p0, p0, p0, p0, p0, p0, p0)


POOL_HALO = 16


def _pool_body(up_ref, uc_ref, un_ref, w_ref, scale_ref, o_ref, *, tiles_per_batch, ctx_tiles):
    j = pl.program_id(1)
    seg_lo = jnp.where(j < ctx_tiles, 0, ctx_tiles * TM)
    seg_hi = jnp.where(j < ctx_tiles, ctx_tiles * TM, tiles_per_batch * TM)
    halo = jnp.concatenate([up_ref[0][TM - POOL_HALO:], uc_ref[0], un_ref[0][:POOL_HALO]], axis=0)
    n_h = TM + 2 * POOL_HALO
    t = j * TM + lax.broadcasted_iota(I32, (TM, n_h), 0)
    pos = j * TM - POOL_HALO + lax.broadcasted_iota(I32, (TM, n_h), 1)
    t_col = j * TM + lax.broadcasted_iota(I32, (TM, 1), 0)
    cur = uc_ref[0]
    for g, w in enumerate(POOL_WINDOWS):
        cols = slice(g * POOL_GROUP_DIM, (g + 1) * POOL_GROUP_DIM)
        lo = jnp.maximum(t - w // 2, seg_lo)
        hi = jnp.minimum(t + w // 2, seg_hi)
        window = ((pos >= lo) & (pos < hi)).astype(BF16)
        cnt = (jnp.minimum(t_col + w // 2, seg_hi) - jnp.maximum(t_col - w // 2, seg_lo)).astype(F32)
        mean = jnp.dot(window, halo[:, cols], preferred_element_type=F32) / cnt
        delta = (mean - cur[:, cols].astype(F32)).astype(BF16)
        y = jnp.dot(delta, w_ref[g], preferred_element_type=F32) * scale_ref[:, cols]
        o_ref[0, :, cols] = y.astype(o_ref.dtype)


def pool_mixer(p0, pool_w, pool_scale, n_ctx):
    bsz, ltot, _ = p0.shape
    tiles_per_batch = ltot // TM
    ctx_tiles = n_ctx // TM

    def tile(dj):
        return pl.BlockSpec((1, TM, POOL_DIM), lambda b, j: (b, jnp.clip(j + dj, 0, tiles_per_batch - 1), 0))

    return pl.pallas_call(
        functools.partial(_pool_body, tiles_per_batch=tiles_per_batch, ctx_tiles=ctx_tiles),
        grid=(bsz, tiles_per_batch),
        in_specs=[tile(-1), tile(0), tile(1),
                  pl.BlockSpec(pool_w.shape, lambda b, j: (0, 0, 0)),
                  pl.BlockSpec((1, POOL_DIM), lambda b, j: (0, 0))],
        out_specs=pl.BlockSpec((1, TM, POOL_DIM), lambda b, j: (b, j, 0)),
        out_shape=jax.ShapeDtypeStruct((bsz, ltot, POOL_DIM), BF16),
        compiler_params=pltpu.CompilerParams(dimension_semantics=("arbitrary", "arbitrary")),
        name="l0_pool",
    )(p0, p0, p0, pool_w.astype(BF16), pool_scale.reshape(1, POOL_DIM).astype(F32))


def kernel(x, c, ctx, c_ctx, l0_mod_w, l0_mod_b, l0_mix_pre, l0_mix_post, l0_ffn_pre, l0_ffn_post, l0_w_in, l0_pool_w, l0_pool_scale, l0_sinks, l0_w_out, l0_ffn_w1, l0_ffn_w3, l0_ffn_w2, l1_mod_w, l1_mod_b, l1_mix_pre, l1_mix_post, l1_ffn_pre, l1_ffn_post, l1_w_in, l1_conv_w, l1_a_log, l1_dt_bias, l1_out_norm, l1_w_out, l1_router, l1_moe_w1, l1_moe_w3, l1_moe_w2):
    bsz, n_lat, d = x.shape
    n_ctx = ctx.shape[1]
    ltot = n_ctx + n_lat
    assert n_ctx % TM == 0 and n_lat % TM == 0 and n_ctx % GC == 0
    assert (bsz * ltot) % (PAIR * TM) == 0 and (bsz * n_lat) % (PAIR * TM) == 0
    tiles = ltot // TM
    ctx_tiles = n_ctx // TM
    lat_tiles = n_lat // TM
    bf = lambda w: w.astype(BF16)

    x_all = jnp.concatenate([ctx, x], axis=1).reshape(bsz * ltot, d)
    mod0 = modulation(c, c_ctx, l0_mod_w, l0_mod_b)
    mod1 = modulation(c, c_ctx, l1_mod_w, l1_mod_b)
    cos, sin = rope_tables(n_ctx, n_lat)

    p0 = pre_mm(x_all, l0_mix_pre, mod0, bf(l0_w_in), cos, sin, tiles, ctx_tiles).reshape(bsz, ltot, -1)
    attn = attention(p0, l0_sinks, n_ctx).reshape(bsz * ltot, Q_DIM)
    pooled = pool_mixer(p0, l0_pool_w, l0_pool_scale, n_ctx).reshape(bsz * ltot, POOL_DIM)
    w_out0 = bf(l0_w_out)
    x1, h1 = post_mm([pooled, attn], [w_out0[:POOL_DIM], w_out0[POOL_DIM:]], x_all, 0, l0_mix_post, l0_ffn_pre,
                     mod0, tiles, ctx_tiles, tiles, BF16, "l0_out_proj")
    x2, h2 = ffn_layer(h1, bf(l0_ffn_w1), bf(l0_ffn_w3), bf(l0_ffn_w2), x1, l0_ffn_post, mod0, l1_mix_pre, mod1,
                       tiles, ctx_tiles)

    n_heads = l1_a_log.shape[1]
    qkvz = 4 * n_heads * GD
    w_in1 = bf(l1_w_in)
    w_gate = jnp.zeros((d, LANES), BF16).at[:, :4 * n_heads].set(w_in1[:, qkvz:])
    p1, ab = gdn_proj(h2, w_in1[:, :qkvz], w_gate)
    ab_rows = gdn_gate_rows(ab.reshape(bsz, ltot, LANES)[..., :4 * n_heads], n_heads)
    y = gdn_core(p1.reshape(bsz, ltot, qkvz), ab_rows, l1_conv_w, l1_a_log, l1_dt_bias, l1_out_norm, n_ctx)
    x3, h3 = post_mm([y.reshape(bsz * n_lat, n_heads * GD)], [bf(l1_w_out)], x2, ctx_tiles, l1_mix_post, l1_ffn_pre,
                     mod1, lat_tiles, 0, tiles, F32, "l1_out_proj")
    out = moe_layer(h3, l1_router, l1_moe_w1, l1_moe_w3, l1_moe_w2, x3, l1_ffn_post, mod1, lat_tiles)
    return out.reshape(bsz, n_lat, d)
```

```python
import functools
import math

import jax
import jax.numpy as jnp
from jax import lax
from jax.experimental import pallas as pl
from jax.experimental.pallas import tpu as pltpu

F32 = jnp.float32
BF16 = jnp.bfloat16
I32 = jnp.int32

LANES = 128
D_MODEL = 1024
GRID_W = 64
EPS = 1e-6
NEG_INF = -1e30

POOL_GROUPS = 4
POOL_GROUP_DIM = 128
POOL_DIM = POOL_GROUPS * POOL_GROUP_DIM
POOL_WINDOWS = (2, 4, 8, 16)
HEAD_DIM = 64
N_HEADS = 8
N_KV_HEADS = 2
GQA_GROUP = N_HEADS // N_KV_HEADS
Q_DIM = N_HEADS * HEAD_DIM
KV_DIM = N_KV_HEADS * HEAD_DIM
WINDOW = 128
BLOCK = 128
ROPE_THETA = 10000.0

GDN_HEADS = 8
GDN_HEAD_DIM = 128
GDN_DIM = GDN_HEADS * GDN_HEAD_DIM
CONV_K = 5
CHUNK = 64

N_EXPERTS = 8
TOP_K = 2

ROUTE_TILE = 512
MOE_TM = 512
MOE_TF = 1792
ROW_TILE = 512
ISSUE_UNROLL = 8


def _split_bf16(a):
    hi = a.astype(BF16)
    lo = (a - hi.astype(F32)).astype(BF16)
    return hi, lo


def _route_body(h_ref, rhi_ref, rlo_ref, route_ref, cnt_ref, carry_ref):
    i = pl.program_id(0)

    @pl.when(i == 0)
    def _():
        carry_ref[...] = jnp.zeros_like(carry_ref)

    h_hi, h_lo = _split_bf16(h_ref[...])
    r_hi = rhi_ref[...]
    r_lo = rlo_ref[...]
    logits = (jnp.dot(h_hi, r_hi, preferred_element_type=F32)
              + jnp.dot(h_hi, r_lo, preferred_element_type=F32)
              + jnp.dot(h_lo, r_hi, preferred_element_type=F32))
    tr = logits.shape[0]
    lane = lax.broadcasted_iota(I32, (tr, LANES), 1)
    logits = jnp.where(lane < N_EXPERTS, logits, -jnp.inf)
    m1 = jnp.max(logits, axis=-1, keepdims=True)
    i1 = jnp.min(jnp.where(logits == m1, lane, LANES), axis=-1, keepdims=True)
    rest = jnp.where(lane == i1, -jnp.inf, logits)
    m2 = jnp.max(rest, axis=-1, keepdims=True)
    i2 = jnp.min(jnp.where(rest == m2, lane, LANES), axis=-1, keepdims=True)
    e2 = jnp.exp(m2 - m1)
    g1 = 1.0 / (1.0 + e2)
    g2 = e2 / (1.0 + e2)

    onehot = ((lane == i1) | (lane == i2)).astype(F32)
    row = lax.broadcasted_iota(I32, (tr, tr), 0)
    col = lax.broadcasted_iota(I32, (tr, tr), 1)
    strict = (row > col).astype(BF16)
    before = jnp.dot(strict, onehot.astype(BF16), preferred_element_type=F32) + carry_ref[...]
    rank1 = jnp.sum(jnp.where(lane == i1, before, 0.0), axis=-1, keepdims=True)
    rank2 = jnp.sum(jnp.where(lane == i2, before, 0.0), axis=-1, keepdims=True)
    carry_ref[...] += jnp.sum(onehot, axis=0, keepdims=True)

    packed = jnp.where(lane == 0, i1.astype(F32), 0.0)
    packed = jnp.where(lane == 1, i2.astype(F32), packed)
    packed = jnp.where(lane == 2, g1, packed)
    packed = jnp.where(lane == 3, g2, packed)
    packed = jnp.where(lane == 4, rank1, packed)
    packed = jnp.where(lane == 5, rank2, packed)
    route_ref[...] = packed
    cnt_ref[...] = jnp.broadcast_to(carry_ref[...], cnt_ref.shape)


def moe_route(h, router):
    t, d = h.shape
    tr = min(ROUTE_TILE, t)
    r_pad = jnp.zeros((d, LANES), F32).at[:, :N_EXPERTS].set(router.astype(F32))
    r_hi, r_lo = _split_bf16(r_pad)
    route, cnt = pl.pallas_call(
        _route_body,
        grid=(t // tr,),
        in_specs=[
            pl.BlockSpec((tr, d), lambda i: (i, 0)),
            pl.BlockSpec((d, LANES), lambda i: (0, 0)),
            pl.BlockSpec((d, LANES), lambda i: (0, 0)),
        ],
        out_specs=[
            pl.BlockSpec((tr, LANES), lambda i: (i, 0)),
            pl.BlockSpec((8, LANES), lambda i: (0, 0)),
        ],
        out_shape=[
            jax.ShapeDtypeStruct((t, LANES), F32),
            jax.ShapeDtypeStruct((8, LANES), F32),
        ],
        scratch_shapes=[pltpu.VMEM((1, LANES), F32)],
        compiler_params=pltpu.CompilerParams(dimension_semantics=("arbitrary",)),
        name="moe_route",
    )(h, r_hi, r_lo)
    return route, cnt


def _row_copy(src_ref, src_row, dst_ref, dst_row, sem):
    return pltpu.make_async_copy(src_ref.at[pl.ds(src_row, 1)], dst_ref.at[pl.ds(dst_row, 1)], sem)


def _scatter_body(pos_ref, h_ref, xs_in_ref, xs_ref, sem, *, ts):
    del xs_in_ref

    def issue(r, c):
        for k in range(TOP_K):
            _row_copy(h_ref, r, xs_ref, pos_ref[0, 0, TOP_K * r + k], sem).start()
        return c

    lax.fori_loop(0, ts, issue, 0, unroll=ISSUE_UNROLL)
    for _ in range(TOP_K):
        pltpu.make_async_copy(h_ref, xs_ref.at[pl.ds(0, ts)], sem).wait()


def moe_scatter(h, pos, p_rows):
    t, d = h.shape
    ts = min(ROW_TILE, t)
    nt = t // ts
    xs0 = jnp.zeros((p_rows, d), h.dtype)
    return pl.pallas_call(
        functools.partial(_scatter_body, ts=ts),
        grid=(nt,),
        in_specs=[
            pl.BlockSpec((1, 1, TOP_K * ts), lambda i: (i, 0, 0), memory_space=pltpu.SMEM),
            pl.BlockSpec((ts, d), lambda i: (i, 0)),
            pl.BlockSpec(memory_space=pl.ANY),
        ],
        out_specs=pl.BlockSpec(memory_space=pl.ANY),
        out_shape=jax.ShapeDtypeStruct((p_rows, d), h.dtype),
        scratch_shapes=[pltpu.SemaphoreType.DMA(())],
        input_output_aliases={2: 0},
        compiler_params=pltpu.CompilerParams(dimension_semantics=("arbitrary",), disable_bounds_checks=True),
        name="moe_scatter",
    )(pos.reshape(nt, 1, TOP_K * ts), h, xs0)


def _expert_body(te_ref, nu_ref, x_ref, w1_ref, w3_ref, w2_ref, o_ref):
    i = pl.program_id(0)
    j = pl.program_id(1)

    @pl.when(j == 0)
    def _():
        o_ref[...] = jnp.zeros_like(o_ref)

    @pl.when(i < nu_ref[0])
    def _():
        x = x_ref[...].astype(BF16)
        a = jnp.dot(x, w1_ref[0], preferred_element_type=F32)
        b = jnp.dot(x, w3_ref[0], preferred_element_type=F32)
        mid = (a * jax.nn.sigmoid(a) * b).astype(BF16)
        o_ref[...] += jnp.dot(mid, w2_ref[0], preferred_element_type=F32)


def moe_experts(xs, w1, w3, w2, tile_expert, n_used, tm):
    p_rows, d = xs.shape
    n_exp, _, d_exp = w1.shape
    tf = MOE_TF if d_exp % MOE_TF == 0 else d_exp
    nj = d_exp // tf
    n_tiles = p_rows // tm

    def jj(i, j, nu):
        return jnp.where(i < nu[0], j, nj - 1)

    grid_spec = pltpu.PrefetchScalarGridSpec(
        num_scalar_prefetch=2,
        grid=(n_tiles, nj),
        in_specs=[
            pl.BlockSpec((tm, d), lambda i, j, te, nu: (i, 0)),
            pl.BlockSpec((1, d, tf), lambda i, j, te, nu: (te[i], 0, jj(i, j, nu))),
            pl.BlockSpec((1, d, tf), lambda i, j, te, nu: (te[i], 0, jj(i, j, nu))),
            pl.BlockSpec((1, tf, d), lambda i, j, te, nu: (te[i], jj(i, j, nu), 0)),
        ],
        out_specs=pl.BlockSpec((tm, d), lambda i, j, te, nu: (i, 0)),
    )
    return pl.pallas_call(
        _expert_body,
        grid_spec=grid_spec,
        out_shape=jax.ShapeDtypeStruct((p_rows, d), F32),
        compiler_params=pltpu.CompilerParams(
            dimension_semantics=("arbitrary", "arbitrary"),
            vmem_limit_bytes=56 * 1024 * 1024),
        name="moe_experts",
    )(tile_expert, n_used, xs, w1, w3, w2)


def _combine_body(pos_ref, pos_next_ref, gate_ref, ys_ref, x_ref, gain_ref, mgate_ref, y_ref, buf, sem, *, ts):
    i = pl.program_id(0)
    slot = i % 2

    def gather(p_ref, s):
        def issue(r, c):
            for k in range(TOP_K):
                _row_copy(ys_ref, p_ref[0, 0, TOP_K * r + k], buf.at[s, k], r, sem.at[s]).start()
            return c
        lax.fori_loop(0, ts, issue, 0, unroll=ISSUE_UNROLL)

    @pl.when(i == 0)
    def _():
        gather(pos_ref, slot)

    @pl.when(i + 1 < pl.num_programs(0))
    def _():
        gather(pos_next_ref, 1 - slot)

    for k in range(TOP_K):
        pltpu.make_async_copy(ys_ref.at[pl.ds(0, ts)], buf.at[slot, k], sem.at[slot]).wait()
    g = gate_ref[...]
    y = g[:, 2:3] * buf[slot, 0] + g[:, 3:4] * buf[slot, 1]
    y_ref[...] = x_ref[...] + mgate_ref[0] * _rms(y, gain_ref[...])


def moe_combine(ys, pos, route, x, gain, mod, rows_per_batch):
    t = route.shape[0]
    d = ys.shape[1]
    ts = min(ROW_TILE, rows_per_batch)
    assert rows_per_batch % ts == 0
    tiles_per_batch = rows_per_batch // ts
    nt = t // ts
    pos3 = pos.reshape(nt, 1, TOP_K * ts)
    return pl.pallas_call(
        functools.partial(_combine_body, ts=ts),
        grid=(nt,),
        in_specs=[
            pl.BlockSpec((1, 1, TOP_K * ts), lambda i: (i, 0, 0), memory_space=pltpu.SMEM),
            pl.BlockSpec((1, 1, TOP_K * ts), lambda i: (jnp.minimum(i + 1, nt - 1), 0, 0), memory_space=pltpu.SMEM),
            pl.BlockSpec((ts, LANES), lambda i: (i, 0)),
            pl.BlockSpec(memory_space=pl.ANY),
            pl.BlockSpec((ts, d), lambda i: (i, 0)),
            pl.BlockSpec((1, d), lambda i: (0, 0)),
            _mod_spec(MOD_GATE_F, tiles_per_batch, 0, d),
        ],
        out_specs=pl.BlockSpec((ts, d), lambda i: (i, 0)),
        out_shape=jax.ShapeDtypeStruct((t, d), F32),
        scratch_shapes=[pltpu.VMEM((2, TOP_K, ts, d), F32), pltpu.SemaphoreType.DMA((2,))],
        compiler_params=pltpu.CompilerParams(dimension_semantics=("arbitrary",), disable_bounds_checks=True),
        name="moe_combine",
    )(pos3, pos3, route, ys, x, gain.reshape(1, d), mod)


def moe_layer(tok, router, w1, w3, w2, x, gain, mod, rows_per_batch):
    t = tok.shape[0]
    tm = min(MOE_TM, t)
    route, cnt = moe_route(tok, router)

    counts = cnt[0, :N_EXPERTS].astype(I32)
    padded = ((counts + tm - 1) // tm) * tm
    ends = jnp.cumsum(padded)
    starts = ends - padded
    expert = route[:, 0:TOP_K].astype(I32)
    rank = route[:, 4:4 + TOP_K].astype(I32)
    start_of = jnp.sum(jnp.where(expert[..., None] == jnp.arange(N_EXPERTS), starts, 0), axis=-1)
    pos = start_of + rank

    n_tiles = (TOP_K * t) // tm + N_EXPERTS
    n_used = (ends[-1] // tm).astype(I32)
    tile_idx = jnp.minimum(jnp.arange(n_tiles, dtype=I32), n_used - 1)
    tile_expert = jnp.sum(tile_idx[:, None] * tm >= ends[None, :], axis=-1).astype(I32)
    tile_expert = jnp.minimum(tile_expert, N_EXPERTS - 1)

    xs = moe_scatter(tok, pos, n_tiles * tm)
    ys = moe_experts(xs, w1.astype(BF16), w3.astype(BF16), w2.astype(BF16), tile_expert, n_used.reshape(1), tm)
    return moe_combine(ys, pos, route, x, gain, mod, rows_per_batch)


GC = 128
GD = GDN_HEAD_DIM
PAD = 8
GDN_UNROLL = 9
GDN_BASE = 16
N_MERGE = 3
M_INCL_F, M_STRICT_F, M_INCL_B, M_STRICT_B, M_EYE, M_BASE, M_OFF0 = 0, 1, 2, 3, 4, 5, 6
N_MASKS = M_OFF0 + N_MERGE


def _bdot(a, b):
    return jnp.dot(a.astype(BF16), b.astype(BF16), preferred_element_type=F32)


def _bdot_nt(a, b):
    return lax.dot_general(a.astype(BF16), b.astype(BF16), (((1,), (1,)), ((), ())), preferred_element_type=F32)


def _gdn_body(alog_ref, dtb_ref, q_ref, k_ref, v_ref, z_ref, ab_ref, cq_ref, ck_ref, cv_ref, onorm_ref,
              o_ref, pq_s, pk_s, pv_s, o_s, b_s, gq_s, gate_s, state_s, mask_s,
              *, n_ctx_chunks, n_chunks, n_heads, unroll):
    h = pl.program_id(1)
    ltot = n_chunks * GC
    lc = n_ctx_chunks * GC

    row = lax.broadcasted_iota(I32, (GC, GC), 0)
    col = lax.broadcasted_iota(I32, (GC, GC), 1)
    mask_s[M_INCL_F] = (row >= col).astype(F32)
    mask_s[M_STRICT_F] = (row > col).astype(F32)
    mask_s[M_INCL_B] = (row <= col).astype(F32)
    mask_s[M_STRICT_B] = (row < col).astype(F32)
    mask_s[M_EYE] = (row == col).astype(F32)
    mask_s[M_BASE] = ((row // GDN_BASE) == (col // GDN_BASE)).astype(F32)
    for lvl in range(N_MERGE):
        s = GDN_BASE << lvl
        mask_s[M_OFF0 + lvl] = (((row // (2 * s)) == (col // (2 * s))) & ((row // s) != (col // s))).astype(F32)

    for src_ref, dst in ((q_ref, pq_s), (k_ref, pk_s), (v_ref, pv_s)):
        for off in (0, PAD + lc, 2 * PAD + ltot):
            dst[pl.ds(off, PAD), :] = jnp.zeros((PAD, GD), F32)
        dst[pl.ds(PAD, lc), :] = src_ref[0, pl.ds(0, lc), :].astype(F32)
        dst[pl.ds(2 * PAD + lc, ltot - lc), :] = src_ref[0, pl.ds(lc, ltot - lc), :].astype(F32)

    for d in range(2):
        a = ab_ref[0, d * 2 * n_heads + h]
        b = ab_ref[0, d * 2 * n_heads + n_heads + h]
        xa = a + dtb_ref[d, h]
        softplus = jnp.maximum(xa, 0.0) + jnp.log1p(jnp.exp(-jnp.abs(xa)))
        g = -jnp.exp(alog_ref[d, h]) * softplus
        beta = 1.0 / (1.0 + jnp.exp(-b))
        tri = mask_s[M_INCL_B if d == 0 else M_INCL_F].astype(BF16)
        g_hi = g.astype(BF16)
        g_lo = (g - g_hi.astype(F32)).astype(BF16)
        gam = (jnp.dot(g_hi, tri, preferred_element_type=F32) + jnp.dot(g_lo, tri, preferred_element_type=F32))
        gate_s[d, 0] = gam
        gate_s[d, 1] = beta
        gate_s[d, 2] = jnp.broadcast_to(jnp.sum(g, axis=-1, keepdims=True), g.shape)

    def conv_chunk(c, pad_ref, w_ref, normalise, scale):
        start = pl.multiple_of(c * GC + jnp.where(c >= n_ctx_chunks, PAD, 0), PAD)
        acc = jnp.zeros((GC, GD), F32)
        for j in range(CONV_K):
            tap = pad_ref[pl.ds(start + (PAD - CONV_K // 2 + j), GC), :]
            acc = acc + tap * w_ref[j:j + 1, :].astype(F32)
        y = acc * jax.nn.sigmoid(acc)
        if normalise:
            y = y * (lax.rsqrt(jnp.sum(y * y, axis=-1, keepdims=True) + EPS) * scale)
        return y

    def group_chunks(i):
        return [jnp.minimum(i * unroll + u, n_chunks - 1) for u in range(unroll)]

    def prep(i, carry):
        chunks = group_chunks(i)
        qs = [conv_chunk(c, pq_s, cq_ref, True, GD ** -0.5) for c in chunks]
        ks = [conv_chunk(c, pk_s, ck_ref, True, 1.0) for c in chunks]
        vs = [conv_chunk(c, pv_s, cv_ref, False, 1.0) for c in chunks]
        kqs = [_bdot_nt(jnp.concatenate([k, q], axis=0), k) for k, q in zip(ks, qs)]
        kts = [k.T for k in ks]
        lanes = [(u, d) for u in range(unroll) for d in range(2)]
        lms, dbs, e_cols, kdts = [], [], [], []
        for u, d in lanes:
            c = chunks[u]
            gam_row = gate_s[d, 0, pl.ds(c, 1), :]
            beta_row = gate_s[d, 1, pl.ds(c, 1), :]
            tot_row = gate_s[d, 2, pl.ds(c, 1), :]
            gam_col = jnp.sum(mask_s[M_EYE] * gam_row, axis=1, keepdims=True)
            db = jnp.exp((gam_col - gam_row) * mask_s[M_INCL_F if d == 0 else M_INCL_B]) * beta_row
            dbs.append(db)
            lms.append(kqs[u][:GC] * db * mask_s[M_STRICT_F if d == 0 else M_STRICT_B])
            e_cols.append(jnp.exp(gam_col))
            kdts.append(kts[u] * (jnp.exp(tot_row - gam_row) * beta_row))
        xs = [-lm * mask_s[M_BASE] for lm in lms]
        ts = [mask_s[M_EYE] + x for x in xs]
        for _ in range(3):
            xs = [_bdot(x, x) for x in xs]
            ts = [t + _bdot(t, x) for t, x in zip(ts, xs)]
        for lvl in range(N_MERGE):
            ys = [_bdot(lm * mask_s[M_OFF0 + lvl], t) for lm, t in zip(lms, ts)]
            ts = [t - _bdot(t, y) for t, y in zip(ts, ys)]
        wus = [_bdot(t, jnp.concatenate([ks[u] * e, vs[u]], axis=1))
               for t, e, (u, d) in zip(ts, e_cols, lanes)]
        x12s = []
        for wu, db, kdt, (u, d) in zip(wus, dbs, kdts, lanes):
            qkb = kqs[u][GC:] * db * mask_s[M_INCL_F if d == 0 else M_INCL_B]
            x12s.append(_bdot(jnp.concatenate([kdt, qkb], axis=0), wu))
        for x12, e, (u, d) in zip(x12s, e_cols, lanes):
            t0 = pl.multiple_of(chunks[u] * GC, GC)
            b_s[d, pl.ds(t0, GC), :] = x12[:GD, GD:]
            o_s[d, pl.ds(t0, GC), :] = x12[GD:, GD:]
            q_eff = qs[u] * e - x12[GD:, :GD]
            gq_s[d, pl.ds(pl.multiple_of(2 * t0, 2 * GC), 2 * GC), :] = (
                jnp.concatenate([x12[:GD, :GD], q_eff], axis=0).astype(BF16))
        return carry

    lax.fori_loop(0, -(-n_chunks // unroll), prep, 0)

    state_s[...] = jnp.zeros_like(state_s)

    def scan(s, carry):
        for d in range(2):
            if d == 0:
                c = s
            else:
                c = jnp.where(s < n_ctx_chunks, n_ctx_chunks - 1 - s, n_chunks - 1 - (s - n_ctx_chunks))
            t0 = pl.multiple_of(c * GC, GC)
            st = state_s[d]
            r = jnp.dot(gq_s[d, pl.ds(pl.multiple_of(2 * t0, 2 * GC), 2 * GC), :], st.astype(BF16),
                        preferred_element_type=F32)
            cd = jnp.exp(gate_s[d, 2, pl.ds(c, 1), 0:1])
            state_s[d] = st * cd - r[:GD] + b_s[d, pl.ds(t0, GC), :]
            o_s[d, pl.ds(t0, GC), :] += r[GD:]
        return carry

    lax.fori_loop(0, n_chunks, scan, 0)

    gain = onorm_ref[...].astype(F32)

    def finish(c, carry):
        t0 = pl.multiple_of(lc + c * GC, GC)
        o = o_s[0, pl.ds(t0, GC), :] + o_s[1, pl.ds(t0, GC), :]
        z = z_ref[0, pl.ds(t0, GC), :].astype(F32)
        y = o * lax.rsqrt(jnp.mean(o * o, axis=-1, keepdims=True) + EPS) * gain
        y = y * (z * jax.nn.sigmoid(z))
        o_ref[0, pl.ds(pl.multiple_of(c * GC, GC), GC), :] = y.astype(o_ref.dtype)
        return carry

    lax.fori_loop(0, n_chunks - n_ctx_chunks, finish, 0, unroll=4)


def gdn_core(p_all, ab_rows, conv_w, a_log, dt_bias, out_norm, n_ctx):
    bsz, ltot, _ = p_all.shape
    n_heads = a_log.shape[1]
    n_chunks = ltot // GC
    n_ctx_chunks = n_ctx // GC
    lat = ltot - n_ctx
    n_rows = ab_rows.shape[2]
    unroll = min(GDN_UNROLL, n_chunks)
    body = functools.partial(_gdn_body, n_ctx_chunks=n_ctx_chunks, n_chunks=n_chunks, n_heads=n_heads,
                             unroll=unroll)
    smem = pl.BlockSpec(memory_space=pltpu.SMEM)

    def col(off):
        return pl.BlockSpec((1, ltot, GD), lambda b, h: (b, 0, off * n_heads + h))

    def cw(off):
        return pl.BlockSpec((CONV_K, GD), lambda b, h: (0, off * n_heads + h))

    return pl.pallas_call(
        body,
        grid=(bsz, n_heads),
        in_specs=[smem, smem, col(0), col(1), col(2), col(3),
                  pl.BlockSpec((1, 4 * n_heads, n_rows, GC), lambda b, h: (b, 0, 0, 0)),
                  cw(0), cw(1), cw(2),
                  pl.BlockSpec((1, GD), lambda b, h: (0, 0))],
        out_specs=pl.BlockSpec((1, lat, GD), lambda b, h: (b, 0, h)),
        out_shape=jax.ShapeDtypeStruct((bsz, lat, n_heads * GD), BF16),
        scratch_shapes=[
            pltpu.VMEM((ltot + 3 * PAD, GD), F32),
            pltpu.VMEM((ltot + 3 * PAD, GD), F32),
            pltpu.VMEM((ltot + 3 * PAD, GD), F32),
            pltpu.VMEM((2, ltot, GD), F32),
            pltpu.VMEM((2, ltot, GD), F32),
            pltpu.VMEM((2, 2 * ltot, GD), BF16),
            pltpu.VMEM((2, 3, n_rows, GC), F32),
            pltpu.VMEM((2, GD, GD), F32),
            pltpu.VMEM((N_MASKS, GC, GC), F32),
        ],
        compiler_params=pltpu.CompilerParams(
            dimension_semantics=("arbitrary", "arbitrary"),
            vmem_limit_bytes=56 * 1024 * 1024),
        name="gdn_core",
    )(a_log.astype(F32), dt_bias.astype(F32), p_all, p_all, p_all, p_all, ab_rows,
      conv_w, conv_w, conv_w, out_norm.reshape(1, GD))


def gdn_gate_rows(ab, n_heads):
    bsz, ltot, _ = ab.shape
    n_chunks = ltot // GC
    n_rows = -(-n_chunks // 8) * 8
    t = jnp.transpose(ab.astype(F32), (0, 2, 1)).reshape(bsz, 4 * n_heads, n_chunks, GC)
    return jnp.pad(t, ((0, 0), (0, 0), (0, n_rows - n_chunks), (0, 0)))


TM = 256
PAIR = 2
FFN_TF = 256
MOD_TN = 512
MOD_SHIFT_M, MOD_SCALE_M, MOD_GATE_M, MOD_SHIFT_F, MOD_SCALE_F, MOD_GATE_F = range(6)
N_MOD = 6
RESIDENT = dict(pipeline_mode=pl.Buffered(1))
VMEM_LIMIT = 56 * 1024 * 1024


def _rms(x, gain):
    return x * lax.rsqrt(jnp.mean(x * x, axis=-1, keepdims=True) + EPS) * gain


def _mod_spec(k, tiles_per_batch, ctx_tiles, d, half=None):
    def index(i, *_):
        t = i if half is None else PAIR * i + half
        seg = jnp.where(t % tiles_per_batch >= ctx_tiles, 1, 0)
        return ((t // tiles_per_batch) * 2 + seg) * N_MOD + k, 0, 0
    return pl.BlockSpec((1, 1, d), index)


def _const_spec(shape):
    return pl.BlockSpec(shape, lambda i, *_: (0,) * len(shape), **RESIDENT)


def _mod_body(c_ref, w_ref, b_ref, o_ref):
    c = c_ref[...]
    s = (c * jax.nn.sigmoid(c)).astype(BF16)
    o_ref[...] = jnp.dot(s, w_ref[...].astype(BF16), preferred_element_type=F32) + b_ref[...]


def modulation(c, c_ctx, mod_w, mod_b):
    bsz, d = c.shape
    n = mod_w.shape[1]
    rows = -(-(bsz + 1) // 8) * 8
    cc = jnp.zeros((rows, d), F32).at[:bsz].set(c).at[bsz].set(c_ctx)
    out = pl.pallas_call(
        _mod_body,
        grid=(n // MOD_TN,),
        in_specs=[pl.BlockSpec((rows, d), lambda j: (0, 0)),
                  pl.BlockSpec((d, MOD_TN), lambda j: (0, j)),
                  pl.BlockSpec((1, MOD_TN), lambda j: (0, j))],
        out_specs=pl.BlockSpec((rows, MOD_TN), lambda j: (0, j)),
        out_shape=jax.ShapeDtypeStruct((rows, n), F32),
        compiler_params=pltpu.CompilerParams(dimension_semantics=("arbitrary",)),
        name="modulation",
    )(cc, mod_w, mod_b.reshape(1, n))
    lat = out[:bsz].reshape(bsz, 1, N_MOD, d)
    ctx = jnp.broadcast_to(out[bsz].reshape(1, 1, N_MOD, d), (bsz, 1, N_MOD, d))
    return jnp.concatenate([ctx, lat], axis=1).reshape(bsz * 2 * N_MOD, 1, d)


def _pre_mm_body(*refs, tiles_per_batch, ctx_tiles, rope_lo, rope_hi, q_hi):
    per_tile = [refs[6 * t:6 * t + 6] for t in range(PAIR)]
    gain_ref, w_ref, o_ref, xo_ref = refs[6 * PAIR:]
    i = pl.program_id(0)
    hs = []
    for t, (xc_ref, xl_ref, sh_ref, sc_ref, _, _) in enumerate(per_tile):
        is_ctx = (PAIR * i + t) % tiles_per_batch < ctx_tiles
        x = jnp.where(is_ctx, xc_ref[...], xl_ref[...])
        xo_ref[t * TM:(t + 1) * TM, :] = x
        hs.append((_rms(x, gain_ref[...]) * (1.0 + sc_ref[0]) + sh_ref[0]).astype(BF16))
    acc = jnp.dot(jnp.concatenate(hs, axis=0), w_ref[...], preferred_element_type=F32)
    lane = lax.broadcasted_iota(I32, (TM, LANES), 1)
    first_half = (lane % (HEAD_DIM // 2)) < (HEAD_DIM // 4)
    for t, (_, _, _, _, cos_ref, sin_ref) in enumerate(per_tile):
        rows = slice(t * TM, (t + 1) * TM)
        for s in range(0, acc.shape[1], LANES):
            blk = acc[rows, s:s + LANES]
            if rope_lo <= s < rope_hi:
                partner = jnp.where(first_half, pltpu.roll(blk, LANES - HEAD_DIM // 4, 1),
                                    pltpu.roll(blk, HEAD_DIM // 4, 1))
                blk = blk * cos_ref[...] + partner * sin_ref[...]
                if s < q_hi:
                    blk = blk * (HEAD_DIM ** -0.5)
            o_ref[rows, s:s + LANES] = blk.astype(o_ref.dtype)


def rope_tables(n_ctx, n_lat):
    half = HEAD_DIM // 2
    inv = ROPE_THETA ** (-jnp.arange(0, half, 2, dtype=F32) / half)
    t = jnp.arange(n_lat)
    pos = jnp.stack([(t // GRID_W).astype(F32), (t % GRID_W).astype(F32)], axis=1)
    lane = jnp.arange(LANES)
    part = (lane % HEAD_DIM) // half
    freq = lane % (half // 2)
    ang = pos[:, part] * inv[freq][None, :]
    sign = jnp.where((lane % half) < half // 2, -1.0, 1.0)
    cos = jnp.concatenate([jnp.ones((n_ctx, LANES), F32), jnp.cos(ang)], axis=0)
    sin = jnp.concatenate([jnp.zeros((n_ctx, LANES), F32), jnp.sin(ang) * sign], axis=0)
    return cos, sin


def pre_mm(ctx, x, gain, mod, w, cos, sin):
    bsz, n_ctx, d = ctx.shape
    n_lat = x.shape[1]
    n = w.shape[1]
    ctx_tiles, lat_tiles = n_ctx // TM, n_lat // TM
    tiles = ctx_tiles + lat_tiles
    r = bsz * tiles * TM
    tmd = PAIR * TM

    def src_spec(half, latent):
        def index(i):
            t = PAIR * i + half
            b, j = t // tiles, t % tiles
            if latent:
                return b * lat_tiles + jnp.clip(j - ctx_tiles, 0, lat_tiles - 1), 0
            return b * ctx_tiles + jnp.minimum(j, ctx_tiles - 1), 0
        return pl.BlockSpec((TM, d), index)

    def table_spec(half):
        return pl.BlockSpec((TM, LANES), lambda i: ((PAIR * i + half) % tiles, 0))

    in_specs, args = [], []
    for t in range(PAIR):
        in_specs += [src_spec(t, False), src_spec(t, True),
                     _mod_spec(MOD_SHIFT_M, tiles, ctx_tiles, d, half=t), _mod_spec(MOD_SCALE_M, tiles, ctx_tiles, d, half=t),
                     table_spec(t), table_spec(t)]
        args += [ctx.reshape(bsz * n_ctx, d), x.reshape(bsz * n_lat, d), mod, mod, cos, sin]
    body = functools.partial(_pre_mm_body, tiles_per_batch=tiles, ctx_tiles=ctx_tiles, rope_lo=POOL_DIM,
                             rope_hi=POOL_DIM + Q_DIM + KV_DIM, q_hi=POOL_DIM + Q_DIM)
    return pl.pallas_call(
        body,
        grid=(r // tmd,),
        in_specs=in_specs + [_const_spec((1, d)), _const_spec((d, n))],
        out_specs=[pl.BlockSpec((tmd, n), lambda i: (i, 0)), pl.BlockSpec((tmd, d), lambda i: (i, 0))],
        out_shape=[jax.ShapeDtypeStruct((r, n), BF16), jax.ShapeDtypeStruct((r, d), F32)],
        compiler_params=pltpu.CompilerParams(dimension_semantics=("arbitrary",), vmem_limit_bytes=VMEM_LIMIT),
        name="l0_norm_proj_rope",
    )(*args, gain.reshape(1, d), w)


def _post_mm_body(*refs, n_a):
    a_refs = refs[:n_a]
    w_refs = refs[n_a:2 * n_a]
    x_refs = refs[2 * n_a:2 * n_a + PAIR]
    gpost_ref, gpre_ref = refs[2 * n_a + PAIR:2 * n_a + PAIR + 2]
    mod_refs = refs[2 * n_a + PAIR + 2:2 * n_a + 4 * PAIR + 2]
    xo_ref, ho_ref = refs[2 * n_a + 4 * PAIR + 2:]
    y = jnp.dot(a_refs[0][...], w_refs[0][...], preferred_element_type=F32)
    for a_ref, w_ref in zip(a_refs[1:], w_refs[1:]):
        y = y + jnp.dot(a_ref[...], w_ref[...], preferred_element_type=F32)
    _residual_epilogue(y, x_refs, gpost_ref, gpre_ref, mod_refs, xo_ref, ho_ref)


def _residual_epilogue(y, x_refs, gpost_ref, gpre_ref, mod_refs, xo_ref, ho_ref):
    for t in range(PAIR):
        rows = slice(t * TM, (t + 1) * TM)
        gate_ref, sh_ref, sc_ref = mod_refs[3 * t:3 * t + 3]
        x = x_refs[t][...] if len(x_refs) == PAIR else x_refs[0][rows, :]
        xn = x + gate_ref[0] * _rms(y[rows], gpost_ref[...])
        xo_ref[rows, :] = xn
        ho_ref[rows, :] = (_rms(xn, gpre_ref[...]) * (1.0 + sc_ref[0]) + sh_ref[0]).astype(ho_ref.dtype)


def _pair_mod_specs(kinds, tiles_per_batch, ctx_tiles, d):
    return [_mod_spec(k, tiles_per_batch, ctx_tiles, d, half=t) for t in range(PAIR) for k in kinds]


def post_mm(a_list, w_list, x, x_tile_offset, g_post, g_pre, mod, tiles_per_batch, ctx_tiles, x_tiles_per_batch,
            h_dtype, name):
    r = a_list[0].shape[0]
    d = w_list[0].shape[1]
    n_a = len(a_list)
    tmd = PAIR * TM

    def x_spec(half):
        def index(i):
            t = PAIR * i + half
            return (t // tiles_per_batch) * x_tiles_per_batch + x_tile_offset + t % tiles_per_batch, 0
        return pl.BlockSpec((TM, d), index)

    in_specs = ([pl.BlockSpec((tmd, a.shape[1]), lambda i: (i, 0)) for a in a_list]
                + [_const_spec(w.shape) for w in w_list]
                + [x_spec(t) for t in range(PAIR)]
                + [_const_spec((1, d)), _const_spec((1, d))]
                + _pair_mod_specs((MOD_GATE_M, MOD_SHIFT_F, MOD_SCALE_F), tiles_per_batch, ctx_tiles, d))
    return pl.pallas_call(
        functools.partial(_post_mm_body, n_a=n_a),
        grid=(r // tmd,),
        in_specs=in_specs,
        out_specs=[pl.BlockSpec((tmd, d), lambda i: (i, 0)), pl.BlockSpec((tmd, d), lambda i: (i, 0))],
        out_shape=[jax.ShapeDtypeStruct((r, d), F32), jax.ShapeDtypeStruct((r, d), h_dtype)],
        compiler_params=pltpu.CompilerParams(dimension_semantics=("arbitrary",), vmem_limit_bytes=VMEM_LIMIT),
        name=name,
    )(*a_list, *w_list, *([x] * PAIR), g_post.reshape(1, d), g_pre.reshape(1, d), *([mod] * (3 * PAIR)))


def _ffn_body(h_ref, w1_ref, w3_ref, w2_ref, x_ref, gpost_ref, gpre_ref, *rest):
    mod_refs = rest[:3 * PAIR]
    xo_ref, ho_ref, acc_ref = rest[3 * PAIR:]
    h = h_ref[...]
    n_f = w1_ref.shape[1]
    for f0 in range(0, n_f, FFN_TF):
        a = jnp.dot(h, w1_ref[:, f0:f0 + FFN_TF], preferred_element_type=F32)
        b = jnp.dot(h, w3_ref[:, f0:f0 + FFN_TF], preferred_element_type=F32)
        mid = (a * jax.nn.sigmoid(a) * b).astype(BF16)
        part = jnp.dot(mid, w2_ref[f0:f0 + FFN_TF, :], preferred_element_type=F32)
        if f0 == 0:
            acc_ref[...] = part
        else:
            acc_ref[...] += part
    _residual_epilogue(acc_ref[...], (x_ref,), gpost_ref, gpre_ref, mod_refs, xo_ref, ho_ref)


def ffn_layer(h, w1, w3, w2, x, g_post, mod, g_pre_next, mod_next, tiles_per_batch, ctx_tiles):
    r, d = h.shape
    tmd = PAIR * TM
    gate_specs = _pair_mod_specs((MOD_GATE_F,), tiles_per_batch, ctx_tiles, d)
    next_specs = _pair_mod_specs((MOD_SHIFT_M, MOD_SCALE_M), tiles_per_batch, ctx_tiles, d)
    mod_specs, mod_args = [], []
    for t in range(PAIR):
        mod_specs += [gate_specs[t], next_specs[2 * t], next_specs[2 * t + 1]]
        mod_args += [mod, mod_next, mod_next]
    return pl.pallas_call(
        _ffn_body,
        grid=(r // tmd,),
        in_specs=[pl.BlockSpec((tmd, d), lambda i: (i, 0)),
                  _const_spec(w1.shape), _const_spec(w3.shape), _const_spec(w2.shape),
                  pl.BlockSpec((tmd, d), lambda i: (i, 0)),
                  _const_spec((1, d)), _const_spec((1, d))] + mod_specs,
        out_specs=[pl.BlockSpec((tmd, d), lambda i: (i, 0)), pl.BlockSpec((tmd, d), lambda i: (i, 0))],
        out_shape=[jax.ShapeDtypeStruct((r, d), F32), jax.ShapeDtypeStruct((r, d), BF16)],
        scratch_shapes=[pltpu.VMEM((tmd, d), F32)],
        compiler_params=pltpu.CompilerParams(dimension_semantics=("arbitrary",), vmem_limit_bytes=VMEM_LIMIT),
        name="l0_ffn",
    )(h, w1, w3, w2, x, g_post.reshape(1, d), g_pre_next.reshape(1, d), *mod_args)


def _proj_body(h_ref, w_ref, wg_ref, o_ref, og_ref):
    h = h_ref[...]
    o_ref[...] = jnp.dot(h, w_ref[...], preferred_element_type=F32).astype(o_ref.dtype)
    og_ref[...] = jnp.dot(h, wg_ref[...], preferred_element_type=F32)


def gdn_proj(h, w, w_gate):
    r, d = h.shape
    n = w.shape[1]
    tmd = PAIR * TM
    return pl.pallas_call(
        _proj_body,
        grid=(r // tmd,),
        in_specs=[pl.BlockSpec((tmd, d), lambda i: (i, 0)), _const_spec(w.shape), _const_spec(w_gate.shape)],
        out_specs=[pl.BlockSpec((tmd, n), lambda i: (i, 0)), pl.BlockSpec((tmd, LANES), lambda i: (i, 0))],
        out_shape=[jax.ShapeDtypeStruct((r, n), BF16), jax.ShapeDtypeStruct((r, LANES), F32)],
        compiler_params=pltpu.CompilerParams(dimension_semantics=("arbitrary",), vmem_limit_bytes=VMEM_LIMIT),
        name="l1_proj",
    )(h, w, w_gate)


ATTN_HEADS_PER_UNIT = 1


def _attn_body(sink_ref, q_ref, kc_ref, vc_ref, k0_ref, k1_ref, k2_ref, v0_ref, v1_ref, v2_ref, o_ref,
               *, n_ctx_blocks, n_blocks):
    n = pl.program_id(1)
    rows = ATTN_HEADS_PER_UNIT * BLOCK
    qi = lax.broadcasted_iota(I32, (rows, 3 * BLOCK), 0) % BLOCK
    kj = lax.broadcasted_iota(I32, (rows, 3 * BLOCK), 1)
    first = jnp.where(n > n_ctx_blocks, 0, BLOCK)
    last = jnp.where(n < n_blocks - 1, 3 * BLOCK, 2 * BLOCK)
    last = jnp.where(n >= n_ctx_blocks, last, 0)
    band_ok = (kj >= qi) & (kj <= qi + 2 * WINDOW) & (kj >= first) & (kj < last)
    q = q_ref[0]
    nt = (((1,), (1,)), ((), ()))
    groups = range(N_KV_HEADS)
    hpu = ATTN_HEADS_PER_UNIT
    units = [(g, [g * GQA_GROUP + s + i for i in range(hpu)]) for g in groups for s in range(0, GQA_GROUP, hpu)]
    idx = range(len(units))
    cols = [slice(g * HEAD_DIM, (g + 1) * HEAD_DIM) for g in groups]
    q_u = [jnp.concatenate([q[:, hd * HEAD_DIM:(hd + 1) * HEAD_DIM] for hd in hs], axis=0) for _, hs in units]
    k_c = [kc_ref[0][:, c] for c in cols]
    k_b = [jnp.concatenate([k0_ref[0][:, c], k1_ref[0][:, c], k2_ref[0][:, c]], axis=0) for c in cols]
    ones_b = jnp.ones((3 * BLOCK, HEAD_DIM), BF16)
    ones_c = jnp.ones((kc_ref.shape[1], HEAD_DIM), BF16)
    v_b = [jnp.concatenate([jnp.concatenate([v0_ref[0][:, c], v1_ref[0][:, c], v2_ref[0][:, c]], axis=0), ones_b],
                           axis=1) for c in cols]
    v_c = [jnp.concatenate([vc_ref[0][:, c], ones_c], axis=1) for c in cols]
    s_c = [lax.dot_general(q_u[u], k_c[units[u][0]], nt, preferred_element_type=F32) for u in idx]
    s_b = [lax.dot_general(q_u[u], k_b[units[u][0]], nt, preferred_element_type=F32) for u in idx]
    s_b = [jnp.where(band_ok, s, NEG_INF) for s in s_b]
    sink = [jnp.concatenate([sink_ref[hd] + jnp.zeros((BLOCK, 1), F32) for hd in hs], axis=0) for _, hs in units]
    m = [jnp.maximum(jnp.maximum(jnp.max(s_c[u], axis=-1, keepdims=True), jnp.max(s_b[u], axis=-1, keepdims=True)),
                     sink[u]) for u in idx]
    p_c = [jnp.exp(s_c[u] - m[u]) for u in idx]
    p_b = [jnp.exp(s_b[u] - m[u]) for u in idx]
    o_c = [jnp.dot(p_c[u].astype(BF16), v_c[units[u][0]], preferred_element_type=F32) for u in idx]
    o_b = [jnp.dot(p_b[u].astype(BF16), v_b[units[u][0]], preferred_element_type=F32) for u in idx]
    outs = []
    for u in idx:
        acc = o_c[u] + o_b[u]
        den = acc[:, HEAD_DIM:HEAD_DIM + 1] + jnp.exp(sink[u] - m[u])
        o = acc[:, :HEAD_DIM] / den
        outs += [o[i * BLOCK:(i + 1) * BLOCK] for i in range(hpu)]
    o_ref[0] = jnp.concatenate(outs, axis=1).astype(o_ref.dtype)


def attention(p0, sinks, n_ctx):
    bsz, ltot, _ = p0.shape
    n_blocks = ltot // BLOCK
    n_ctx_blocks = n_ctx // BLOCK
    k_blk = (POOL_DIM + Q_DIM) // KV_DIM
    v_blk = k_blk + 1

    def band(dn, blk):
        return pl.BlockSpec((1, BLOCK, KV_DIM),
                            lambda b, n: (b, jnp.clip(n + dn, n_ctx_blocks, n_blocks - 1), blk))

    return pl.pallas_call(
        functools.partial(_attn_body, n_ctx_blocks=n_ctx_blocks, n_blocks=n_blocks),
        grid=(bsz, n_blocks),
        in_specs=[pl.BlockSpec(memory_space=pltpu.SMEM),
                  pl.BlockSpec((1, BLOCK, Q_DIM), lambda b, n: (b, n, POOL_DIM // Q_DIM)),
                  pl.BlockSpec((1, n_ctx, KV_DIM), lambda b, n: (b, 0, k_blk)),
                  pl.BlockSpec((1, n_ctx, KV_DIM), lambda b, n: (b, 0, v_blk)),
                  band(-1, k_blk), band(0, k_blk), band(1, k_blk),
                  band(-1, v_blk), band(0, v_blk), band(1, v_blk)],
        out_specs=pl.BlockSpec((1, BLOCK, Q_DIM), lambda b, n: (b, n, 0)),
        out_shape=jax.ShapeDtypeStruct((bsz, ltot, Q_DIM), BF16),
        compiler_params=pltpu.CompilerParams(dimension_semantics=("arbitrary", "arbitrary"),
                                             vmem_limit_bytes=VMEM_LIMIT),
        name="l0_attention",
    )(sinks.astype(F32), p0, p0, p0, p0, p0, p0, p0, p0, p0)


POOL_HALO = 16


def _pool_body(up_ref, uc_ref, un_ref, w_ref, scale_ref, o_ref, *, tiles_per_batch, ctx_tiles):
    j = pl.program_id(1)
    seg_lo = jnp.where(j < ctx_tiles, 0, ctx_tiles * TM)
    seg_hi = jnp.where(j < ctx_tiles, ctx_tiles * TM, tiles_per_batch * TM)
    halo = jnp.concatenate([up_ref[0][TM - POOL_HALO:], uc_ref[0], un_ref[0][:POOL_HALO]], axis=0)
    n_h = TM + 2 * POOL_HALO
    t = j * TM + lax.broadcasted_iota(I32, (TM, n_h), 0)
    pos = j * TM - POOL_HALO + lax.broadcasted_iota(I32, (TM, n_h), 1)
    t_col = j * TM + lax.broadcasted_iota(I32, (TM, 1), 0)
    cur = uc_ref[0]
    for g, w in enumerate(POOL_WINDOWS):
        cols = slice(g * POOL_GROUP_DIM, (g + 1) * POOL_GROUP_DIM)
        lo = jnp.maximum(t - w // 2, seg_lo)
        hi = jnp.minimum(t + w // 2, seg_hi)
        window = ((pos >= lo) & (pos < hi)).astype(BF16)
        cnt = (jnp.minimum(t_col + w // 2, seg_hi) - jnp.maximum(t_col - w // 2, seg_lo)).astype(F32)
        mean = jnp.dot(window, halo[:, cols], preferred_element_type=F32) / cnt
        delta = (mean - cur[:, cols].astype(F32)).astype(BF16)
        y = jnp.dot(delta, w_ref[g], preferred_element_type=F32) * scale_ref[:, cols]
        o_ref[0, :, cols] = y.astype(o_ref.dtype)


def pool_mixer(p0, pool_w, pool_scale, n_ctx):
    bsz, ltot, _ = p0.shape
    tiles_per_batch = ltot // TM
    ctx_tiles = n_ctx // TM

    def tile(dj):
        return pl.BlockSpec((1, TM, POOL_DIM), lambda b, j: (b, jnp.clip(j + dj, 0, tiles_per_batch - 1), 0))

    return pl.pallas_call(
        functools.partial(_pool_body, tiles_per_batch=tiles_per_batch, ctx_tiles=ctx_tiles),
        grid=(bsz, tiles_per_batch),
        in_specs=[tile(-1), tile(0), tile(1),
                  pl.BlockSpec(pool_w.shape, lambda b, j: (0, 0, 0)),
                  pl.BlockSpec((1, POOL_DIM), lambda b, j: (0, 0))],
        out_specs=pl.BlockSpec((1, TM, POOL_DIM), lambda b, j: (b, j, 0)),
        out_shape=jax.ShapeDtypeStruct((bsz, ltot, POOL_DIM), BF16),
        compiler_params=pltpu.CompilerParams(dimension_semantics=("arbitrary", "arbitrary")),
        name="l0_pool",
    )(p0, p0, p0, pool_w.astype(BF16), pool_scale.reshape(1, POOL_DIM).astype(F32))


def kernel(x, c, ctx, c_ctx, l0_mod_w, l0_mod_b, l0_mix_pre, l0_mix_post, l0_ffn_pre, l0_ffn_post, l0_w_in, l0_pool_w, l0_pool_scale, l0_sinks, l0_w_out, l0_ffn_w1, l0_ffn_w3, l0_ffn_w2, l1_mod_w, l1_mod_b, l1_mix_pre, l1_mix_post, l1_ffn_pre, l1_ffn_post, l1_w_in, l1_conv_w, l1_a_log, l1_dt_bias, l1_out_norm, l1_w_out, l1_router, l1_moe_w1, l1_moe_w3, l1_moe_w2):
    bsz, n_lat, d = x.shape
    n_ctx = ctx.shape[1]
    ltot = n_ctx + n_lat
    assert n_ctx % TM == 0 and n_lat % TM == 0 and n_ctx % GC == 0
    assert (bsz * ltot) % (PAIR * TM) == 0 and (bsz * n_lat) % (PAIR * TM) == 0
    tiles = ltot // TM
    ctx_tiles = n_ctx // TM
    lat_tiles = n_lat // TM
    bf = lambda w: w.astype(BF16)

    mod0 = modulation(c, c_ctx, l0_mod_w, l0_mod_b)
    mod1 = modulation(c, c_ctx, l1_mod_w, l1_mod_b)
    cos, sin = rope_tables(n_ctx, n_lat)

    p0, x_all = pre_mm(ctx, x, l0_mix_pre, mod0, bf(l0_w_in), cos, sin)
    p0 = p0.reshape(bsz, ltot, -1)
    attn = attention(p0, l0_sinks, n_ctx).reshape(bsz * ltot, Q_DIM)
    pooled = pool_mixer(p0, l0_pool_w, l0_pool_scale, n_ctx).reshape(bsz * ltot, POOL_DIM)
    w_out0 = bf(l0_w_out)
    x1, h1 = post_mm([pooled, attn], [w_out0[:POOL_DIM], w_out0[POOL_DIM:]], x_all, 0, l0_mix_post, l0_ffn_pre,
                     mod0, tiles, ctx_tiles, tiles, BF16, "l0_out_proj")
    x2, h2 = ffn_layer(h1, bf(l0_ffn_w1), bf(l0_ffn_w3), bf(l0_ffn_w2), x1, l0_ffn_post, mod0, l1_mix_pre, mod1,
                       tiles, ctx_tiles)

    n_heads = l1_a_log.shape[1]
    qkvz = 4 * n_heads * GD
    w_in1 = bf(l1_w_in)
    w_gate = jnp.zeros((d, LANES), BF16).at[:, :4 * n_heads].set(w_in1[:, qkvz:])
    p1, ab = gdn_proj(h2, w_in1[:, :qkvz], w_gate)
    ab_rows = gdn_gate_rows(ab.reshape(bsz, ltot, LANES)[..., :4 * n_heads], n_heads)
    y = gdn_core(p1.reshape(bsz, ltot, qkvz), ab_rows, l1_conv_w, l1_a_log, l1_dt_bias, l1_out_norm, n_ctx)
    x3, h3 = post_mm([y.reshape(bsz * n_lat, n_heads * GD)], [bf(l1_w_out)], x2, ctx_tiles, l1_mix_post, l1_ffn_pre,
                     mod1, lat_tiles, 0, tiles, F32, "l1_out_proj")
    out = moe_layer(h3, l1_router, l1_moe_w1, l1_moe_w3, l1_moe_w2, x3, l1_ffn_post, mod1, n_lat)
    return out.reshape(bsz, n_lat, d)
```

```python
import functools
import math

import jax
import jax.numpy as jnp
from jax import lax
from jax.experimental import pallas as pl
from jax.experimental.pallas import tpu as pltpu

F32 = jnp.float32
BF16 = jnp.bfloat16
I32 = jnp.int32

LANES = 128
D_MODEL = 1024
GRID_W = 64
EPS = 1e-6
NEG_INF = -1e30

POOL_GROUPS = 4
POOL_GROUP_DIM = 128
POOL_DIM = POOL_GROUPS * POOL_GROUP_DIM
POOL_WINDOWS = (2, 4, 8, 16)
HEAD_DIM = 64
N_HEADS = 8
N_KV_HEADS = 2
GQA_GROUP = N_HEADS // N_KV_HEADS
Q_DIM = N_HEADS * HEAD_DIM
KV_DIM = N_KV_HEADS * HEAD_DIM
WINDOW = 128
BLOCK = 128
ROPE_THETA = 10000.0

GDN_HEADS = 8
GDN_HEAD_DIM = 128
GDN_DIM = GDN_HEADS * GDN_HEAD_DIM
CONV_K = 5
CHUNK = 64

N_EXPERTS = 8
TOP_K = 2

ROUTE_TILE = 512
MOE_TM = 512
MOE_TF = 1792
ROW_TILE = 512
ISSUE_UNROLL = 8


def _split_bf16(a):
    hi = a.astype(BF16)
    lo = (a - hi.astype(F32)).astype(BF16)
    return hi, lo


def _route_body(h_ref, rhi_ref, rlo_ref, route_ref, cnt_ref, carry_ref):
    i = pl.program_id(0)
    tr = h_ref.shape[0]

    @pl.when(i == 0)
    def _():
        carry_ref[...] = jnp.zeros_like(carry_ref)

    h_hi, h_lo = _split_bf16(h_ref[...])
    r_hi = rhi_ref[...]
    r_lo = rlo_ref[...]
    logits = (jnp.dot(h_hi, r_hi, preferred_element_type=F32)
              + jnp.dot(h_hi, r_lo, preferred_element_type=F32)
              + jnp.dot(h_lo, r_hi, preferred_element_type=F32))
    lane = lax.broadcasted_iota(I32, (tr, LANES), 1)
    logits = jnp.where(lane < N_EXPERTS, logits, -jnp.inf)
    m1 = jnp.max(logits, axis=-1, keepdims=True)
    i1 = jnp.min(jnp.where(logits == m1, lane, LANES), axis=-1, keepdims=True)
    rest = jnp.where(lane == i1, -jnp.inf, logits)
    m2 = jnp.max(rest, axis=-1, keepdims=True)
    i2 = jnp.min(jnp.where(rest == m2, lane, LANES), axis=-1, keepdims=True)
    e2 = jnp.exp(m2 - m1)
    g1 = 1.0 / (1.0 + e2)
    g2 = e2 / (1.0 + e2)

    onehot = ((lane == i1) | (lane == i2)).astype(F32)
    row = lax.broadcasted_iota(I32, (tr, tr), 0)
    col = lax.broadcasted_iota(I32, (tr, tr), 1)
    strict = (row > col).astype(BF16)
    before = jnp.dot(strict, onehot.astype(BF16), preferred_element_type=F32) + carry_ref[...]
    rank1 = jnp.sum(jnp.where(lane == i1, before, 0.0), axis=-1, keepdims=True)
    rank2 = jnp.sum(jnp.where(lane == i2, before, 0.0), axis=-1, keepdims=True)
    carry_ref[...] += jnp.sum(onehot, axis=0, keepdims=True)

    packed = jnp.where(lane == 0, i1.astype(F32), 0.0)
    packed = jnp.where(lane == 1, i2.astype(F32), packed)
    packed = jnp.where(lane == 2, g1, packed)
    packed = jnp.where(lane == 3, g2, packed)
    packed = jnp.where(lane == 4, rank1, packed)
    packed = jnp.where(lane == 5, rank2, packed)
    route_ref[...] = packed
    cnt_ref[...] = jnp.broadcast_to(carry_ref[...], cnt_ref.shape)


def moe_route(h, router):
    t, d = h.shape
    tr = min(ROUTE_TILE, t)
    r_pad = jnp.zeros((d, LANES), F32).at[:, :N_EXPERTS].set(router.astype(F32))
    r_hi, r_lo = _split_bf16(r_pad)
    route, cnt = pl.pallas_call(
        _route_body,
        grid=(t // tr,),
        in_specs=[
            pl.BlockSpec((tr, d), lambda i: (i, 0)),
            pl.BlockSpec((d, LANES), lambda i: (0, 0)),
            pl.BlockSpec((d, LANES), lambda i: (0, 0)),
        ],
        out_specs=[
            pl.BlockSpec((tr, LANES), lambda i: (i, 0)),
            pl.BlockSpec((8, LANES), lambda i: (0, 0)),
        ],
        out_shape=[
            jax.ShapeDtypeStruct((t, LANES), F32),
            jax.ShapeDtypeStruct((8, LANES), F32),
        ],
        scratch_shapes=[pltpu.VMEM((1, LANES), F32)],
        compiler_params=pltpu.CompilerParams(dimension_semantics=("arbitrary",)),
        name="moe_route",
    )(h, r_hi, r_lo)
    return route, cnt


def _row_copy(src_ref, src_row, dst_ref, dst_row, sem):
    return pltpu.make_async_copy(src_ref.at[pl.ds(src_row, 1)], dst_ref.at[pl.ds(dst_row, 1)], sem)


def _scatter_body(pos_ref, h_ref, xs_in_ref, xs_ref, sem, *, ts):
    del xs_in_ref

    def issue(r, c):
        for k in range(TOP_K):
            _row_copy(h_ref, r, xs_ref, pos_ref[0, 0, TOP_K * r + k], sem).start()
        return c

    lax.fori_loop(0, ts, issue, 0, unroll=ISSUE_UNROLL)
    for _ in range(TOP_K):
        pltpu.make_async_copy(h_ref, xs_ref.at[pl.ds(0, ts)], sem).wait()


def moe_scatter(h, pos, p_rows):
    t, d = h.shape
    ts = min(ROW_TILE, t)
    nt = t // ts
    xs0 = jnp.zeros((p_rows, d), h.dtype)
    return pl.pallas_call(
        functools.partial(_scatter_body, ts=ts),
        grid=(nt,),
        in_specs=[
            pl.BlockSpec((1, 1, TOP_K * ts), lambda i: (i, 0, 0), memory_space=pltpu.SMEM),
            pl.BlockSpec((ts, d), lambda i: (i, 0)),
            pl.BlockSpec(memory_space=pl.ANY),
        ],
        out_specs=pl.BlockSpec(memory_space=pl.ANY),
        out_shape=jax.ShapeDtypeStruct((p_rows, d), h.dtype),
        scratch_shapes=[pltpu.SemaphoreType.DMA(())],
        input_output_aliases={2: 0},
        compiler_params=pltpu.CompilerParams(dimension_semantics=("arbitrary",), disable_bounds_checks=True),
        name="moe_scatter",
    )(pos.reshape(nt, 1, TOP_K * ts), h, xs0)


def _expert_body(te_ref, nu_ref, x_ref, w1_ref, w3_ref, w2_ref, o_ref):
    i = pl.program_id(0)
    j = pl.program_id(1)

    @pl.when(j == 0)
    def _():
        o_ref[...] = jnp.zeros_like(o_ref)

    @pl.when(i < nu_ref[0])
    def _():
        x = x_ref[...].astype(BF16)
        a = jnp.dot(x, w1_ref[0], preferred_element_type=F32)
        b = jnp.dot(x, w3_ref[0], preferred_element_type=F32)
        mid = (a * jax.nn.sigmoid(a) * b).astype(BF16)
        o_ref[...] += jnp.dot(mid, w2_ref[0], preferred_element_type=F32)


def moe_experts(xs, w1, w3, w2, tile_expert, n_used, tm):
    p_rows, d = xs.shape
    n_exp, _, d_exp = w1.shape
    tf = MOE_TF if d_exp % MOE_TF == 0 else d_exp
    nj = d_exp // tf
    n_tiles = p_rows // tm

    def jj(i, j, nu):
        return jnp.where(i < nu[0], j, nj - 1)

    grid_spec = pltpu.PrefetchScalarGridSpec(
        num_scalar_prefetch=2,
        grid=(n_tiles, nj),
        in_specs=[
            pl.BlockSpec((tm, d), lambda i, j, te, nu: (i, 0)),
            pl.BlockSpec((1, d, tf), lambda i, j, te, nu: (te[i], 0, jj(i, j, nu))),
            pl.BlockSpec((1, d, tf), lambda i, j, te, nu: (te[i], 0, jj(i, j, nu))),
            pl.BlockSpec((1, tf, d), lambda i, j, te, nu: (te[i], jj(i, j, nu), 0)),
        ],
        out_specs=pl.BlockSpec((tm, d), lambda i, j, te, nu: (i, 0)),
    )
    return pl.pallas_call(
        _expert_body,
        grid_spec=grid_spec,
        out_shape=jax.ShapeDtypeStruct((p_rows, d), F32),
        compiler_params=pltpu.CompilerParams(
            dimension_semantics=("arbitrary", "arbitrary"),
            vmem_limit_bytes=56 * 1024 * 1024),
        name="moe_experts",
    )(tile_expert, n_used, xs, w1, w3, w2)


def _combine_body(pos_ref, pos_next_ref, gate_ref, ys_ref, x_ref, gain_ref, mgate_ref, y_ref, buf, sem, *, ts):
    i = pl.program_id(0)
    slot = i % 2

    def gather(p_ref, s):
        def issue(r, c):
            for k in range(TOP_K):
                _row_copy(ys_ref, p_ref[0, 0, TOP_K * r + k], buf.at[s, k], r, sem.at[s]).start()
            return c
        lax.fori_loop(0, ts, issue, 0, unroll=ISSUE_UNROLL)

    @pl.when(i == 0)
    def _():
        gather(pos_ref, slot)

    @pl.when(i + 1 < pl.num_programs(0))
    def _():
        gather(pos_next_ref, 1 - slot)

    for k in range(TOP_K):
        pltpu.make_async_copy(ys_ref.at[pl.ds(0, ts)], buf.at[slot, k], sem.at[slot]).wait()
    g = gate_ref[...]
    y = g[:, 2:3] * buf[slot, 0] + g[:, 3:4] * buf[slot, 1]
    y_ref[...] = x_ref[...] + mgate_ref[0] * _rms(y, gain_ref[...])


def moe_combine(ys, pos, route, x, gain, mod, rows_per_batch):
    t = route.shape[0]
    d = ys.shape[1]
    ts = min(ROW_TILE, rows_per_batch)
    assert rows_per_batch % ts == 0
    tiles_per_batch = rows_per_batch // ts
    nt = t // ts
    pos3 = pos.reshape(nt, 1, TOP_K * ts)
    return pl.pallas_call(
        functools.partial(_combine_body, ts=ts),
        grid=(nt,),
        in_specs=[
            pl.BlockSpec((1, 1, TOP_K * ts), lambda i: (i, 0, 0), memory_space=pltpu.SMEM),
            pl.BlockSpec((1, 1, TOP_K * ts), lambda i: (jnp.minimum(i + 1, nt - 1), 0, 0), memory_space=pltpu.SMEM),
            pl.BlockSpec((ts, LANES), lambda i: (i, 0)),
            pl.BlockSpec(memory_space=pl.ANY),
            pl.BlockSpec((ts, d), lambda i: (i, 0)),
            pl.BlockSpec((1, d), lambda i: (0, 0)),
            _mod_spec(MOD_GATE_F, tiles_per_batch, 0, d),
        ],
        out_specs=pl.BlockSpec((ts, d), lambda i: (i, 0)),
        out_shape=jax.ShapeDtypeStruct((t, d), F32),
        scratch_shapes=[pltpu.VMEM((2, TOP_K, ts, d), F32), pltpu.SemaphoreType.DMA((2,))],
        compiler_params=pltpu.CompilerParams(dimension_semantics=("arbitrary",), disable_bounds_checks=True),
        name="moe_combine",
    )(pos3, pos3, route, ys, x, gain.reshape(1, d), mod)


def moe_layer(tok, router, w1, w3, w2, x, gain, mod, rows_per_batch):
    t = tok.shape[0]
    tm = min(MOE_TM, t)
    route, cnt = moe_route(tok, router)

    counts = cnt[0, :N_EXPERTS].astype(I32)
    padded = ((counts + tm - 1) // tm) * tm
    ends = jnp.cumsum(padded)
    starts = ends - padded
    expert = route[:, 0:TOP_K].astype(I32)
    rank = route[:, 4:4 + TOP_K].astype(I32)
    start_of = jnp.sum(jnp.where(expert[..., None] == jnp.arange(N_EXPERTS), starts, 0), axis=-1)
    pos = start_of + rank

    n_tiles = (TOP_K * t) // tm + N_EXPERTS
    n_used = (ends[-1] // tm).astype(I32)
    tile_idx = jnp.minimum(jnp.arange(n_tiles, dtype=I32), n_used - 1)
    tile_expert = jnp.sum(tile_idx[:, None] * tm >= ends[None, :], axis=-1).astype(I32)
    tile_expert = jnp.minimum(tile_expert, N_EXPERTS - 1)

    xs = moe_scatter(tok, pos, n_tiles * tm)
    ys = moe_experts(xs, w1.astype(BF16), w3.astype(BF16), w2.astype(BF16), tile_expert, n_used.reshape(1), tm)
    return moe_combine(ys, pos, route, x, gain, mod, rows_per_batch)


GC = 128
GD = GDN_HEAD_DIM
PAD = 8
GDN_UNROLL = 9
GDN_BASE = 16
N_MERGE = 3
M_INCL_F, M_STRICT_F, M_INCL_B, M_STRICT_B, M_EYE, M_BASE, M_OFF0 = 0, 1, 2, 3, 4, 5, 6
N_MASKS = M_OFF0 + N_MERGE


def _bdot(a, b):
    return jnp.dot(a.astype(BF16), b.astype(BF16), preferred_element_type=F32)


def _bdot_nt(a, b):
    return lax.dot_general(a.astype(BF16), b.astype(BF16), (((1,), (1,)), ((), ())), preferred_element_type=F32)


def _gdn_body(alog_ref, dtb_ref, q_ref, k_ref, v_ref, z_ref, ab_ref, cq_ref, ck_ref, cv_ref, onorm_ref,
              o_ref, pq_s, pk_s, pv_s, o_s, b_s, gq_s, gate_s, state_s, mask_s,
              *, n_ctx_chunks, n_chunks, n_heads, unroll):
    h = pl.program_id(1)
    ltot = n_chunks * GC
    lc = n_ctx_chunks * GC

    row = lax.broadcasted_iota(I32, (GC, GC), 0)
    col = lax.broadcasted_iota(I32, (GC, GC), 1)
    mask_s[M_INCL_F] = (row >= col).astype(F32)
    mask_s[M_STRICT_F] = (row > col).astype(F32)
    mask_s[M_INCL_B] = (row <= col).astype(F32)
    mask_s[M_STRICT_B] = (row < col).astype(F32)
    mask_s[M_EYE] = (row == col).astype(F32)
    mask_s[M_BASE] = ((row // GDN_BASE) == (col // GDN_BASE)).astype(F32)
    for lvl in range(N_MERGE):
        s = GDN_BASE << lvl
        mask_s[M_OFF0 + lvl] = (((row // (2 * s)) == (col // (2 * s))) & ((row // s) != (col // s))).astype(F32)

    for src_ref, dst in ((q_ref, pq_s), (k_ref, pk_s), (v_ref, pv_s)):
        for off in (0, PAD + lc, 2 * PAD + ltot):
            dst[pl.ds(off, PAD), :] = jnp.zeros((PAD, GD), F32)
        dst[pl.ds(PAD, lc), :] = src_ref[0, pl.ds(0, lc), :].astype(F32)
        dst[pl.ds(2 * PAD + lc, ltot - lc), :] = src_ref[0, pl.ds(lc, ltot - lc), :].astype(F32)

    for d in range(2):
        a = ab_ref[0, d * 2 * n_heads + h]
        b = ab_ref[0, d * 2 * n_heads + n_heads + h]
        xa = a + dtb_ref[d, h]
        softplus = jnp.maximum(xa, 0.0) + jnp.log1p(jnp.exp(-jnp.abs(xa)))
        g = -jnp.exp(alog_ref[d, h]) * softplus
        beta = 1.0 / (1.0 + jnp.exp(-b))
        tri = mask_s[M_INCL_B if d == 0 else M_INCL_F].astype(BF16)
        g_hi = g.astype(BF16)
        g_lo = (g - g_hi.astype(F32)).astype(BF16)
        gam = (jnp.dot(g_hi, tri, preferred_element_type=F32) + jnp.dot(g_lo, tri, preferred_element_type=F32))
        gate_s[d, 0] = gam
        gate_s[d, 1] = beta
        gate_s[d, 2] = jnp.broadcast_to(jnp.sum(g, axis=-1, keepdims=True), g.shape)

    def conv_chunk(c, pad_ref, w_ref, normalise, scale):
        start = pl.multiple_of(c * GC + jnp.where(c >= n_ctx_chunks, PAD, 0), PAD)
        acc = jnp.zeros((GC, GD), F32)
        for j in range(CONV_K):
            tap = pad_ref[pl.ds(start + (PAD - CONV_K // 2 + j), GC), :]
            acc = acc + tap * w_ref[j:j + 1, :].astype(F32)
        y = acc * jax.nn.sigmoid(acc)
        if normalise:
            y = y * (lax.rsqrt(jnp.sum(y * y, axis=-1, keepdims=True) + EPS) * scale)
        return y

    def group_chunks(i):
        return [jnp.minimum(i * unroll + u, n_chunks - 1) for u in range(unroll)]

    def prep(i, carry):
        chunks = group_chunks(i)
        qs = [conv_chunk(c, pq_s, cq_ref, True, GD ** -0.5) for c in chunks]
        ks = [conv_chunk(c, pk_s, ck_ref, True, 1.0) for c in chunks]
        vs = [conv_chunk(c, pv_s, cv_ref, False, 1.0) for c in chunks]
        kqs = [_bdot_nt(jnp.concatenate([k, q], axis=0), k) for k, q in zip(ks, qs)]
        kts = [k.T for k in ks]
        lanes = [(u, d) for u in range(unroll) for d in range(2)]
        lms, dbs, e_cols, kdts = [], [], [], []
        for u, d in lanes:
            c = chunks[u]
            gam_row = gate_s[d, 0, pl.ds(c, 1), :]
            beta_row = gate_s[d, 1, pl.ds(c, 1), :]
            tot_row = gate_s[d, 2, pl.ds(c, 1), :]
            gam_col = jnp.sum(mask_s[M_EYE] * gam_row, axis=1, keepdims=True)
            db = jnp.exp((gam_col - gam_row) * mask_s[M_INCL_F if d == 0 else M_INCL_B]) * beta_row
            dbs.append(db)
            lms.append(kqs[u][:GC] * db * mask_s[M_STRICT_F if d == 0 else M_STRICT_B])
            e_cols.append(jnp.exp(gam_col))
            kdts.append(kts[u] * (jnp.exp(tot_row - gam_row) * beta_row))
        xs = [-lm * mask_s[M_BASE] for lm in lms]
        ts = [mask_s[M_EYE] + x for x in xs]
        for _ in range(3):
            xs = [_bdot(x, x) for x in xs]
            ts = [t + _bdot(t, x) for t, x in zip(ts, xs)]
        for lvl in range(N_MERGE):
            ys = [_bdot(lm * mask_s[M_OFF0 + lvl], t) for lm, t in zip(lms, ts)]
            ts = [t - _bdot(t, y) for t, y in zip(ts, ys)]
        wus = [_bdot(t, jnp.concatenate([ks[u] * e, vs[u]], axis=1))
               for t, e, (u, d) in zip(ts, e_cols, lanes)]
        x12s = []
        for wu, db, kdt, (u, d) in zip(wus, dbs, kdts, lanes):
            qkb = kqs[u][GC:] * db * mask_s[M_INCL_F if d == 0 else M_INCL_B]
            x12s.append(_bdot(jnp.concatenate([kdt, qkb], axis=0), wu))
        for x12, e, (u, d) in zip(x12s, e_cols, lanes):
            t0 = pl.multiple_of(chunks[u] * GC, GC)
            b_s[d, pl.ds(t0, GC), :] = x12[:GD, GD:]
            o_s[d, pl.ds(t0, GC), :] = x12[GD:, GD:]
            q_eff = qs[u] * e - x12[GD:, :GD]
            gq_s[d, pl.ds(pl.multiple_of(2 * t0, 2 * GC), 2 * GC), :] = (
                jnp.concatenate([x12[:GD, :GD], q_eff], axis=0).astype(BF16))
        return carry

    lax.fori_loop(0, -(-n_chunks // unroll), prep, 0)

    state_s[...] = jnp.zeros_like(state_s)

    def scan(s, carry):
        for d in range(2):
            if d == 0:
                c = s
            else:
                c = jnp.where(s < n_ctx_chunks, n_ctx_chunks - 1 - s, n_chunks - 1 - (s - n_ctx_chunks))
            t0 = pl.multiple_of(c * GC, GC)
            st = state_s[d]
            r = jnp.dot(gq_s[d, pl.ds(pl.multiple_of(2 * t0, 2 * GC), 2 * GC), :], st.astype(BF16),
                        preferred_element_type=F32)
            cd = jnp.exp(gate_s[d, 2, pl.ds(c, 1), 0:1])
            state_s[d] = st * cd - r[:GD] + b_s[d, pl.ds(t0, GC), :]
            o_s[d, pl.ds(t0, GC), :] += r[GD:]
        return carry

    lax.fori_loop(0, n_chunks, scan, 0)

    gain = onorm_ref[...].astype(F32)

    def finish(c, carry):
        t0 = pl.multiple_of(lc + c * GC, GC)
        o = o_s[0, pl.ds(t0, GC), :] + o_s[1, pl.ds(t0, GC), :]
        z = z_ref[0, pl.ds(t0, GC), :].astype(F32)
        y = o * lax.rsqrt(jnp.mean(o * o, axis=-1, keepdims=True) + EPS) * gain
        y = y * (z * jax.nn.sigmoid(z))
        o_ref[0, pl.ds(pl.multiple_of(c * GC, GC), GC), :] = y.astype(o_ref.dtype)
        return carry

    lax.fori_loop(0, n_chunks - n_ctx_chunks, finish, 0, unroll=4)


def gdn_core(p_all, ab_rows, conv_w, a_log, dt_bias, out_norm, n_ctx):
    bsz, ltot, _ = p_all.shape
    n_heads = a_log.shape[1]
    n_chunks = ltot // GC
    n_ctx_chunks = n_ctx // GC
    lat = ltot - n_ctx
    n_rows = ab_rows.shape[2]
    unroll = min(GDN_UNROLL, n_chunks)
    body = functools.partial(_gdn_body, n_ctx_chunks=n_ctx_chunks, n_chunks=n_chunks, n_heads=n_heads,
                             unroll=unroll)
    smem = pl.BlockSpec(memory_space=pltpu.SMEM)

    def col(off):
        return pl.BlockSpec((1, ltot, GD), lambda b, h: (b, 0, off * n_heads + h))

    def cw(off):
        return pl.BlockSpec((CONV_K, GD), lambda b, h: (0, off * n_heads + h))

    return pl.pallas_call(
        body,
        grid=(bsz, n_heads),
        in_specs=[smem, smem, col(0), col(1), col(2), col(3),
                  pl.BlockSpec((1, 4 * n_heads, n_rows, GC), lambda b, h: (b, 0, 0, 0)),
                  cw(0), cw(1), cw(2),
                  pl.BlockSpec((1, GD), lambda b, h: (0, 0))],
        out_specs=pl.BlockSpec((1, lat, GD), lambda b, h: (b, 0, h)),
        out_shape=jax.ShapeDtypeStruct((bsz, lat, n_heads * GD), BF16),
        scratch_shapes=[
            pltpu.VMEM((ltot + 3 * PAD, GD), F32),
            pltpu.VMEM((ltot + 3 * PAD, GD), F32),
            pltpu.VMEM((ltot + 3 * PAD, GD), F32),
            pltpu.VMEM((2, ltot, GD), F32),
            pltpu.VMEM((2, ltot, GD), F32),
            pltpu.VMEM((2, 2 * ltot, GD), BF16),
            pltpu.VMEM((2, 3, n_rows, GC), F32),
            pltpu.VMEM((2, GD, GD), F32),
            pltpu.VMEM((N_MASKS, GC, GC), F32),
        ],
        compiler_params=pltpu.CompilerParams(
            dimension_semantics=("arbitrary", "arbitrary"),
            vmem_limit_bytes=56 * 1024 * 1024),
        name="gdn_core",
    )(a_log.astype(F32), dt_bias.astype(F32), p_all, p_all, p_all, p_all, ab_rows,
      conv_w, conv_w, conv_w, out_norm.reshape(1, GD))


def gdn_gate_rows(ab, n_heads):
    bsz, ltot, _ = ab.shape
    n_chunks = ltot // GC
    n_rows = -(-n_chunks // 8) * 8
    t = jnp.transpose(ab.astype(F32), (0, 2, 1)).reshape(bsz, 4 * n_heads, n_chunks, GC)
    return jnp.pad(t, ((0, 0), (0, 0), (0, n_rows - n_chunks), (0, 0)))


TM = 256
EPI_ROWS = 128
PAIR = 2
FFN_TF = 256
MOD_TN = 512
MOD_SHIFT_M, MOD_SCALE_M, MOD_GATE_M, MOD_SHIFT_F, MOD_SCALE_F, MOD_GATE_F = range(6)
N_MOD = 6
RESIDENT = dict(pipeline_mode=pl.Buffered(1))
VMEM_LIMIT = 56 * 1024 * 1024


def _rms(x, gain):
    return x * lax.rsqrt(jnp.mean(x * x, axis=-1, keepdims=True) + EPS) * gain


def _mod_spec(k, tiles_per_batch, ctx_tiles, d, half=None):
    def index(i, *_):
        t = i if half is None else PAIR * i + half
        seg = jnp.where(t % tiles_per_batch >= ctx_tiles, 1, 0)
        return ((t // tiles_per_batch) * 2 + seg) * N_MOD + k, 0, 0
    return pl.BlockSpec((1, 1, d), index)


def _const_spec(shape):
    return pl.BlockSpec(shape, lambda i, *_: (0,) * len(shape), **RESIDENT)


def _mod_body(c_ref, w_ref, b_ref, o_ref):
    c = c_ref[...]
    s = (c * jax.nn.sigmoid(c)).astype(BF16)
    o_ref[...] = jnp.dot(s, w_ref[...].astype(BF16), preferred_element_type=F32) + b_ref[...]


def modulation(c, c_ctx, mod_w, mod_b):
    bsz, d = c.shape
    n = mod_w.shape[1]
    rows = -(-(bsz + 1) // 8) * 8
    cc = jnp.zeros((rows, d), F32).at[:bsz].set(c).at[bsz].set(c_ctx)
    out = pl.pallas_call(
        _mod_body,
        grid=(n // MOD_TN,),
        in_specs=[pl.BlockSpec((rows, d), lambda j: (0, 0)),
                  pl.BlockSpec((d, MOD_TN), lambda j: (0, j)),
                  pl.BlockSpec((1, MOD_TN), lambda j: (0, j))],
        out_specs=pl.BlockSpec((rows, MOD_TN), lambda j: (0, j)),
        out_shape=jax.ShapeDtypeStruct((rows, n), F32),
        compiler_params=pltpu.CompilerParams(dimension_semantics=("arbitrary",)),
        name="modulation",
    )(cc, mod_w, mod_b.reshape(1, n))
    lat = out[:bsz].reshape(bsz, 1, N_MOD, d)
    ctx = jnp.broadcast_to(out[bsz].reshape(1, 1, N_MOD, d), (bsz, 1, N_MOD, d))
    return jnp.concatenate([ctx, lat], axis=1).reshape(bsz * 2 * N_MOD, 1, d)


def _pre_mm_body(*refs, tiles_per_batch, ctx_tiles, rope_lo, rope_hi, q_hi):
    per_tile = [refs[6 * t:6 * t + 6] for t in range(PAIR)]
    gain_ref, w_ref, o_ref, xo_ref = refs[6 * PAIR:]
    i = pl.program_id(0)
    hs = []
    for t, (xc_ref, xl_ref, sh_ref, sc_ref, _, _) in enumerate(per_tile):
        is_ctx = (PAIR * i + t) % tiles_per_batch < ctx_tiles
        x = jnp.where(is_ctx, xc_ref[...], xl_ref[...])
        xo_ref[t * TM:(t + 1) * TM, :] = x
        hs.append((_rms(x, gain_ref[...]) * (1.0 + sc_ref[0]) + sh_ref[0]).astype(BF16))
    acc = jnp.dot(jnp.concatenate(hs, axis=0), w_ref[...], preferred_element_type=F32)
    lane = lax.broadcasted_iota(I32, (TM, LANES), 1)
    first_half = (lane % (HEAD_DIM // 2)) < (HEAD_DIM // 4)
    for t, (_, _, _, _, cos_ref, sin_ref) in enumerate(per_tile):
        rows = slice(t * TM, (t + 1) * TM)
        for s in range(0, acc.shape[1], LANES):
            blk = acc[rows, s:s + LANES]
            if rope_lo <= s < rope_hi:
                partner = jnp.where(first_half, pltpu.roll(blk, LANES - HEAD_DIM // 4, 1),
                                    pltpu.roll(blk, HEAD_DIM // 4, 1))
                blk = blk * cos_ref[...] + partner * sin_ref[...]
                if s < q_hi:
                    blk = blk * (HEAD_DIM ** -0.5)
            o_ref[rows, s:s + LANES] = blk.astype(o_ref.dtype)


def rope_tables(n_ctx, n_lat):
    half = HEAD_DIM // 2
    inv = ROPE_THETA ** (-jnp.arange(0, half, 2, dtype=F32) / half)
    t = jnp.arange(n_lat)
    pos = jnp.stack([(t // GRID_W).astype(F32), (t % GRID_W).astype(F32)], axis=1)
    lane = jnp.arange(LANES)
    part = (lane % HEAD_DIM) // half
    freq = lane % (half // 2)
    ang = pos[:, part] * inv[freq][None, :]
    sign = jnp.where((lane % half) < half // 2, -1.0, 1.0)
    cos = jnp.concatenate([jnp.ones((n_ctx, LANES), F32), jnp.cos(ang)], axis=0)
    sin = jnp.concatenate([jnp.zeros((n_ctx, LANES), F32), jnp.sin(ang) * sign], axis=0)
    return cos, sin


def pre_mm(ctx, x, gain, mod, w, cos, sin):
    bsz, n_ctx, d = ctx.shape
    n_lat = x.shape[1]
    n = w.shape[1]
    ctx_tiles, lat_tiles = n_ctx // TM, n_lat // TM
    tiles = ctx_tiles + lat_tiles
    r = bsz * tiles * TM
    tmd = PAIR * TM

    def src_spec(half, latent):
        def index(i):
            t = PAIR * i + half
            b, j = t // tiles, t % tiles
            if latent:
                return b * lat_tiles + jnp.clip(j - ctx_tiles, 0, lat_tiles - 1), 0
            return b * ctx_tiles + jnp.minimum(j, ctx_tiles - 1), 0
        return pl.BlockSpec((TM, d), index)

    def table_spec(half):
        return pl.BlockSpec((TM, LANES), lambda i: ((PAIR * i + half) % tiles, 0))

    in_specs, args = [], []
    for t in range(PAIR):
        in_specs += [src_spec(t, False), src_spec(t, True),
                     _mod_spec(MOD_SHIFT_M, tiles, ctx_tiles, d, half=t), _mod_spec(MOD_SCALE_M, tiles, ctx_tiles, d, half=t),
                     table_spec(t), table_spec(t)]
        args += [ctx.reshape(bsz * n_ctx, d), x.reshape(bsz * n_lat, d), mod, mod, cos, sin]
    body = functools.partial(_pre_mm_body, tiles_per_batch=tiles, ctx_tiles=ctx_tiles, rope_lo=POOL_DIM,
                             rope_hi=POOL_DIM + Q_DIM + KV_DIM, q_hi=POOL_DIM + Q_DIM)
    return pl.pallas_call(
        body,
        grid=(r // tmd,),
        in_specs=in_specs + [_const_spec((1, d)), _const_spec((d, n))],
        out_specs=[pl.BlockSpec((tmd, n), lambda i: (i, 0)), pl.BlockSpec((tmd, d), lambda i: (i, 0))],
        out_shape=[jax.ShapeDtypeStruct((r, n), BF16), jax.ShapeDtypeStruct((r, d), F32)],
        compiler_params=pltpu.CompilerParams(dimension_semantics=("arbitrary",), vmem_limit_bytes=VMEM_LIMIT),
        name="l0_norm_proj_rope",
    )(*args, gain.reshape(1, d), w)


def _post_mm_body(*refs, n_a):
    a_refs = refs[:n_a]
    w_refs = refs[n_a:2 * n_a]
    x_refs = refs[2 * n_a:2 * n_a + PAIR]
    gpost_ref, gpre_ref = refs[2 * n_a + PAIR:2 * n_a + PAIR + 2]
    mod_refs = refs[2 * n_a + PAIR + 2:2 * n_a + 4 * PAIR + 2]
    xo_ref, ho_ref = refs[2 * n_a + 4 * PAIR + 2:]
    ys = []
    for s in range(0, PAIR * TM, EPI_ROWS):
        rows = slice(s, s + EPI_ROWS)
        y = jnp.dot(a_refs[0][rows, :], w_refs[0][...], preferred_element_type=F32)
        for a_ref, w_ref in zip(a_refs[1:], w_refs[1:]):
            y = y + jnp.dot(a_ref[rows, :], w_ref[...], preferred_element_type=F32)
        ys.append(y)
    _residual_epilogue(ys, x_refs, gpost_ref, gpre_ref, mod_refs, xo_ref, ho_ref)


def _residual_epilogue(ys, x_refs, gpost_ref, gpre_ref, mod_refs, xo_ref, ho_ref):
    for k, y in enumerate(ys):
        s = k * EPI_ROWS
        t, r0 = s // TM, s % TM
        rows = slice(s, s + EPI_ROWS)
        gate_ref, sh_ref, sc_ref = mod_refs[3 * t:3 * t + 3]
        x = x_refs[t][r0:r0 + EPI_ROWS, :] if len(x_refs) == PAIR else x_refs[0][rows, :]
        xn = x + gate_ref[0] * _rms(y, gpost_ref[...])
        xo_ref[rows, :] = xn
        ho_ref[rows, :] = (_rms(xn, gpre_ref[...]) * (1.0 + sc_ref[0]) + sh_ref[0]).astype(ho_ref.dtype)


def _pair_mod_specs(kinds, tiles_per_batch, ctx_tiles, d):
    return [_mod_spec(k, tiles_per_batch, ctx_tiles, d, half=t) for t in range(PAIR) for k in kinds]


def post_mm(a_list, w_list, x, x_tile_offset, g_post, g_pre, mod, tiles_per_batch, ctx_tiles, x_tiles_per_batch,
            h_dtype, name):
    r = a_list[0].shape[0]
    d = w_list[0].shape[1]
    n_a = len(a_list)
    tmd = PAIR * TM

    def x_spec(half):
        def index(i):
            t = PAIR * i + half
            return (t // tiles_per_batch) * x_tiles_per_batch + x_tile_offset + t % tiles_per_batch, 0
        return pl.BlockSpec((TM, d), index)

    in_specs = ([pl.BlockSpec((tmd, a.shape[1]), lambda i: (i, 0)) for a in a_list]
                + [_const_spec(w.shape) for w in w_list]
                + [x_spec(t) for t in range(PAIR)]
                + [_const_spec((1, d)), _const_spec((1, d))]
                + _pair_mod_specs((MOD_GATE_M, MOD_SHIFT_F, MOD_SCALE_F), tiles_per_batch, ctx_tiles, d))
    return pl.pallas_call(
        functools.partial(_post_mm_body, n_a=n_a),
        grid=(r // tmd,),
        in_specs=in_specs,
        out_specs=[pl.BlockSpec((tmd, d), lambda i: (i, 0)), pl.BlockSpec((tmd, d), lambda i: (i, 0))],
        out_shape=[jax.ShapeDtypeStruct((r, d), F32), jax.ShapeDtypeStruct((r, d), h_dtype)],
        compiler_params=pltpu.CompilerParams(dimension_semantics=("arbitrary",), vmem_limit_bytes=VMEM_LIMIT),
        name=name,
    )(*a_list, *w_list, *([x] * PAIR), g_post.reshape(1, d), g_pre.reshape(1, d), *([mod] * (3 * PAIR)))


def _out_ffn_body(*refs):
    pooled_ref, attn_ref, wa_ref, wb_ref = refs[:4]
    x_refs = refs[4:4 + PAIR]
    g_mix_post, g_ffn_pre, g_ffn_post, g_next_pre = refs[4 + PAIR:8 + PAIR]
    mix_mods = refs[8 + PAIR:8 + 4 * PAIR]
    ffn_mods = refs[8 + 4 * PAIR:8 + 7 * PAIR]
    w1_ref, w3_ref, w2_ref = refs[8 + 7 * PAIR:11 + 7 * PAIR]
    xo_ref, ho_ref, x1_s, h1_s, acc_ref = refs[11 + 7 * PAIR:]
    blocks = range(0, PAIR * TM, EPI_ROWS)
    ys = [jnp.dot(pooled_ref[s:s + EPI_ROWS, :], wa_ref[...], preferred_element_type=F32)
          + jnp.dot(attn_ref[s:s + EPI_ROWS, :], wb_ref[...], preferred_element_type=F32) for s in blocks]
    _residual_epilogue(ys, x_refs, g_mix_post, g_ffn_pre, mix_mods, x1_s, h1_s)
    h = h1_s[...]
    n_f = w1_ref.shape[1]
    for f0 in range(0, n_f, FFN_TF):
        a = jnp.dot(h, w1_ref[:, f0:f0 + FFN_TF], preferred_element_type=F32)
        b = jnp.dot(h, w3_ref[:, f0:f0 + FFN_TF], preferred_element_type=F32)
        mid = (a * jax.nn.sigmoid(a) * b).astype(BF16)
        part = jnp.dot(mid, w2_ref[f0:f0 + FFN_TF, :], preferred_element_type=F32)
        if f0 == 0:
            acc_ref[...] = part
        else:
            acc_ref[...] += part
    _residual_epilogue([acc_ref[s:s + EPI_ROWS, :] for s in blocks], (x1_s,), g_ffn_post, g_next_pre, ffn_mods,
                       xo_ref, ho_ref)


def out_ffn_layer(pooled, attn, w_out, x, g_mix_post, g_ffn_pre, g_ffn_post, g_next_pre, mod, mod_next,
                  w1, w3, w2, tiles_per_batch, ctx_tiles):
    r, d = x.shape
    tmd = PAIR * TM
    n_pool = pooled.shape[1]
    gate_specs = _pair_mod_specs((MOD_GATE_F,), tiles_per_batch, ctx_tiles, d)
    next_specs = _pair_mod_specs((MOD_SHIFT_M, MOD_SCALE_M), tiles_per_batch, ctx_tiles, d)
    ffn_specs, ffn_args = [], []
    for t in range(PAIR):
        ffn_specs += [gate_specs[t], next_specs[2 * t], next_specs[2 * t + 1]]
        ffn_args += [mod, mod_next, mod_next]
    vec = lambda g: g.reshape(1, d)
    return pl.pallas_call(
        _out_ffn_body,
        grid=(r // tmd,),
        in_specs=([pl.BlockSpec((tmd, n_pool), lambda i: (i, 0)), pl.BlockSpec((tmd, attn.shape[1]), lambda i: (i, 0)),
                   _const_spec((n_pool, d)), _const_spec((w_out.shape[0] - n_pool, d))]
                  + [pl.BlockSpec((TM, d), lambda i, t=t: (PAIR * i + t, 0)) for t in range(PAIR)]
                  + [_const_spec((1, d))] * 4
                  + _pair_mod_specs((MOD_GATE_M, MOD_SHIFT_F, MOD_SCALE_F), tiles_per_batch, ctx_tiles, d)
                  + ffn_specs
                  + [_const_spec(w1.shape), _const_spec(w3.shape), _const_spec(w2.shape)]),
        out_specs=[pl.BlockSpec((tmd, d), lambda i: (i, 0)), pl.BlockSpec((tmd, d), lambda i: (i, 0))],
        out_shape=[jax.ShapeDtypeStruct((r, d), F32), jax.ShapeDtypeStruct((r, d), BF16)],
        scratch_shapes=[pltpu.VMEM((tmd, d), F32), pltpu.VMEM((tmd, d), BF16), pltpu.VMEM((tmd, d), F32)],
        compiler_params=pltpu.CompilerParams(dimension_semantics=("arbitrary",), vmem_limit_bytes=VMEM_LIMIT),
        name="l0_out_ffn",
    )(pooled, attn, w_out[:n_pool], w_out[n_pool:], *([x] * PAIR), vec(g_mix_post), vec(g_ffn_pre), vec(g_ffn_post),
      vec(g_next_pre), *([mod] * (3 * PAIR)), *ffn_args, w1, w3, w2)


def _proj_body(h_ref, w_ref, wg_ref, o_ref, og_ref):
    h = h_ref[...]
    o_ref[...] = jnp.dot(h, w_ref[...], preferred_element_type=F32).astype(o_ref.dtype)
    og_ref[...] = jnp.dot(h, wg_ref[...], preferred_element_type=F32)


def gdn_proj(h, w, w_gate):
    r, d = h.shape
    n = w.shape[1]
    tmd = PAIR * TM
    return pl.pallas_call(
        _proj_body,
        grid=(r // tmd,),
        in_specs=[pl.BlockSpec((tmd, d), lambda i: (i, 0)), _const_spec(w.shape), _const_spec(w_gate.shape)],
        out_specs=[pl.BlockSpec((tmd, n), lambda i: (i, 0)), pl.BlockSpec((tmd, LANES), lambda i: (i, 0))],
        out_shape=[jax.ShapeDtypeStruct((r, n), BF16), jax.ShapeDtypeStruct((r, LANES), F32)],
        compiler_params=pltpu.CompilerParams(dimension_semantics=("arbitrary",), vmem_limit_bytes=VMEM_LIMIT),
        name="l1_proj",
    )(h, w, w_gate)


ATTN_HEADS_PER_UNIT = 1


def _attn_body(sink_ref, q_ref, kc_ref, vc_ref, k0_ref, k1_ref, k2_ref, v0_ref, v1_ref, v2_ref, o_ref,
               *, n_ctx_blocks, n_blocks):
    n = pl.program_id(1)
    rows = ATTN_HEADS_PER_UNIT * BLOCK
    qi = lax.broadcasted_iota(I32, (rows, 3 * BLOCK), 0) % BLOCK
    kj = lax.broadcasted_iota(I32, (rows, 3 * BLOCK), 1)
    first = jnp.where(n > n_ctx_blocks, 0, BLOCK)
    last = jnp.where(n < n_blocks - 1, 3 * BLOCK, 2 * BLOCK)
    last = jnp.where(n >= n_ctx_blocks, last, 0)
    band_ok = (kj >= qi) & (kj <= qi + 2 * WINDOW) & (kj >= first) & (kj < last)
    q = q_ref[0]
    nt = (((1,), (1,)), ((), ()))
    groups = range(N_KV_HEADS)
    hpu = ATTN_HEADS_PER_UNIT
    units = [(g, [g * GQA_GROUP + s + i for i in range(hpu)]) for g in groups for s in range(0, GQA_GROUP, hpu)]
    idx = range(len(units))
    cols = [slice(g * HEAD_DIM, (g + 1) * HEAD_DIM) for g in groups]
    q_u = [jnp.concatenate([q[:, hd * HEAD_DIM:(hd + 1) * HEAD_DIM] for hd in hs], axis=0) for _, hs in units]
    k_c = [kc_ref[0][:, c] for c in cols]
    k_b = [jnp.concatenate([k0_ref[0][:, c], k1_ref[0][:, c], k2_ref[0][:, c]], axis=0) for c in cols]
    ones_b = jnp.ones((3 * BLOCK, HEAD_DIM), BF16)
    ones_c = jnp.ones((kc_ref.shape[1], HEAD_DIM), BF16)
    v_b = [jnp.concatenate([jnp.concatenate([v0_ref[0][:, c], v1_ref[0][:, c], v2_ref[0][:, c]], axis=0), ones_b],
                           axis=1) for c in cols]
    v_c = [jnp.concatenate([vc_ref[0][:, c], ones_c], axis=1) for c in cols]
    s_c = [lax.dot_general(q_u[u], k_c[units[u][0]], nt, preferred_element_type=F32) for u in idx]
    s_b = [lax.dot_general(q_u[u], k_b[units[u][0]], nt, preferred_element_type=F32) for u in idx]
    s_b = [jnp.where(band_ok, s, NEG_INF) for s in s_b]
    sink = [jnp.concatenate([sink_ref[hd] + jnp.zeros((BLOCK, 1), F32) for hd in hs], axis=0) for _, hs in units]
    m = [jnp.maximum(jnp.maximum(jnp.max(s_c[u], axis=-1, keepdims=True), jnp.max(s_b[u], axis=-1, keepdims=True)),
                     sink[u]) for u in idx]
    p_c = [jnp.exp(s_c[u] - m[u]) for u in idx]
    p_b = [jnp.exp(s_b[u] - m[u]) for u in idx]
    o_c = [jnp.dot(p_c[u].astype(BF16), v_c[units[u][0]], preferred_element_type=F32) for u in idx]
    o_b = [jnp.dot(p_b[u].astype(BF16), v_b[units[u][0]], preferred_element_type=F32) for u in idx]
    outs = []
    for u in idx:
        acc = o_c[u] + o_b[u]
        den = acc[:, HEAD_DIM:HEAD_DIM + 1] + jnp.exp(sink[u] - m[u])
        o = acc[:, :HEAD_DIM] / den
        outs += [o[i * BLOCK:(i + 1) * BLOCK] for i in range(hpu)]
    o_ref[0] = jnp.concatenate(outs, axis=1).astype(o_ref.dtype)


def attention(p0, sinks, n_ctx):
    bsz, ltot, _ = p0.shape
    n_blocks = ltot // BLOCK
    n_ctx_blocks = n_ctx // BLOCK
    k_blk = (POOL_DIM + Q_DIM) // KV_DIM
    v_blk = k_blk + 1

    def band(dn, blk):
        return pl.BlockSpec((1, BLOCK, KV_DIM),
                            lambda b, n: (b, jnp.clip(n + dn, n_ctx_blocks, n_blocks - 1), blk))

    return pl.pallas_call(
        functools.partial(_attn_body, n_ctx_blocks=n_ctx_blocks, n_blocks=n_blocks),
        grid=(bsz, n_blocks),
        in_specs=[pl.BlockSpec(memory_space=pltpu.SMEM),
                  pl.BlockSpec((1, BLOCK, Q_DIM), lambda b, n: (b, n, POOL_DIM // Q_DIM)),
                  pl.BlockSpec((1, n_ctx, KV_DIM), lambda b, n: (b, 0, k_blk)),
                  pl.BlockSpec((1, n_ctx, KV_DIM), lambda b, n: (b, 0, v_blk)),
                  band(-1, k_blk), band(0, k_blk), band(1, k_blk),
                  band(-1, v_blk), band(0, v_blk), band(1, v_blk)],
        out_specs=pl.BlockSpec((1, BLOCK, Q_DIM), lambda b, n: (b, n, 0)),
        out_shape=jax.ShapeDtypeStruct((bsz, ltot, Q_DIM), BF16),
        compiler_params=pltpu.CompilerParams(dimension_semantics=("arbitrary", "arbitrary"),
                                             vmem_limit_bytes=VMEM_LIMIT),
        name="l0_attention",
    )(sinks.astype(F32), p0, p0, p0, p0, p0, p0, p0, p0, p0)


POOL_HALO = 16


def _pool_body(up_ref, uc_ref, un_ref, w_ref, scale_ref, o_ref, *, tiles_per_batch, ctx_tiles):
    j = pl.program_id(1)
    seg_lo = jnp.where(j < ctx_tiles, 0, ctx_tiles * TM)
    seg_hi = jnp.where(j < ctx_tiles, ctx_tiles * TM, tiles_per_batch * TM)
    halo = jnp.concatenate([up_ref[0][TM - POOL_HALO:], uc_ref[0], un_ref[0][:POOL_HALO]], axis=0)
    n_h = TM + 2 * POOL_HALO
    t = j * TM + lax.broadcasted_iota(I32, (TM, n_h), 0)
    pos = j * TM - POOL_HALO + lax.broadcasted_iota(I32, (TM, n_h), 1)
    t_col = j * TM + lax.broadcasted_iota(I32, (TM, 1), 0)
    cur = uc_ref[0]
    groups = range(len(POOL_WINDOWS))
    cols = [slice(g * POOL_GROUP_DIM, (g + 1) * POOL_GROUP_DIM) for g in groups]
    windows, cnts = [], []
    for w in POOL_WINDOWS:
        lo = jnp.maximum(t - w // 2, seg_lo)
        hi = jnp.minimum(t + w // 2, seg_hi)
        windows.append(((pos >= lo) & (pos < hi)).astype(BF16))
        cnts.append((jnp.minimum(t_col + w // 2, seg_hi) - jnp.maximum(t_col - w // 2, seg_lo)).astype(F32))
    sums = [jnp.dot(windows[g], halo[:, cols[g]], preferred_element_type=F32) for g in groups]
    deltas = [(sums[g] / cnts[g] - cur[:, cols[g]].astype(F32)).astype(BF16) for g in groups]
    ys = [jnp.dot(deltas[g], w_ref[g], preferred_element_type=F32) for g in groups]
    for g in groups:
        o_ref[0, :, cols[g]] = (ys[g] * scale_ref[:, cols[g]]).astype(o_ref.dtype)


def pool_mixer(p0, pool_w, pool_scale, n_ctx):
    bsz, ltot, _ = p0.shape
    tiles_per_batch = ltot // TM
    ctx_tiles = n_ctx // TM

    def tile(dj):
        return pl.BlockSpec((1, TM, POOL_DIM), lambda b, j: (b, jnp.clip(j + dj, 0, tiles_per_batch - 1), 0))

    return pl.pallas_call(
        functools.partial(_pool_body, tiles_per_batch=tiles_per_batch, ctx_tiles=ctx_tiles),
        grid=(bsz, tiles_per_batch),
        in_specs=[tile(-1), tile(0), tile(1),
                  pl.BlockSpec(pool_w.shape, lambda b, j: (0, 0, 0)),
                  pl.BlockSpec((1, POOL_DIM), lambda b, j: (0, 0))],
        out_specs=pl.BlockSpec((1, TM, POOL_DIM), lambda b, j: (b, j, 0)),
        out_shape=jax.ShapeDtypeStruct((bsz, ltot, POOL_DIM), BF16),
        compiler_params=pltpu.CompilerParams(dimension_semantics=("arbitrary", "arbitrary")),
        name="l0_pool",
    )(p0, p0, p0, pool_w.astype(BF16), pool_scale.reshape(1, POOL_DIM).astype(F32))


def kernel(x, c, ctx, c_ctx, l0_mod_w, l0_mod_b, l0_mix_pre, l0_mix_post, l0_ffn_pre, l0_ffn_post, l0_w_in, l0_pool_w, l0_pool_scale, l0_sinks, l0_w_out, l0_ffn_w1, l0_ffn_w3, l0_ffn_w2, l1_mod_w, l1_mod_b, l1_mix_pre, l1_mix_post, l1_ffn_pre, l1_ffn_post, l1_w_in, l1_conv_w, l1_a_log, l1_dt_bias, l1_out_norm, l1_w_out, l1_router, l1_moe_w1, l1_moe_w3, l1_moe_w2):
    bsz, n_lat, d = x.shape
    n_ctx = ctx.shape[1]
    ltot = n_ctx + n_lat
    assert n_ctx % TM == 0 and n_lat % TM == 0 and n_ctx % GC == 0
    assert (bsz * ltot) % (PAIR * TM) == 0 and (bsz * n_lat) % (PAIR * TM) == 0
    tiles = ltot // TM
    ctx_tiles = n_ctx // TM
    lat_tiles = n_lat // TM
    bf = lambda w: w.astype(BF16)

    mod0 = modulation(c, c_ctx, l0_mod_w, l0_mod_b)
    mod1 = modulation(c, c_ctx, l1_mod_w, l1_mod_b)
    cos, sin = rope_tables(n_ctx, n_lat)

    p0, x_all = pre_mm(ctx, x, l0_mix_pre, mod0, bf(l0_w_in), cos, sin)
    p0 = p0.reshape(bsz, ltot, -1)
    attn = attention(p0, l0_sinks, n_ctx).reshape(bsz * ltot, Q_DIM)
    pooled = pool_mixer(p0, l0_pool_w, l0_pool_scale, n_ctx).reshape(bsz * ltot, POOL_DIM)
    x2, h2 = out_ffn_layer(pooled, attn, bf(l0_w_out), x_all, l0_mix_post, l0_ffn_pre, l0_ffn_post, l1_mix_pre,
                           mod0, mod1, bf(l0_ffn_w1), bf(l0_ffn_w3), bf(l0_ffn_w2), tiles, ctx_tiles)

    n_heads = l1_a_log.shape[1]
    qkvz = 4 * n_heads * GD
    w_in1 = bf(l1_w_in)
    w_gate = jnp.zeros((d, LANES), BF16).at[:, :4 * n_heads].set(w_in1[:, qkvz:])
    p1, ab = gdn_proj(h2, w_in1[:, :qkvz], w_gate)
    ab_rows = gdn_gate_rows(ab.reshape(bsz, ltot, LANES)[..., :4 * n_heads], n_heads)
    y = gdn_core(p1.reshape(bsz, ltot, qkvz), ab_rows, l1_conv_w, l1_a_log, l1_dt_bias, l1_out_norm, n_ctx)
    x3, h3 = post_mm([y.reshape(bsz * n_lat, n_heads * GD)], [bf(l1_w_out)], x2, ctx_tiles, l1_mix_post, l1_ffn_pre,
                     mod1, lat_tiles, 0, tiles, F32, "l1_out_proj")
    out = moe_layer(h3, l1_router, l1_moe_w1, l1_moe_w3, l1_moe_w2, x3, l1_ffn_post, mod1, n_lat)
    return out.reshape(bsz, n_lat, d)
```

```python
import functools
import math

import jax
import jax.numpy as jnp
from jax import lax
from jax.experimental import pallas as pl
from jax.experimental.pallas import tpu as pltpu

F32 = jnp.float32
BF16 = jnp.bfloat16
I32 = jnp.int32

LANES = 128
D_MODEL = 1024
GRID_W = 64
EPS = 1e-6
NEG_INF = -1e30

POOL_GROUPS = 4
POOL_GROUP_DIM = 128
POOL_DIM = POOL_GROUPS * POOL_GROUP_DIM
POOL_WINDOWS = (2, 4, 8, 16)
HEAD_DIM = 64
N_HEADS = 8
N_KV_HEADS = 2
GQA_GROUP = N_HEADS // N_KV_HEADS
Q_DIM = N_HEADS * HEAD_DIM
KV_DIM = N_KV_HEADS * HEAD_DIM
WINDOW = 128
BLOCK = 128
ROPE_THETA = 10000.0

GDN_HEADS = 8
GDN_HEAD_DIM = 128
GDN_DIM = GDN_HEADS * GDN_HEAD_DIM
CONV_K = 5
CHUNK = 64

N_EXPERTS = 8
TOP_K = 2

MOE_TM = 512
MOE_TF = 1792
ROW_TILE = 512
ISSUE_UNROLL = 8


def _split_bf16(a):
    hi = a.astype(BF16)
    lo = (a - hi.astype(F32)).astype(BF16)
    return hi, lo


def _route_rows(h, rhi_ref, rlo_ref, route_ref, cnt_ref, carry_ref):
    i = pl.program_id(0)
    tr = h.shape[0]

    @pl.when(i == 0)
    def _():
        carry_ref[...] = jnp.zeros_like(carry_ref)

    h_hi, h_lo = _split_bf16(h)
    r_hi = rhi_ref[...]
    r_lo = rlo_ref[...]
    logits = (jnp.dot(h_hi, r_hi, preferred_element_type=F32)
              + jnp.dot(h_hi, r_lo, preferred_element_type=F32)
              + jnp.dot(h_lo, r_hi, preferred_element_type=F32))
    lane = lax.broadcasted_iota(I32, (tr, LANES), 1)
    logits = jnp.where(lane < N_EXPERTS, logits, -jnp.inf)
    m1 = jnp.max(logits, axis=-1, keepdims=True)
    i1 = jnp.min(jnp.where(logits == m1, lane, LANES), axis=-1, keepdims=True)
    rest = jnp.where(lane == i1, -jnp.inf, logits)
    m2 = jnp.max(rest, axis=-1, keepdims=True)
    i2 = jnp.min(jnp.where(rest == m2, lane, LANES), axis=-1, keepdims=True)
    e2 = jnp.exp(m2 - m1)
    g1 = 1.0 / (1.0 + e2)
    g2 = e2 / (1.0 + e2)

    onehot = ((lane == i1) | (lane == i2)).astype(F32)
    row = lax.broadcasted_iota(I32, (tr, tr), 0)
    col = lax.broadcasted_iota(I32, (tr, tr), 1)
    strict = (row > col).astype(BF16)
    before = jnp.dot(strict, onehot.astype(BF16), preferred_element_type=F32) + carry_ref[...]
    rank1 = jnp.sum(jnp.where(lane == i1, before, 0.0), axis=-1, keepdims=True)
    rank2 = jnp.sum(jnp.where(lane == i2, before, 0.0), axis=-1, keepdims=True)
    carry_ref[...] += jnp.sum(onehot, axis=0, keepdims=True)

    packed = jnp.where(lane == 0, i1.astype(F32), 0.0)
    packed = jnp.where(lane == 1, i2.astype(F32), packed)
    packed = jnp.where(lane == 2, g1, packed)
    packed = jnp.where(lane == 3, g2, packed)
    packed = jnp.where(lane == 4, rank1, packed)
    packed = jnp.where(lane == 5, rank2, packed)
    route_ref[...] = packed
    cnt_ref[...] = jnp.broadcast_to(carry_ref[...], cnt_ref.shape)


def router_operands(router, d):
    r_pad = jnp.zeros((d, LANES), F32).at[:, :N_EXPERTS].set(router.astype(F32))
    return _split_bf16(r_pad)


def _row_copy(src_ref, src_row, dst_ref, dst_row, sem):
    return pltpu.make_async_copy(src_ref.at[pl.ds(src_row, 1)], dst_ref.at[pl.ds(dst_row, 1)], sem)


def _scatter_body(pos_ref, h_ref, xs_in_ref, xs_ref, sem, *, ts):
    del xs_in_ref

    def issue(r, c):
        for k in range(TOP_K):
            _row_copy(h_ref, r, xs_ref, pos_ref[0, 0, TOP_K * r + k], sem).start()
        return c

    lax.fori_loop(0, ts, issue, 0, unroll=ISSUE_UNROLL)
    for _ in range(TOP_K):
        pltpu.make_async_copy(h_ref, xs_ref.at[pl.ds(0, ts)], sem).wait()


def moe_scatter(h, pos, p_rows):
    t, d = h.shape
    ts = min(ROW_TILE, t)
    nt = t // ts
    xs0 = jnp.zeros((p_rows, d), h.dtype)
    return pl.pallas_call(
        functools.partial(_scatter_body, ts=ts),
        grid=(nt,),
        in_specs=[
            pl.BlockSpec((1, 1, TOP_K * ts), lambda i: (i, 0, 0), memory_space=pltpu.SMEM),
            pl.BlockSpec((ts, d), lambda i: (i, 0)),
            pl.BlockSpec(memory_space=pl.ANY),
        ],
        out_specs=pl.BlockSpec(memory_space=pl.ANY),
        out_shape=jax.ShapeDtypeStruct((p_rows, d), h.dtype),
        scratch_shapes=[pltpu.SemaphoreType.DMA(())],
        input_output_aliases={2: 0},
        compiler_params=pltpu.CompilerParams(dimension_semantics=("arbitrary",), disable_bounds_checks=True),
        name="moe_scatter",
    )(pos.reshape(nt, 1, TOP_K * ts), h, xs0)


def _expert_body(te_ref, nu_ref, x_ref, w1_ref, w3_ref, w2_ref, o_ref):
    i = pl.program_id(0)
    j = pl.program_id(1)

    @pl.when(j == 0)
    def _():
        o_ref[...] = jnp.zeros_like(o_ref)

    @pl.when(i < nu_ref[0])
    def _():
        x = x_ref[...].astype(BF16)
        a = jnp.dot(x, w1_ref[0], preferred_element_type=F32)
        b = jnp.dot(x, w3_ref[0], preferred_element_type=F32)
        mid = (a * jax.nn.sigmoid(a) * b).astype(BF16)
        o_ref[...] += jnp.dot(mid, w2_ref[0], preferred_element_type=F32)


def moe_experts(xs, w1, w3, w2, tile_expert, n_used, tm):
    p_rows, d = xs.shape
    n_exp, _, d_exp = w1.shape
    tf = MOE_TF if d_exp % MOE_TF == 0 else d_exp
    nj = d_exp // tf
    n_tiles = p_rows // tm

    def jj(i, j, nu):
        return jnp.where(i < nu[0], j, nj - 1)

    grid_spec = pltpu.PrefetchScalarGridSpec(
        num_scalar_prefetch=2,
        grid=(n_tiles, nj),
        in_specs=[
            pl.BlockSpec((tm, d), lambda i, j, te, nu: (i, 0)),
            pl.BlockSpec((1, d, tf), lambda i, j, te, nu: (te[i], 0, jj(i, j, nu))),
            pl.BlockSpec((1, d, tf), lambda i, j, te, nu: (te[i], 0, jj(i, j, nu))),
            pl.BlockSpec((1, tf, d), lambda i, j, te, nu: (te[i], jj(i, j, nu), 0)),
        ],
        out_specs=pl.BlockSpec((tm, d), lambda i, j, te, nu: (i, 0)),
    )
    return pl.pallas_call(
        _expert_body,
        grid_spec=grid_spec,
        out_shape=jax.ShapeDtypeStruct((p_rows, d), F32),
        compiler_params=pltpu.CompilerParams(
            dimension_semantics=("arbitrary", "arbitrary"),
            vmem_limit_bytes=56 * 1024 * 1024),
        name="moe_experts",
    )(tile_expert, n_used, xs, w1, w3, w2)


def _combine_body(pos_ref, pos_next_ref, gate_ref, ys_ref, x_ref, gain_ref, mgate_ref, y_ref, buf, sem, *, ts):
    i = pl.program_id(0)
    slot = i % 2

    def gather(p_ref, s):
        def issue(r, c):
            for k in range(TOP_K):
                _row_copy(ys_ref, p_ref[0, 0, TOP_K * r + k], buf.at[s, k], r, sem.at[s]).start()
            return c
        lax.fori_loop(0, ts, issue, 0, unroll=ISSUE_UNROLL)

    @pl.when(i == 0)
    def _():
        gather(pos_ref, slot)

    @pl.when(i + 1 < pl.num_programs(0))
    def _():
        gather(pos_next_ref, 1 - slot)

    for k in range(TOP_K):
        pltpu.make_async_copy(ys_ref.at[pl.ds(0, ts)], buf.at[slot, k], sem.at[slot]).wait()
    g = gate_ref[...]
    y = g[:, 2:3] * buf[slot, 0] + g[:, 3:4] * buf[slot, 1]
    y_ref[...] = x_ref[...] + mgate_ref[0] * _rms(y, gain_ref[...])


def moe_combine(ys, pos, route, x, gain, mod, rows_per_batch):
    t = route.shape[0]
    d = ys.shape[1]
    ts = min(ROW_TILE, rows_per_batch)
    assert rows_per_batch % ts == 0
    tiles_per_batch = rows_per_batch // ts
    nt = t // ts
    pos3 = pos.reshape(nt, 1, TOP_K * ts)
    return pl.pallas_call(
        functools.partial(_combine_body, ts=ts),
        grid=(nt,),
        in_specs=[
            pl.BlockSpec((1, 1, TOP_K * ts), lambda i: (i, 0, 0), memory_space=pltpu.SMEM),
            pl.BlockSpec((1, 1, TOP_K * ts), lambda i: (jnp.minimum(i + 1, nt - 1), 0, 0), memory_space=pltpu.SMEM),
            pl.BlockSpec((ts, LANES), lambda i: (i, 0)),
            pl.BlockSpec(memory_space=pl.ANY),
            pl.BlockSpec((ts, d), lambda i: (i, 0)),
            pl.BlockSpec((1, d), lambda i: (0, 0)),
            _mod_spec(MOD_GATE_F, tiles_per_batch, 0, d),
        ],
        out_specs=pl.BlockSpec((ts, d), lambda i: (i, 0)),
        out_shape=jax.ShapeDtypeStruct((t, d), F32),
        scratch_shapes=[pltpu.VMEM((2, TOP_K, ts, d), F32), pltpu.SemaphoreType.DMA((2,))],
        compiler_params=pltpu.CompilerParams(dimension_semantics=("arbitrary",), disable_bounds_checks=True),
        name="moe_combine",
    )(pos3, pos3, route, ys, x, gain.reshape(1, d), mod)


def moe_layer(tok, route, cnt, w1, w3, w2, x, gain, mod, rows_per_batch):
    t = tok.shape[0]
    tm = min(MOE_TM, t)

    counts = cnt[0, :N_EXPERTS].astype(I32)
    padded = ((counts + tm - 1) // tm) * tm
    ends = jnp.cumsum(padded)
    starts = ends - padded
    expert = route[:, 0:TOP_K].astype(I32)
    rank = route[:, 4:4 + TOP_K].astype(I32)
    start_of = jnp.sum(jnp.where(expert[..., None] == jnp.arange(N_EXPERTS), starts, 0), axis=-1)
    pos = start_of + rank

    n_tiles = (TOP_K * t) // tm + N_EXPERTS
    n_used = (ends[-1] // tm).astype(I32)
    tile_idx = jnp.minimum(jnp.arange(n_tiles, dtype=I32), n_used - 1)
    tile_expert = jnp.sum(tile_idx[:, None] * tm >= ends[None, :], axis=-1).astype(I32)
    tile_expert = jnp.minimum(tile_expert, N_EXPERTS - 1)

    xs = moe_scatter(tok, pos, n_tiles * tm)
    ys = moe_experts(xs, w1.astype(BF16), w3.astype(BF16), w2.astype(BF16), tile_expert, n_used.reshape(1), tm)
    return moe_combine(ys, pos, route, x, gain, mod, rows_per_batch)


GC = 128
GD = GDN_HEAD_DIM
PAD = 8
GDN_UNROLL = 9
GDN_BASE = 16
N_MERGE = 3
M_INCL_F, M_STRICT_F, M_INCL_B, M_STRICT_B, M_EYE, M_BASE, M_OFF0 = 0, 1, 2, 3, 4, 5, 6
N_MASKS = M_OFF0 + N_MERGE


def _bdot(a, b):
    return jnp.dot(a.astype(BF16), b.astype(BF16), preferred_element_type=F32)


def _bdot_nt(a, b):
    return lax.dot_general(a.astype(BF16), b.astype(BF16), (((1,), (1,)), ((), ())), preferred_element_type=F32)


def _gdn_body(alog_ref, dtb_ref, q_ref, k_ref, v_ref, z_ref, ab_ref, cq_ref, ck_ref, cv_ref, onorm_ref,
              o_ref, pq_s, pk_s, pv_s, o_s, b_s, gq_s, gate_s, state_s, mask_s,
              *, n_ctx_chunks, n_chunks, n_heads, unroll):
    h = pl.program_id(1)
    ltot = n_chunks * GC
    lc = n_ctx_chunks * GC

    row = lax.broadcasted_iota(I32, (GC, GC), 0)
    col = lax.broadcasted_iota(I32, (GC, GC), 1)
    mask_s[M_INCL_F] = (row >= col).astype(F32)
    mask_s[M_STRICT_F] = (row > col).astype(F32)
    mask_s[M_INCL_B] = (row <= col).astype(F32)
    mask_s[M_STRICT_B] = (row < col).astype(F32)
    mask_s[M_EYE] = (row == col).astype(F32)
    mask_s[M_BASE] = ((row // GDN_BASE) == (col // GDN_BASE)).astype(F32)
    for lvl in range(N_MERGE):
        s = GDN_BASE << lvl
        mask_s[M_OFF0 + lvl] = (((row // (2 * s)) == (col // (2 * s))) & ((row // s) != (col // s))).astype(F32)

    for src_ref, dst in ((q_ref, pq_s), (k_ref, pk_s), (v_ref, pv_s)):
        for off in (0, PAD + lc, 2 * PAD + ltot):
            dst[pl.ds(off, PAD), :] = jnp.zeros((PAD, GD), F32)
        dst[pl.ds(PAD, lc), :] = src_ref[0, pl.ds(0, lc), :].astype(F32)
        dst[pl.ds(2 * PAD + lc, ltot - lc), :] = src_ref[0, pl.ds(lc, ltot - lc), :].astype(F32)

    for d in range(2):
        a = ab_ref[0, d * 2 * n_heads + h]
        b = ab_ref[0, d * 2 * n_heads + n_heads + h]
        xa = a + dtb_ref[d, h]
        softplus = jnp.maximum(xa, 0.0) + jnp.log1p(jnp.exp(-jnp.abs(xa)))
        g = -jnp.exp(alog_ref[d, h]) * softplus
        beta = 1.0 / (1.0 + jnp.exp(-b))
        tri = mask_s[M_INCL_B if d == 0 else M_INCL_F].astype(BF16)
        g_hi = g.astype(BF16)
        g_lo = (g - g_hi.astype(F32)).astype(BF16)
        gam = (jnp.dot(g_hi, tri, preferred_element_type=F32) + jnp.dot(g_lo, tri, preferred_element_type=F32))
        gate_s[d, 0] = gam
        gate_s[d, 1] = beta
        gate_s[d, 2] = jnp.broadcast_to(jnp.sum(g, axis=-1, keepdims=True), g.shape)

    def conv_chunk(c, pad_ref, w_ref, normalise, scale):
        start = pl.multiple_of(c * GC + jnp.where(c >= n_ctx_chunks, PAD, 0), PAD)
        acc = jnp.zeros((GC, GD), F32)
        for j in range(CONV_K):
            tap = pad_ref[pl.ds(start + (PAD - CONV_K // 2 + j), GC), :]
            acc = acc + tap * w_ref[j:j + 1, :].astype(F32)
        y = acc * jax.nn.sigmoid(acc)
        if normalise:
            y = y * (lax.rsqrt(jnp.sum(y * y, axis=-1, keepdims=True) + EPS) * scale)
        return y

    def group_chunks(i):
        return [jnp.minimum(i * unroll + u, n_chunks - 1) for u in range(unroll)]

    def prep(i, carry):
        chunks = group_chunks(i)
        qs = [conv_chunk(c, pq_s, cq_ref, True, GD ** -0.5) for c in chunks]
        ks = [conv_chunk(c, pk_s, ck_ref, True, 1.0) for c in chunks]
        vs = [conv_chunk(c, pv_s, cv_ref, False, 1.0) for c in chunks]
        kqs = [_bdot_nt(jnp.concatenate([k, q], axis=0), k) for k, q in zip(ks, qs)]
        kts = [k.T for k in ks]
        lanes = [(u, d) for u in range(unroll) for d in range(2)]
        lms, dbs, e_cols, kdts = [], [], [], []
        for u, d in lanes:
            c = chunks[u]
            gam_row = gate_s[d, 0, pl.ds(c, 1), :]
            beta_row = gate_s[d, 1, pl.ds(c, 1), :]
            tot_row = gate_s[d, 2, pl.ds(c, 1), :]
            gam_col = jnp.sum(mask_s[M_EYE] * gam_row, axis=1, keepdims=True)
            db = jnp.exp((gam_col - gam_row) * mask_s[M_INCL_F if d == 0 else M_INCL_B]) * beta_row
            dbs.append(db)
            lms.append(kqs[u][:GC] * db * mask_s[M_STRICT_F if d == 0 else M_STRICT_B])
            e_cols.append(jnp.exp(gam_col))
            kdts.append(kts[u] * (jnp.exp(tot_row - gam_row) * beta_row))
        xs = [-lm * mask_s[M_BASE] for lm in lms]
        ts = [mask_s[M_EYE] + x for x in xs]
        for _ in range(3):
            xs = [_bdot(x, x) for x in xs]
            ts = [t + _bdot(t, x) for t, x in zip(ts, xs)]
        for lvl in range(N_MERGE):
            ys = [_bdot(lm * mask_s[M_OFF0 + lvl], t) for lm, t in zip(lms, ts)]
            ts = [t - _bdot(t, y) for t, y in zip(ts, ys)]
        wus = [_bdot(t, jnp.concatenate([ks[u] * e, vs[u]], axis=1))
               for t, e, (u, d) in zip(ts, e_cols, lanes)]
        x12s = []
        for wu, db, kdt, (u, d) in zip(wus, dbs, kdts, lanes):
            qkb = kqs[u][GC:] * db * mask_s[M_INCL_F if d == 0 else M_INCL_B]
            x12s.append(_bdot(jnp.concatenate([kdt, qkb], axis=0), wu))
        for x12, e, (u, d) in zip(x12s, e_cols, lanes):
            t0 = pl.multiple_of(chunks[u] * GC, GC)
            b_s[d, pl.ds(t0, GC), :] = x12[:GD, GD:]
            o_s[d, pl.ds(t0, GC), :] = x12[GD:, GD:]
            q_eff = qs[u] * e - x12[GD:, :GD]
            gq_s[d, pl.ds(pl.multiple_of(2 * t0, 2 * GC), 2 * GC), :] = (
                jnp.concatenate([x12[:GD, :GD], q_eff], axis=0).astype(BF16))
        return carry

    lax.fori_loop(0, -(-n_chunks // unroll), prep, 0)

    state_s[...] = jnp.zeros_like(state_s)

    def scan(s, carry):
        for d in range(2):
            if d == 0:
                c = s
            else:
                c = jnp.where(s < n_ctx_chunks, n_ctx_chunks - 1 - s, n_chunks - 1 - (s - n_ctx_chunks))
            t0 = pl.multiple_of(c * GC, GC)
            st = state_s[d]
            r = jnp.dot(gq_s[d, pl.ds(pl.multiple_of(2 * t0, 2 * GC), 2 * GC), :], st.astype(BF16),
                        preferred_element_type=F32)
            cd = jnp.exp(gate_s[d, 2, pl.ds(c, 1), 0:1])
            state_s[d] = st * cd - r[:GD] + b_s[d, pl.ds(t0, GC), :]
            o_s[d, pl.ds(t0, GC), :] += r[GD:]
        return carry

    lax.fori_loop(0, n_chunks, scan, 0)

    gain = onorm_ref[...].astype(F32)

    def finish(c, carry):
        t0 = pl.multiple_of(lc + c * GC, GC)
        o = o_s[0, pl.ds(t0, GC), :] + o_s[1, pl.ds(t0, GC), :]
        z = z_ref[0, pl.ds(t0, GC), :].astype(F32)
        y = o * lax.rsqrt(jnp.mean(o * o, axis=-1, keepdims=True) + EPS) * gain
        y = y * (z * jax.nn.sigmoid(z))
        o_ref[0, pl.ds(pl.multiple_of(c * GC, GC), GC), :] = y.astype(o_ref.dtype)
        return carry

    lax.fori_loop(0, n_chunks - n_ctx_chunks, finish, 0, unroll=4)


def gdn_core(p_all, ab_rows, conv_w, a_log, dt_bias, out_norm, n_ctx):
    bsz, ltot, _ = p_all.shape
    n_heads = a_log.shape[1]
    n_chunks = ltot // GC
    n_ctx_chunks = n_ctx // GC
    lat = ltot - n_ctx
    n_rows = ab_rows.shape[2]
    unroll = min(GDN_UNROLL, n_chunks)
    body = functools.partial(_gdn_body, n_ctx_chunks=n_ctx_chunks, n_chunks=n_chunks, n_heads=n_heads,
                             unroll=unroll)
    smem = pl.BlockSpec(memory_space=pltpu.SMEM)

    def col(off):
        return pl.BlockSpec((1, ltot, GD), lambda b, h: (b, 0, off * n_heads + h))

    def cw(off):
        return pl.BlockSpec((CONV_K, GD), lambda b, h: (0, off * n_heads + h))

    return pl.pallas_call(
        body,
        grid=(bsz, n_heads),
        in_specs=[smem, smem, col(0), col(1), col(2), col(3),
                  pl.BlockSpec((1, 4 * n_heads, n_rows, GC), lambda b, h: (b, 0, 0, 0)),
                  cw(0), cw(1), cw(2),
                  pl.BlockSpec((1, GD), lambda b, h: (0, 0))],
        out_specs=pl.BlockSpec((1, lat, GD), lambda b, h: (b, 0, h)),
        out_shape=jax.ShapeDtypeStruct((bsz, lat, n_heads * GD), BF16),
        scratch_shapes=[
            pltpu.VMEM((ltot + 3 * PAD, GD), F32),
            pltpu.VMEM((ltot + 3 * PAD, GD), F32),
            pltpu.VMEM((ltot + 3 * PAD, GD), F32),
            pltpu.VMEM((2, ltot, GD), F32),
            pltpu.VMEM((2, ltot, GD), F32),
            pltpu.VMEM((2, 2 * ltot, GD), BF16),
            pltpu.VMEM((2, 3, n_rows, GC), F32),
            pltpu.VMEM((2, GD, GD), F32),
            pltpu.VMEM((N_MASKS, GC, GC), F32),
        ],
        compiler_params=pltpu.CompilerParams(
            dimension_semantics=("arbitrary", "arbitrary"),
            vmem_limit_bytes=56 * 1024 * 1024),
        name="gdn_core",
    )(a_log.astype(F32), dt_bias.astype(F32), p_all, p_all, p_all, p_all, ab_rows,
      conv_w, conv_w, conv_w, out_norm.reshape(1, GD))


def gdn_gate_rows(ab, n_heads):
    bsz, ltot, _ = ab.shape
    n_chunks = ltot // GC
    n_rows = -(-n_chunks // 8) * 8
    t = jnp.transpose(ab.astype(F32), (0, 2, 1)).reshape(bsz, 4 * n_heads, n_chunks, GC)
    return jnp.pad(t, ((0, 0), (0, 0), (0, n_rows - n_chunks), (0, 0)))


TM = 256
EPI_ROWS = 128
PAIR = 2
FFN_TF = 256
MOD_TN = 512
MOD_SHIFT_M, MOD_SCALE_M, MOD_GATE_M, MOD_SHIFT_F, MOD_SCALE_F, MOD_GATE_F = range(6)
N_MOD = 6
RESIDENT = dict(pipeline_mode=pl.Buffered(1))
VMEM_LIMIT = 56 * 1024 * 1024


def _rms(x, gain):
    return x * lax.rsqrt(jnp.mean(x * x, axis=-1, keepdims=True) + EPS) * gain


def _mod_spec(k, tiles_per_batch, ctx_tiles, d, half=None):
    def index(i, *_):
        t = i if half is None else PAIR * i + half
        seg = jnp.where(t % tiles_per_batch >= ctx_tiles, 1, 0)
        return ((t // tiles_per_batch) * 2 + seg) * N_MOD + k, 0, 0
    return pl.BlockSpec((1, 1, d), index)


def _const_spec(shape):
    return pl.BlockSpec(shape, lambda i, *_: (0,) * len(shape), **RESIDENT)


def _mod_body(c_ref, w_ref, b_ref, o_ref):
    c = c_ref[...]
    s = (c * jax.nn.sigmoid(c)).astype(BF16)
    o_ref[...] = jnp.dot(s, w_ref[...].astype(BF16), preferred_element_type=F32) + b_ref[...]


def modulation(c, c_ctx, mod_w, mod_b):
    bsz, d = c.shape
    n = mod_w.shape[1]
    rows = -(-(bsz + 1) // 8) * 8
    cc = jnp.zeros((rows, d), F32).at[:bsz].set(c).at[bsz].set(c_ctx)
    out = pl.pallas_call(
        _mod_body,
        grid=(n // MOD_TN,),
        in_specs=[pl.BlockSpec((rows, d), lambda j: (0, 0)),
                  pl.BlockSpec((d, MOD_TN), lambda j: (0, j)),
                  pl.BlockSpec((1, MOD_TN), lambda j: (0, j))],
        out_specs=pl.BlockSpec((rows, MOD_TN), lambda j: (0, j)),
        out_shape=jax.ShapeDtypeStruct((rows, n), F32),
        compiler_params=pltpu.CompilerParams(dimension_semantics=("arbitrary",)),
        name="modulation",
    )(cc, mod_w, mod_b.reshape(1, n))
    lat = out[:bsz].reshape(bsz, 1, N_MOD, d)
    ctx = jnp.broadcast_to(out[bsz].reshape(1, 1, N_MOD, d), (bsz, 1, N_MOD, d))
    return jnp.concatenate([ctx, lat], axis=1).reshape(bsz * 2 * N_MOD, 1, d)


def _pre_mm_body(*refs, tiles_per_batch, ctx_tiles, rope_lo, rope_hi, q_hi):
    per_tile = [refs[6 * t:6 * t + 6] for t in range(PAIR)]
    gain_ref, w_ref, o_ref, xo_ref = refs[6 * PAIR:]
    i = pl.program_id(0)
    hs = []
    for t, (xc_ref, xl_ref, sh_ref, sc_ref, _, _) in enumerate(per_tile):
        is_ctx = (PAIR * i + t) % tiles_per_batch < ctx_tiles
        x = jnp.where(is_ctx, xc_ref[...], xl_ref[...])
        xo_ref[t * TM:(t + 1) * TM, :] = x
        hs.append((_rms(x, gain_ref[...]) * (1.0 + sc_ref[0]) + sh_ref[0]).astype(BF16))
    acc = jnp.dot(jnp.concatenate(hs, axis=0), w_ref[...], preferred_element_type=F32)
    lane = lax.broadcasted_iota(I32, (TM, LANES), 1)
    first_half = (lane % (HEAD_DIM // 2)) < (HEAD_DIM // 4)
    for t, (_, _, _, _, cos_ref, sin_ref) in enumerate(per_tile):
        rows = slice(t * TM, (t + 1) * TM)
        for s in range(0, acc.shape[1], LANES):
            blk = acc[rows, s:s + LANES]
            if rope_lo <= s < rope_hi:
                partner = jnp.where(first_half, pltpu.roll(blk, LANES - HEAD_DIM // 4, 1),
                                    pltpu.roll(blk, HEAD_DIM // 4, 1))
                blk = blk * cos_ref[...] + partner * sin_ref[...]
                if s < q_hi:
                    blk = blk * (HEAD_DIM ** -0.5)
            o_ref[rows, s:s + LANES] = blk.astype(o_ref.dtype)


def rope_tables(n_ctx, n_lat):
    half = HEAD_DIM // 2
    inv = ROPE_THETA ** (-jnp.arange(0, half, 2, dtype=F32) / half)
    t = jnp.arange(n_lat)
    pos = jnp.stack([(t // GRID_W).astype(F32), (t % GRID_W).astype(F32)], axis=1)
    lane = jnp.arange(LANES)
    part = (lane % HEAD_DIM) // half
    freq = lane % (half // 2)
    ang = pos[:, part] * inv[freq][None, :]
    sign = jnp.where((lane % half) < half // 2, -1.0, 1.0)
    cos = jnp.concatenate([jnp.ones((n_ctx, LANES), F32), jnp.cos(ang)], axis=0)
    sin = jnp.concatenate([jnp.zeros((n_ctx, LANES), F32), jnp.sin(ang) * sign], axis=0)
    return cos, sin


def pre_mm(ctx, x, gain, mod, w, cos, sin):
    bsz, n_ctx, d = ctx.shape
    n_lat = x.shape[1]
    n = w.shape[1]
    ctx_tiles, lat_tiles = n_ctx // TM, n_lat // TM
    tiles = ctx_tiles + lat_tiles
    r = bsz * tiles * TM
    tmd = PAIR * TM

    def src_spec(half, latent):
        def index(i):
            t = PAIR * i + half
            b, j = t // tiles, t % tiles
            if latent:
                return b * lat_tiles + jnp.clip(j - ctx_tiles, 0, lat_tiles - 1), 0
            return b * ctx_tiles + jnp.minimum(j, ctx_tiles - 1), 0
        return pl.BlockSpec((TM, d), index)

    def table_spec(half):
        return pl.BlockSpec((TM, LANES), lambda i: ((PAIR * i + half) % tiles, 0))

    in_specs, args = [], []
    for t in range(PAIR):
        in_specs += [src_spec(t, False), src_spec(t, True),
                     _mod_spec(MOD_SHIFT_M, tiles, ctx_tiles, d, half=t), _mod_spec(MOD_SCALE_M, tiles, ctx_tiles, d, half=t),
                     table_spec(t), table_spec(t)]
        args += [ctx.reshape(bsz * n_ctx, d), x.reshape(bsz * n_lat, d), mod, mod, cos, sin]
    body = functools.partial(_pre_mm_body, tiles_per_batch=tiles, ctx_tiles=ctx_tiles, rope_lo=POOL_DIM,
                             rope_hi=POOL_DIM + Q_DIM + KV_DIM, q_hi=POOL_DIM + Q_DIM)
    return pl.pallas_call(
        body,
        grid=(r // tmd,),
        in_specs=in_specs + [_const_spec((1, d)), _const_spec((d, n))],
        out_specs=[pl.BlockSpec((tmd, n), lambda i: (i, 0)), pl.BlockSpec((tmd, d), lambda i: (i, 0))],
        out_shape=[jax.ShapeDtypeStruct((r, n), BF16), jax.ShapeDtypeStruct((r, d), F32)],
        compiler_params=pltpu.CompilerParams(dimension_semantics=("arbitrary",), vmem_limit_bytes=VMEM_LIMIT),
        name="l0_norm_proj_rope",
    )(*args, gain.reshape(1, d), w)


def _post_mm_body(*refs, n_a, route):
    a_refs = refs[:n_a]
    w_refs = refs[n_a:2 * n_a]
    x_refs = refs[2 * n_a:2 * n_a + PAIR]
    gpost_ref, gpre_ref = refs[2 * n_a + PAIR:2 * n_a + PAIR + 2]
    mod_refs = refs[2 * n_a + PAIR + 2:2 * n_a + 4 * PAIR + 2]
    rest = refs[2 * n_a + 4 * PAIR + 2:]
    if route:
        rhi_ref, rlo_ref, xo_ref, ho_ref, route_ref, cnt_ref, carry_ref = rest
    else:
        xo_ref, ho_ref = rest
    ys = []
    for s in range(0, PAIR * TM, EPI_ROWS):
        rows = slice(s, s + EPI_ROWS)
        y = jnp.dot(a_refs[0][rows, :], w_refs[0][...], preferred_element_type=F32)
        for a_ref, w_ref in zip(a_refs[1:], w_refs[1:]):
            y = y + jnp.dot(a_ref[rows, :], w_ref[...], preferred_element_type=F32)
        ys.append(y)
    _residual_epilogue(ys, x_refs, gpost_ref, gpre_ref, mod_refs, xo_ref, ho_ref)
    if route:
        _route_rows(ho_ref[...], rhi_ref, rlo_ref, route_ref, cnt_ref, carry_ref)


def _residual_epilogue(ys, x_refs, gpost_ref, gpre_ref, mod_refs, xo_ref, ho_ref):
    for k, y in enumerate(ys):
        s = k * EPI_ROWS
        t, r0 = s // TM, s % TM
        rows = slice(s, s + EPI_ROWS)
        gate_ref, sh_ref, sc_ref = mod_refs[3 * t:3 * t + 3]
        x = x_refs[t][r0:r0 + EPI_ROWS, :] if len(x_refs) == PAIR else x_refs[0][rows, :]
        xn = x + gate_ref[0] * _rms(y, gpost_ref[...])
        xo_ref[rows, :] = xn
        ho_ref[rows, :] = (_rms(xn, gpre_ref[...]) * (1.0 + sc_ref[0]) + sh_ref[0]).astype(ho_ref.dtype)


def _pair_mod_specs(kinds, tiles_per_batch, ctx_tiles, d):
    return [_mod_spec(k, tiles_per_batch, ctx_tiles, d, half=t) for t in range(PAIR) for k in kinds]


def post_mm(a_list, w_list, x, x_tile_offset, g_post, g_pre, mod, tiles_per_batch, ctx_tiles, x_tiles_per_batch,
            h_dtype, name, router=None):
    r = a_list[0].shape[0]
    d = w_list[0].shape[1]
    n_a = len(a_list)
    tmd = PAIR * TM
    route = router is not None
    route_in = [_const_spec((d, LANES)), _const_spec((d, LANES))] if route else []
    route_args = list(router_operands(router, d)) if route else []
    route_out = [pl.BlockSpec((tmd, LANES), lambda i: (i, 0)), pl.BlockSpec((8, LANES), lambda i: (0, 0))] if route else []
    route_shape = [jax.ShapeDtypeStruct((r, LANES), F32), jax.ShapeDtypeStruct((8, LANES), F32)] if route else []

    def x_spec(half):
        def index(i):
            t = PAIR * i + half
            return (t // tiles_per_batch) * x_tiles_per_batch + x_tile_offset + t % tiles_per_batch, 0
        return pl.BlockSpec((TM, d), index)

    in_specs = ([pl.BlockSpec((tmd, a.shape[1]), lambda i: (i, 0)) for a in a_list]
                + [_const_spec(w.shape) for w in w_list]
                + [x_spec(t) for t in range(PAIR)]
                + [_const_spec((1, d)), _const_spec((1, d))]
                + _pair_mod_specs((MOD_GATE_M, MOD_SHIFT_F, MOD_SCALE_F), tiles_per_batch, ctx_tiles, d)
                + route_in)
    return pl.pallas_call(
        functools.partial(_post_mm_body, n_a=n_a, route=route),
        grid=(r // tmd,),
        in_specs=in_specs,
        out_specs=[pl.BlockSpec((tmd, d), lambda i: (i, 0)), pl.BlockSpec((tmd, d), lambda i: (i, 0))] + route_out,
        out_shape=[jax.ShapeDtypeStruct((r, d), F32), jax.ShapeDtypeStruct((r, d), h_dtype)] + route_shape,
        scratch_shapes=[pltpu.VMEM((1, LANES), F32)] if route else [],
        compiler_params=pltpu.CompilerParams(dimension_semantics=("arbitrary",), vmem_limit_bytes=VMEM_LIMIT),
        name=name,
    )(*a_list, *w_list, *([x] * PAIR), g_post.reshape(1, d), g_pre.reshape(1, d), *([mod] * (3 * PAIR)),
      *route_args)


def _out_ffn_body(*refs):
    pooled_ref, attn_ref, wa_ref, wb_ref = refs[:4]
    x_refs = refs[4:4 + PAIR]
    g_mix_post, g_ffn_pre, g_ffn_post, g_next_pre = refs[4 + PAIR:8 + PAIR]
    mix_mods = refs[8 + PAIR:8 + 4 * PAIR]
    ffn_mods = refs[8 + 4 * PAIR:8 + 7 * PAIR]
    w1_ref, w3_ref, w2_ref = refs[8 + 7 * PAIR:11 + 7 * PAIR]
    xo_ref, ho_ref, x1_s, h1_s, acc_ref = refs[11 + 7 * PAIR:]
    blocks = range(0, PAIR * TM, EPI_ROWS)
    ys = [jnp.dot(pooled_ref[s:s + EPI_ROWS, :], wa_ref[...], preferred_element_type=F32)
          + jnp.dot(attn_ref[s:s + EPI_ROWS, :], wb_ref[...], preferred_element_type=F32) for s in blocks]
    _residual_epilogue(ys, x_refs, g_mix_post, g_ffn_pre, mix_mods, x1_s, h1_s)
    h = h1_s[...]
    n_f = w1_ref.shape[1]
    for f0 in range(0, n_f, FFN_TF):
        a = jnp.dot(h, w1_ref[:, f0:f0 + FFN_TF], preferred_element_type=F32)
        b = jnp.dot(h, w3_ref[:, f0:f0 + FFN_TF], preferred_element_type=F32)
        mid = (a * jax.nn.sigmoid(a) * b).astype(BF16)
        part = jnp.dot(mid, w2_ref[f0:f0 + FFN_TF, :], preferred_element_type=F32)
        if f0 == 0:
            acc_ref[...] = part
        else:
            acc_ref[...] += part
    _residual_epilogue([acc_ref[s:s + EPI_ROWS, :] for s in blocks], (x1_s,), g_ffn_post, g_next_pre, ffn_mods,
                       xo_ref, ho_ref)


def out_ffn_layer(pooled, attn, w_out, x, g_mix_post, g_ffn_pre, g_ffn_post, g_next_pre, mod, mod_next,
                  w1, w3, w2, tiles_per_batch, ctx_tiles):
    r, d = x.shape
    tmd = PAIR * TM
    n_pool = pooled.shape[1]
    gate_specs = _pair_mod_specs((MOD_GATE_F,), tiles_per_batch, ctx_tiles, d)
    next_specs = _pair_mod_specs((MOD_SHIFT_M, MOD_SCALE_M), tiles_per_batch, ctx_tiles, d)
    ffn_specs, ffn_args = [], []
    for t in range(PAIR):
        ffn_specs += [gate_specs[t], next_specs[2 * t], next_specs[2 * t + 1]]
        ffn_args += [mod, mod_next, mod_next]
    vec = lambda g: g.reshape(1, d)
    return pl.pallas_call(
        _out_ffn_body,
        grid=(r // tmd,),
        in_specs=([pl.BlockSpec((tmd, n_pool), lambda i: (i, 0)), pl.BlockSpec((tmd, attn.shape[1]), lambda i: (i, 0)),
                   _const_spec((n_pool, d)), _const_spec((w_out.shape[0] - n_pool, d))]
                  + [pl.BlockSpec((TM, d), lambda i, t=t: (PAIR * i + t, 0)) for t in range(PAIR)]
                  + [_const_spec((1, d))] * 4
                  + _pair_mod_specs((MOD_GATE_M, MOD_SHIFT_F, MOD_SCALE_F), tiles_per_batch, ctx_tiles, d)
                  + ffn_specs
                  + [_const_spec(w1.shape), _const_spec(w3.shape), _const_spec(w2.shape)]),
        out_specs=[pl.BlockSpec((tmd, d), lambda i: (i, 0)), pl.BlockSpec((tmd, d), lambda i: (i, 0))],
        out_shape=[jax.ShapeDtypeStruct((r, d), F32), jax.ShapeDtypeStruct((r, d), BF16)],
        scratch_shapes=[pltpu.VMEM((tmd, d), F32), pltpu.VMEM((tmd, d), BF16), pltpu.VMEM((tmd, d), F32)],
        compiler_params=pltpu.CompilerParams(dimension_semantics=("arbitrary",), vmem_limit_bytes=VMEM_LIMIT),
        name="l0_out_ffn",
    )(pooled, attn, w_out[:n_pool], w_out[n_pool:], *([x] * PAIR), vec(g_mix_post), vec(g_ffn_pre), vec(g_ffn_post),
      vec(g_next_pre), *([mod] * (3 * PAIR)), *ffn_args, w1, w3, w2)


def _proj_body(h_ref, w_ref, wg_ref, o_ref, og_ref):
    h = h_ref[...]
    o_ref[...] = jnp.dot(h, w_ref[...], preferred_element_type=F32).astype(o_ref.dtype)
    og_ref[...] = jnp.dot(h, wg_ref[...], preferred_element_type=F32)


def gdn_proj(h, w, w_gate):
    r, d = h.shape
    n = w.shape[1]
    tmd = PAIR * TM
    return pl.pallas_call(
        _proj_body,
        grid=(r // tmd,),
        in_specs=[pl.BlockSpec((tmd, d), lambda i: (i, 0)), _const_spec(w.shape), _const_spec(w_gate.shape)],
        out_specs=[pl.BlockSpec((tmd, n), lambda i: (i, 0)), pl.BlockSpec((tmd, LANES), lambda i: (i, 0))],
        out_shape=[jax.ShapeDtypeStruct((r, n), BF16), jax.ShapeDtypeStruct((r, LANES), F32)],
        compiler_params=pltpu.CompilerParams(dimension_semantics=("arbitrary",), vmem_limit_bytes=VMEM_LIMIT),
        name="l1_proj",
    )(h, w, w_gate)


ATTN_HEADS_PER_UNIT = 1


def _attn_body(sink_ref, q_ref, kc_ref, vc_ref, k0_ref, k1_ref, k2_ref, v0_ref, v1_ref, v2_ref, o_ref,
               *, n_ctx_blocks, n_blocks):
    n = pl.program_id(1)
    rows = ATTN_HEADS_PER_UNIT * BLOCK
    qi = lax.broadcasted_iota(I32, (rows, 3 * BLOCK), 0) % BLOCK
    kj = lax.broadcasted_iota(I32, (rows, 3 * BLOCK), 1)
    first = jnp.where(n > n_ctx_blocks, 0, BLOCK)
    last = jnp.where(n < n_blocks - 1, 3 * BLOCK, 2 * BLOCK)
    last = jnp.where(n >= n_ctx_blocks, last, 0)
    band_ok = (kj >= qi) & (kj <= qi + 2 * WINDOW) & (kj >= first) & (kj < last)
    q = q_ref[0]
    nt = (((1,), (1,)), ((), ()))
    groups = range(N_KV_HEADS)
    hpu = ATTN_HEADS_PER_UNIT
    units = [(g, [g * GQA_GROUP + s + i for i in range(hpu)]) for g in groups for s in range(0, GQA_GROUP, hpu)]
    idx = range(len(units))
    cols = [slice(g * HEAD_DIM, (g + 1) * HEAD_DIM) for g in groups]
    q_u = [jnp.concatenate([q[:, hd * HEAD_DIM:(hd + 1) * HEAD_DIM] for hd in hs], axis=0) for _, hs in units]
    k_c = [kc_ref[0][:, c] for c in cols]
    k_b = [jnp.concatenate([k0_ref[0][:, c], k1_ref[0][:, c], k2_ref[0][:, c]], axis=0) for c in cols]
    ones_b = jnp.ones((3 * BLOCK, HEAD_DIM), BF16)
    ones_c = jnp.ones((kc_ref.shape[1], HEAD_DIM), BF16)
    v_b = [jnp.concatenate([jnp.concatenate([v0_ref[0][:, c], v1_ref[0][:, c], v2_ref[0][:, c]], axis=0), ones_b],
                           axis=1) for c in cols]
    v_c = [jnp.concatenate([vc_ref[0][:, c], ones_c], axis=1) for c in cols]
    s_c = [lax.dot_general(q_u[u], k_c[units[u][0]], nt, preferred_element_type=F32) for u in idx]
    s_b = [lax.dot_general(q_u[u], k_b[units[u][0]], nt, preferred_element_type=F32) for u in idx]
    s_b = [jnp.where(band_ok, s, NEG_INF) for s in s_b]
    sink = [jnp.concatenate([sink_ref[hd] + jnp.zeros((BLOCK, 1), F32) for hd in hs], axis=0) for _, hs in units]
    m = [jnp.maximum(jnp.maximum(jnp.max(s_c[u], axis=-1, keepdims=True), jnp.max(s_b[u], axis=-1, keepdims=True)),
                     sink[u]) for u in idx]
    p_c = [jnp.exp(s_c[u] - m[u]) for u in idx]
    p_b = [jnp.exp(s_b[u] - m[u]) for u in idx]
    o_c = [jnp.dot(p_c[u].astype(BF16), v_c[units[u][0]], preferred_element_type=F32) for u in idx]
    o_b = [jnp.dot(p_b[u].astype(BF16), v_b[units[u][0]], preferred_element_type=F32) for u in idx]
    outs = []
    for u in idx:
        acc = o_c[u] + o_b[u]
        den = acc[:, HEAD_DIM:HEAD_DIM + 1] + jnp.exp(sink[u] - m[u])
        o = acc[:, :HEAD_DIM] / den
        outs += [o[i * BLOCK:(i + 1) * BLOCK] for i in range(hpu)]
    o_ref[0] = jnp.concatenate(outs, axis=1).astype(o_ref.dtype)


def attention(p0, sinks, n_ctx):
    bsz, ltot, _ = p0.shape
    n_blocks = ltot // BLOCK
    n_ctx_blocks = n_ctx // BLOCK
    k_blk = (POOL_DIM + Q_DIM) // KV_DIM
    v_blk = k_blk + 1

    def band(dn, blk):
        return pl.BlockSpec((1, BLOCK, KV_DIM),
                            lambda b, n: (b, jnp.clip(n + dn, n_ctx_blocks, n_blocks - 1), blk))

    return pl.pallas_call(
        functools.partial(_attn_body, n_ctx_blocks=n_ctx_blocks, n_blocks=n_blocks),
        grid=(bsz, n_blocks),
        in_specs=[pl.BlockSpec(memory_space=pltpu.SMEM),
                  pl.BlockSpec((1, BLOCK, Q_DIM), lambda b, n: (b, n, POOL_DIM // Q_DIM)),
                  pl.BlockSpec((1, n_ctx, KV_DIM), lambda b, n: (b, 0, k_blk)),
                  pl.BlockSpec((1, n_ctx, KV_DIM), lambda b, n: (b, 0, v_blk)),
                  band(-1, k_blk), band(0, k_blk), band(1, k_blk),
                  band(-1, v_blk), band(0, v_blk), band(1, v_blk)],
        out_specs=pl.BlockSpec((1, BLOCK, Q_DIM), lambda b, n: (b, n, 0)),
        out_shape=jax.ShapeDtypeStruct((bsz, ltot, Q_DIM), BF16),
        compiler_params=pltpu.CompilerParams(dimension_semantics=("arbitrary", "arbitrary"),
                                             vmem_limit_bytes=VMEM_LIMIT),
        name="l0_attention",
    )(sinks.astype(F32), p0, p0, p0, p0, p0, p0, p0, p0, p0)


POOL_HALO = 16


def _pool_body(up_ref, uc_ref, un_ref, w_ref, scale_ref, o_ref, *, tiles_per_batch, ctx_tiles):
    j = pl.program_id(1)
    seg_lo = jnp.where(j < ctx_tiles, 0, ctx_tiles * TM)
    seg_hi = jnp.where(j < ctx_tiles, ctx_tiles * TM, tiles_per_batch * TM)
    halo = jnp.concatenate([up_ref[0][TM - POOL_HALO:], uc_ref[0], un_ref[0][:POOL_HALO]], axis=0)
    n_h = TM + 2 * POOL_HALO
    t = j * TM + lax.broadcasted_iota(I32, (TM, n_h), 0)
    pos = j * TM - POOL_HALO + lax.broadcasted_iota(I32, (TM, n_h), 1)
    t_col = j * TM + lax.broadcasted_iota(I32, (TM, 1), 0)
    cur = uc_ref[0]
    groups = range(len(POOL_WINDOWS))
    cols = [slice(g * POOL_GROUP_DIM, (g + 1) * POOL_GROUP_DIM) for g in groups]
    windows, cnts = [], []
    for w in POOL_WINDOWS:
        lo = jnp.maximum(t - w // 2, seg_lo)
        hi = jnp.minimum(t + w // 2, seg_hi)
        windows.append(((pos >= lo) & (pos < hi)).astype(BF16))
        cnts.append((jnp.minimum(t_col + w // 2, seg_hi) - jnp.maximum(t_col - w // 2, seg_lo)).astype(F32))
    sums = [jnp.dot(windows[g], halo[:, cols[g]], preferred_element_type=F32) for g in groups]
    deltas = [(sums[g] / cnts[g] - cur[:, cols[g]].astype(F32)).astype(BF16) for g in groups]
    ys = [jnp.dot(deltas[g], w_ref[g], preferred_element_type=F32) for g in groups]
    for g in groups:
        o_ref[0, :, cols[g]] = (ys[g] * scale_ref[:, cols[g]]).astype(o_ref.dtype)


def pool_mixer(p0, pool_w, pool_scale, n_ctx):
    bsz, ltot, _ = p0.shape
    tiles_per_batch = ltot // TM
    ctx_tiles = n_ctx // TM

    def tile(dj):
        return pl.BlockSpec((1, TM, POOL_DIM), lambda b, j: (b, jnp.clip(j + dj, 0, tiles_per_batch - 1), 0))

    return pl.pallas_call(
        functools.partial(_pool_body, tiles_per_batch=tiles_per_batch, ctx_tiles=ctx_tiles),
        grid=(bsz, tiles_per_batch),
        in_specs=[tile(-1), tile(0), tile(1),
                  pl.BlockSpec(pool_w.shape, lambda b, j: (0, 0, 0)),
                  pl.BlockSpec((1, POOL_DIM), lambda b, j: (0, 0))],
        out_specs=pl.BlockSpec((1, TM, POOL_DIM), lambda b, j: (b, j, 0)),
        out_shape=jax.ShapeDtypeStruct((bsz, ltot, POOL_DIM), BF16),
        compiler_params=pltpu.CompilerParams(dimension_semantics=("arbitrary", "arbitrary")),
        name="l0_pool",
    )(p0, p0, p0, pool_w.astype(BF16), pool_scale.reshape(1, POOL_DIM).astype(F32))


def kernel(x, c, ctx, c_ctx, l0_mod_w, l0_mod_b, l0_mix_pre, l0_mix_post, l0_ffn_pre, l0_ffn_post, l0_w_in, l0_pool_w, l0_pool_scale, l0_sinks, l0_w_out, l0_ffn_w1, l0_ffn_w3, l0_ffn_w2, l1_mod_w, l1_mod_b, l1_mix_pre, l1_mix_post, l1_ffn_pre, l1_ffn_post, l1_w_in, l1_conv_w, l1_a_log, l1_dt_bias, l1_out_norm, l1_w_out, l1_router, l1_moe_w1, l1_moe_w3, l1_moe_w2):
    bsz, n_lat, d = x.shape
    n_ctx = ctx.shape[1]
    ltot = n_ctx + n_lat
    assert n_ctx % TM == 0 and n_lat % TM == 0 and n_ctx % GC == 0
    assert (bsz * ltot) % (PAIR * TM) == 0 and (bsz * n_lat) % (PAIR * TM) == 0
    tiles = ltot // TM
    ctx_tiles = n_ctx // TM
    lat_tiles = n_lat // TM
    bf = lambda w: w.astype(BF16)

    mod0 = modulation(c, c_ctx, l0_mod_w, l0_mod_b)
    mod1 = modulation(c, c_ctx, l1_mod_w, l1_mod_b)
    cos, sin = rope_tables(n_ctx, n_lat)

    p0, x_all = pre_mm(ctx, x, l0_mix_pre, mod0, bf(l0_w_in), cos, sin)
    p0 = p0.reshape(bsz, ltot, -1)
    attn = attention(p0, l0_sinks, n_ctx).reshape(bsz * ltot, Q_DIM)
    pooled = pool_mixer(p0, l0_pool_w, l0_pool_scale, n_ctx).reshape(bsz * ltot, POOL_DIM)
    x2, h2 = out_ffn_layer(pooled, attn, bf(l0_w_out), x_all, l0_mix_post, l0_ffn_pre, l0_ffn_post, l1_mix_pre,
                           mod0, mod1, bf(l0_ffn_w1), bf(l0_ffn_w3), bf(l0_ffn_w2), tiles, ctx_tiles)

    n_heads = l1_a_log.shape[1]
    qkvz = 4 * n_heads * GD
    w_in1 = bf(l1_w_in)
    w_gate = jnp.zeros((d, LANES), BF16).at[:, :4 * n_heads].set(w_in1[:, qkvz:])
    p1, ab = gdn_proj(h2, w_in1[:, :qkvz], w_gate)
    ab_rows = gdn_gate_rows(ab.reshape(bsz, ltot, LANES)[..., :4 * n_heads], n_heads)
    y = gdn_core(p1.reshape(bsz, ltot, qkvz), ab_rows, l1_conv_w, l1_a_log, l1_dt_bias, l1_out_norm, n_ctx)
    x3, h3, route, cnt = post_mm([y.reshape(bsz * n_lat, n_heads * GD)], [bf(l1_w_out)], x2, ctx_tiles, l1_mix_post,
                                 l1_ffn_pre, mod1, lat_tiles, 0, tiles, F32, "l1_out_proj_route", router=l1_router)
    out = moe_layer(h3, route, cnt, l1_moe_w1, l1_moe_w3, l1_moe_w2, x3, l1_ffn_post, mod1, n_lat)
    return out.reshape(bsz, n_lat, d)
```

```python
import functools
import math

import jax
import jax.numpy as jnp
from jax import lax
from jax.experimental import pallas as pl
from jax.experimental.pallas import tpu as pltpu

F32 = jnp.float32
BF16 = jnp.bfloat16
I32 = jnp.int32

LANES = 128
D_MODEL = 1024
GRID_W = 64
EPS = 1e-6
NEG_INF = -1e30

POOL_GROUPS = 4
POOL_GROUP_DIM = 128
POOL_DIM = POOL_GROUPS * POOL_GROUP_DIM
POOL_WINDOWS = (2, 4, 8, 16)
HEAD_DIM = 64
N_HEADS = 8
N_KV_HEADS = 2
GQA_GROUP = N_HEADS // N_KV_HEADS
Q_DIM = N_HEADS * HEAD_DIM
KV_DIM = N_KV_HEADS * HEAD_DIM
WINDOW = 128
BLOCK = 128
ROPE_THETA = 10000.0

GDN_HEADS = 8
GDN_HEAD_DIM = 128
GDN_DIM = GDN_HEADS * GDN_HEAD_DIM
CONV_K = 5
CHUNK = 64

N_EXPERTS = 8
TOP_K = 2

MOE_TM = 512
MOE_TF = 1792
ROW_TILE = 512
ISSUE_UNROLL = 8


def _split_bf16(a):
    hi = a.astype(BF16)
    lo = (a - hi.astype(F32)).astype(BF16)
    return hi, lo


def _route_rows(h, rhi_ref, rlo_ref, route_ref, cnt_ref, carry_ref):
    i = pl.program_id(0)
    tr = h.shape[0]

    @pl.when(i == 0)
    def _():
        carry_ref[...] = jnp.zeros_like(carry_ref)

    h_hi, h_lo = _split_bf16(h)
    r_hi = rhi_ref[...]
    r_lo = rlo_ref[...]
    logits = (jnp.dot(h_hi, r_hi, preferred_element_type=F32)
              + jnp.dot(h_hi, r_lo, preferred_element_type=F32)
              + jnp.dot(h_lo, r_hi, preferred_element_type=F32))
    lane = lax.broadcasted_iota(I32, (tr, LANES), 1)
    logits = jnp.where(lane < N_EXPERTS, logits, -jnp.inf)
    m1 = jnp.max(logits, axis=-1, keepdims=True)
    i1 = jnp.min(jnp.where(logits == m1, lane, LANES), axis=-1, keepdims=True)
    rest = jnp.where(lane == i1, -jnp.inf, logits)
    m2 = jnp.max(rest, axis=-1, keepdims=True)
    i2 = jnp.min(jnp.where(rest == m2, lane, LANES), axis=-1, keepdims=True)
    e2 = jnp.exp(m2 - m1)
    g1 = 1.0 / (1.0 + e2)
    g2 = e2 / (1.0 + e2)

    onehot = ((lane == i1) | (lane == i2)).astype(F32)
    row = lax.broadcasted_iota(I32, (tr, tr), 0)
    col = lax.broadcasted_iota(I32, (tr, tr), 1)
    strict = (row > col).astype(BF16)
    before = jnp.dot(strict, onehot.astype(BF16), preferred_element_type=F32) + carry_ref[...]
    rank1 = jnp.sum(jnp.where(lane == i1, before, 0.0), axis=-1, keepdims=True)
    rank2 = jnp.sum(jnp.where(lane == i2, before, 0.0), axis=-1, keepdims=True)
    carry_ref[...] += jnp.sum(onehot, axis=0, keepdims=True)

    packed = jnp.where(lane == 0, i1.astype(F32), 0.0)
    packed = jnp.where(lane == 1, i2.astype(F32), packed)
    packed = jnp.where(lane == 2, g1, packed)
    packed = jnp.where(lane == 3, g2, packed)
    packed = jnp.where(lane == 4, rank1, packed)
    packed = jnp.where(lane == 5, rank2, packed)
    route_ref[...] = packed
    cnt_ref[...] = jnp.broadcast_to(carry_ref[...], cnt_ref.shape)


def router_operands(router, d):
    r_pad = jnp.zeros((d, LANES), F32).at[:, :N_EXPERTS].set(router.astype(F32))
    return _split_bf16(r_pad)


def _row_copy(src_ref, src_row, dst_ref, dst_row, sem):
    return pltpu.make_async_copy(src_ref.at[pl.ds(src_row, 1)], dst_ref.at[pl.ds(dst_row, 1)], sem)


def _scatter_body(pos_ref, h_ref, xs_in_ref, xs_ref, sem, *, ts):
    del xs_in_ref

    def issue(r, c):
        for k in range(TOP_K):
            _row_copy(h_ref, r, xs_ref, pos_ref[0, 0, TOP_K * r + k], sem).start()
        return c

    lax.fori_loop(0, ts, issue, 0, unroll=ISSUE_UNROLL)
    for _ in range(TOP_K):
        pltpu.make_async_copy(h_ref, xs_ref.at[pl.ds(0, ts)], sem).wait()


def moe_scatter(h, pos, p_rows):
    t, d = h.shape
    ts = min(ROW_TILE, t)
    nt = t // ts
    xs0 = jnp.zeros((p_rows, d), h.dtype)
    return pl.pallas_call(
        functools.partial(_scatter_body, ts=ts),
        grid=(nt,),
        in_specs=[
            pl.BlockSpec((1, 1, TOP_K * ts), lambda i: (i, 0, 0), memory_space=pltpu.SMEM),
            pl.BlockSpec((ts, d), lambda i: (i, 0)),
            pl.BlockSpec(memory_space=pl.ANY),
        ],
        out_specs=pl.BlockSpec(memory_space=pl.ANY),
        out_shape=jax.ShapeDtypeStruct((p_rows, d), h.dtype),
        scratch_shapes=[pltpu.SemaphoreType.DMA(())],
        input_output_aliases={2: 0},
        compiler_params=pltpu.CompilerParams(dimension_semantics=("arbitrary",), disable_bounds_checks=True),
        name="moe_scatter",
    )(pos.reshape(nt, 1, TOP_K * ts), h, xs0)


def _expert_body(te_ref, nu_ref, x_ref, w1_ref, w3_ref, w2_ref, o_ref):
    i = pl.program_id(0)
    j = pl.program_id(1)

    @pl.when(j == 0)
    def _():
        o_ref[...] = jnp.zeros_like(o_ref)

    @pl.when(i < nu_ref[0])
    def _():
        x = x_ref[...].astype(BF16)
        a = jnp.dot(x, w1_ref[0], preferred_element_type=F32)
        b = jnp.dot(x, w3_ref[0], preferred_element_type=F32)
        mid = (a * jax.nn.sigmoid(a) * b).astype(BF16)
        o_ref[...] += jnp.dot(mid, w2_ref[0].astype(BF16), preferred_element_type=F32)


def moe_experts(xs, w1, w3, w2, tile_expert, n_used, tm):
    p_rows, d = xs.shape
    n_exp, _, d_exp = w1.shape
    tf = MOE_TF if d_exp % MOE_TF == 0 else d_exp
    nj = d_exp // tf
    n_tiles = p_rows // tm

    def jj(i, j, nu):
        return jnp.where(i < nu[0], j, nj - 1)

    grid_spec = pltpu.PrefetchScalarGridSpec(
        num_scalar_prefetch=2,
        grid=(n_tiles, nj),
        in_specs=[
            pl.BlockSpec((tm, d), lambda i, j, te, nu: (i, 0)),
            pl.BlockSpec((1, d, tf), lambda i, j, te, nu: (te[i], 0, jj(i, j, nu))),
            pl.BlockSpec((1, d, tf), lambda i, j, te, nu: (te[i], 0, jj(i, j, nu))),
            pl.BlockSpec((1, tf, d), lambda i, j, te, nu: (te[i], jj(i, j, nu), 0)),
        ],
        out_specs=pl.BlockSpec((tm, d), lambda i, j, te, nu: (i, 0)),
    )
    return pl.pallas_call(
        _expert_body,
        grid_spec=grid_spec,
        out_shape=jax.ShapeDtypeStruct((p_rows, d), F32),
        compiler_params=pltpu.CompilerParams(
            dimension_semantics=("arbitrary", "arbitrary"),
            vmem_limit_bytes=56 * 1024 * 1024),
        name="moe_experts",
    )(tile_expert, n_used, xs, w1, w3, w2)


def _combine_body(pos_ref, pos_next_ref, gate_ref, ys_ref, x_ref, gain_ref, mgate_ref, y_ref, buf, sem, *, ts):
    i = pl.program_id(0)
    slot = i % 2

    def gather(p_ref, s):
        def issue(r, c):
            for k in range(TOP_K):
                _row_copy(ys_ref, p_ref[0, 0, TOP_K * r + k], buf.at[s, k], r, sem.at[s]).start()
            return c
        lax.fori_loop(0, ts, issue, 0, unroll=ISSUE_UNROLL)

    @pl.when(i == 0)
    def _():
        gather(pos_ref, slot)

    @pl.when(i + 1 < pl.num_programs(0))
    def _():
        gather(pos_next_ref, 1 - slot)

    for k in range(TOP_K):
        pltpu.make_async_copy(ys_ref.at[pl.ds(0, ts)], buf.at[slot, k], sem.at[slot]).wait()
    g = gate_ref[...]
    y = g[:, 2:3] * buf[slot, 0] + g[:, 3:4] * buf[slot, 1]
    y_ref[...] = x_ref[...] + mgate_ref[0] * _rms(y, gain_ref[...])


def moe_combine(ys, pos, route, x, gain, mod, rows_per_batch):
    t = route.shape[0]
    d = ys.shape[1]
    ts = min(ROW_TILE, rows_per_batch)
    assert rows_per_batch % ts == 0
    tiles_per_batch = rows_per_batch // ts
    nt = t // ts
    pos3 = pos.reshape(nt, 1, TOP_K * ts)
    return pl.pallas_call(
        functools.partial(_combine_body, ts=ts),
        grid=(nt,),
        in_specs=[
            pl.BlockSpec((1, 1, TOP_K * ts), lambda i: (i, 0, 0), memory_space=pltpu.SMEM),
            pl.BlockSpec((1, 1, TOP_K * ts), lambda i: (jnp.minimum(i + 1, nt - 1), 0, 0), memory_space=pltpu.SMEM),
            pl.BlockSpec((ts, LANES), lambda i: (i, 0)),
            pl.BlockSpec(memory_space=pl.ANY),
            pl.BlockSpec((ts, d), lambda i: (i, 0)),
            pl.BlockSpec((1, d), lambda i: (0, 0)),
            _mod_spec(MOD_GATE_F, tiles_per_batch, 0, d),
        ],
        out_specs=pl.BlockSpec((ts, d), lambda i: (i, 0)),
        out_shape=jax.ShapeDtypeStruct((t, d), F32),
        scratch_shapes=[pltpu.VMEM((2, TOP_K, ts, d), F32), pltpu.SemaphoreType.DMA((2,))],
        compiler_params=pltpu.CompilerParams(dimension_semantics=("arbitrary",), disable_bounds_checks=True),
        name="moe_combine",
    )(pos3, pos3, route, ys, x, gain.reshape(1, d), mod)


def moe_layer(tok, route, cnt, w1, w3, w2, x, gain, mod, rows_per_batch):
    t = tok.shape[0]
    tm = min(MOE_TM, t)

    counts = cnt[0, :N_EXPERTS].astype(I32)
    padded = ((counts + tm - 1) // tm) * tm
    ends = jnp.cumsum(padded)
    starts = ends - padded
    expert = route[:, 0:TOP_K].astype(I32)
    rank = route[:, 4:4 + TOP_K].astype(I32)
    start_of = jnp.sum(jnp.where(expert[..., None] == jnp.arange(N_EXPERTS), starts, 0), axis=-1)
    pos = start_of + rank

    n_tiles = (TOP_K * t) // tm + N_EXPERTS
    n_used = (ends[-1] // tm).astype(I32)
    tile_idx = jnp.minimum(jnp.arange(n_tiles, dtype=I32), n_used - 1)
    tile_expert = jnp.sum(tile_idx[:, None] * tm >= ends[None, :], axis=-1).astype(I32)
    tile_expert = jnp.minimum(tile_expert, N_EXPERTS - 1)

    xs = moe_scatter(tok, pos, n_tiles * tm)
    ys = moe_experts(xs, w1.astype(BF16), w3.astype(BF16), w2, tile_expert, n_used.reshape(1), tm)
    return moe_combine(ys, pos, route, x, gain, mod, rows_per_batch)


GC = 128
GD = GDN_HEAD_DIM
PAD = 8
GDN_UNROLL = 9
GDN_BASE = 16
N_MERGE = 3
M_INCL_F, M_STRICT_F, M_INCL_B, M_STRICT_B, M_EYE, M_BASE, M_OFF0 = 0, 1, 2, 3, 4, 5, 6
N_MASKS = M_OFF0 + N_MERGE


def _bdot(a, b):
    return jnp.dot(a.astype(BF16), b.astype(BF16), preferred_element_type=F32)


def _bdot_nt(a, b):
    return lax.dot_general(a.astype(BF16), b.astype(BF16), (((1,), (1,)), ((), ())), preferred_element_type=F32)


def _gdn_body(alog_ref, dtb_ref, q_ref, k_ref, v_ref, z_ref, ab_ref, cq_ref, ck_ref, cv_ref, onorm_ref,
              o_ref, pq_s, pk_s, pv_s, o_s, b_s, gq_s, gate_s, state_s, mask_s,
              *, n_ctx_chunks, n_chunks, n_heads, unroll):
    h = pl.program_id(1)
    ltot = n_chunks * GC
    lc = n_ctx_chunks * GC

    row = lax.broadcasted_iota(I32, (GC, GC), 0)
    col = lax.broadcasted_iota(I32, (GC, GC), 1)
    mask_s[M_INCL_F] = (row >= col).astype(F32)
    mask_s[M_STRICT_F] = (row > col).astype(F32)
    mask_s[M_INCL_B] = (row <= col).astype(F32)
    mask_s[M_STRICT_B] = (row < col).astype(F32)
    mask_s[M_EYE] = (row == col).astype(F32)
    mask_s[M_BASE] = ((row // GDN_BASE) == (col // GDN_BASE)).astype(F32)
    for lvl in range(N_MERGE):
        s = GDN_BASE << lvl
        mask_s[M_OFF0 + lvl] = (((row // (2 * s)) == (col // (2 * s))) & ((row // s) != (col // s))).astype(F32)

    for src_ref, dst in ((q_ref, pq_s), (k_ref, pk_s), (v_ref, pv_s)):
        for off in (0, PAD + lc, 2 * PAD + ltot):
            dst[pl.ds(off, PAD), :] = jnp.zeros((PAD, GD), F32)
        dst[pl.ds(PAD, lc), :] = src_ref[0, pl.ds(0, lc), :].astype(F32)
        dst[pl.ds(2 * PAD + lc, ltot - lc), :] = src_ref[0, pl.ds(lc, ltot - lc), :].astype(F32)

    for d in range(2):
        a = ab_ref[0, d * 2 * n_heads + h]
        b = ab_ref[0, d * 2 * n_heads + n_heads + h]
        xa = a + dtb_ref[d, h]
        softplus = jnp.maximum(xa, 0.0) + jnp.log1p(jnp.exp(-jnp.abs(xa)))
        g = -jnp.exp(alog_ref[d, h]) * softplus
        beta = 1.0 / (1.0 + jnp.exp(-b))
        tri = mask_s[M_INCL_B if d == 0 else M_INCL_F].astype(BF16)
        g_hi = g.astype(BF16)
        g_lo = (g - g_hi.astype(F32)).astype(BF16)
        gam = (jnp.dot(g_hi, tri, preferred_element_type=F32) + jnp.dot(g_lo, tri, preferred_element_type=F32))
        gate_s[d, 0] = gam
        gate_s[d, 1] = beta
        gate_s[d, 2] = jnp.broadcast_to(jnp.sum(g, axis=-1, keepdims=True), g.shape)

    def conv_chunk(c, pad_ref, w_ref, normalise, scale):
        start = pl.multiple_of(c * GC + jnp.where(c >= n_ctx_chunks, PAD, 0), PAD)
        acc = jnp.zeros((GC, GD), F32)
        for j in range(CONV_K):
            tap = pad_ref[pl.ds(start + (PAD - CONV_K // 2 + j), GC), :]
            acc = acc + tap * w_ref[j:j + 1, :].astype(F32)
        y = acc * jax.nn.sigmoid(acc)
        if normalise:
            y = y * (lax.rsqrt(jnp.sum(y * y, axis=-1, keepdims=True) + EPS) * scale)
        return y

    def group_chunks(i):
        return [jnp.minimum(i * unroll + u, n_chunks - 1) for u in range(unroll)]

    def prep(i, carry):
        chunks = group_chunks(i)
        qs = [conv_chunk(c, pq_s, cq_ref, True, GD ** -0.5) for c in chunks]
        ks = [conv_chunk(c, pk_s, ck_ref, True, 1.0) for c in chunks]
        vs = [conv_chunk(c, pv_s, cv_ref, False, 1.0) for c in chunks]
        kqs = [_bdot_nt(jnp.concatenate([k, q], axis=0), k) for k, q in zip(ks, qs)]
        kts = [k.T for k in ks]
        lanes = [(u, d) for u in range(unroll) for d in range(2)]
        lms, dbs, e_cols, kdts = [], [], [], []
        for u, d in lanes:
            c = chunks[u]
            gam_row = gate_s[d, 0, pl.ds(c, 1), :]
            beta_row = gate_s[d, 1, pl.ds(c, 1), :]
            tot_row = gate_s[d, 2, pl.ds(c, 1), :]
            gam_col = jnp.sum(mask_s[M_EYE] * gam_row, axis=1, keepdims=True)
            db = jnp.exp((gam_col - gam_row) * mask_s[M_INCL_F if d == 0 else M_INCL_B]) * beta_row
            dbs.append(db)
            lms.append(kqs[u][:GC] * db * mask_s[M_STRICT_F if d == 0 else M_STRICT_B])
            e_cols.append(jnp.exp(gam_col))
            kdts.append(kts[u] * (jnp.exp(tot_row - gam_row) * beta_row))
        xs = [-lm * mask_s[M_BASE] for lm in lms]
        ts = [mask_s[M_EYE] + x for x in xs]
        for _ in range(3):
            xs = [_bdot(x, x) for x in xs]
            ts = [t + _bdot(t, x) for t, x in zip(ts, xs)]
        for lvl in range(N_MERGE):
            ys = [_bdot(lm * mask_s[M_OFF0 + lvl], t) for lm, t in zip(lms, ts)]
            ts = [t - _bdot(t, y) for t, y in zip(ts, ys)]
        wus = [_bdot(t, jnp.concatenate([ks[u] * e, vs[u]], axis=1))
               for t, e, (u, d) in zip(ts, e_cols, lanes)]
        x12s = []
        for wu, db, kdt, (u, d) in zip(wus, dbs, kdts, lanes):
            qkb = kqs[u][GC:] * db * mask_s[M_INCL_F if d == 0 else M_INCL_B]
            x12s.append(_bdot(jnp.concatenate([kdt, qkb], axis=0), wu))
        for x12, e, (u, d) in zip(x12s, e_cols, lanes):
            t0 = pl.multiple_of(chunks[u] * GC, GC)
            b_s[d, pl.ds(t0, GC), :] = x12[:GD, GD:]
            o_s[d, pl.ds(t0, GC), :] = x12[GD:, GD:]
            q_eff = qs[u] * e - x12[GD:, :GD]
            gq_s[d, pl.ds(pl.multiple_of(2 * t0, 2 * GC), 2 * GC), :] = (
                jnp.concatenate([x12[:GD, :GD], q_eff], axis=0).astype(BF16))
        return carry

    lax.fori_loop(0, -(-n_chunks // unroll), prep, 0)

    state_s[...] = jnp.zeros_like(state_s)

    def scan(s, carry):
        for d in range(2):
            if d == 0:
                c = s
            else:
                c = jnp.where(s < n_ctx_chunks, n_ctx_chunks - 1 - s, n_chunks - 1 - (s - n_ctx_chunks))
            t0 = pl.multiple_of(c * GC, GC)
            st = state_s[d]
            r = jnp.dot(gq_s[d, pl.ds(pl.multiple_of(2 * t0, 2 * GC), 2 * GC), :], st.astype(BF16),
                        preferred_element_type=F32)
            cd = jnp.exp(gate_s[d, 2, pl.ds(c, 1), 0:1])
            state_s[d] = st * cd - r[:GD] + b_s[d, pl.ds(t0, GC), :]
            o_s[d, pl.ds(t0, GC), :] += r[GD:]
        return carry

    lax.fori_loop(0, n_chunks, scan, 0)

    gain = onorm_ref[...].astype(F32)

    def finish(c, carry):
        t0 = pl.multiple_of(lc + c * GC, GC)
        o = o_s[0, pl.ds(t0, GC), :] + o_s[1, pl.ds(t0, GC), :]
        z = z_ref[0, pl.ds(t0, GC), :].astype(F32)
        y = o * lax.rsqrt(jnp.mean(o * o, axis=-1, keepdims=True) + EPS) * gain
        y = y * (z * jax.nn.sigmoid(z))
        o_ref[0, pl.ds(pl.multiple_of(c * GC, GC), GC), :] = y.astype(o_ref.dtype)
        return carry

    lax.fori_loop(0, n_chunks - n_ctx_chunks, finish, 0, unroll=4)


def gdn_core(p_all, ab_rows, conv_w, a_log, dt_bias, out_norm, n_ctx):
    bsz, ltot, _ = p_all.shape
    n_heads = a_log.shape[1]
    n_chunks = ltot // GC
    n_ctx_chunks = n_ctx // GC
    lat = ltot - n_ctx
    n_rows = ab_rows.shape[2]
    unroll = min(GDN_UNROLL, n_chunks)
    body = functools.partial(_gdn_body, n_ctx_chunks=n_ctx_chunks, n_chunks=n_chunks, n_heads=n_heads,
                             unroll=unroll)
    smem = pl.BlockSpec(memory_space=pltpu.SMEM)

    def col(off):
        return pl.BlockSpec((1, ltot, GD), lambda b, h: (b, 0, off * n_heads + h))

    def cw(off):
        return pl.BlockSpec((CONV_K, GD), lambda b, h: (0, off * n_heads + h))

    return pl.pallas_call(
        body,
        grid=(bsz, n_heads),
        in_specs=[smem, smem, col(0), col(1), col(2), col(3),
                  pl.BlockSpec((1, 4 * n_heads, n_rows, GC), lambda b, h: (b, 0, 0, 0)),
                  cw(0), cw(1), cw(2),
                  pl.BlockSpec((1, GD), lambda b, h: (0, 0))],
        out_specs=pl.BlockSpec((1, lat, GD), lambda b, h: (b, 0, h)),
        out_shape=jax.ShapeDtypeStruct((bsz, lat, n_heads * GD), BF16),
        scratch_shapes=[
            pltpu.VMEM((ltot + 3 * PAD, GD), F32),
            pltpu.VMEM((ltot + 3 * PAD, GD), F32),
            pltpu.VMEM((ltot + 3 * PAD, GD), F32),
            pltpu.VMEM((2, ltot, GD), F32),
            pltpu.VMEM((2, ltot, GD), F32),
            pltpu.VMEM((2, 2 * ltot, GD), BF16),
            pltpu.VMEM((2, 3, n_rows, GC), F32),
            pltpu.VMEM((2, GD, GD), F32),
            pltpu.VMEM((N_MASKS, GC, GC), F32),
        ],
        compiler_params=pltpu.CompilerParams(
            dimension_semantics=("arbitrary", "arbitrary"),
            vmem_limit_bytes=56 * 1024 * 1024),
        name="gdn_core",
    )(a_log.astype(F32), dt_bias.astype(F32), p_all, p_all, p_all, p_all, ab_rows,
      conv_w, conv_w, conv_w, out_norm.reshape(1, GD))


def gdn_gate_rows(ab, n_heads):
    bsz, ltot, _ = ab.shape
    n_chunks = ltot // GC
    n_rows = -(-n_chunks // 8) * 8
    t = jnp.transpose(ab.astype(F32), (0, 2, 1)).reshape(bsz, 4 * n_heads, n_chunks, GC)
    return jnp.pad(t, ((0, 0), (0, 0), (0, n_rows - n_chunks), (0, 0)))


TM = 256
EPI_ROWS = 128
PAIR = 2
FFN_TF = 256
MOD_TN = 512
MOD_SHIFT_M, MOD_SCALE_M, MOD_GATE_M, MOD_SHIFT_F, MOD_SCALE_F, MOD_GATE_F = range(6)
N_MOD = 6
RESIDENT = dict(pipeline_mode=pl.Buffered(1))
VMEM_LIMIT = 56 * 1024 * 1024


def _rms(x, gain):
    return x * lax.rsqrt(jnp.mean(x * x, axis=-1, keepdims=True) + EPS) * gain


def _mod_spec(k, tiles_per_batch, ctx_tiles, d, half=None):
    def index(i, *_):
        t = i if half is None else PAIR * i + half
        seg = jnp.where(t % tiles_per_batch >= ctx_tiles, 1, 0)
        return ((t // tiles_per_batch) * 2 + seg) * N_MOD + k, 0, 0
    return pl.BlockSpec((1, 1, d), index)


def _const_spec(shape):
    return pl.BlockSpec(shape, lambda i, *_: (0,) * len(shape), **RESIDENT)


def _mod_body(c_ref, w_ref, b_ref, o_ref):
    c = c_ref[...]
    s = (c * jax.nn.sigmoid(c)).astype(BF16)
    o_ref[...] = jnp.dot(s, w_ref[...].astype(BF16), preferred_element_type=F32) + b_ref[...]


def modulation(c, c_ctx, mod_w, mod_b):
    bsz, d = c.shape
    n = mod_w.shape[1]
    rows = -(-(bsz + 1) // 8) * 8
    cc = jnp.zeros((rows, d), F32).at[:bsz].set(c).at[bsz].set(c_ctx)
    out = pl.pallas_call(
        _mod_body,
        grid=(n // MOD_TN,),
        in_specs=[pl.BlockSpec((rows, d), lambda j: (0, 0)),
                  pl.BlockSpec((d, MOD_TN), lambda j: (0, j)),
                  pl.BlockSpec((1, MOD_TN), lambda j: (0, j))],
        out_specs=pl.BlockSpec((rows, MOD_TN), lambda j: (0, j)),
        out_shape=jax.ShapeDtypeStruct((rows, n), F32),
        compiler_params=pltpu.CompilerParams(dimension_semantics=("arbitrary",)),
        name="modulation",
    )(cc, mod_w, mod_b.reshape(1, n))
    lat = out[:bsz].reshape(bsz, 1, N_MOD, d)
    ctx = jnp.broadcast_to(out[bsz].reshape(1, 1, N_MOD, d), (bsz, 1, N_MOD, d))
    return jnp.concatenate([ctx, lat], axis=1).reshape(bsz * 2 * N_MOD, 1, d)


def _pre_mm_body(*refs, tiles_per_batch, ctx_tiles, rope_lo, rope_hi, q_hi):
    per_tile = [refs[6 * t:6 * t + 6] for t in range(PAIR)]
    gain_ref, w_ref, o_ref, xo_ref = refs[6 * PAIR:]
    i = pl.program_id(0)
    hs = []
    for t, (xc_ref, xl_ref, sh_ref, sc_ref, _, _) in enumerate(per_tile):
        is_ctx = (PAIR * i + t) % tiles_per_batch < ctx_tiles
        x = jnp.where(is_ctx, xc_ref[...], xl_ref[...])
        xo_ref[t * TM:(t + 1) * TM, :] = x
        hs.append((_rms(x, gain_ref[...]) * (1.0 + sc_ref[0]) + sh_ref[0]).astype(BF16))
    acc = jnp.dot(jnp.concatenate(hs, axis=0), w_ref[...], preferred_element_type=F32)
    lane = lax.broadcasted_iota(I32, (TM, LANES), 1)
    first_half = (lane % (HEAD_DIM // 2)) < (HEAD_DIM // 4)
    for t, (_, _, _, _, cos_ref, sin_ref) in enumerate(per_tile):
        rows = slice(t * TM, (t + 1) * TM)
        for s in range(0, acc.shape[1], LANES):
            blk = acc[rows, s:s + LANES]
            if rope_lo <= s < rope_hi:
                partner = jnp.where(first_half, pltpu.roll(blk, LANES - HEAD_DIM // 4, 1),
                                    pltpu.roll(blk, HEAD_DIM // 4, 1))
                blk = blk * cos_ref[...] + partner * sin_ref[...]
                if s < q_hi:
                    blk = blk * (HEAD_DIM ** -0.5)
            o_ref[rows, s:s + LANES] = blk.astype(o_ref.dtype)


def rope_tables(n_ctx, n_lat):
    half = HEAD_DIM // 2
    inv = ROPE_THETA ** (-jnp.arange(0, half, 2, dtype=F32) / half)
    t = jnp.arange(n_lat)
    pos = jnp.stack([(t // GRID_W).astype(F32), (t % GRID_W).astype(F32)], axis=1)
    lane = jnp.arange(LANES)
    part = (lane % HEAD_DIM) // half
    freq = lane % (half // 2)
    ang = pos[:, part] * inv[freq][None, :]
    sign = jnp.where((lane % half) < half // 2, -1.0, 1.0)
    cos = jnp.concatenate([jnp.ones((n_ctx, LANES), F32), jnp.cos(ang)], axis=0)
    sin = jnp.concatenate([jnp.zeros((n_ctx, LANES), F32), jnp.sin(ang) * sign], axis=0)
    return cos, sin


def pre_mm(ctx, x, gain, mod, w, cos, sin):
    bsz, n_ctx, d = ctx.shape
    n_lat = x.shape[1]
    n = w.shape[1]
    ctx_tiles, lat_tiles = n_ctx // TM, n_lat // TM
    tiles = ctx_tiles + lat_tiles
    r = bsz * tiles * TM
    tmd = PAIR * TM

    def src_spec(half, latent):
        def index(i):
            t = PAIR * i + half
            b, j = t // tiles, t % tiles
            if latent:
                return b * lat_tiles + jnp.clip(j - ctx_tiles, 0, lat_tiles - 1), 0
            return b * ctx_tiles + jnp.minimum(j, ctx_tiles - 1), 0
        return pl.BlockSpec((TM, d), index)

    def table_spec(half):
        return pl.BlockSpec((TM, LANES), lambda i: ((PAIR * i + half) % tiles, 0))

    in_specs, args = [], []
    for t in range(PAIR):
        in_specs += [src_spec(t, False), src_spec(t, True),
                     _mod_spec(MOD_SHIFT_M, tiles, ctx_tiles, d, half=t), _mod_spec(MOD_SCALE_M, tiles, ctx_tiles, d, half=t),
                     table_spec(t), table_spec(t)]
        args += [ctx.reshape(bsz * n_ctx, d), x.reshape(bsz * n_lat, d), mod, mod, cos, sin]
    body = functools.partial(_pre_mm_body, tiles_per_batch=tiles, ctx_tiles=ctx_tiles, rope_lo=POOL_DIM,
                             rope_hi=POOL_DIM + Q_DIM + KV_DIM, q_hi=POOL_DIM + Q_DIM)
    return pl.pallas_call(
        body,
        grid=(r // tmd,),
        in_specs=in_specs + [_const_spec((1, d)), _const_spec((d, n))],
        out_specs=[pl.BlockSpec((tmd, n), lambda i: (i, 0)), pl.BlockSpec((tmd, d), lambda i: (i, 0))],
        out_shape=[jax.ShapeDtypeStruct((r, n), BF16), jax.ShapeDtypeStruct((r, d), F32)],
        compiler_params=pltpu.CompilerParams(dimension_semantics=("arbitrary",), vmem_limit_bytes=VMEM_LIMIT),
        name="l0_norm_proj_rope",
    )(*args, gain.reshape(1, d), w)


def _post_mm_body(*refs, n_a, route):
    a_refs = refs[:n_a]
    w_refs = refs[n_a:2 * n_a]
    x_refs = refs[2 * n_a:2 * n_a + PAIR]
    gpost_ref, gpre_ref = refs[2 * n_a + PAIR:2 * n_a + PAIR + 2]
    mod_refs = refs[2 * n_a + PAIR + 2:2 * n_a + 4 * PAIR + 2]
    rest = refs[2 * n_a + 4 * PAIR + 2:]
    if route:
        rhi_ref, rlo_ref, xo_ref, ho_ref, route_ref, cnt_ref, carry_ref = rest
    else:
        xo_ref, ho_ref = rest
    ys = []
    for s in range(0, PAIR * TM, EPI_ROWS):
        rows = slice(s, s + EPI_ROWS)
        y = jnp.dot(a_refs[0][rows, :], w_refs[0][...], preferred_element_type=F32)
        for a_ref, w_ref in zip(a_refs[1:], w_refs[1:]):
            y = y + jnp.dot(a_ref[rows, :], w_ref[...], preferred_element_type=F32)
        ys.append(y)
    _residual_epilogue(ys, x_refs, gpost_ref, gpre_ref, mod_refs, xo_ref, ho_ref)
    if route:
        _route_rows(ho_ref[...], rhi_ref, rlo_ref, route_ref, cnt_ref, carry_ref)


def _residual_epilogue(ys, x_refs, gpost_ref, gpre_ref, mod_refs, xo_ref, ho_ref):
    for k, y in enumerate(ys):
        s = k * EPI_ROWS
        t, r0 = s // TM, s % TM
        rows = slice(s, s + EPI_ROWS)
        gate_ref, sh_ref, sc_ref = mod_refs[3 * t:3 * t + 3]
        x = x_refs[t][r0:r0 + EPI_ROWS, :] if len(x_refs) == PAIR else x_refs[0][rows, :]
        xn = x + gate_ref[0] * _rms(y, gpost_ref[...])
        xo_ref[rows, :] = xn
        ho_ref[rows, :] = (_rms(xn, gpre_ref[...]) * (1.0 + sc_ref[0]) + sh_ref[0]).astype(ho_ref.dtype)


def _pair_mod_specs(kinds, tiles_per_batch, ctx_tiles, d):
    return [_mod_spec(k, tiles_per_batch, ctx_tiles, d, half=t) for t in range(PAIR) for k in kinds]


def post_mm(a_list, w_list, x, x_tile_offset, g_post, g_pre, mod, tiles_per_batch, ctx_tiles, x_tiles_per_batch,
            h_dtype, name, router=None):
    r = a_list[0].shape[0]
    d = w_list[0].shape[1]
    n_a = len(a_list)
    tmd = PAIR * TM
    route = router is not None
    route_in = [_const_spec((d, LANES)), _const_spec((d, LANES))] if route else []
    route_args = list(router_operands(router, d)) if route else []
    route_out = [pl.BlockSpec((tmd, LANES), lambda i: (i, 0)), pl.BlockSpec((8, LANES), lambda i: (0, 0))] if route else []
    route_shape = [jax.ShapeDtypeStruct((r, LANES), F32), jax.ShapeDtypeStruct((8, LANES), F32)] if route else []

    def x_spec(half):
        def index(i):
            t = PAIR * i + half
            return (t // tiles_per_batch) * x_tiles_per_batch + x_tile_offset + t % tiles_per_batch, 0
        return pl.BlockSpec((TM, d), index)

    in_specs = ([pl.BlockSpec((tmd, a.shape[1]), lambda i: (i, 0)) for a in a_list]
                + [_const_spec(w.shape) for w in w_list]
                + [x_spec(t) for t in range(PAIR)]
                + [_const_spec((1, d)), _const_spec((1, d))]
                + _pair_mod_specs((MOD_GATE_M, MOD_SHIFT_F, MOD_SCALE_F), tiles_per_batch, ctx_tiles, d)
                + route_in)
    return pl.pallas_call(
        functools.partial(_post_mm_body, n_a=n_a, route=route),
        grid=(r // tmd,),
        in_specs=in_specs,
        out_specs=[pl.BlockSpec((tmd, d), lambda i: (i, 0)), pl.BlockSpec((tmd, d), lambda i: (i, 0))] + route_out,
        out_shape=[jax.ShapeDtypeStruct((r, d), F32), jax.ShapeDtypeStruct((r, d), h_dtype)] + route_shape,
        scratch_shapes=[pltpu.VMEM((1, LANES), F32)] if route else [],
        compiler_params=pltpu.CompilerParams(dimension_semantics=("arbitrary",), vmem_limit_bytes=VMEM_LIMIT),
        name=name,
    )(*a_list, *w_list, *([x] * PAIR), g_post.reshape(1, d), g_pre.reshape(1, d), *([mod] * (3 * PAIR)),
      *route_args)


def _out_ffn_body(*refs):
    pooled_ref, attn_ref, wa_ref, wb_ref = refs[:4]
    x_refs = refs[4:4 + PAIR]
    g_mix_post, g_ffn_pre, g_ffn_post, g_next_pre = refs[4 + PAIR:8 + PAIR]
    mix_mods = refs[8 + PAIR:8 + 4 * PAIR]
    ffn_mods = refs[8 + 4 * PAIR:8 + 7 * PAIR]
    w1_ref, w3_ref, w2_ref = refs[8 + 7 * PAIR:11 + 7 * PAIR]
    xo_ref, ho_ref, x1_s, h1_s, acc_ref = refs[11 + 7 * PAIR:]
    blocks = range(0, PAIR * TM, EPI_ROWS)
    ys = [jnp.dot(pooled_ref[s:s + EPI_ROWS, :], wa_ref[...], preferred_element_type=F32)
          + jnp.dot(attn_ref[s:s + EPI_ROWS, :], wb_ref[...], preferred_element_type=F32) for s in blocks]
    _residual_epilogue(ys, x_refs, g_mix_post, g_ffn_pre, mix_mods, x1_s, h1_s)
    h = h1_s[...]
    n_f = w1_ref.shape[1]
    for f0 in range(0, n_f, FFN_TF):
        a = jnp.dot(h, w1_ref[:, f0:f0 + FFN_TF], preferred_element_type=F32)
        b = jnp.dot(h, w3_ref[:, f0:f0 + FFN_TF], preferred_element_type=F32)
        mid = (a * jax.nn.sigmoid(a) * b).astype(BF16)
        part = jnp.dot(mid, w2_ref[f0:f0 + FFN_TF, :], preferred_element_type=F32)
        if f0 == 0:
            acc_ref[...] = part
        else:
            acc_ref[...] += part
    _residual_epilogue([acc_ref[s:s + EPI_ROWS, :] for s in blocks], (x1_s,), g_ffn_post, g_next_pre, ffn_mods,
                       xo_ref, ho_ref)


def out_ffn_layer(pooled, attn, w_out, x, g_mix_post, g_ffn_pre, g_ffn_post, g_next_pre, mod, mod_next,
                  w1, w3, w2, tiles_per_batch, ctx_tiles):
    r, d = x.shape
    tmd = PAIR * TM
    n_pool = pooled.shape[1]
    gate_specs = _pair_mod_specs((MOD_GATE_F,), tiles_per_batch, ctx_tiles, d)
    next_specs = _pair_mod_specs((MOD_SHIFT_M, MOD_SCALE_M), tiles_per_batch, ctx_tiles, d)
    ffn_specs, ffn_args = [], []
    for t in range(PAIR):
        ffn_specs += [gate_specs[t], next_specs[2 * t], next_specs[2 * t + 1]]
        ffn_args += [mod, mod_next, mod_next]
    vec = lambda g: g.reshape(1, d)
    return pl.pallas_call(
        _out_ffn_body,
        grid=(r // tmd,),
        in_specs=([pl.BlockSpec((tmd, n_pool), lambda i: (i, 0)), pl.BlockSpec((tmd, attn.shape[1]), lambda i: (i, 0)),
                   _const_spec((n_pool, d)), _const_spec((w_out.shape[0] - n_pool, d))]
                  + [pl.BlockSpec((TM, d), lambda i, t=t: (PAIR * i + t, 0)) for t in range(PAIR)]
                  + [_const_spec((1, d))] * 4
                  + _pair_mod_specs((MOD_GATE_M, MOD_SHIFT_F, MOD_SCALE_F), tiles_per_batch, ctx_tiles, d)
                  + ffn_specs
                  + [_const_spec(w1.shape), _const_spec(w3.shape), _const_spec(w2.shape)]),
        out_specs=[pl.BlockSpec((tmd, d), lambda i: (i, 0)), pl.BlockSpec((tmd, d), lambda i: (i, 0))],
        out_shape=[jax.ShapeDtypeStruct((r, d), F32), jax.ShapeDtypeStruct((r, d), BF16)],
        scratch_shapes=[pltpu.VMEM((tmd, d), F32), pltpu.VMEM((tmd, d), BF16), pltpu.VMEM((tmd, d), F32)],
        compiler_params=pltpu.CompilerParams(dimension_semantics=("arbitrary",), vmem_limit_bytes=VMEM_LIMIT),
        name="l0_out_ffn",
    )(pooled, attn, w_out[:n_pool], w_out[n_pool:], *([x] * PAIR), vec(g_mix_post), vec(g_ffn_pre), vec(g_ffn_post),
      vec(g_next_pre), *([mod] * (3 * PAIR)), *ffn_args, w1, w3, w2)


def _proj_body(h_ref, w_ref, wg_ref, o_ref, og_ref):
    h = h_ref[...]
    o_ref[...] = jnp.dot(h, w_ref[...], preferred_element_type=F32).astype(o_ref.dtype)
    og_ref[...] = jnp.dot(h, wg_ref[...], preferred_element_type=F32)


def gdn_proj(h, w, w_gate):
    r, d = h.shape
    n = w.shape[1]
    tmd = PAIR * TM
    return pl.pallas_call(
        _proj_body,
        grid=(r // tmd,),
        in_specs=[pl.BlockSpec((tmd, d), lambda i: (i, 0)), _const_spec(w.shape), _const_spec(w_gate.shape)],
        out_specs=[pl.BlockSpec((tmd, n), lambda i: (i, 0)), pl.BlockSpec((tmd, LANES), lambda i: (i, 0))],
        out_shape=[jax.ShapeDtypeStruct((r, n), BF16), jax.ShapeDtypeStruct((r, LANES), F32)],
        compiler_params=pltpu.CompilerParams(dimension_semantics=("arbitrary",), vmem_limit_bytes=VMEM_LIMIT),
        name="l1_proj",
    )(h, w, w_gate)


ATTN_HEADS_PER_UNIT = 1


def _attn_body(sink_ref, q_ref, kc_ref, vc_ref, k0_ref, k1_ref, k2_ref, v0_ref, v1_ref, v2_ref, o_ref,
               *, n_ctx_blocks, n_blocks):
    n = pl.program_id(1)
    rows = ATTN_HEADS_PER_UNIT * BLOCK
    qi = lax.broadcasted_iota(I32, (rows, 3 * BLOCK), 0) % BLOCK
    kj = lax.broadcasted_iota(I32, (rows, 3 * BLOCK), 1)
    first = jnp.where(n > n_ctx_blocks, 0, BLOCK)
    last = jnp.where(n < n_blocks - 1, 3 * BLOCK, 2 * BLOCK)
    last = jnp.where(n >= n_ctx_blocks, last, 0)
    band_ok = (kj >= qi) & (kj <= qi + 2 * WINDOW) & (kj >= first) & (kj < last)
    q = q_ref[0]
    nt = (((1,), (1,)), ((), ()))
    groups = range(N_KV_HEADS)
    hpu = ATTN_HEADS_PER_UNIT
    units = [(g, [g * GQA_GROUP + s + i for i in range(hpu)]) for g in groups for s in range(0, GQA_GROUP, hpu)]
    idx = range(len(units))
    cols = [slice(g * HEAD_DIM, (g + 1) * HEAD_DIM) for g in groups]
    q_u = [jnp.concatenate([q[:, hd * HEAD_DIM:(hd + 1) * HEAD_DIM] for hd in hs], axis=0) for _, hs in units]
    k_c = [kc_ref[0][:, c] for c in cols]
    k_b = [jnp.concatenate([k0_ref[0][:, c], k1_ref[0][:, c], k2_ref[0][:, c]], axis=0) for c in cols]
    ones_b = jnp.ones((3 * BLOCK, HEAD_DIM), BF16)
    ones_c = jnp.ones((kc_ref.shape[1], HEAD_DIM), BF16)
    v_b = [jnp.concatenate([jnp.concatenate([v0_ref[0][:, c], v1_ref[0][:, c], v2_ref[0][:, c]], axis=0), ones_b],
                           axis=1) for c in cols]
    v_c = [jnp.concatenate([vc_ref[0][:, c], ones_c], axis=1) for c in cols]
    s_c = [lax.dot_general(q_u[u], k_c[units[u][0]], nt, preferred_element_type=F32) for u in idx]
    s_b = [lax.dot_general(q_u[u], k_b[units[u][0]], nt, preferred_element_type=F32) for u in idx]
    s_b = [jnp.where(band_ok, s, NEG_INF) for s in s_b]
    sink = [jnp.concatenate([sink_ref[hd] + jnp.zeros((BLOCK, 1), F32) for hd in hs], axis=0) for _, hs in units]
    m = [jnp.maximum(jnp.maximum(jnp.max(s_c[u], axis=-1, keepdims=True), jnp.max(s_b[u], axis=-1, keepdims=True)),
                     sink[u]) for u in idx]
    p_c = [jnp.exp(s_c[u] - m[u]) for u in idx]
    p_b = [jnp.exp(s_b[u] - m[u]) for u in idx]
    o_c = [jnp.dot(p_c[u].astype(BF16), v_c[units[u][0]], preferred_element_type=F32) for u in idx]
    o_b = [jnp.dot(p_b[u].astype(BF16), v_b[units[u][0]], preferred_element_type=F32) for u in idx]
    outs = []
    for u in idx:
        acc = o_c[u] + o_b[u]
        den = acc[:, HEAD_DIM:HEAD_DIM + 1] + jnp.exp(sink[u] - m[u])
        o = acc[:, :HEAD_DIM] / den
        outs += [o[i * BLOCK:(i + 1) * BLOCK] for i in range(hpu)]
    o_ref[0] = jnp.concatenate(outs, axis=1).astype(o_ref.dtype)


def attention(p0, sinks, n_ctx):
    bsz, ltot, _ = p0.shape
    n_blocks = ltot // BLOCK
    n_ctx_blocks = n_ctx // BLOCK
    k_blk = (POOL_DIM + Q_DIM) // KV_DIM
    v_blk = k_blk + 1

    def band(dn, blk):
        return pl.BlockSpec((1, BLOCK, KV_DIM),
                            lambda b, n: (b, jnp.clip(n + dn, n_ctx_blocks, n_blocks - 1), blk))

    return pl.pallas_call(
        functools.partial(_attn_body, n_ctx_blocks=n_ctx_blocks, n_blocks=n_blocks),
        grid=(bsz, n_blocks),
        in_specs=[pl.BlockSpec(memory_space=pltpu.SMEM),
                  pl.BlockSpec((1, BLOCK, Q_DIM), lambda b, n: (b, n, POOL_DIM // Q_DIM)),
                  pl.BlockSpec((1, n_ctx, KV_DIM), lambda b, n: (b, 0, k_blk)),
                  pl.BlockSpec((1, n_ctx, KV_DIM), lambda b, n: (b, 0, v_blk)),
                  band(-1, k_blk), band(0, k_blk), band(1, k_blk),
                  band(-1, v_blk), band(0, v_blk), band(1, v_blk)],
        out_specs=pl.BlockSpec((1, BLOCK, Q_DIM), lambda b, n: (b, n, 0)),
        out_shape=jax.ShapeDtypeStruct((bsz, ltot, Q_DIM), BF16),
        compiler_params=pltpu.CompilerParams(dimension_semantics=("arbitrary", "arbitrary"),
                                             vmem_limit_bytes=VMEM_LIMIT),
        name="l0_attention",
    )(sinks.astype(F32), p0, p0, p0, p0, p0, p0, p0, p0, p0)


POOL_HALO = 16


def _pool_body(up_ref, uc_ref, un_ref, w_ref, scale_ref, o_ref, *, tiles_per_batch, ctx_tiles):
    j = pl.program_id(1)
    seg_lo = jnp.where(j < ctx_tiles, 0, ctx_tiles * TM)
    seg_hi = jnp.where(j < ctx_tiles, ctx_tiles * TM, tiles_per_batch * TM)
    halo = jnp.concatenate([up_ref[0][TM - POOL_HALO:], uc_ref[0], un_ref[0][:POOL_HALO]], axis=0)
    n_h = TM + 2 * POOL_HALO
    t = j * TM + lax.broadcasted_iota(I32, (TM, n_h), 0)
    pos = j * TM - POOL_HALO + lax.broadcasted_iota(I32, (TM, n_h), 1)
    t_col = j * TM + lax.broadcasted_iota(I32, (TM, 1), 0)
    cur = uc_ref[0]
    groups = range(len(POOL_WINDOWS))
    cols = [slice(g * POOL_GROUP_DIM, (g + 1) * POOL_GROUP_DIM) for g in groups]
    windows, cnts = [], []
    for w in POOL_WINDOWS:
        lo = jnp.maximum(t - w // 2, seg_lo)
        hi = jnp.minimum(t + w // 2, seg_hi)
        windows.append(((pos >= lo) & (pos < hi)).astype(BF16))
        cnts.append((jnp.minimum(t_col + w // 2, seg_hi) - jnp.maximum(t_col - w // 2, seg_lo)).astype(F32))
    sums = [jnp.dot(windows[g], halo[:, cols[g]], preferred_element_type=F32) for g in groups]
    deltas = [(sums[g] / cnts[g] - cur[:, cols[g]].astype(F32)).astype(BF16) for g in groups]
    ys = [jnp.dot(deltas[g], w_ref[g], preferred_element_type=F32) for g in groups]
    for g in groups:
        o_ref[0, :, cols[g]] = (ys[g] * scale_ref[:, cols[g]]).astype(o_ref.dtype)


def pool_mixer(p0, pool_w, pool_scale, n_ctx):
    bsz, ltot, _ = p0.shape
    tiles_per_batch = ltot // TM
    ctx_tiles = n_ctx // TM

    def tile(dj):
        return pl.BlockSpec((1, TM, POOL_DIM), lambda b, j: (b, jnp.clip(j + dj, 0, tiles_per_batch - 1), 0))

    return pl.pallas_call(
        functools.partial(_pool_body, tiles_per_batch=tiles_per_batch, ctx_tiles=ctx_tiles),
        grid=(bsz, tiles_per_batch),
        in_specs=[tile(-1), tile(0), tile(1),
                  pl.BlockSpec(pool_w.shape, lambda b, j: (0, 0, 0)),
                  pl.BlockSpec((1, POOL_DIM), lambda b, j: (0, 0))],
        out_specs=pl.BlockSpec((1, TM, POOL_DIM), lambda b, j: (b, j, 0)),
        out_shape=jax.ShapeDtypeStruct((bsz, ltot, POOL_DIM), BF16),
        compiler_params=pltpu.CompilerParams(dimension_semantics=("arbitrary", "arbitrary")),
        name="l0_pool",
    )(p0, p0, p0, pool_w.astype(BF16), pool_scale.reshape(1, POOL_DIM).astype(F32))


def kernel(x, c, ctx, c_ctx, l0_mod_w, l0_mod_b, l0_mix_pre, l0_mix_post, l0_ffn_pre, l0_ffn_post, l0_w_in, l0_pool_w, l0_pool_scale, l0_sinks, l0_w_out, l0_ffn_w1, l0_ffn_w3, l0_ffn_w2, l1_mod_w, l1_mod_b, l1_mix_pre, l1_mix_post, l1_ffn_pre, l1_ffn_post, l1_w_in, l1_conv_w, l1_a_log, l1_dt_bias, l1_out_norm, l1_w_out, l1_router, l1_moe_w1, l1_moe_w3, l1_moe_w2):
    bsz, n_lat, d = x.shape
    n_ctx = ctx.shape[1]
    ltot = n_ctx + n_lat
    assert n_ctx % TM == 0 and n_lat % TM == 0 and n_ctx % GC == 0
    assert (bsz * ltot) % (PAIR * TM) == 0 and (bsz * n_lat) % (PAIR * TM) == 0
    tiles = ltot // TM
    ctx_tiles = n_ctx // TM
    lat_tiles = n_lat // TM
    bf = lambda w: w.astype(BF16)

    mod0 = modulation(c, c_ctx, l0_mod_w, l0_mod_b)
    mod1 = modulation(c, c_ctx, l1_mod_w, l1_mod_b)
    cos, sin = rope_tables(n_ctx, n_lat)

    p0, x_all = pre_mm(ctx, x, l0_mix_pre, mod0, bf(l0_w_in), cos, sin)
    p0 = p0.reshape(bsz, ltot, -1)
    attn = attention(p0, l0_sinks, n_ctx).reshape(bsz * ltot, Q_DIM)
    pooled = pool_mixer(p0, l0_pool_w, l0_pool_scale, n_ctx).reshape(bsz * ltot, POOL_DIM)
    x2, h2 = out_ffn_layer(pooled, attn, bf(l0_w_out), x_all, l0_mix_post, l0_ffn_pre, l0_ffn_post, l1_mix_pre,
                           mod0, mod1, bf(l0_ffn_w1), bf(l0_ffn_w3), bf(l0_ffn_w2), tiles, ctx_tiles)

    n_heads = l1_a_log.shape[1]
    qkvz = 4 * n_heads * GD
    w_in1 = bf(l1_w_in)
    w_gate = jnp.zeros((d, LANES), BF16).at[:, :4 * n_heads].set(w_in1[:, qkvz:])
    p1, ab = gdn_proj(h2, w_in1[:, :qkvz], w_gate)
    ab_rows = gdn_gate_rows(ab.reshape(bsz, ltot, LANES)[..., :4 * n_heads], n_heads)
    y = gdn_core(p1.reshape(bsz, ltot, qkvz), ab_rows, l1_conv_w, l1_a_log, l1_dt_bias, l1_out_norm, n_ctx)
    x3, h3, route, cnt = post_mm([y.reshape(bsz * n_lat, n_heads * GD)], [bf(l1_w_out)], x2, ctx_tiles, l1_mix_post,
                                 l1_ffn_pre, mod1, lat_tiles, 0, tiles, F32, "l1_out_proj_route", router=l1_router)
    out = moe_layer(h3, route, cnt, l1_moe_w1, l1_moe_w3, l1_moe_w2, x3, l1_ffn_post, mod1, n_lat)
    return out.reshape(bsz, n_lat, d)
```

```python
import functools
import math

import jax
import jax.numpy as jnp
from jax import lax
from jax.experimental import pallas as pl
from jax.experimental.pallas import tpu as pltpu

F32 = jnp.float32
BF16 = jnp.bfloat16
I32 = jnp.int32

LANES = 128
D_MODEL = 1024
GRID_W = 64
EPS = 1e-6
NEG_INF = -1e30

POOL_GROUPS = 4
POOL_GROUP_DIM = 128
POOL_DIM = POOL_GROUPS * POOL_GROUP_DIM
POOL_WINDOWS = (2, 4, 8, 16)
HEAD_DIM = 64
N_HEADS = 8
N_KV_HEADS = 2
GQA_GROUP = N_HEADS // N_KV_HEADS
Q_DIM = N_HEADS * HEAD_DIM
KV_DIM = N_KV_HEADS * HEAD_DIM
WINDOW = 128
BLOCK = 128
ROPE_THETA = 10000.0

GDN_HEADS = 8
GDN_HEAD_DIM = 128
GDN_DIM = GDN_HEADS * GDN_HEAD_DIM
CONV_K = 5
CHUNK = 64

N_EXPERTS = 8
TOP_K = 2

MOE_TM = 512
MOE_TF = 1792
ROW_TILE = 512
ISSUE_UNROLL = 8


def _split_bf16(a):
    hi = a.astype(BF16)
    lo = (a - hi.astype(F32)).astype(BF16)
    return hi, lo


def _route_rows(h, rhi_ref, rlo_ref, route_ref, cnt_ref, carry_ref):
    i = pl.program_id(0)
    tr = h.shape[0]

    @pl.when(i == 0)
    def _():
        carry_ref[...] = jnp.zeros_like(carry_ref)

    h_hi, h_lo = _split_bf16(h)
    r_hi = rhi_ref[...]
    r_lo = rlo_ref[...]
    logits = (jnp.dot(h_hi, r_hi, preferred_element_type=F32)
              + jnp.dot(h_hi, r_lo, preferred_element_type=F32)
              + jnp.dot(h_lo, r_hi, preferred_element_type=F32))
    lane = lax.broadcasted_iota(I32, (tr, LANES), 1)
    logits = jnp.where(lane < N_EXPERTS, logits, -jnp.inf)
    m1 = jnp.max(logits, axis=-1, keepdims=True)
    i1 = jnp.min(jnp.where(logits == m1, lane, LANES), axis=-1, keepdims=True)
    rest = jnp.where(lane == i1, -jnp.inf, logits)
    m2 = jnp.max(rest, axis=-1, keepdims=True)
    i2 = jnp.min(jnp.where(rest == m2, lane, LANES), axis=-1, keepdims=True)
    e2 = jnp.exp(m2 - m1)
    g1 = 1.0 / (1.0 + e2)
    g2 = e2 / (1.0 + e2)

    onehot = ((lane == i1) | (lane == i2)).astype(F32)
    row = lax.broadcasted_iota(I32, (tr, tr), 0)
    col = lax.broadcasted_iota(I32, (tr, tr), 1)
    strict = (row > col).astype(BF16)
    before = jnp.dot(strict, onehot.astype(BF16), preferred_element_type=F32) + carry_ref[...]
    rank1 = jnp.sum(jnp.where(lane == i1, before, 0.0), axis=-1, keepdims=True)
    rank2 = jnp.sum(jnp.where(lane == i2, before, 0.0), axis=-1, keepdims=True)
    carry_ref[...] += jnp.sum(onehot, axis=0, keepdims=True)

    packed = jnp.where(lane == 0, i1.astype(F32), 0.0)
    packed = jnp.where(lane == 1, i2.astype(F32), packed)
    packed = jnp.where(lane == 2, g1, packed)
    packed = jnp.where(lane == 3, g2, packed)
    packed = jnp.where(lane == 4, rank1, packed)
    packed = jnp.where(lane == 5, rank2, packed)
    route_ref[...] = packed
    cnt_ref[...] = jnp.broadcast_to(carry_ref[...], cnt_ref.shape)


def router_operands(router, d):
    r_pad = jnp.zeros((d, LANES), F32).at[:, :N_EXPERTS].set(router.astype(F32))
    return _split_bf16(r_pad)


def _row_copy(src_ref, src_row, dst_ref, dst_row, sem):
    return pltpu.make_async_copy(src_ref.at[pl.ds(src_row, 1)], dst_ref.at[pl.ds(dst_row, 1)], sem)


def _scatter_body(pos_ref, h_ref, xs_in_ref, xs_ref, sem, *, ts):
    del xs_in_ref

    def issue(r, c):
        for k in range(TOP_K):
            _row_copy(h_ref, r, xs_ref, pos_ref[0, 0, TOP_K * r + k], sem).start(priority=k)
        return c

    lax.fori_loop(0, ts, issue, 0, unroll=ISSUE_UNROLL)
    for _ in range(TOP_K):
        pltpu.make_async_copy(h_ref, xs_ref.at[pl.ds(0, ts)], sem).wait()


def moe_scatter(h, pos, p_rows):
    t, d = h.shape
    ts = min(ROW_TILE, t)
    nt = t // ts
    xs0 = jnp.zeros((p_rows, d), h.dtype)
    return pl.pallas_call(
        functools.partial(_scatter_body, ts=ts),
        grid=(nt,),
        in_specs=[
            pl.BlockSpec((1, 1, TOP_K * ts), lambda i: (i, 0, 0), memory_space=pltpu.SMEM),
            pl.BlockSpec((ts, d), lambda i: (i, 0)),
            pl.BlockSpec(memory_space=pl.ANY),
        ],
        out_specs=pl.BlockSpec(memory_space=pl.ANY),
        out_shape=jax.ShapeDtypeStruct((p_rows, d), h.dtype),
        scratch_shapes=[pltpu.SemaphoreType.DMA(())],
        input_output_aliases={2: 0},
        compiler_params=pltpu.CompilerParams(dimension_semantics=("arbitrary",), disable_bounds_checks=True),
        name="moe_scatter",
    )(pos.reshape(nt, 1, TOP_K * ts), h, xs0)


def _expert_body(te_ref, nu_ref, x_ref, w1_ref, w3_ref, w2_ref, o_ref):
    i = pl.program_id(0)
    j = pl.program_id(1)

    @pl.when(j == 0)
    def _():
        o_ref[...] = jnp.zeros_like(o_ref)

    @pl.when(i < nu_ref[0])
    def _():
        x = x_ref[...].astype(BF16)
        a = jnp.dot(x, w1_ref[0], preferred_element_type=F32)
        b = jnp.dot(x, w3_ref[0], preferred_element_type=F32)
        mid = (a * jax.nn.sigmoid(a) * b).astype(BF16)
        o_ref[...] += jnp.dot(mid, w2_ref[0].astype(BF16), preferred_element_type=F32)


def moe_experts(xs, w1, w3, w2, tile_expert, n_used, tm):
    p_rows, d = xs.shape
    n_exp, _, d_exp = w1.shape
    tf = MOE_TF if d_exp % MOE_TF == 0 else d_exp
    nj = d_exp // tf
    n_tiles = p_rows // tm

    def jj(i, j, nu):
        return jnp.where(i < nu[0], j, nj - 1)

    grid_spec = pltpu.PrefetchScalarGridSpec(
        num_scalar_prefetch=2,
        grid=(n_tiles, nj),
        in_specs=[
            pl.BlockSpec((tm, d), lambda i, j, te, nu: (i, 0)),
            pl.BlockSpec((1, d, tf), lambda i, j, te, nu: (te[i], 0, jj(i, j, nu))),
            pl.BlockSpec((1, d, tf), lambda i, j, te, nu: (te[i], 0, jj(i, j, nu))),
            pl.BlockSpec((1, tf, d), lambda i, j, te, nu: (te[i], jj(i, j, nu), 0)),
        ],
        out_specs=pl.BlockSpec((tm, d), lambda i, j, te, nu: (i, 0)),
    )
    return pl.pallas_call(
        _expert_body,
        grid_spec=grid_spec,
        out_shape=jax.ShapeDtypeStruct((p_rows, d), F32),
        compiler_params=pltpu.CompilerParams(
            dimension_semantics=("arbitrary", "arbitrary"),
            vmem_limit_bytes=56 * 1024 * 1024),
        name="moe_experts",
    )(tile_expert, n_used, xs, w1, w3, w2)


def _combine_body(pos_ref, pos_next_ref, gate_ref, ys_ref, x_ref, gain_ref, mgate_ref, y_ref, buf, sem, *, ts):
    i = pl.program_id(0)
    slot = i % 2

    def gather(p_ref, s):
        def issue(r, c):
            for k in range(TOP_K):
                _row_copy(ys_ref, p_ref[0, 0, TOP_K * r + k], buf.at[s, k], r, sem.at[s]).start(priority=k)
            return c
        lax.fori_loop(0, ts, issue, 0, unroll=ISSUE_UNROLL)

    @pl.when(i == 0)
    def _():
        gather(pos_ref, slot)

    @pl.when(i + 1 < pl.num_programs(0))
    def _():
        gather(pos_next_ref, 1 - slot)

    for k in range(TOP_K):
        pltpu.make_async_copy(ys_ref.at[pl.ds(0, ts)], buf.at[slot, k], sem.at[slot]).wait()
    g = gate_ref[...]
    y = g[:, 2:3] * buf[slot, 0] + g[:, 3:4] * buf[slot, 1]
    y_ref[...] = x_ref[...] + mgate_ref[0] * _rms(y, gain_ref[...])


def moe_combine(ys, pos, route, x, gain, mod, rows_per_batch):
    t = route.shape[0]
    d = ys.shape[1]
    ts = min(ROW_TILE, rows_per_batch)
    assert rows_per_batch % ts == 0
    tiles_per_batch = rows_per_batch // ts
    nt = t // ts
    pos3 = pos.reshape(nt, 1, TOP_K * ts)
    return pl.pallas_call(
        functools.partial(_combine_body, ts=ts),
        grid=(nt,),
        in_specs=[
            pl.BlockSpec((1, 1, TOP_K * ts), lambda i: (i, 0, 0), memory_space=pltpu.SMEM),
            pl.BlockSpec((1, 1, TOP_K * ts), lambda i: (jnp.minimum(i + 1, nt - 1), 0, 0), memory_space=pltpu.SMEM),
            pl.BlockSpec((ts, LANES), lambda i: (i, 0)),
            pl.BlockSpec(memory_space=pl.ANY),
            pl.BlockSpec((ts, d), lambda i: (i, 0)),
            pl.BlockSpec((1, d), lambda i: (0, 0)),
            _mod_spec(MOD_GATE_F, tiles_per_batch, 0, d),
        ],
        out_specs=pl.BlockSpec((ts, d), lambda i: (i, 0)),
        out_shape=jax.ShapeDtypeStruct((t, d), F32),
        scratch_shapes=[pltpu.VMEM((2, TOP_K, ts, d), F32), pltpu.SemaphoreType.DMA((2,))],
        compiler_params=pltpu.CompilerParams(dimension_semantics=("arbitrary",), disable_bounds_checks=True),
        name="moe_combine",
    )(pos3, pos3, route, ys, x, gain.reshape(1, d), mod)


def moe_layer(tok, route, cnt, w1, w3, w2, x, gain, mod, rows_per_batch):
    t = tok.shape[0]
    tm = min(MOE_TM, t)

    counts = cnt[0, :N_EXPERTS].astype(I32)
    padded = ((counts + tm - 1) // tm) * tm
    ends = jnp.cumsum(padded)
    starts = ends - padded
    expert = route[:, 0:TOP_K].astype(I32)
    rank = route[:, 4:4 + TOP_K].astype(I32)
    start_of = jnp.sum(jnp.where(expert[..., None] == jnp.arange(N_EXPERTS), starts, 0), axis=-1)
    pos = start_of + rank

    n_tiles = (TOP_K * t) // tm + N_EXPERTS
    n_used = (ends[-1] // tm).astype(I32)
    tile_idx = jnp.minimum(jnp.arange(n_tiles, dtype=I32), n_used - 1)
    tile_expert = jnp.sum(tile_idx[:, None] * tm >= ends[None, :], axis=-1).astype(I32)
    tile_expert = jnp.minimum(tile_expert, N_EXPERTS - 1)

    xs = moe_scatter(tok, pos, n_tiles * tm)
    ys = moe_experts(xs, w1.astype(BF16), w3.astype(BF16), w2, tile_expert, n_used.reshape(1), tm)
    return moe_combine(ys, pos, route, x, gain, mod, rows_per_batch)


GC = 128
GD = GDN_HEAD_DIM
PAD = 8
GDN_UNROLL = 9
GDN_BASE = 16
N_MERGE = 3
M_INCL_F, M_STRICT_F, M_INCL_B, M_STRICT_B, M_EYE, M_BASE, M_OFF0 = 0, 1, 2, 3, 4, 5, 6
N_MASKS = M_OFF0 + N_MERGE


def _bdot(a, b):
    return jnp.dot(a.astype(BF16), b.astype(BF16), preferred_element_type=F32)


def _bdot_nt(a, b):
    return lax.dot_general(a.astype(BF16), b.astype(BF16), (((1,), (1,)), ((), ())), preferred_element_type=F32)


def _gdn_body(alog_ref, dtb_ref, q_ref, k_ref, v_ref, z_ref, ab_ref, cq_ref, ck_ref, cv_ref, onorm_ref,
              o_ref, pq_s, pk_s, pv_s, o_s, b_s, gq_s, gate_s, state_s, mask_s,
              *, n_ctx_chunks, n_chunks, n_heads, unroll):
    h = pl.program_id(1)
    ltot = n_chunks * GC
    lc = n_ctx_chunks * GC

    row = lax.broadcasted_iota(I32, (GC, GC), 0)
    col = lax.broadcasted_iota(I32, (GC, GC), 1)
    mask_s[M_INCL_F] = (row >= col).astype(F32)
    mask_s[M_STRICT_F] = (row > col).astype(F32)
    mask_s[M_INCL_B] = (row <= col).astype(F32)
    mask_s[M_STRICT_B] = (row < col).astype(F32)
    mask_s[M_EYE] = (row == col).astype(F32)
    mask_s[M_BASE] = ((row // GDN_BASE) == (col // GDN_BASE)).astype(F32)
    for lvl in range(N_MERGE):
        s = GDN_BASE << lvl
        mask_s[M_OFF0 + lvl] = (((row // (2 * s)) == (col // (2 * s))) & ((row // s) != (col // s))).astype(F32)

    for src_ref, dst in ((q_ref, pq_s), (k_ref, pk_s), (v_ref, pv_s)):
        for off in (0, PAD + lc, 2 * PAD + ltot):
            dst[pl.ds(off, PAD), :] = jnp.zeros((PAD, GD), F32)
        dst[pl.ds(PAD, lc), :] = src_ref[0, pl.ds(0, lc), :].astype(F32)
        dst[pl.ds(2 * PAD + lc, ltot - lc), :] = src_ref[0, pl.ds(lc, ltot - lc), :].astype(F32)

    for d in range(2):
        a = ab_ref[0, d * 2 * n_heads + h]
        b = ab_ref[0, d * 2 * n_heads + n_heads + h]
        xa = a + dtb_ref[d, h]
        softplus = jnp.maximum(xa, 0.0) + jnp.log1p(jnp.exp(-jnp.abs(xa)))
        g = -jnp.exp(alog_ref[d, h]) * softplus
        beta = 1.0 / (1.0 + jnp.exp(-b))
        tri = mask_s[M_INCL_B if d == 0 else M_INCL_F].astype(BF16)
        g_hi = g.astype(BF16)
        g_lo = (g - g_hi.astype(F32)).astype(BF16)
        gam = (jnp.dot(g_hi, tri, preferred_element_type=F32) + jnp.dot(g_lo, tri, preferred_element_type=F32))
        gate_s[d, 0] = gam
        gate_s[d, 1] = beta
        gate_s[d, 2] = jnp.broadcast_to(jnp.sum(g, axis=-1, keepdims=True), g.shape)

    def conv_chunk(c, pad_ref, w_ref, normalise, scale):
        start = pl.multiple_of(c * GC + jnp.where(c >= n_ctx_chunks, PAD, 0), PAD)
        acc = jnp.zeros((GC, GD), F32)
        for j in range(CONV_K):
            tap = pad_ref[pl.ds(start + (PAD - CONV_K // 2 + j), GC), :]
            acc = acc + tap * w_ref[j:j + 1, :].astype(F32)
        y = acc * jax.nn.sigmoid(acc)
        if normalise:
            y = y * (lax.rsqrt(jnp.sum(y * y, axis=-1, keepdims=True) + EPS) * scale)
        return y

    def group_chunks(i):
        return [jnp.minimum(i * unroll + u, n_chunks - 1) for u in range(unroll)]

    def prep(i, carry):
        chunks = group_chunks(i)
        qs = [conv_chunk(c, pq_s, cq_ref, True, GD ** -0.5) for c in chunks]
        ks = [conv_chunk(c, pk_s, ck_ref, True, 1.0) for c in chunks]
        vs = [conv_chunk(c, pv_s, cv_ref, False, 1.0) for c in chunks]
        kqs = [_bdot_nt(jnp.concatenate([k, q], axis=0), k) for k, q in zip(ks, qs)]
        kts = [k.T for k in ks]
        lanes = [(u, d) for u in range(unroll) for d in range(2)]
        lms, dbs, e_cols, kdts = [], [], [], []
        for u, d in lanes:
            c = chunks[u]
            gam_row = gate_s[d, 0, pl.ds(c, 1), :]
            beta_row = gate_s[d, 1, pl.ds(c, 1), :]
            tot_row = gate_s[d, 2, pl.ds(c, 1), :]
            gam_col = jnp.sum(mask_s[M_EYE] * gam_row, axis=1, keepdims=True)
            db = jnp.exp((gam_col - gam_row) * mask_s[M_INCL_F if d == 0 else M_INCL_B]) * beta_row
            dbs.append(db)
            lms.append(kqs[u][:GC] * db * mask_s[M_STRICT_F if d == 0 else M_STRICT_B])
            e_cols.append(jnp.exp(gam_col))
            kdts.append(kts[u] * (jnp.exp(tot_row - gam_row) * beta_row))
        xs = [-lm * mask_s[M_BASE] for lm in lms]
        ts = [mask_s[M_EYE] + x for x in xs]
        for _ in range(3):
            xs = [_bdot(x, x) for x in xs]
            ts = [t + _bdot(t, x) for t, x in zip(ts, xs)]
        for lvl in range(N_MERGE):
            ys = [_bdot(lm * mask_s[M_OFF0 + lvl], t) for lm, t in zip(lms, ts)]
            ts = [t - _bdot(t, y) for t, y in zip(ts, ys)]
        wus = [_bdot(t, jnp.concatenate([ks[u] * e, vs[u]], axis=1))
               for t, e, (u, d) in zip(ts, e_cols, lanes)]
        x12s = []
        for wu, db, kdt, (u, d) in zip(wus, dbs, kdts, lanes):
            qkb = kqs[u][GC:] * db * mask_s[M_INCL_F if d == 0 else M_INCL_B]
            x12s.append(_bdot(jnp.concatenate([kdt, qkb], axis=0), wu))
        for x12, e, (u, d) in zip(x12s, e_cols, lanes):
            t0 = pl.multiple_of(chunks[u] * GC, GC)
            b_s[d, pl.ds(t0, GC), :] = x12[:GD, GD:]
            o_s[d, pl.ds(t0, GC), :] = x12[GD:, GD:]
            q_eff = qs[u] * e - x12[GD:, :GD]
            gq_s[d, pl.ds(pl.multiple_of(2 * t0, 2 * GC), 2 * GC), :] = (
                jnp.concatenate([x12[:GD, :GD], q_eff], axis=0).astype(BF16))
        return carry

    lax.fori_loop(0, -(-n_chunks // unroll), prep, 0)

    state_s[...] = jnp.zeros_like(state_s)

    def scan(s, carry):
        for d in range(2):
            if d == 0:
                c = s
            else:
                c = jnp.where(s < n_ctx_chunks, n_ctx_chunks - 1 - s, n_chunks - 1 - (s - n_ctx_chunks))
            t0 = pl.multiple_of(c * GC, GC)
            st = state_s[d]
            r = jnp.dot(gq_s[d, pl.ds(pl.multiple_of(2 * t0, 2 * GC), 2 * GC), :], st.astype(BF16),
                        preferred_element_type=F32)
            cd = jnp.exp(gate_s[d, 2, pl.ds(c, 1), 0:1])
            state_s[d] = st * cd - r[:GD] + b_s[d, pl.ds(t0, GC), :]
            o_s[d, pl.ds(t0, GC), :] += r[GD:]
        return carry

    lax.fori_loop(0, n_chunks, scan, 0)

    gain = onorm_ref[...].astype(F32)

    def finish(c, carry):
        t0 = pl.multiple_of(lc + c * GC, GC)
        o = o_s[0, pl.ds(t0, GC), :] + o_s[1, pl.ds(t0, GC), :]
        z = z_ref[0, pl.ds(t0, GC), :].astype(F32)
        y = o * lax.rsqrt(jnp.mean(o * o, axis=-1, keepdims=True) + EPS) * gain
        y = y * (z * jax.nn.sigmoid(z))
        o_ref[0, pl.ds(pl.multiple_of(c * GC, GC), GC), :] = y.astype(o_ref.dtype)
        return carry

    lax.fori_loop(0, n_chunks - n_ctx_chunks, finish, 0, unroll=4)


def gdn_core(p_all, ab_rows, conv_w, a_log, dt_bias, out_norm, n_ctx):
    bsz, ltot, _ = p_all.shape
    n_heads = a_log.shape[1]
    n_chunks = ltot // GC
    n_ctx_chunks = n_ctx // GC
    lat = ltot - n_ctx
    n_rows = ab_rows.shape[2]
    unroll = min(GDN_UNROLL, n_chunks)
    body = functools.partial(_gdn_body, n_ctx_chunks=n_ctx_chunks, n_chunks=n_chunks, n_heads=n_heads,
                             unroll=unroll)
    smem = pl.BlockSpec(memory_space=pltpu.SMEM)

    def col(off):
        return pl.BlockSpec((1, ltot, GD), lambda b, h: (b, 0, off * n_heads + h))

    def cw(off):
        return pl.BlockSpec((CONV_K, GD), lambda b, h: (0, off * n_heads + h))

    return pl.pallas_call(
        body,
        grid=(bsz, n_heads),
        in_specs=[smem, smem, col(0), col(1), col(2), col(3),
                  pl.BlockSpec((1, 4 * n_heads, n_rows, GC), lambda b, h: (b, 0, 0, 0)),
                  cw(0), cw(1), cw(2),
                  pl.BlockSpec((1, GD), lambda b, h: (0, 0))],
        out_specs=pl.BlockSpec((1, lat, GD), lambda b, h: (b, 0, h)),
        out_shape=jax.ShapeDtypeStruct((bsz, lat, n_heads * GD), BF16),
        scratch_shapes=[
            pltpu.VMEM((ltot + 3 * PAD, GD), F32),
            pltpu.VMEM((ltot + 3 * PAD, GD), F32),
            pltpu.VMEM((ltot + 3 * PAD, GD), F32),
            pltpu.VMEM((2, ltot, GD), F32),
            pltpu.VMEM((2, ltot, GD), F32),
            pltpu.VMEM((2, 2 * ltot, GD), BF16),
            pltpu.VMEM((2, 3, n_rows, GC), F32),
            pltpu.VMEM((2, GD, GD), F32),
            pltpu.VMEM((N_MASKS, GC, GC), F32),
        ],
        compiler_params=pltpu.CompilerParams(
            dimension_semantics=("arbitrary", "arbitrary"),
            vmem_limit_bytes=56 * 1024 * 1024),
        name="gdn_core",
    )(a_log.astype(F32), dt_bias.astype(F32), p_all, p_all, p_all, p_all, ab_rows,
      conv_w, conv_w, conv_w, out_norm.reshape(1, GD))


def gdn_gate_rows(ab, n_heads):
    bsz, ltot, _ = ab.shape
    n_chunks = ltot // GC
    n_rows = -(-n_chunks // 8) * 8
    t = jnp.transpose(ab.astype(F32), (0, 2, 1)).reshape(bsz, 4 * n_heads, n_chunks, GC)
    return jnp.pad(t, ((0, 0), (0, 0), (0, n_rows - n_chunks), (0, 0)))


TM = 256
EPI_ROWS = 128
PAIR = 2
FFN_TF = 256
MOD_TN = 512
MOD_SHIFT_M, MOD_SCALE_M, MOD_GATE_M, MOD_SHIFT_F, MOD_SCALE_F, MOD_GATE_F = range(6)
N_MOD = 6
RESIDENT = dict(pipeline_mode=pl.Buffered(1))
VMEM_LIMIT = 56 * 1024 * 1024


def _rms(x, gain):
    return x * lax.rsqrt(jnp.mean(x * x, axis=-1, keepdims=True) + EPS) * gain


def _mod_spec(k, tiles_per_batch, ctx_tiles, d, half=None):
    def index(i, *_):
        t = i if half is None else PAIR * i + half
        seg = jnp.where(t % tiles_per_batch >= ctx_tiles, 1, 0)
        return ((t // tiles_per_batch) * 2 + seg) * N_MOD + k, 0, 0
    return pl.BlockSpec((1, 1, d), index)


def _const_spec(shape):
    return pl.BlockSpec(shape, lambda i, *_: (0,) * len(shape), **RESIDENT)


def _mod_body(c_ref, w_ref, b_ref, o_ref):
    c = c_ref[...]
    s = (c * jax.nn.sigmoid(c)).astype(BF16)
    o_ref[...] = jnp.dot(s, w_ref[...].astype(BF16), preferred_element_type=F32) + b_ref[...]


def modulation(c, c_ctx, mod_w, mod_b):
    bsz, d = c.shape
    n = mod_w.shape[1]
    rows = -(-(bsz + 1) // 8) * 8
    cc = jnp.zeros((rows, d), F32).at[:bsz].set(c).at[bsz].set(c_ctx)
    out = pl.pallas_call(
        _mod_body,
        grid=(n // MOD_TN,),
        in_specs=[pl.BlockSpec((rows, d), lambda j: (0, 0)),
                  pl.BlockSpec((d, MOD_TN), lambda j: (0, j)),
                  pl.BlockSpec((1, MOD_TN), lambda j: (0, j))],
        out_specs=pl.BlockSpec((rows, MOD_TN), lambda j: (0, j)),
        out_shape=jax.ShapeDtypeStruct((rows, n), F32),
        compiler_params=pltpu.CompilerParams(dimension_semantics=("arbitrary",)),
        name="modulation",
    )(cc, mod_w, mod_b.reshape(1, n))
    lat = out[:bsz].reshape(bsz, 1, N_MOD, d)
    ctx = jnp.broadcast_to(out[bsz].reshape(1, 1, N_MOD, d), (bsz, 1, N_MOD, d))
    return jnp.concatenate([ctx, lat], axis=1).reshape(bsz * 2 * N_MOD, 1, d)


def _pre_mm_body(*refs, tiles_per_batch, ctx_tiles, rope_lo, rope_hi, q_hi):
    per_tile = [refs[6 * t:6 * t + 6] for t in range(PAIR)]
    gain_ref, w_ref, o_ref, xo_ref = refs[6 * PAIR:]
    i = pl.program_id(0)
    hs = []
    for t, (xc_ref, xl_ref, sh_ref, sc_ref, _, _) in enumerate(per_tile):
        is_ctx = (PAIR * i + t) % tiles_per_batch < ctx_tiles
        x = jnp.where(is_ctx, xc_ref[...], xl_ref[...])
        xo_ref[t * TM:(t + 1) * TM, :] = x
        hs.append((_rms(x, gain_ref[...]) * (1.0 + sc_ref[0]) + sh_ref[0]).astype(BF16))
    acc = jnp.dot(jnp.concatenate(hs, axis=0), w_ref[...], preferred_element_type=F32)
    lane = lax.broadcasted_iota(I32, (TM, LANES), 1)
    first_half = (lane % (HEAD_DIM // 2)) < (HEAD_DIM // 4)
    for t, (_, _, _, _, cos_ref, sin_ref) in enumerate(per_tile):
        rows = slice(t * TM, (t + 1) * TM)
        for s in range(0, acc.shape[1], LANES):
            blk = acc[rows, s:s + LANES]
            if rope_lo <= s < rope_hi:
                partner = jnp.where(first_half, pltpu.roll(blk, LANES - HEAD_DIM // 4, 1),
                                    pltpu.roll(blk, HEAD_DIM // 4, 1))
                blk = blk * cos_ref[...] + partner * sin_ref[...]
                if s < q_hi:
                    blk = blk * (HEAD_DIM ** -0.5)
            o_ref[rows, s:s + LANES] = blk.astype(o_ref.dtype)


def rope_tables(n_ctx, n_lat):
    half = HEAD_DIM // 2
    inv = ROPE_THETA ** (-jnp.arange(0, half, 2, dtype=F32) / half)
    t = jnp.arange(n_lat)
    pos = jnp.stack([(t // GRID_W).astype(F32), (t % GRID_W).astype(F32)], axis=1)
    lane = jnp.arange(LANES)
    part = (lane % HEAD_DIM) // half
    freq = lane % (half // 2)
    ang = pos[:, part] * inv[freq][None, :]
    sign = jnp.where((lane % half) < half // 2, -1.0, 1.0)
    cos = jnp.concatenate([jnp.ones((n_ctx, LANES), F32), jnp.cos(ang)], axis=0)
    sin = jnp.concatenate([jnp.zeros((n_ctx, LANES), F32), jnp.sin(ang) * sign], axis=0)
    return cos, sin


def pre_mm(ctx, x, gain, mod, w, cos, sin):
    bsz, n_ctx, d = ctx.shape
    n_lat = x.shape[1]
    n = w.shape[1]
    ctx_tiles, lat_tiles = n_ctx // TM, n_lat // TM
    tiles = ctx_tiles + lat_tiles
    r = bsz * tiles * TM
    tmd = PAIR * TM

    def src_spec(half, latent):
        def index(i):
            t = PAIR * i + half
            b, j = t // tiles, t % tiles
            if latent:
                return b * lat_tiles + jnp.clip(j - ctx_tiles, 0, lat_tiles - 1), 0
            return b * ctx_tiles + jnp.minimum(j, ctx_tiles - 1), 0
        return pl.BlockSpec((TM, d), index)

    def table_spec(half):
        return pl.BlockSpec((TM, LANES), lambda i: ((PAIR * i + half) % tiles, 0))

    in_specs, args = [], []
    for t in range(PAIR):
        in_specs += [src_spec(t, False), src_spec(t, True),
                     _mod_spec(MOD_SHIFT_M, tiles, ctx_tiles, d, half=t), _mod_spec(MOD_SCALE_M, tiles, ctx_tiles, d, half=t),
                     table_spec(t), table_spec(t)]
        args += [ctx.reshape(bsz * n_ctx, d), x.reshape(bsz * n_lat, d), mod, mod, cos, sin]
    body = functools.partial(_pre_mm_body, tiles_per_batch=tiles, ctx_tiles=ctx_tiles, rope_lo=POOL_DIM,
                             rope_hi=POOL_DIM + Q_DIM + KV_DIM, q_hi=POOL_DIM + Q_DIM)
    return pl.pallas_call(
        body,
        grid=(r // tmd,),
        in_specs=in_specs + [_const_spec((1, d)), _const_spec((d, n))],
        out_specs=[pl.BlockSpec((tmd, n), lambda i: (i, 0)), pl.BlockSpec((tmd, d), lambda i: (i, 0))],
        out_shape=[jax.ShapeDtypeStruct((r, n), BF16), jax.ShapeDtypeStruct((r, d), F32)],
        compiler_params=pltpu.CompilerParams(dimension_semantics=("arbitrary",), vmem_limit_bytes=VMEM_LIMIT),
        name="l0_norm_proj_rope",
    )(*args, gain.reshape(1, d), w)


def _post_mm_body(*refs, n_a, route):
    a_refs = refs[:n_a]
    w_refs = refs[n_a:2 * n_a]
    x_refs = refs[2 * n_a:2 * n_a + PAIR]
    gpost_ref, gpre_ref = refs[2 * n_a + PAIR:2 * n_a + PAIR + 2]
    mod_refs = refs[2 * n_a + PAIR + 2:2 * n_a + 4 * PAIR + 2]
    rest = refs[2 * n_a + 4 * PAIR + 2:]
    if route:
        rhi_ref, rlo_ref, xo_ref, ho_ref, route_ref, cnt_ref, carry_ref = rest
    else:
        xo_ref, ho_ref = rest
    ys = []
    for s in range(0, PAIR * TM, EPI_ROWS):
        rows = slice(s, s + EPI_ROWS)
        y = jnp.dot(a_refs[0][rows, :], w_refs[0][...], preferred_element_type=F32)
        for a_ref, w_ref in zip(a_refs[1:], w_refs[1:]):
            y = y + jnp.dot(a_ref[rows, :], w_ref[...], preferred_element_type=F32)
        ys.append(y)
    _residual_epilogue(ys, x_refs, gpost_ref, gpre_ref, mod_refs, xo_ref, ho_ref)
    if route:
        _route_rows(ho_ref[...], rhi_ref, rlo_ref, route_ref, cnt_ref, carry_ref)


def _residual_epilogue(ys, x_refs, gpost_ref, gpre_ref, mod_refs, xo_ref, ho_ref):
    for k, y in enumerate(ys):
        s = k * EPI_ROWS
        t, r0 = s // TM, s % TM
        rows = slice(s, s + EPI_ROWS)
        gate_ref, sh_ref, sc_ref = mod_refs[3 * t:3 * t + 3]
        x = x_refs[t][r0:r0 + EPI_ROWS, :] if len(x_refs) == PAIR else x_refs[0][rows, :]
        xn = x + gate_ref[0] * _rms(y, gpost_ref[...])
        xo_ref[rows, :] = xn
        ho_ref[rows, :] = (_rms(xn, gpre_ref[...]) * (1.0 + sc_ref[0]) + sh_ref[0]).astype(ho_ref.dtype)


def _pair_mod_specs(kinds, tiles_per_batch, ctx_tiles, d):
    return [_mod_spec(k, tiles_per_batch, ctx_tiles, d, half=t) for t in range(PAIR) for k in kinds]


def post_mm(a_list, w_list, x, x_tile_offset, g_post, g_pre, mod, tiles_per_batch, ctx_tiles, x_tiles_per_batch,
            h_dtype, name, router=None):
    r = a_list[0].shape[0]
    d = w_list[0].shape[1]
    n_a = len(a_list)
    tmd = PAIR * TM
    route = router is not None
    route_in = [_const_spec((d, LANES)), _const_spec((d, LANES))] if route else []
    route_args = list(router_operands(router, d)) if route else []
    route_out = [pl.BlockSpec((tmd, LANES), lambda i: (i, 0)), pl.BlockSpec((8, LANES), lambda i: (0, 0))] if route else []
    route_shape = [jax.ShapeDtypeStruct((r, LANES), F32), jax.ShapeDtypeStruct((8, LANES), F32)] if route else []

    def x_spec(half):
        def index(i):
            t = PAIR * i + half
            return (t // tiles_per_batch) * x_tiles_per_batch + x_tile_offset + t % tiles_per_batch, 0
        return pl.BlockSpec((TM, d), index)

    in_specs = ([pl.BlockSpec((tmd, a.shape[1]), lambda i: (i, 0)) for a in a_list]
                + [_const_spec(w.shape) for w in w_list]
                + [x_spec(t) for t in range(PAIR)]
                + [_const_spec((1, d)), _const_spec((1, d))]
                + _pair_mod_specs((MOD_GATE_M, MOD_SHIFT_F, MOD_SCALE_F), tiles_per_batch, ctx_tiles, d)
                + route_in)
    return pl.pallas_call(
        functools.partial(_post_mm_body, n_a=n_a, route=route),
        grid=(r // tmd,),
        in_specs=in_specs,
        out_specs=[pl.BlockSpec((tmd, d), lambda i: (i, 0)), pl.BlockSpec((tmd, d), lambda i: (i, 0))] + route_out,
        out_shape=[jax.ShapeDtypeStruct((r, d), F32), jax.ShapeDtypeStruct((r, d), h_dtype)] + route_shape,
        scratch_shapes=[pltpu.VMEM((1, LANES), F32)] if route else [],
        compiler_params=pltpu.CompilerParams(dimension_semantics=("arbitrary",), vmem_limit_bytes=VMEM_LIMIT),
        name=name,
    )(*a_list, *w_list, *([x] * PAIR), g_post.reshape(1, d), g_pre.reshape(1, d), *([mod] * (3 * PAIR)),
      *route_args)


def _out_ffn_body(*refs):
    pooled_ref, attn_ref, wa_ref, wb_ref = refs[:4]
    x_refs = refs[4:4 + PAIR]
    g_mix_post, g_ffn_pre, g_ffn_post, g_next_pre = refs[4 + PAIR:8 + PAIR]
    mix_mods = refs[8 + PAIR:8 + 4 * PAIR]
    ffn_mods = refs[8 + 4 * PAIR:8 + 7 * PAIR]
    w1_ref, w3_ref, w2_ref = refs[8 + 7 * PAIR:11 + 7 * PAIR]
    xo_ref, ho_ref, x1_s, h1_s, acc_ref = refs[11 + 7 * PAIR:]
    blocks = range(0, PAIR * TM, EPI_ROWS)
    ys = [jnp.dot(pooled_ref[s:s + EPI_ROWS, :], wa_ref[...], preferred_element_type=F32)
          + jnp.dot(attn_ref[s:s + EPI_ROWS, :], wb_ref[...], preferred_element_type=F32) for s in blocks]
    _residual_epilogue(ys, x_refs, g_mix_post, g_ffn_pre, mix_mods, x1_s, h1_s)
    h = h1_s[...]
    n_f = w1_ref.shape[1]
    for f0 in range(0, n_f, FFN_TF):
        a = jnp.dot(h, w1_ref[:, f0:f0 + FFN_TF], preferred_element_type=F32)
        b = jnp.dot(h, w3_ref[:, f0:f0 + FFN_TF], preferred_element_type=F32)
        mid = (a * jax.nn.sigmoid(a) * b).astype(BF16)
        part = jnp.dot(mid, w2_ref[f0:f0 + FFN_TF, :], preferred_element_type=F32)
        if f0 == 0:
            acc_ref[...] = part
        else:
            acc_ref[...] += part
    _residual_epilogue([acc_ref[s:s + EPI_ROWS, :] for s in blocks], (x1_s,), g_ffn_post, g_next_pre, ffn_mods,
                       xo_ref, ho_ref)


def out_ffn_layer(pooled, attn, w_out, x, g_mix_post, g_ffn_pre, g_ffn_post, g_next_pre, mod, mod_next,
                  w1, w3, w2, tiles_per_batch, ctx_tiles):
    r, d = x.shape
    tmd = PAIR * TM
    n_pool = pooled.shape[1]
    gate_specs = _pair_mod_specs((MOD_GATE_F,), tiles_per_batch, ctx_tiles, d)
    next_specs = _pair_mod_specs((MOD_SHIFT_M, MOD_SCALE_M), tiles_per_batch, ctx_tiles, d)
    ffn_specs, ffn_args = [], []
    for t in range(PAIR):
        ffn_specs += [gate_specs[t], next_specs[2 * t], next_specs[2 * t + 1]]
        ffn_args += [mod, mod_next, mod_next]
    vec = lambda g: g.reshape(1, d)
    return pl.pallas_call(
        _out_ffn_body,
        grid=(r // tmd,),
        in_specs=([pl.BlockSpec((tmd, n_pool), lambda i: (i, 0)), pl.BlockSpec((tmd, attn.shape[1]), lambda i: (i, 0)),
                   _const_spec((n_pool, d)), _const_spec((w_out.shape[0] - n_pool, d))]
                  + [pl.BlockSpec((TM, d), lambda i, t=t: (PAIR * i + t, 0)) for t in range(PAIR)]
                  + [_const_spec((1, d))] * 4
                  + _pair_mod_specs((MOD_GATE_M, MOD_SHIFT_F, MOD_SCALE_F), tiles_per_batch, ctx_tiles, d)
                  + ffn_specs
                  + [_const_spec(w1.shape), _const_spec(w3.shape), _const_spec(w2.shape)]),
        out_specs=[pl.BlockSpec((tmd, d), lambda i: (i, 0)), pl.BlockSpec((tmd, d), lambda i: (i, 0))],
        out_shape=[jax.ShapeDtypeStruct((r, d), F32), jax.ShapeDtypeStruct((r, d), BF16)],
        scratch_shapes=[pltpu.VMEM((tmd, d), F32), pltpu.VMEM((tmd, d), BF16), pltpu.VMEM((tmd, d), F32)],
        compiler_params=pltpu.CompilerParams(dimension_semantics=("arbitrary",), vmem_limit_bytes=VMEM_LIMIT),
        name="l0_out_ffn",
    )(pooled, attn, w_out[:n_pool], w_out[n_pool:], *([x] * PAIR), vec(g_mix_post), vec(g_ffn_pre), vec(g_ffn_post),
      vec(g_next_pre), *([mod] * (3 * PAIR)), *ffn_args, w1, w3, w2)


def _proj_body(h_ref, w_ref, wg_ref, o_ref, og_ref):
    h = h_ref[...]
    o_ref[...] = jnp.dot(h, w_ref[...], preferred_element_type=F32).astype(o_ref.dtype)
    og_ref[...] = jnp.dot(h, wg_ref[...], preferred_element_type=F32)


def gdn_proj(h, w, w_gate):
    r, d = h.shape
    n = w.shape[1]
    tmd = PAIR * TM
    return pl.pallas_call(
        _proj_body,
        grid=(r // tmd,),
        in_specs=[pl.BlockSpec((tmd, d), lambda i: (i, 0)), _const_spec(w.shape), _const_spec(w_gate.shape)],
        out_specs=[pl.BlockSpec((tmd, n), lambda i: (i, 0)), pl.BlockSpec((tmd, LANES), lambda i: (i, 0))],
        out_shape=[jax.ShapeDtypeStruct((r, n), BF16), jax.ShapeDtypeStruct((r, LANES), F32)],
        compiler_params=pltpu.CompilerParams(dimension_semantics=("arbitrary",), vmem_limit_bytes=VMEM_LIMIT),
        name="l1_proj",
    )(h, w, w_gate)


ATTN_HEADS_PER_UNIT = 1


def _attn_body(sink_ref, q_ref, kc_ref, vc_ref, k0_ref, k1_ref, k2_ref, v0_ref, v1_ref, v2_ref, o_ref,
               *, n_ctx_blocks, n_blocks):
    n = pl.program_id(1)
    rows = ATTN_HEADS_PER_UNIT * BLOCK
    qi = lax.broadcasted_iota(I32, (rows, 3 * BLOCK), 0) % BLOCK
    kj = lax.broadcasted_iota(I32, (rows, 3 * BLOCK), 1)
    first = jnp.where(n > n_ctx_blocks, 0, BLOCK)
    last = jnp.where(n < n_blocks - 1, 3 * BLOCK, 2 * BLOCK)
    last = jnp.where(n >= n_ctx_blocks, last, 0)
    band_ok = (kj >= qi) & (kj <= qi + 2 * WINDOW) & (kj >= first) & (kj < last)
    q = q_ref[0]
    nt = (((1,), (1,)), ((), ()))
    groups = range(N_KV_HEADS)
    hpu = ATTN_HEADS_PER_UNIT
    units = [(g, [g * GQA_GROUP + s + i for i in range(hpu)]) for g in groups for s in range(0, GQA_GROUP, hpu)]
    idx = range(len(units))
    cols = [slice(g * HEAD_DIM, (g + 1) * HEAD_DIM) for g in groups]
    q_u = [jnp.concatenate([q[:, hd * HEAD_DIM:(hd + 1) * HEAD_DIM] for hd in hs], axis=0) for _, hs in units]
    k_c = [kc_ref[0][:, c] for c in cols]
    k_b = [jnp.concatenate([k0_ref[0][:, c], k1_ref[0][:, c], k2_ref[0][:, c]], axis=0) for c in cols]
    ones_b = jnp.ones((3 * BLOCK, HEAD_DIM), BF16)
    ones_c = jnp.ones((kc_ref.shape[1], HEAD_DIM), BF16)
    v_b = [jnp.concatenate([jnp.concatenate([v0_ref[0][:, c], v1_ref[0][:, c], v2_ref[0][:, c]], axis=0), ones_b],
                           axis=1) for c in cols]
    v_c = [jnp.concatenate([vc_ref[0][:, c], ones_c], axis=1) for c in cols]
    s_c = [lax.dot_general(q_u[u], k_c[units[u][0]], nt, preferred_element_type=F32) for u in idx]
    s_b = [lax.dot_general(q_u[u], k_b[units[u][0]], nt, preferred_element_type=F32) for u in idx]
    s_b = [jnp.where(band_ok, s, NEG_INF) for s in s_b]
    sink = [jnp.concatenate([sink_ref[hd] + jnp.zeros((BLOCK, 1), F32) for hd in hs], axis=0) for _, hs in units]
    m = [jnp.maximum(jnp.maximum(jnp.max(s_c[u], axis=-1, keepdims=True), jnp.max(s_b[u], axis=-1, keepdims=True)),
                     sink[u]) for u in idx]
    p_c = [jnp.exp(s_c[u] - m[u]) for u in idx]
    p_b = [jnp.exp(s_b[u] - m[u]) for u in idx]
    o_c = [jnp.dot(p_c[u].astype(BF16), v_c[units[u][0]], preferred_element_type=F32) for u in idx]
    o_b = [jnp.dot(p_b[u].astype(BF16), v_b[units[u][0]], preferred_element_type=F32) for u in idx]
    outs = []
    for u in idx:
        acc = o_c[u] + o_b[u]
        den = acc[:, HEAD_DIM:HEAD_DIM + 1] + jnp.exp(sink[u] - m[u])
        o = acc[:, :HEAD_DIM] / den
        outs += [o[i * BLOCK:(i + 1) * BLOCK] for i in range(hpu)]
    o_ref[0] = jnp.concatenate(outs, axis=1).astype(o_ref.dtype)


def attention(p0, sinks, n_ctx):
    bsz, ltot, _ = p0.shape
    n_blocks = ltot // BLOCK
    n_ctx_blocks = n_ctx // BLOCK
    k_blk = (POOL_DIM + Q_DIM) // KV_DIM
    v_blk = k_blk + 1

    def band(dn, blk):
        return pl.BlockSpec((1, BLOCK, KV_DIM),
                            lambda b, n: (b, jnp.clip(n + dn, n_ctx_blocks, n_blocks - 1), blk))

    return pl.pallas_call(
        functools.partial(_attn_body, n_ctx_blocks=n_ctx_blocks, n_blocks=n_blocks),
        grid=(bsz, n_blocks),
        in_specs=[pl.BlockSpec(memory_space=pltpu.SMEM),
                  pl.BlockSpec((1, BLOCK, Q_DIM), lambda b, n: (b, n, POOL_DIM // Q_DIM)),
                  pl.BlockSpec((1, n_ctx, KV_DIM), lambda b, n: (b, 0, k_blk)),
                  pl.BlockSpec((1, n_ctx, KV_DIM), lambda b, n: (b, 0, v_blk)),
                  band(-1, k_blk), band(0, k_blk), band(1, k_blk),
                  band(-1, v_blk), band(0, v_blk), band(1, v_blk)],
        out_specs=pl.BlockSpec((1, BLOCK, Q_DIM), lambda b, n: (b, n, 0)),
        out_shape=jax.ShapeDtypeStruct((bsz, ltot, Q_DIM), BF16),
        compiler_params=pltpu.CompilerParams(dimension_semantics=("arbitrary", "arbitrary"),
                                             vmem_limit_bytes=VMEM_LIMIT),
        name="l0_attention",
    )(sinks.astype(F32), p0, p0, p0, p0, p0, p0, p0, p0, p0)


POOL_HALO = 16


def _pool_body(up_ref, uc_ref, un_ref, w_ref, scale_ref, o_ref, *, tiles_per_batch, ctx_tiles):
    j = pl.program_id(1)
    seg_lo = jnp.where(j < ctx_tiles, 0, ctx_tiles * TM)
    seg_hi = jnp.where(j < ctx_tiles, ctx_tiles * TM, tiles_per_batch * TM)
    halo = jnp.concatenate([up_ref[0][TM - POOL_HALO:], uc_ref[0], un_ref[0][:POOL_HALO]], axis=0)
    n_h = TM + 2 * POOL_HALO
    t = j * TM + lax.broadcasted_iota(I32, (TM, n_h), 0)
    pos = j * TM - POOL_HALO + lax.broadcasted_iota(I32, (TM, n_h), 1)
    t_col = j * TM + lax.broadcasted_iota(I32, (TM, 1), 0)
    cur = uc_ref[0]
    groups = range(len(POOL_WINDOWS))
    cols = [slice(g * POOL_GROUP_DIM, (g + 1) * POOL_GROUP_DIM) for g in groups]
    windows, cnts = [], []
    for w in POOL_WINDOWS:
        lo = jnp.maximum(t - w // 2, seg_lo)
        hi = jnp.minimum(t + w // 2, seg_hi)
        windows.append(((pos >= lo) & (pos < hi)).astype(BF16))
        cnts.append((jnp.minimum(t_col + w // 2, seg_hi) - jnp.maximum(t_col - w // 2, seg_lo)).astype(F32))
    sums = [jnp.dot(windows[g], halo[:, cols[g]], preferred_element_type=F32) for g in groups]
    deltas = [(sums[g] / cnts[g] - cur[:, cols[g]].astype(F32)).astype(BF16) for g in groups]
    ys = [jnp.dot(deltas[g], w_ref[g], preferred_element_type=F32) for g in groups]
    for g in groups:
        o_ref[0, :, cols[g]] = (ys[g] * scale_ref[:, cols[g]]).astype(o_ref.dtype)


def pool_mixer(p0, pool_w, pool_scale, n_ctx):
    bsz, ltot, _ = p0.shape
    tiles_per_batch = ltot // TM
    ctx_tiles = n_ctx // TM

    def tile(dj):
        return pl.BlockSpec((1, TM, POOL_DIM), lambda b, j: (b, jnp.clip(j + dj, 0, tiles_per_batch - 1), 0))

    return pl.pallas_call(
        functools.partial(_pool_body, tiles_per_batch=tiles_per_batch, ctx_tiles=ctx_tiles),
        grid=(bsz, tiles_per_batch),
        in_specs=[tile(-1), tile(0), tile(1),
                  pl.BlockSpec(pool_w.shape, lambda b, j: (0, 0, 0)),
                  pl.BlockSpec((1, POOL_DIM), lambda b, j: (0, 0))],
        out_specs=pl.BlockSpec((1, TM, POOL_DIM), lambda b, j: (b, j, 0)),
        out_shape=jax.ShapeDtypeStruct((bsz, ltot, POOL_DIM), BF16),
        compiler_params=pltpu.CompilerParams(dimension_semantics=("arbitrary", "arbitrary")),
        name="l0_pool",
    )(p0, p0, p0, pool_w.astype(BF16), pool_scale.reshape(1, POOL_DIM).astype(F32))


def kernel(x, c, ctx, c_ctx, l0_mod_w, l0_mod_b, l0_mix_pre, l0_mix_post, l0_ffn_pre, l0_ffn_post, l0_w_in, l0_pool_w, l0_pool_scale, l0_sinks, l0_w_out, l0_ffn_w1, l0_ffn_w3, l0_ffn_w2, l1_mod_w, l1_mod_b, l1_mix_pre, l1_mix_post, l1_ffn_pre, l1_ffn_post, l1_w_in, l1_conv_w, l1_a_log, l1_dt_bias, l1_out_norm, l1_w_out, l1_router, l1_moe_w1, l1_moe_w3, l1_moe_w2):
    bsz, n_lat, d = x.shape
    n_ctx = ctx.shape[1]
    ltot = n_ctx + n_lat
    assert n_ctx % TM == 0 and n_lat % TM == 0 and n_ctx % GC == 0
    assert (bsz * ltot) % (PAIR * TM) == 0 and (bsz * n_lat) % (PAIR * TM) == 0
    tiles = ltot // TM
    ctx_tiles = n_ctx // TM
    lat_tiles = n_lat // TM
    bf = lambda w: w.astype(BF16)

    mod0 = modulation(c, c_ctx, l0_mod_w, l0_mod_b)
    mod1 = modulation(c, c_ctx, l1_mod_w, l1_mod_b)
    cos, sin = rope_tables(n_ctx, n_lat)

    p0, x_all = pre_mm(ctx, x, l0_mix_pre, mod0, bf(l0_w_in), cos, sin)
    p0 = p0.reshape(bsz, ltot, -1)
    attn = attention(p0, l0_sinks, n_ctx).reshape(bsz * ltot, Q_DIM)
    pooled = pool_mixer(p0, l0_pool_w, l0_pool_scale, n_ctx).reshape(bsz * ltot, POOL_DIM)
    x2, h2 = out_ffn_layer(pooled, attn, bf(l0_w_out), x_all, l0_mix_post, l0_ffn_pre, l0_ffn_post, l1_mix_pre,
                           mod0, mod1, bf(l0_ffn_w1), bf(l0_ffn_w3), bf(l0_ffn_w2), tiles, ctx_tiles)

    n_heads = l1_a_log.shape[1]
    qkvz = 4 * n_heads * GD
    w_in1 = bf(l1_w_in)
    w_gate = jnp.zeros((d, LANES), BF16).at[:, :4 * n_heads].set(w_in1[:, qkvz:])
    p1, ab = gdn_proj(h2, w_in1[:, :qkvz], w_gate)
    ab_rows = gdn_gate_rows(ab.reshape(bsz, ltot, LANES)[..., :4 * n_heads], n_heads)
    y = gdn_core(p1.reshape(bsz, ltot, qkvz), ab_rows, l1_conv_w, l1_a_log, l1_dt_bias, l1_out_norm, n_ctx)
    x3, h3, route, cnt = post_mm([y.reshape(bsz * n_lat, n_heads * GD)], [bf(l1_w_out)], x2, ctx_tiles, l1_mix_post,
                                 l1_ffn_pre, mod1, lat_tiles, 0, tiles, F32, "l1_out_proj_route", router=l1_router)
    out = moe_layer(h3, route, cnt, l1_moe_w1, l1_moe_w3, l1_moe_w2, x3, l1_ffn_post, mod1, n_lat)
    return out.reshape(bsz, n_lat, d)
```
